```python
import math
import jax, jax.numpy as jnp
from jax import lax
import numpy as np

D_MODEL = 1024
BATCH = 32
SEQ = 2048
DEPTH = 4

N_META = 16
GLA_H = 4
GLA_DK = D_MODEL // 2 // GLA_H
GLA_DV = D_MODEL // GLA_H
GLA_LR = 16
GLA_TAU = 16.0
GLA_CHUNK = 64
DSA_H = 16
DSA_DH = D_MODEL // DSA_H
DSA_Q_LORA = D_MODEL // 4
DSA_KV_LAT = D_MODEL // 8
IDX_H = 8
IDX_D = 64
TOPK_MAX = 256
QBLK = 128
ML_H = 4
ML_DK = D_MODEL // 2 // ML_H
ML_DV = D_MODEL // ML_H
ML_CONV = 4
ML_CHUNK = 64
T5_BUCKETS = 32
T5_MAX_DIST = 128
D_FF = 4 * D_MODEL
EPS = 1e-6
NEG = -1e30

COL_SPLITS = (
    GLA_H * GLA_DK, GLA_H * GLA_DK, GLA_H * GLA_DV, GLA_H * GLA_DV, GLA_LR,
    DSA_Q_LORA, DSA_KV_LAT, IDX_D, IDX_H,
    ML_H * ML_DK, ML_H * ML_DK, ML_H * ML_DV, ML_H * ML_DV, ML_H, ML_H,
    D_MODEL, D_MODEL, D_MODEL,
)
N_COLS = sum(COL_SPLITS)

kernel_name = "hybrid_gla_dsa_mlstm_trunk"


def rms_norm(x, g):
    x32 = x.astype(jnp.float32)
    y = x32 * lax.rsqrt(jnp.mean(x32 * x32, axis=-1, keepdims=True) + EPS)
    return (y * g.astype(jnp.float32)).astype(x.dtype)


def layer_norm(x, g, b):
    x32 = x.astype(jnp.float32)
    mu = jnp.mean(x32, axis=-1, keepdims=True)
    var = jnp.mean(jnp.square(x32 - mu), axis=-1, keepdims=True)
    y = (x32 - mu) * lax.rsqrt(var + EPS) * g.astype(jnp.float32) + b.astype(jnp.float32)
    return y.astype(x.dtype)


def to_chunks(a, chunk):
    front = (-a.shape[1]) % chunk
    a = jnp.pad(a, [(0, 0), (front, 0)] + [(0, 0)] * (a.ndim - 2))
    b, tp = a.shape[:2]
    a = a.reshape(b, tp // chunk, chunk, *a.shape[2:])
    perm = (1, 0, 3, 2) + tuple(range(4, a.ndim))
    return a.transpose(perm), front


def from_chunks(a, front):
    a = a.transpose(1, 0, 3, 2, 4)
    b, nc, c, h, d = a.shape
    return a.reshape(b, nc * c, h, d)[:, front:]


def causal_dwconv(x, w):
    return lax.conv_general_dilated(
        x, w[:, None, :].astype(x.dtype), window_strides=(1,), padding=[(ML_CONV - 1, 0)],
        dimension_numbers=("NWC", "WIO", "NWC"), feature_group_count=x.shape[-1])


def t5_bucket(rel):
    n = jnp.maximum(rel, 0)
    max_exact = T5_BUCKETS // 2
    nf = jnp.maximum(n, 1).astype(jnp.float32)
    large = max_exact + (jnp.log(nf / max_exact) / math.log(T5_MAX_DIST / max_exact)
                         * (T5_BUCKETS - max_exact)).astype(jnp.int32)
    large = jnp.minimum(large, T5_BUCKETS - 1)
    return jnp.where(n < max_exact, n, large)


def gla_chunk_step(S, xs):
    q, k, v, g = xs
    C = q.shape[2]
    G = jnp.cumsum(g, axis=2)
    causal = jnp.tril(jnp.ones((C, C), bool))
    diff = G[:, :, :, None, :] - G[:, :, None, :, :]
    decay = jnp.exp(jnp.where(causal[None, None, :, :, None], diff, -jnp.inf))
    A = jnp.einsum("bhtk,bhtsk,bhsk->bhts", q, decay, k)
    o = jnp.einsum("bhtk,bhkv->bhtv", q * jnp.exp(G), S) + jnp.einsum("bhts,bhsv->bhtv", A, v)
    G_last = G[:, :, -1:, :]
    S_new = jnp.exp(G_last[:, :, 0, :])[..., None] * S + jnp.einsum(
        "bhsk,bhsv->bhkv", k * jnp.exp(G_last - G), v)
    return S_new, o


def gla_branch(q, k, v, r, a_lr, p):
    B, T = q.shape[:2]
    q = q.reshape(B, T, GLA_H, GLA_DK) * GLA_DK ** -0.5
    k = k.reshape(B, T, GLA_H, GLA_DK)
    v = v.reshape(B, T, GLA_H, GLA_DV)
    g = jax.nn.log_sigmoid((a_lr @ p["w_gla_a2"] + p["b_gla_a"]).astype(jnp.float32)) / GLA_TAU
    g = g.reshape(B, T, GLA_H, GLA_DK)
    qc, front = to_chunks(q, GLA_CHUNK)
    kc, _ = to_chunks(k, GLA_CHUNK)
    vc, _ = to_chunks(v, GLA_CHUNK)
    gc, _ = to_chunks(g, GLA_CHUNK)
    s0 = jnp.zeros((B, GLA_H, GLA_DK, GLA_DV), jnp.float32)
    _, oc = lax.scan(gla_chunk_step, s0, (qc, kc, vc, gc))
    o = from_chunks(oc, front)
    o = rms_norm(o, p["g_gla_head"]) * jax.nn.silu(r.reshape(B, T, GLA_H, GLA_DV))
    return o.reshape(B, T, GLA_H * GLA_DV).astype(v.dtype)


def dsa_branch(c_q, c_kv, k_idx, w_idx, p, t5_table):
    B, T = c_q.shape[:2]
    topk = min(TOPK_MAX, (T - N_META) // 4)
    cq = rms_norm(c_q, p["g_dsa_q"])
    q = (cq @ p["w_dsa_uq"]).reshape(B, T, DSA_H, DSA_KV_LAT) * DSA_KV_LAT ** -0.5
    qi = (cq @ p["w_idx_q"]).reshape(B, T, IDX_H, IDX_D) * IDX_D ** -0.5
    wi = w_idx * IDX_H ** -0.5
    kv = rms_norm(c_kv, p["g_dsa_kv"])
    ki = layer_norm(k_idx, p["idx_ln_g"], p["idx_ln_b"])

    front = (-T) % QBLK
    nblk = (T + front) // QBLK

    def blk(a):
        a = jnp.pad(a, [(0, 0), (front, 0)] + [(0, 0)] * (a.ndim - 2))
        return a.reshape(B, nblk, QBLK, *a.shape[2:]).swapaxes(0, 1)

    qpos = (jnp.arange(T + front, dtype=jnp.int32) - front).reshape(nblk, QBLK)
    kpos = jnp.arange(T, dtype=jnp.int32)

    def attend_block(args):
        qb, qib, wb, pb = args
        score = jnp.einsum("bqh,bqhs->bqs", wb,
                           jax.nn.relu(jnp.einsum("bqhd,bsd->bqhs", qib, ki))).astype(jnp.float32)
        score = jnp.where(kpos[None, None, :] <= pb[None, :, None], score, NEG)
        _, idx = lax.top_k(score, topk)
        kv_sel = jax.vmap(lambda kv_b, i_b: kv_b[i_b])(kv, idx)
        bias = t5_table[t5_bucket(pb[None, :, None] - idx)]
        logits = (jnp.einsum("bqhc,bqkc->bhqk", qb, kv_sel).astype(jnp.float32)
                  + bias.transpose(0, 3, 1, 2).astype(jnp.float32))
        logits = jnp.where((idx <= pb[None, :, None])[:, None], logits, NEG)
        prob = jax.nn.softmax(logits, axis=-1).astype(kv_sel.dtype)
        return jnp.einsum("bhqk,bqkc->bqhc", prob, kv_sel)

    o_lat = lax.map(attend_block, (blk(q), blk(qi), blk(wi), qpos))
    o_lat = o_lat.swapaxes(0, 1).reshape(B, nblk * QBLK, DSA_H, DSA_KV_LAT)[:, front:]
    o = jnp.einsum("bthc,hcd->bthd", o_lat, p["w_dsa_uv"])
    return o.reshape(B, T, DSA_H * DSA_DH)


def mlstm_chunk_step(carry, xs):
    Cs, ns, ms = carry
    q, k, v, li, lf = xs
    C = q.shape[2]
    causal = jnp.tril(jnp.ones((C, C), bool))
    b = jnp.cumsum(lf, axis=-1)
    D = jnp.where(causal, b[..., :, None] - b[..., None, :] + li[..., None, :], -jnp.inf)
    m_inter = b + ms[..., None]
    m = jnp.maximum(m_inter, jnp.max(D, axis=-1))
    w_inter = jnp.exp(m_inter - m)
    P = jnp.exp(D - m[..., None]) * jnp.einsum("bhtk,bhsk->bhts", q, k)
    num = w_inter[..., None] * jnp.einsum("bhtk,bhkv->bhtv", q, Cs) + jnp.einsum("bhts,bhsv->bhtv", P, v)
    den = w_inter * jnp.einsum("bhtk,bhk->bht", q, ns) + jnp.sum(P, axis=-1)
    h = num / jnp.maximum(jnp.abs(den), jnp.exp(-m))[..., None]
    b_last = b[..., -1]
    dec = b_last[..., None] - b + li
    m_new = jnp.maximum(b_last + ms, jnp.max(dec, axis=-1))
    w_old = jnp.exp(b_last + ms - m_new)
    w_new = jnp.exp(dec - m_new[..., None])
    Cs_new = w_old[..., None, None] * Cs + jnp.einsum("bhs,bhsk,bhsv->bhkv", w_new, k, v)
    ns_new = w_old[..., None] * ns + jnp.einsum("bhs,bhsk->bhk", w_new, k)
    return (Cs_new, ns_new, m_new), h


def mlstm_branch(q, k, v, o_pre, i_pre, f_pre, p):
    B, T = q.shape[:2]
    qk = jax.nn.silu(causal_dwconv(jnp.concatenate([q, k], axis=-1), p["ml_conv"]))
    q, k = jnp.split(qk, 2, axis=-1)
    q = q.reshape(B, T, ML_H, ML_DK) * ML_DK ** -0.5
    k = k.reshape(B, T, ML_H, ML_DK)
    v = v.reshape(B, T, ML_H, ML_DV)
    li = i_pre.astype(jnp.float32)
    lf = jax.nn.log_sigmoid(f_pre.astype(jnp.float32) + p["ml_f_bias"].astype(jnp.float32))
    qc, front = to_chunks(q, ML_CHUNK)
    kc, _ = to_chunks(k, ML_CHUNK)
    vc, _ = to_chunks(v, ML_CHUNK)
    lic, _ = to_chunks(li, ML_CHUNK)
    lfc, _ = to_chunks(lf, ML_CHUNK)
    init = (jnp.zeros((B, ML_H, ML_DK, ML_DV), jnp.float32),
            jnp.zeros((B, ML_H, ML_DK), jnp.float32),
            jnp.zeros((B, ML_H), jnp.float32))
    _, hc = lax.scan(mlstm_chunk_step, init, (qc, kc, vc, lic, lfc))
    h = from_chunks(hc, front)
    h = rms_norm(h, p["g_ml_head"]) * jax.nn.sigmoid(o_pre).reshape(B, T, ML_H, ML_DV)
    return h.reshape(B, T, ML_H * ML_DV).astype(v.dtype)


def hybrid_layer(h, p, t5_table):
    xn = rms_norm(h, p["g_pre_mix"])
    z = xn @ p["w_in"] + p["b_in"]
    (gla_q, gla_k, gla_v, gla_r, gla_a, dsa_cq, dsa_ckv, idx_k, idx_w,
     ml_q, ml_k, ml_v, ml_o, ml_i, ml_f, gate_a, gate_b, gate_c) = jnp.split(
        z, np.cumsum(COL_SPLITS)[:-1].tolist(), axis=-1)
    y_a = gla_branch(gla_q, gla_k, gla_v, gla_r, gla_a, p) @ p["w_br_gla"]
    y_b = dsa_branch(dsa_cq, dsa_ckv, idx_k, idx_w, p, t5_table) @ p["w_br_dsa"]
    y_c = mlstm_branch(ml_q, ml_k, ml_v, ml_o, ml_i, ml_f, p) @ p["w_br_ml"]
    mix = jax.nn.sigmoid(gate_a) * y_a + jax.nn.sigmoid(gate_b) * y_b + jax.nn.sigmoid(gate_c) * y_c
    h = h + rms_norm(mix @ p["w_out"], p["g_post_mix"])
    xn = rms_norm(h, p["g_pre_mlp"])
    u = jnp.square(jax.nn.relu(xn @ p["w_up"]))
    return h + rms_norm(u @ p["w_down"], p["g_post_mlp"])


def setup_inputs(seed: int = 0) -> dict:
    key = jax.random.key(seed)
    ks = jax.random.split(key, 32)
    f32 = jnp.float32
    L = DEPTH

    def nrm(k, shape, scale):
        return scale * jax.random.normal(k, shape, f32)

    def gain(k, shape):
        return 1.0 + 0.05 * jax.random.normal(k, shape, f32)

    return {
        "x": nrm(ks[0], (BATCH, SEQ, D_MODEL), 1.0),
        "meta_tokens": nrm(ks[1], (N_META, D_MODEL), 1.0),
        "t5_bias": nrm(ks[2], (T5_BUCKETS, DSA_H), 0.5),
        "g_pre_mix": gain(ks[3], (L, D_MODEL)),
        "w_in": nrm(ks[4], (L, D_MODEL, N_COLS), D_MODEL ** -0.5),
        "b_in": nrm(ks[5], (L, N_COLS), 0.01),
        "w_gla_a2": nrm(ks[6], (L, GLA_LR, GLA_H * GLA_DK), GLA_LR ** -0.5),
        "b_gla_a": nrm(ks[7], (L, GLA_H * GLA_DK), 0.1),
        "g_gla_head": gain(ks[8], (L, GLA_DV)),
        "w_br_gla": nrm(ks[9], (L, GLA_H * GLA_DV, D_MODEL), (GLA_H * GLA_DV) ** -0.5),
        "g_dsa_q": gain(ks[10], (L, DSA_Q_LORA)),
        "w_dsa_uq": nrm(ks[11], (L, DSA_Q_LORA, DSA_H * DSA_KV_LAT), DSA_Q_LORA ** -0.5),
        "w_idx_q": nrm(ks[12], (L, DSA_Q_LORA, IDX_H * IDX_D), DSA_Q_LORA ** -0.5),
        "g_dsa_kv": gain(ks[13], (L, DSA_KV_LAT)),
        "w_dsa_uv": nrm(ks[14], (L, DSA_H, DSA_KV_LAT, DSA_DH), DSA_KV_LAT ** -0.5),
        "idx_ln_g": gain(ks[15], (L, IDX_D)),
        "idx_ln_b": nrm(ks[16], (L, IDX_D), 0.01),
        "w_br_dsa": nrm(ks[17], (L, DSA_H * DSA_DH, D_MODEL), (DSA_H * DSA_DH) ** -0.5),
        "ml_conv": nrm(ks[18], (L, ML_CONV, 2 * ML_H * ML_DK), ML_CONV ** -0.5),
        "ml_f_bias": jnp.linspace(3.0, 6.0, ML_H, dtype=f32)[None, :] + nrm(ks[19], (L, ML_H), 0.1),
        "g_ml_head": gain(ks[20], (L, ML_DV)),
        "w_br_ml": nrm(ks[21], (L, ML_H * ML_DV, D_MODEL), (ML_H * ML_DV) ** -0.5),
        "w_out": nrm(ks[22], (L, D_MODEL, D_MODEL), D_MODEL ** -0.5),
        "g_post_mix": gain(ks[23], (L, D_MODEL)),
        "g_pre_mlp": gain(ks[24], (L, D_MODEL)),
        "w_up": nrm(ks[25], (L, D_MODEL, D_FF), D_MODEL ** -0.5),
        "w_down": nrm(ks[26], (L, D_FF, D_MODEL), D_FF ** -0.5),
        "g_post_mlp": gain(ks[27], (L, D_MODEL)),
    }


def reference(x, meta_tokens, t5_bias, g_pre_mix, w_in, b_in, w_gla_a2, b_gla_a, g_gla_head,
              w_br_gla, g_dsa_q, w_dsa_uq, w_idx_q, g_dsa_kv, w_dsa_uv, idx_ln_g, idx_ln_b,
              w_br_dsa, ml_conv, ml_f_bias, g_ml_head, w_br_ml, w_out, g_post_mix, g_pre_mlp,
              w_up, w_down, g_post_mlp):
    B = x.shape[0]
    meta = jnp.broadcast_to(meta_tokens.astype(x.dtype)[None], (B, N_META, x.shape[-1]))
    h = jnp.concatenate([meta, x], axis=1)
    for l in range(DEPTH):
        p = {
            "g_pre_mix": g_pre_mix[l], "w_in": w_in[l], "b_in": b_in[l],
            "w_gla_a2": w_gla_a2[l], "b_gla_a": b_gla_a[l], "g_gla_head": g_gla_head[l],
            "w_br_gla": w_br_gla[l],
            "g_dsa_q": g_dsa_q[l], "w_dsa_uq": w_dsa_uq[l], "w_idx_q": w_idx_q[l],
            "g_dsa_kv": g_dsa_kv[l], "w_dsa_uv": w_dsa_uv[l], "idx_ln_g": idx_ln_g[l],
            "idx_ln_b": idx_ln_b[l], "w_br_dsa": w_br_dsa[l],
            "ml_conv": ml_conv[l], "ml_f_bias": ml_f_bias[l], "g_ml_head": g_ml_head[l],
            "w_br_ml": w_br_ml[l],
            "w_out": w_out[l], "g_post_mix": g_post_mix[l], "g_pre_mlp": g_pre_mlp[l],
            "w_up": w_up[l], "w_down": w_down[l], "g_post_mlp": g_post_mlp[l],
        }
        h = hybrid_layer(h, p, t5_bias)
    return h[:, N_META:]
```

```python
import functools
import math

import numpy as np
import jax
import jax.numpy as jnp
from jax import lax
from jax.experimental import pallas as pl
from jax.experimental.pallas import tpu as pltpu

D_MODEL = 1024
N_META = 16
GLA_H, GLA_DK, GLA_DV, GLA_LR, GLA_TAU = 4, 128, 256, 16, 16.0
DSA_H, DSA_DH, DSA_Q_LORA, DSA_KV_LAT = 16, 64, 256, 128
IDX_H, IDX_D, TOPK_MAX = 8, 64, 256
ML_H, ML_DK, ML_DV, ML_CONV = 4, 128, 256, 4
T5_BUCKETS, T5_MAX_DIST = 32, 128
D_FF = 4 * D_MODEL
EPS = 1e-6
NEG = -1e30

LANES = 128
SUBLANES = 8
VMEM_LIMIT = 56 * 1024 * 1024

MXU_DT = jnp.bfloat16
ACT_DT = jnp.bfloat16

F32 = jnp.float32
I32 = jnp.int32
INT_MIN = -2147483648
HI = lax.Precision.HIGHEST
NT_DIMS = (((1,), (1,)), ((), ()))

Z_GLA_Q, Z_GLA_K, Z_GLA_V, Z_GLA_R = 0, 512, 1024, 2048
Z_ML_Q, Z_ML_K, Z_ML_V, Z_ML_O = 3072, 3584, 4096, 5120
Z_GATE_A, Z_GATE_B, Z_GATE_C = 6144, 7168, 8192
Z_CQ, Z_CKV = 9216, 9472
Z_SM0, Z_SM1, Z_SM2 = 9600, 9728, 9856
NZ = 9984
NZ_TILE = 768


def _params(sem):
    return pltpu.CompilerParams(dimension_semantics=sem, vmem_limit_bytes=VMEM_LIMIT)


def _mm(a, b):
    return jnp.dot(a.astype(MXU_DT), b.astype(MXU_DT), preferred_element_type=F32)


def _mm_nt(a, b):
    return lax.dot_general(a.astype(MXU_DT), b.astype(MXU_DT), NT_DIMS, preferred_element_type=F32)


def _rms(x, g):
    return x * lax.rsqrt(jnp.mean(x * x, axis=-1, keepdims=True) + EPS) * g


def _log_sigmoid(x):
    return jnp.minimum(x, 0.0) - jnp.log1p(jnp.exp(-jnp.abs(x)))


def _sigmoid(x):
    return 1.0 / (1.0 + jnp.exp(-x))


def _norm_matmul_kernel(x_ref, g_ref, w_ref, b_ref, o_ref, xn_ref, *, relu2):
    @pl.when(pl.program_id(1) == 0)
    def _():
        xn_ref[...] = _rms(x_ref[...], g_ref[...]).astype(xn_ref.dtype)

    acc = jnp.dot(xn_ref[...], w_ref[...], preferred_element_type=F32) + b_ref[...]
    if relu2:
        acc = jnp.square(jnp.maximum(acc, 0.0))
    o_ref[...] = acc.astype(o_ref.dtype)


def _norm_matmul(x, g, w, b, *, tm, tn, relu2, out_dtype):
    R, K = x.shape
    N = w.shape[1]
    return pl.pallas_call(
        functools.partial(_norm_matmul_kernel, relu2=relu2),
        grid=(R // tm, N // tn),
        in_specs=[pl.BlockSpec((tm, K), lambda i, j: (i, 0)),
                  pl.BlockSpec((1, K), lambda i, j: (0, 0)),
                  pl.BlockSpec((K, tn), lambda i, j: (0, j)),
                  pl.BlockSpec((1, tn), lambda i, j: (0, j))],
        out_specs=pl.BlockSpec((tm, tn), lambda i, j: (i, j)),
        out_shape=jax.ShapeDtypeStruct((R, N), out_dtype),
        scratch_shapes=[pltpu.VMEM((tm, K), MXU_DT)],
        compiler_params=_params(("parallel", "arbitrary")),
        name="norm_matmul_relu2" if relu2 else "norm_matmul",
    )(x, g, w, b)


def _matmul_norm_res_kernel(u_ref, w_ref, g_ref, h_ref, o_ref):
    y = jnp.dot(u_ref[...], w_ref[...], preferred_element_type=F32)
    o_ref[...] = h_ref[...] + _rms(y, g_ref[...])


def _matmul_norm_res(u, w, g, h, *, tm):
    R, K = u.shape
    N = w.shape[1]
    return pl.pallas_call(
        _matmul_norm_res_kernel,
        grid=(R // tm,),
        in_specs=[pl.BlockSpec((tm, K), lambda i: (i, 0)),
                  pl.BlockSpec((K, N), lambda i: (0, 0)),
                  pl.BlockSpec((1, N), lambda i: (0, 0)),
                  pl.BlockSpec((tm, N), lambda i: (i, 0))],
        out_specs=pl.BlockSpec((tm, N), lambda i: (i, 0)),
        out_shape=jax.ShapeDtypeStruct((R, N), F32),
        compiler_params=_params(("parallel",)),
        name="mlp_down",
    )(u, w, g, h)


def _mix_kernel(oa_ref, ob_ref, oc_ref, ga_ref, gb_ref, gc_ref, wa_ref, wb_ref, wc_ref,
                wo_ref, g_ref, h_ref, o_ref):
    ya = jnp.dot(oa_ref[...], wa_ref[...], preferred_element_type=F32)
    yb = jnp.dot(ob_ref[...], wb_ref[...], preferred_element_type=F32)
    yc = jnp.dot(oc_ref[...], wc_ref[...], preferred_element_type=F32)
    mix = _sigmoid(ga_ref[...]) * ya + _sigmoid(gb_ref[...]) * yb + _sigmoid(gc_ref[...]) * yc
    y = _mm(mix, wo_ref[...])
    o_ref[...] = h_ref[...] + _rms(y, g_ref[...])


def _mix(oa, ob, oc, z, wa, wb, wc, wo, g, h, *, tm):
    R, D = h.shape
    row = lambda i: (i, 0)
    const = lambda i: (0, 0)
    gate = lambda off: pl.BlockSpec((tm, D), lambda i: (i, off // D))
    return pl.pallas_call(
        _mix_kernel,
        grid=(R // tm,),
        in_specs=[pl.BlockSpec((tm, D), row), pl.BlockSpec((tm, D), row), pl.BlockSpec((tm, D), row),
                  gate(Z_GATE_A), gate(Z_GATE_B), gate(Z_GATE_C),
                  pl.BlockSpec((D, D), const), pl.BlockSpec((D, D), const), pl.BlockSpec((D, D), const),
                  pl.BlockSpec((D, D), const), pl.BlockSpec((1, D), const), pl.BlockSpec((tm, D), row)],
        out_specs=pl.BlockSpec((tm, D), row),
        out_shape=jax.ShapeDtypeStruct((R, D), F32),
        compiler_params=_params(("parallel",)),
        name="mix_out",
    )(oa, ob, oc, z, z, z, wa, wb, wc, wo, g, h)


def _gla_kernel(q_ref, k_ref, v_ref, r_ref, a_ref, wa_ref, ba_ref, gh_ref, o_ref, s_ref, *, front, nchunks):
    C = LANES
    s_ref[...] = jnp.zeros_like(s_ref)
    r0 = lax.broadcasted_iota(I32, (C, C), 0)
    r1 = lax.broadcasted_iota(I32, (C, C), 1)
    causal = r0 >= r1
    tril = causal.astype(F32)
    rowi = lax.broadcasted_iota(I32, (C, 1), 0)

    def body(c, carry):
        sl = pl.ds(pl.multiple_of(c * C, C), C)
        live = (c * C + rowi) >= front
        q = q_ref[0, sl, :] * (GLA_DK ** -0.5)
        k = jnp.where(live, k_ref[0, sl, :], 0.0)
        v = jnp.where(live, v_ref[0, sl, :], 0.0)
        g = _log_sigmoid(_mm(a_ref[0, sl, :], wa_ref[0]) + ba_ref[0]) / GLA_TAU
        G = jnp.dot(tril, g, precision=HI, preferred_element_type=F32)
        g_last = G[C - 1:C, :]
        g_mid = G[C // 2 - 1:C // 2, :]
        A = _mm_nt(q * jnp.exp(G - g_mid), k * jnp.exp(g_mid - G))
        A = jnp.where(causal, A, 0.0)
        S = s_ref[...]
        o = _mm(q * jnp.exp(G), S) + _mm(A, v)
        kd_t = jnp.transpose(k * jnp.exp(g_last - G))
        dec_col = jnp.transpose(jnp.broadcast_to(jnp.exp(g_last), (C, GLA_DK)))
        s_ref[...] = jnp.concatenate([dec_col, dec_col], axis=1) * S + _mm(kd_t, v)
        r = r_ref[0, sl, :]
        o = _rms(o, gh_ref[...]) * (r * _sigmoid(r))
        o_ref[0, sl, :] = o.astype(o_ref.dtype)
        return carry

    lax.fori_loop(0, nchunks, body, 0)


def _gla(z, wa, ba, gh, *, front):
    B, Tp, _ = z.shape
    nchunks = Tp // LANES
    col = lambda off, w: (lambda b, h: (b, 0, off // w + h))
    return pl.pallas_call(
        functools.partial(_gla_kernel, front=front, nchunks=nchunks),
        grid=(B, GLA_H),
        in_specs=[pl.BlockSpec((1, Tp, GLA_DK), col(Z_GLA_Q, GLA_DK)),
                  pl.BlockSpec((1, Tp, GLA_DK), col(Z_GLA_K, GLA_DK)),
                  pl.BlockSpec((1, Tp, GLA_DV), col(Z_GLA_V, GLA_DV)),
                  pl.BlockSpec((1, Tp, GLA_DV), col(Z_GLA_R, GLA_DV)),
                  pl.BlockSpec((1, Tp, LANES), lambda b, h: (b, 0, Z_SM0 // LANES)),
                  pl.BlockSpec((1, LANES, GLA_DK), lambda b, h: (h, 0, 0)),
                  pl.BlockSpec((1, 1, GLA_DK), lambda b, h: (h, 0, 0)),
                  pl.BlockSpec((1, GLA_DV), lambda b, h: (0, 0))],
        out_specs=pl.BlockSpec((1, Tp, GLA_DV), lambda b, h: (b, 0, h)),
        out_shape=jax.ShapeDtypeStruct((B, Tp, GLA_H * GLA_DV), ACT_DT),
        scratch_shapes=[pltpu.VMEM((GLA_DK, GLA_DV), F32)],
        compiler_params=_params(("parallel", "parallel")),
        name="gla",
    )(z, z, z, z, z, wa, ba, gh)


def _mlstm_kernel(q_ref, k_ref, v_ref, og_ref, gcol_ref, grow_ref, cq_ref, ck_ref, fbrow_ref, fb_ref,
                  gh_ref, o_ref, s_ref, *, front, nchunks):
    C = LANES
    h = pl.program_id(1)
    s_ref[...] = jnp.zeros_like(s_ref)
    r0 = lax.broadcasted_iota(I32, (C, C), 0)
    r1 = lax.broadcasted_iota(I32, (C, C), 1)
    causal = r0 >= r1
    tril = causal.astype(F32)
    triu = (r0 <= r1).astype(F32)
    pick_f = (r0 == h + ML_H).astype(F32)
    rowi = lax.broadcasted_iota(I32, (C, 1), 0)
    fb = fb_ref[h]

    def conv_silu(x_ref, w_ref, c, sl, slp):
        x = jnp.where((c * C + rowi) >= front, x_ref[0, sl, :], 0.0)
        xp = jnp.where(((c - 1) * C + rowi) >= front, x_ref[0, slp, :], 0.0)
        acc = x * w_ref[ML_CONV - 1:ML_CONV, :]
        for d in range(1, ML_CONV):
            sh = jnp.where(rowi >= d, pltpu.roll(x, d, 0), pltpu.roll(xp, d, 0))
            acc = acc + sh * w_ref[ML_CONV - 1 - d:ML_CONV - d, :]
        return acc * _sigmoid(acc)

    def body(c, m_prev):
        sl = pl.ds(pl.multiple_of(c * C, C), C)
        slp = pl.ds(pl.multiple_of(jnp.maximum(c - 1, 0) * C, C), C)
        q = conv_silu(q_ref, cq_ref, c, sl, slp) * (ML_DK ** -0.5)
        k = conv_silu(k_ref, ck_ref, c, sl, slp)
        live = (c * C + rowi) >= front
        v = jnp.where(live, v_ref[0, sl, :], 0.0)
        v_aug = jnp.concatenate([v, jnp.ones((C, LANES), F32)], axis=1)
        lf_col = _log_sigmoid(gcol_ref[0, sl, :] + fbrow_ref[...])
        b_bc = jnp.dot(tril, jnp.dot(lf_col, pick_f, precision=HI, preferred_element_type=F32),
                       precision=HI, preferred_element_type=F32)
        li_row = grow_ref[0, h, pl.ds(c, 1), :]
        lf_row = _log_sigmoid(grow_ref[0, ML_H + h, pl.ds(c, 1), :] + fb)
        b_row = jnp.dot(lf_row, triu, precision=HI, preferred_element_type=F32)
        b_last = b_row[:, C - 1:C]
        Dm = jnp.where(causal, b_bc - (b_row - li_row), -jnp.inf)
        m_inter = b_bc + m_prev
        m = jnp.maximum(m_inter, jnp.max(Dm, axis=1, keepdims=True))
        w_inter = jnp.exp(m_inter - m)
        P = jnp.exp(Dm - m) * _mm_nt(q, k)
        S = s_ref[...]
        qs = _mm(q, S)
        w3 = jnp.concatenate([w_inter, w_inter, w_inter], axis=1)
        num = w3 * qs + _mm(P, v_aug)
        den = num[:, ML_DV:]
        dd = jnp.maximum(jnp.abs(den), jnp.exp(-m))
        hval = num[:, :ML_DV] / jnp.concatenate([dd, dd], axis=1)
        dec = b_last - b_row + li_row
        m_new = jnp.maximum(b_last + m_prev, jnp.max(dec, axis=1, keepdims=True))
        w_old = jnp.exp(b_last + m_prev - m_new)
        w_new = jnp.exp(dec - m_new)
        kt_w = jnp.transpose(k) * w_new
        s_ref[...] = jnp.concatenate([w_old, w_old, w_old], axis=1) * S + _mm(kt_w, v_aug)
        og = og_ref[0, sl, :]
        o_ref[0, sl, :] = (_rms(hval, gh_ref[...]) * _sigmoid(og)).astype(o_ref.dtype)
        return m_new

    lax.fori_loop(0, nchunks, body, jnp.zeros((1, C), F32))


def _mlstm(z, gates_row, conv_w, fb_row, fb, gh, *, front):
    B, Tp, _ = z.shape
    nchunks = Tp // LANES
    col = lambda off, w: (lambda b, h: (b, 0, off // w + h))
    return pl.pallas_call(
        functools.partial(_mlstm_kernel, front=front, nchunks=nchunks),
        grid=(B, ML_H),
        in_specs=[pl.BlockSpec((1, Tp, ML_DK), col(Z_ML_Q, ML_DK)),
                  pl.BlockSpec((1, Tp, ML_DK), col(Z_ML_K, ML_DK)),
                  pl.BlockSpec((1, Tp, ML_DV), col(Z_ML_V, ML_DV)),
                  pl.BlockSpec((1, Tp, ML_DV), col(Z_ML_O, ML_DV)),
                  pl.BlockSpec((1, Tp, LANES), lambda b, h: (b, 0, Z_SM2 // LANES)),
                  pl.BlockSpec((1, 2 * ML_H, nchunks, LANES), lambda b, h: (b, 0, 0, 0)),
                  pl.BlockSpec((ML_CONV, ML_DK), lambda b, h: (0, h)),
                  pl.BlockSpec((ML_CONV, ML_DK), lambda b, h: (0, ML_H + h)),
                  pl.BlockSpec((1, LANES), lambda b, h: (0, 0)),
                  pl.BlockSpec(memory_space=pltpu.SMEM),
                  pl.BlockSpec((1, ML_DV), lambda b, h: (0, 0))],
        out_specs=pl.BlockSpec((1, Tp, ML_DV), lambda b, h: (b, 0, h)),
        out_shape=jax.ShapeDtypeStruct((B, Tp, ML_H * ML_DV), ACT_DT),
        scratch_shapes=[pltpu.VMEM((ML_DK, ML_DV + LANES), F32)],
        compiler_params=_params(("parallel", "parallel")),
        name="mlstm",
    )(z, z, z, z, z, gates_row, conv_w, conv_w, fb_row, fb, gh)


def _dsa_prep_kernel(cq_ref, ckv_ref, sm_ref, gq_ref, gkv_ref, lng_ref, lnb_ref, wuq_ref, wiq_ref,
                     q_ref, qi_ref, kv_ref, ki_ref):
    cq = _rms(cq_ref[0], gq_ref[...]).astype(MXU_DT)
    qf = jnp.dot(cq, wuq_ref[...], preferred_element_type=F32) * (DSA_KV_LAT ** -0.5)
    for hh in range(DSA_H):
        q_ref[0, hh] = qf[:, hh * DSA_KV_LAT:(hh + 1) * DSA_KV_LAT].astype(q_ref.dtype)
    for hh in range(IDX_H):
        qi = jnp.dot(cq, wiq_ref[hh], preferred_element_type=F32) * (IDX_D ** -0.5)
        qi_ref[0, hh] = qi.astype(qi_ref.dtype)
    kv_ref[0] = _rms(ckv_ref[0], gkv_ref[...]).astype(kv_ref.dtype)
    x = sm_ref[0][:, :IDX_D]
    mu = jnp.mean(x, axis=-1, keepdims=True)
    var = jnp.mean(jnp.square(x - mu), axis=-1, keepdims=True)
    ki_ref[0] = ((x - mu) * lax.rsqrt(var + EPS) * lng_ref[...] + lnb_ref[...]).astype(ki_ref.dtype)


def _dsa_prep(z, gq, gkv, lng, lnb, wuq, wiq):
    B, Tp, _ = z.shape
    tm = LANES
    c2 = lambda b, i: (0, 0)
    return pl.pallas_call(
        _dsa_prep_kernel,
        grid=(B, Tp // tm),
        in_specs=[pl.BlockSpec((1, tm, DSA_Q_LORA), lambda b, i: (b, i, Z_CQ // DSA_Q_LORA)),
                  pl.BlockSpec((1, tm, DSA_KV_LAT), lambda b, i: (b, i, Z_CKV // DSA_KV_LAT)),
                  pl.BlockSpec((1, tm, LANES), lambda b, i: (b, i, Z_SM1 // LANES)),
                  pl.BlockSpec((1, DSA_Q_LORA), c2), pl.BlockSpec((1, DSA_KV_LAT), c2),
                  pl.BlockSpec((1, IDX_D), c2), pl.BlockSpec((1, IDX_D), c2),
                  pl.BlockSpec((DSA_Q_LORA, DSA_H * DSA_KV_LAT), c2),
                  pl.BlockSpec((IDX_H, DSA_Q_LORA, IDX_D), lambda b, i: (0, 0, 0))],
        out_specs=[pl.BlockSpec((1, DSA_H, tm, DSA_KV_LAT), lambda b, i: (b, 0, i, 0)),
                   pl.BlockSpec((1, IDX_H, tm, IDX_D), lambda b, i: (b, 0, i, 0)),
                   pl.BlockSpec((1, tm, DSA_KV_LAT), lambda b, i: (b, i, 0)),
                   pl.BlockSpec((1, tm, IDX_D), lambda b, i: (b, i, 0))],
        out_shape=[jax.ShapeDtypeStruct((B, DSA_H, Tp, DSA_KV_LAT), ACT_DT),
                   jax.ShapeDtypeStruct((B, IDX_H, Tp, IDX_D), ACT_DT),
                   jax.ShapeDtypeStruct((B, Tp, DSA_KV_LAT), ACT_DT),
                   jax.ShapeDtypeStruct((B, Tp, IDX_D), ACT_DT)],
        compiler_params=_params(("parallel", "parallel")),
        name="dsa_prep",
    )(z, z, z, gq, gkv, lng, lnb, wuq, wiq)


def _sortable(x):
    bits = pltpu.bitcast(x, I32)
    key = jnp.where(bits < 0, bits ^ 0x7FFFFFFF, bits)
    return jnp.where(x == 0.0, 0, key)


_KEY_NEG = int(np.array(NEG, np.float32).view(np.int32)) ^ 0x7FFFFFFF


def _dsa_kernel(ki_ref, kv_ref, kvt_ref, qi_ref, q_ref, wi_ref, bias_ref, wuv_ref, o_ref,
                key_ref, madd_ref, lg_ref, *, front, topk, Tp):
    QB = LANES
    j = pl.program_id(1)
    nkb = j + 1
    s_loc = lax.broadcasted_iota(I32, (QB, QB), 0)
    t_glob = j * QB + lax.broadcasted_iota(I32, (QB, QB), 1)
    wi = wi_ref[0] * (IDX_H ** -0.5)
    n_rest = Tp - nkb * QB

    def blk(kb):
        return pl.ds(pl.multiple_of(kb * QB, QB), QB)

    def score_body(kb, c):
        ki = ki_ref[0, blk(kb), :]
        acc = jnp.zeros((QB, QB), F32)
        for hh in range(IDX_H):
            s = lax.dot_general(ki, qi_ref[0, hh], NT_DIMS, preferred_element_type=F32)
            acc = acc + jnp.maximum(s, 0.0) * wi[hh:hh + 1, :]
        s_glob = kb * QB + s_loc
        key = _sortable(jnp.where(s_glob <= t_glob, acc, NEG))
        key_ref[blk(kb), :] = jnp.where(s_glob >= front, key, INT_MIN)
        return c

    lax.fori_loop(0, nkb, score_body, 0)

    def count(pred_fn):
        def body(kb, acc):
            hit = pred_fn(key_ref[blk(kb), :], kb).astype(I32)
            return acc + jnp.sum(hit.reshape(QB // SUBLANES, SUBLANES, QB), axis=0)
        acc = lax.fori_loop(0, nkb, body, jnp.zeros((SUBLANES, QB), I32))
        return jnp.sum(acc, axis=0, keepdims=True)

    def count_ge(cand):
        return count(lambda kblk, kb: kblk >= cand) + jnp.where(_KEY_NEG >= cand, n_rest, 0)

    base = jnp.where(count_ge(jnp.zeros((1, QB), I32)) >= topk, 0, INT_MIN).astype(I32)

    def bit_body(i, base):
        cand = base | jnp.left_shift(jnp.int32(1), 30 - i)
        return jnp.where(count_ge(cand) >= topk, cand, base)

    tau = lax.fori_loop(0, 31, bit_body, base)

    def valid_of(kb):
        s_glob = kb * QB + s_loc
        return (s_glob >= front) & (s_glob <= t_glob)

    c_gt = count(lambda kblk, kb: kblk > tau) + jnp.where(_KEY_NEG > tau, n_rest, 0)
    c_eq = count(lambda kblk, kb: (kblk == tau) & valid_of(kb))
    need = topk - c_gt

    @pl.when(jnp.max((c_eq > need).astype(I32)) > 0)
    def _():
        lstrict = (s_loc > lax.broadcasted_iota(I32, (QB, QB), 1)).astype(MXU_DT)
        needf = need.astype(F32)

        def tie_body(kb, seen):
            kblk = key_ref[blk(kb), :]
            tie = (kblk == tau) & valid_of(kb)
            tief = tie.astype(F32)
            rank = seen + jnp.dot(lstrict, tief.astype(MXU_DT), preferred_element_type=F32)
            key_ref[blk(kb), :] = jnp.where(tie & (rank >= needf), INT_MIN, kblk)
            return seen + jnp.sum(tief, axis=0, keepdims=True)

        lax.fori_loop(0, nkb, tie_body, jnp.zeros((1, QB), F32))

    def mask_body(kb, c):
        sel = (key_ref[blk(kb), :] >= tau) & valid_of(kb)
        madd_ref[blk(kb), :] = jnp.where(sel, 0.0, NEG)
        return c

    lax.fori_loop(0, nkb, mask_body, 0)

    for p in range(DSA_H // 2):
        q2 = q_ref[0, 2 * p:2 * p + 2].reshape(2 * QB, DSA_KV_LAT)

        def logit_body(kb, mx):
            lg = lax.dot_general(kv_ref[0, blk(kb), :], q2, NT_DIMS, preferred_element_type=F32)
            ma = madd_ref[blk(kb), :]
            lg = lg + bias_ref[p, jnp.minimum(j - kb, 2)] + jnp.concatenate([ma, ma], axis=1)
            lg_ref[blk(kb), :] = lg
            return jnp.maximum(mx, jnp.max(lg.reshape(QB // SUBLANES, SUBLANES, 2 * QB), axis=0))

        mx = lax.fori_loop(0, nkb, logit_body, jnp.full((SUBLANES, 2 * QB), -jnp.inf, F32))
        mx = jnp.max(mx, axis=0, keepdims=True)

        def pv_body(kb, carry):
            lsum, acc = carry
            pm = jnp.exp(lg_ref[blk(kb), :] - mx)
            lsum = lsum + jnp.sum(pm.reshape(QB // SUBLANES, SUBLANES, 2 * QB), axis=0)
            acc = acc + jnp.dot(kvt_ref[0, kb], pm.astype(MXU_DT), preferred_element_type=F32)
            return lsum, acc

        lsum, acc = lax.fori_loop(0, nkb, pv_body, (jnp.zeros((SUBLANES, 2 * QB), F32),
                                                    jnp.zeros((DSA_KV_LAT, 2 * QB), F32)))
        o_t = acc / jnp.sum(lsum, axis=0, keepdims=True)
        outs = []
        for hh in range(2):
            o_h = jnp.transpose(o_t[:, hh * QB:(hh + 1) * QB])
            outs.append(_mm(o_h, wuv_ref[2 * p + hh]))
        o_ref[0, :, 2 * p * DSA_DH:(2 * p + 2) * DSA_DH] = jnp.concatenate(outs, axis=1).astype(o_ref.dtype)


def _dsa(ki, kv, kvt, qi, q, wi_t, bias_tiles, wuv, *, front, topk):
    B, Tp, _ = kv.shape
    nblk = Tp // LANES
    return pl.pallas_call(
        functools.partial(_dsa_kernel, front=front, topk=topk, Tp=Tp),
        grid=(B, nblk),
        in_specs=[pl.BlockSpec((1, Tp, IDX_D), lambda b, j: (b, 0, 0)),
                  pl.BlockSpec((1, Tp, DSA_KV_LAT), lambda b, j: (b, 0, 0)),
                  pl.BlockSpec((1, nblk, DSA_KV_LAT, LANES), lambda b, j: (b, 0, 0, 0)),
                  pl.BlockSpec((1, IDX_H, LANES, IDX_D), lambda b, j: (b, 0, j, 0)),
                  pl.BlockSpec((1, DSA_H, LANES, DSA_KV_LAT), lambda b, j: (b, 0, j, 0)),
                  pl.BlockSpec((1, IDX_H, LANES), lambda b, j: (b, 0, j)),
                  pl.BlockSpec((DSA_H // 2, 3, LANES, 2 * LANES), lambda b, j: (0, 0, 0, 0)),
                  pl.BlockSpec((DSA_H, DSA_KV_LAT, DSA_DH), lambda b, j: (0, 0, 0))],
        out_specs=pl.BlockSpec((1, LANES, DSA_H * DSA_DH), lambda b, j: (b, j, 0)),
        out_shape=jax.ShapeDtypeStruct((B, Tp, DSA_H * DSA_DH), ACT_DT),
        scratch_shapes=[pltpu.VMEM((Tp, LANES), I32), pltpu.VMEM((Tp, LANES), F32),
                        pltpu.VMEM((Tp, 2 * LANES), F32)],
        compiler_params=_params(("parallel", "arbitrary")),
        name="dsa_attn",
    )(ki, kv, kvt, qi, q, wi_t, bias_tiles, wuv)


def _t5_bucket(rel):
    n = jnp.maximum(rel, 0)
    max_exact = T5_BUCKETS // 2
    nf = jnp.maximum(n, 1).astype(F32)
    large = max_exact + (jnp.log(nf / max_exact) / math.log(T5_MAX_DIST / max_exact)
                         * (T5_BUCKETS - max_exact)).astype(I32)
    large = jnp.minimum(large, T5_BUCKETS - 1)
    return jnp.where(n < max_exact, n, large)


def _t5_tiles(t5_bias):
    s = jnp.arange(LANES, dtype=I32)[:, None]
    t = jnp.arange(LANES, dtype=I32)[None, :]
    tiles = []
    for shift in (0, LANES, 2 * LANES):
        tiles.append(t5_bias[_t5_bucket(shift + t - s)])
    tiles = jnp.stack(tiles, axis=0)
    tiles = tiles.transpose(3, 0, 1, 2).reshape(DSA_H // 2, 2, 3, LANES, LANES)
    return tiles.transpose(0, 2, 3, 1, 4).reshape(DSA_H // 2, 3, LANES, 2 * LANES).astype(F32)


def _pack_in_proj(w_in, b_in):
    splits = np.cumsum([GLA_H * GLA_DK, GLA_H * GLA_DK, GLA_H * GLA_DV, GLA_H * GLA_DV, GLA_LR,
                        DSA_Q_LORA, DSA_KV_LAT, IDX_D, IDX_H,
                        ML_H * ML_DK, ML_H * ML_DK, ML_H * ML_DV, ML_H * ML_DV, ML_H, ML_H,
                        D_MODEL, D_MODEL])[:].tolist()

    def pack(a):
        (gq, gk, gv, gr, ga, cq, ckv, ik, iw, mq, mk, mv, mo, mi, mf, a_, b_, c_) = jnp.split(a, splits, axis=-1)
        pad = lambda n: jnp.zeros(a.shape[:-1] + (n,), a.dtype)
        return jnp.concatenate([gq, gk, gv, gr, mq, mk, mv, mo, a_, b_, c_, cq, ckv,
                                ga, pad(LANES - GLA_LR),
                                ik, iw, pad(LANES - IDX_D - IDX_H),
                                mi, mf, pad(LANES - 2 * ML_H)], axis=-1)

    return pack(w_in), pack(b_in)


def kernel(x, meta_tokens, t5_bias, g_pre_mix, w_in, b_in, w_gla_a2, b_gla_a, g_gla_head, w_br_gla, g_dsa_q, w_dsa_uq, w_idx_q, g_dsa_kv, w_dsa_uv, idx_ln_g, idx_ln_b, w_br_dsa, ml_conv, ml_f_bias, g_ml_head, w_br_ml, w_out, g_post_mix, g_pre_mlp, w_up, w_down, g_post_mlp):
    B, S, D = x.shape
    L = w_in.shape[0]
    T = S + N_META
    front = (-T) % LANES
    Tp = T + front
    R = B * Tp
    nblk = Tp // LANES
    topk = min(TOPK_MAX, (T - N_META) // 4)
    tm_big = 1024 if R % 1024 == 0 else (512 if R % 512 == 0 else LANES)
    tm_mid = 512 if R % 512 == 0 else LANES
    tm_small = 256 if R % 256 == 0 else LANES

    w_in_p, b_in_p = _pack_in_proj(w_in, b_in)
    wa2 = jnp.pad(w_gla_a2, ((0, 0), (0, LANES - GLA_LR), (0, 0)))
    wa2 = wa2.reshape(L, LANES, GLA_H, GLA_DK).transpose(0, 2, 1, 3).astype(MXU_DT)
    fb_row = jnp.pad(ml_f_bias, ((0, 0), (ML_H, LANES - 2 * ML_H)))[:, None, :]
    layers = dict(
        g_pre_mix=g_pre_mix[:, None, :], w_in=w_in_p.astype(MXU_DT), b_in=b_in_p[:, None, :],
        wa2=wa2, ba=b_gla_a.reshape(L, GLA_H, 1, GLA_DK), g_gla_head=g_gla_head[:, None, :],
        w_br_gla=w_br_gla.astype(MXU_DT),
        g_dsa_q=g_dsa_q[:, None, :], w_dsa_uq=w_dsa_uq.astype(MXU_DT),
        w_idx_q=w_idx_q.reshape(L, DSA_Q_LORA, IDX_H, IDX_D).transpose(0, 2, 1, 3).astype(MXU_DT),
        g_dsa_kv=g_dsa_kv[:, None, :], w_dsa_uv=w_dsa_uv.astype(MXU_DT),
        idx_ln_g=idx_ln_g[:, None, :], idx_ln_b=idx_ln_b[:, None, :], w_br_dsa=w_br_dsa.astype(MXU_DT),
        ml_conv=ml_conv, fb_row=fb_row, ml_f_bias=ml_f_bias, g_ml_head=g_ml_head[:, None, :],
        w_br_ml=w_br_ml.astype(MXU_DT), w_out=w_out.astype(MXU_DT), g_post_mix=g_post_mix[:, None, :],
        g_pre_mlp=g_pre_mlp[:, None, :], w_up=w_up.astype(MXU_DT), w_down=w_down.astype(MXU_DT),
        g_post_mlp=g_post_mlp[:, None, :],
    )
    bias_tiles = _t5_tiles(t5_bias)
    zero_ff = jnp.zeros((1, D_FF), F32)

    meta = jnp.broadcast_to(meta_tokens.astype(x.dtype)[None], (B, N_META, D))
    h0 = jnp.concatenate([jnp.zeros((B, front, D), x.dtype), meta, x], axis=1).reshape(R, D)

    def layer(h, p):
        z = _norm_matmul(h, p["g_pre_mix"], p["w_in"], p["b_in"], tm=tm_big, tn=NZ_TILE,
                         relu2=False, out_dtype=F32)
        z3 = z.reshape(B, Tp, NZ)
        o_gla = _gla(z3, p["wa2"], p["ba"], p["g_gla_head"], front=front)
        gates_row = z3[:, :, Z_SM2:Z_SM2 + 2 * ML_H].transpose(0, 2, 1).reshape(B, 2 * ML_H, nblk, LANES)
        o_ml = _mlstm(z3, gates_row, p["ml_conv"], p["fb_row"], p["ml_f_bias"], p["g_ml_head"], front=front)
        q, qi, kv, ki = _dsa_prep(z3, p["g_dsa_q"], p["g_dsa_kv"], p["idx_ln_g"], p["idx_ln_b"],
                                  p["w_dsa_uq"], p["w_idx_q"])
        kvt = kv.reshape(B, nblk, LANES, DSA_KV_LAT).transpose(0, 1, 3, 2)
        wi_t = z3[:, :, Z_SM1 + IDX_D:Z_SM1 + IDX_D + IDX_H].transpose(0, 2, 1)
        o_dsa = _dsa(ki, kv, kvt, qi, q, wi_t, bias_tiles, p["w_dsa_uv"], front=front, topk=topk)
        h = _mix(o_gla.reshape(R, D), o_dsa.reshape(R, D), o_ml.reshape(R, D), z,
                 p["w_br_gla"], p["w_br_dsa"], p["w_br_ml"], p["w_out"], p["g_post_mix"], h, tm=tm_small)
        u = _norm_matmul(h, p["g_pre_mlp"], p["w_up"], zero_ff, tm=tm_mid, tn=1024,
                         relu2=True, out_dtype=ACT_DT)
        h = _matmul_norm_res(u, p["w_down"], p["g_post_mlp"], h, tm=tm_mid)
        return h, None

    h, _ = lax.scan(layer, h0, layers)
    return h.reshape(B, Tp, D)[:, front + N_META:]
```

```python
import functools
import math

import numpy as np
import jax
import jax.numpy as jnp
from jax import lax
from jax.experimental import pallas as pl
from jax.experimental.pallas import tpu as pltpu

D_MODEL = 1024
N_META = 16
GLA_H, GLA_DK, GLA_DV, GLA_LR, GLA_TAU = 4, 128, 256, 16, 16.0
DSA_H, DSA_DH, DSA_Q_LORA, DSA_KV_LAT = 16, 64, 256, 128
IDX_H, IDX_D, TOPK_MAX = 8, 64, 256
ML_H, ML_DK, ML_DV, ML_CONV = 4, 128, 256, 4
T5_BUCKETS, T5_MAX_DIST = 32, 128
D_FF = 4 * D_MODEL
EPS = 1e-6
NEG = -1e30

LANES = 128
SUBLANES = 8
VMEM_LIMIT = 56 * 1024 * 1024

MXU_DT = jnp.bfloat16
ACT_DT = jnp.bfloat16

F32 = jnp.float32
I32 = jnp.int32
INT_MIN = -2147483648
HI = lax.Precision.HIGHEST
NT_DIMS = (((1,), (1,)), ((), ()))

Z_GLA_Q, Z_GLA_K, Z_GLA_V, Z_GLA_R = 0, 512, 1024, 2048
Z_ML_Q, Z_ML_K, Z_ML_V, Z_ML_O = 3072, 3584, 4096, 5120
Z_GATE_A, Z_GATE_B, Z_GATE_C = 6144, 7168, 8192
Z_CQ, Z_CKV = 9216, 9472
Z_SM0, Z_SM1, Z_SM2 = 9600, 9728, 9856
NZ = 9984
NZ_TILE = 768


def _params(sem):
    return pltpu.CompilerParams(dimension_semantics=sem, vmem_limit_bytes=VMEM_LIMIT)


def _mm(a, b):
    return jnp.dot(a.astype(MXU_DT), b.astype(MXU_DT), preferred_element_type=F32)


def _mm_nt(a, b):
    return lax.dot_general(a.astype(MXU_DT), b.astype(MXU_DT), NT_DIMS, preferred_element_type=F32)


def _rms(x, g):
    return x * lax.rsqrt(jnp.mean(x * x, axis=-1, keepdims=True) + EPS) * g


def _log_sigmoid(x):
    return jnp.minimum(x, 0.0) - jnp.log1p(jnp.exp(-jnp.abs(x)))


def _sigmoid(x):
    return 1.0 / (1.0 + jnp.exp(-x))


def _norm_matmul_kernel(x_ref, g_ref, w_ref, b_ref, o_ref, xn_ref, *, relu2):
    @pl.when(pl.program_id(1) == 0)
    def _():
        xn_ref[...] = _rms(x_ref[...], g_ref[...]).astype(xn_ref.dtype)

    acc = jnp.dot(xn_ref[...], w_ref[...], preferred_element_type=F32) + b_ref[...]
    if relu2:
        acc = jnp.square(jnp.maximum(acc, 0.0))
    o_ref[...] = acc.astype(o_ref.dtype)


def _norm_matmul(x, g, w, b, *, tm, tn, relu2, out_dtype):
    R, K = x.shape
    N = w.shape[1]
    return pl.pallas_call(
        functools.partial(_norm_matmul_kernel, relu2=relu2),
        grid=(R // tm, N // tn),
        in_specs=[pl.BlockSpec((tm, K), lambda i, j: (i, 0)),
                  pl.BlockSpec((1, K), lambda i, j: (0, 0)),
                  pl.BlockSpec((K, tn), lambda i, j: (0, j)),
                  pl.BlockSpec((1, tn), lambda i, j: (0, j))],
        out_specs=pl.BlockSpec((tm, tn), lambda i, j: (i, j)),
        out_shape=jax.ShapeDtypeStruct((R, N), out_dtype),
        scratch_shapes=[pltpu.VMEM((tm, K), MXU_DT)],
        compiler_params=_params(("parallel", "arbitrary")),
        name="norm_matmul_relu2" if relu2 else "norm_matmul",
    )(x, g, w, b)


def _matmul_norm_res_kernel(u_ref, w_ref, g_ref, h_ref, o_ref):
    y = jnp.dot(u_ref[...], w_ref[...], preferred_element_type=F32)
    o_ref[...] = h_ref[...] + _rms(y, g_ref[...])


def _matmul_norm_res(u, w, g, h, *, tm):
    R, K = u.shape
    N = w.shape[1]
    return pl.pallas_call(
        _matmul_norm_res_kernel,
        grid=(R // tm,),
        in_specs=[pl.BlockSpec((tm, K), lambda i: (i, 0)),
                  pl.BlockSpec((K, N), lambda i: (0, 0)),
                  pl.BlockSpec((1, N), lambda i: (0, 0)),
                  pl.BlockSpec((tm, N), lambda i: (i, 0))],
        out_specs=pl.BlockSpec((tm, N), lambda i: (i, 0)),
        out_shape=jax.ShapeDtypeStruct((R, N), F32),
        compiler_params=_params(("parallel",)),
        name="mlp_down",
    )(u, w, g, h)


def _mix_kernel(oa_ref, ob_ref, oc_ref, ga_ref, gb_ref, gc_ref, wa_ref, wb_ref, wc_ref,
                wo_ref, g_ref, h_ref, o_ref):
    ya = jnp.dot(oa_ref[...], wa_ref[...], preferred_element_type=F32)
    yb = jnp.dot(ob_ref[...], wb_ref[...], preferred_element_type=F32)
    yc = jnp.dot(oc_ref[...], wc_ref[...], preferred_element_type=F32)
    mix = _sigmoid(ga_ref[...]) * ya + _sigmoid(gb_ref[...]) * yb + _sigmoid(gc_ref[...]) * yc
    y = _mm(mix, wo_ref[...])
    o_ref[...] = h_ref[...] + _rms(y, g_ref[...])


def _mix(oa, ob, oc, z, wa, wb, wc, wo, g, h, *, tm):
    R, D = h.shape
    row = lambda i: (i, 0)
    const = lambda i: (0, 0)
    gate = lambda off: pl.BlockSpec((tm, D), lambda i: (i, off // D))
    return pl.pallas_call(
        _mix_kernel,
        grid=(R // tm,),
        in_specs=[pl.BlockSpec((tm, D), row), pl.BlockSpec((tm, D), row), pl.BlockSpec((tm, D), row),
                  gate(Z_GATE_A), gate(Z_GATE_B), gate(Z_GATE_C),
                  pl.BlockSpec((D, D), const), pl.BlockSpec((D, D), const), pl.BlockSpec((D, D), const),
                  pl.BlockSpec((D, D), const), pl.BlockSpec((1, D), const), pl.BlockSpec((tm, D), row)],
        out_specs=pl.BlockSpec((tm, D), row),
        out_shape=jax.ShapeDtypeStruct((R, D), F32),
        compiler_params=_params(("parallel",)),
        name="mix_out",
    )(oa, ob, oc, z, z, z, wa, wb, wc, wo, g, h)


def _gla_kernel(q_ref, k_ref, v_ref, r_ref, a_ref, wa_ref, ba_ref, gh_ref, o_ref, s_ref, *, front, nchunks):
    C = LANES
    s_ref[...] = jnp.zeros_like(s_ref)
    r0 = lax.broadcasted_iota(I32, (C, C), 0)
    r1 = lax.broadcasted_iota(I32, (C, C), 1)
    causal = r0 >= r1
    tril = causal.astype(F32)
    rowi = lax.broadcasted_iota(I32, (C, 1), 0)

    def body(c, carry):
        sl = pl.ds(pl.multiple_of(c * C, C), C)
        live = (c * C + rowi) >= front
        q = q_ref[0, sl, :] * (GLA_DK ** -0.5)
        k = jnp.where(live, k_ref[0, sl, :], 0.0)
        v = jnp.where(live, v_ref[0, sl, :], 0.0)
        g = _log_sigmoid(_mm(a_ref[0, sl, :], wa_ref[0]) + ba_ref[0]) / GLA_TAU
        G = jnp.dot(tril, g, precision=HI, preferred_element_type=F32)
        g_last = G[C - 1:C, :]
        g_mid = G[C // 2 - 1:C // 2, :]
        A = _mm_nt(q * jnp.exp(G - g_mid), k * jnp.exp(g_mid - G))
        A = jnp.where(causal, A, 0.0)
        S = s_ref[...]
        o = _mm(q * jnp.exp(G), S) + _mm(A, v)
        kd_t = jnp.transpose(k * jnp.exp(g_last - G))
        dec_col = jnp.transpose(jnp.broadcast_to(jnp.exp(g_last), (C, GLA_DK)))
        s_ref[...] = jnp.concatenate([dec_col, dec_col], axis=1) * S + _mm(kd_t, v)
        r = r_ref[0, sl, :]
        o = _rms(o, gh_ref[...]) * (r * _sigmoid(r))
        o_ref[0, sl, :] = o.astype(o_ref.dtype)
        return carry

    lax.fori_loop(0, nchunks, body, 0)


def _gla(z, wa, ba, gh, *, front):
    B, Tp, _ = z.shape
    nchunks = Tp // LANES
    col = lambda off, w: (lambda b, h: (b, 0, off // w + h))
    return pl.pallas_call(
        functools.partial(_gla_kernel, front=front, nchunks=nchunks),
        grid=(B, GLA_H),
        in_specs=[pl.BlockSpec((1, Tp, GLA_DK), col(Z_GLA_Q, GLA_DK)),
                  pl.BlockSpec((1, Tp, GLA_DK), col(Z_GLA_K, GLA_DK)),
                  pl.BlockSpec((1, Tp, GLA_DV), col(Z_GLA_V, GLA_DV)),
                  pl.BlockSpec((1, Tp, GLA_DV), col(Z_GLA_R, GLA_DV)),
                  pl.BlockSpec((1, Tp, LANES), lambda b, h: (b, 0, Z_SM0 // LANES)),
                  pl.BlockSpec((1, LANES, GLA_DK), lambda b, h: (h, 0, 0)),
                  pl.BlockSpec((1, 1, GLA_DK), lambda b, h: (h, 0, 0)),
                  pl.BlockSpec((1, GLA_DV), lambda b, h: (0, 0))],
        out_specs=pl.BlockSpec((1, Tp, GLA_DV), lambda b, h: (b, 0, h)),
        out_shape=jax.ShapeDtypeStruct((B, Tp, GLA_H * GLA_DV), ACT_DT),
        scratch_shapes=[pltpu.VMEM((GLA_DK, GLA_DV), F32)],
        compiler_params=_params(("parallel", "parallel")),
        name="gla",
    )(z, z, z, z, z, wa, ba, gh)


def _mlstm_kernel(q_ref, k_ref, v_ref, og_ref, gcol_ref, grow_ref, cq_ref, ck_ref, fbrow_ref, fb_ref,
                  gh_ref, o_ref, s_ref, *, front, nchunks):
    C = LANES
    h = pl.program_id(1)
    s_ref[...] = jnp.zeros_like(s_ref)
    r0 = lax.broadcasted_iota(I32, (C, C), 0)
    r1 = lax.broadcasted_iota(I32, (C, C), 1)
    causal = r0 >= r1
    tril = causal.astype(F32)
    triu = (r0 <= r1).astype(F32)
    pick_f = (r0 == h + ML_H).astype(F32)
    rowi = lax.broadcasted_iota(I32, (C, 1), 0)
    fb = fb_ref[h]

    def conv_silu(x_ref, w_ref, c, sl, slp):
        x = jnp.where((c * C + rowi) >= front, x_ref[0, sl, :], 0.0)
        xp = jnp.where(((c - 1) * C + rowi) >= front, x_ref[0, slp, :], 0.0)
        acc = x * w_ref[ML_CONV - 1:ML_CONV, :]
        for d in range(1, ML_CONV):
            sh = jnp.where(rowi >= d, pltpu.roll(x, d, 0), pltpu.roll(xp, d, 0))
            acc = acc + sh * w_ref[ML_CONV - 1 - d:ML_CONV - d, :]
        return acc * _sigmoid(acc)

    def body(c, m_prev):
        sl = pl.ds(pl.multiple_of(c * C, C), C)
        slp = pl.ds(pl.multiple_of(jnp.maximum(c - 1, 0) * C, C), C)
        q = conv_silu(q_ref, cq_ref, c, sl, slp) * (ML_DK ** -0.5)
        k = conv_silu(k_ref, ck_ref, c, sl, slp)
        live = (c * C + rowi) >= front
        v = jnp.where(live, v_ref[0, sl, :], 0.0)
        v_aug = jnp.concatenate([v, jnp.ones((C, LANES), F32)], axis=1)
        lf_col = _log_sigmoid(gcol_ref[0, sl, :] + fbrow_ref[...])
        b_bc = jnp.dot(tril, jnp.dot(lf_col, pick_f, precision=HI, preferred_element_type=F32),
                       precision=HI, preferred_element_type=F32)
        li_row = grow_ref[0, h, pl.ds(c, 1), :]
        lf_row = _log_sigmoid(grow_ref[0, ML_H + h, pl.ds(c, 1), :] + fb)
        b_row = jnp.dot(lf_row, triu, precision=HI, preferred_element_type=F32)
        b_last = b_row[:, C - 1:C]
        Dm = jnp.where(causal, b_bc - (b_row - li_row), -jnp.inf)
        m_inter = b_bc + m_prev
        m = jnp.maximum(m_inter, jnp.max(Dm, axis=1, keepdims=True))
        w_inter = jnp.exp(m_inter - m)
        P = jnp.exp(Dm - m) * _mm_nt(q, k)
        S = s_ref[...]
        qs = _mm(q, S)
        w3 = jnp.concatenate([w_inter, w_inter, w_inter], axis=1)
        num = w3 * qs + _mm(P, v_aug)
        den = num[:, ML_DV:]
        dd = jnp.maximum(jnp.abs(den), jnp.exp(-m))
        hval = num[:, :ML_DV] / jnp.concatenate([dd, dd], axis=1)
        dec = b_last - b_row + li_row
        m_new = jnp.maximum(b_last + m_prev, jnp.max(dec, axis=1, keepdims=True))
        w_old = jnp.exp(b_last + m_prev - m_new)
        w_new = jnp.exp(dec - m_new)
        kt_w = jnp.transpose(k) * w_new
        s_ref[...] = jnp.concatenate([w_old, w_old, w_old], axis=1) * S + _mm(kt_w, v_aug)
        og = og_ref[0, sl, :]
        o_ref[0, sl, :] = (_rms(hval, gh_ref[...]) * _sigmoid(og)).astype(o_ref.dtype)
        return m_new

    lax.fori_loop(0, nchunks, body, jnp.zeros((1, C), F32))


def _mlstm(z, gates_row, conv_w, fb_row, fb, gh, *, front):
    B, Tp, _ = z.shape
    nchunks = Tp // LANES
    col = lambda off, w: (lambda b, h: (b, 0, off // w + h))
    return pl.pallas_call(
        functools.partial(_mlstm_kernel, front=front, nchunks=nchunks),
        grid=(B, ML_H),
        in_specs=[pl.BlockSpec((1, Tp, ML_DK), col(Z_ML_Q, ML_DK)),
                  pl.BlockSpec((1, Tp, ML_DK), col(Z_ML_K, ML_DK)),
                  pl.BlockSpec((1, Tp, ML_DV), col(Z_ML_V, ML_DV)),
                  pl.BlockSpec((1, Tp, ML_DV), col(Z_ML_O, ML_DV)),
                  pl.BlockSpec((1, Tp, LANES), lambda b, h: (b, 0, Z_SM2 // LANES)),
                  pl.BlockSpec((1, 2 * ML_H, nchunks, LANES), lambda b, h: (b, 0, 0, 0)),
                  pl.BlockSpec((ML_CONV, ML_DK), lambda b, h: (0, h)),
                  pl.BlockSpec((ML_CONV, ML_DK), lambda b, h: (0, ML_H + h)),
                  pl.BlockSpec((1, LANES), lambda b, h: (0, 0)),
                  pl.BlockSpec(memory_space=pltpu.SMEM),
                  pl.BlockSpec((1, ML_DV), lambda b, h: (0, 0))],
        out_specs=pl.BlockSpec((1, Tp, ML_DV), lambda b, h: (b, 0, h)),
        out_shape=jax.ShapeDtypeStruct((B, Tp, ML_H * ML_DV), ACT_DT),
        scratch_shapes=[pltpu.VMEM((ML_DK, ML_DV + LANES), F32)],
        compiler_params=_params(("parallel", "parallel")),
        name="mlstm",
    )(z, z, z, z, z, gates_row, conv_w, conv_w, fb_row, fb, gh)


def _dsa_prep_kernel(cq_ref, ckv_ref, sm_ref, gq_ref, gkv_ref, lng_ref, lnb_ref, wuq_ref, wiq_ref,
                     q_ref, qi_ref, kv_ref, ki_ref):
    cq = _rms(cq_ref[0], gq_ref[...]).astype(MXU_DT)
    qf = jnp.dot(cq, wuq_ref[...], preferred_element_type=F32) * (DSA_KV_LAT ** -0.5)
    for hh in range(DSA_H):
        q_ref[0, hh] = qf[:, hh * DSA_KV_LAT:(hh + 1) * DSA_KV_LAT].astype(q_ref.dtype)
    for hh in range(IDX_H):
        qi = jnp.dot(cq, wiq_ref[hh], preferred_element_type=F32) * (IDX_D ** -0.5)
        qi_ref[0, hh] = qi.astype(qi_ref.dtype)
    kv_ref[0] = _rms(ckv_ref[0], gkv_ref[...]).astype(kv_ref.dtype)
    x = sm_ref[0][:, :IDX_D]
    mu = jnp.mean(x, axis=-1, keepdims=True)
    var = jnp.mean(jnp.square(x - mu), axis=-1, keepdims=True)
    ki_ref[0] = ((x - mu) * lax.rsqrt(var + EPS) * lng_ref[...] + lnb_ref[...]).astype(ki_ref.dtype)


def _dsa_prep(z, gq, gkv, lng, lnb, wuq, wiq):
    B, Tp, _ = z.shape
    tm = LANES
    c2 = lambda b, i: (0, 0)
    return pl.pallas_call(
        _dsa_prep_kernel,
        grid=(B, Tp // tm),
        in_specs=[pl.BlockSpec((1, tm, DSA_Q_LORA), lambda b, i: (b, i, Z_CQ // DSA_Q_LORA)),
                  pl.BlockSpec((1, tm, DSA_KV_LAT), lambda b, i: (b, i, Z_CKV // DSA_KV_LAT)),
                  pl.BlockSpec((1, tm, LANES), lambda b, i: (b, i, Z_SM1 // LANES)),
                  pl.BlockSpec((1, DSA_Q_LORA), c2), pl.BlockSpec((1, DSA_KV_LAT), c2),
                  pl.BlockSpec((1, IDX_D), c2), pl.BlockSpec((1, IDX_D), c2),
                  pl.BlockSpec((DSA_Q_LORA, DSA_H * DSA_KV_LAT), c2),
                  pl.BlockSpec((IDX_H, DSA_Q_LORA, IDX_D), lambda b, i: (0, 0, 0))],
        out_specs=[pl.BlockSpec((1, DSA_H, tm, DSA_KV_LAT), lambda b, i: (b, 0, i, 0)),
                   pl.BlockSpec((1, IDX_H, tm, IDX_D), lambda b, i: (b, 0, i, 0)),
                   pl.BlockSpec((1, tm, DSA_KV_LAT), lambda b, i: (b, i, 0)),
                   pl.BlockSpec((1, tm, IDX_D), lambda b, i: (b, i, 0))],
        out_shape=[jax.ShapeDtypeStruct((B, DSA_H, Tp, DSA_KV_LAT), ACT_DT),
                   jax.ShapeDtypeStruct((B, IDX_H, Tp, IDX_D), ACT_DT),
                   jax.ShapeDtypeStruct((B, Tp, DSA_KV_LAT), ACT_DT),
                   jax.ShapeDtypeStruct((B, Tp, IDX_D), ACT_DT)],
        compiler_params=_params(("parallel", "parallel")),
        name="dsa_prep",
    )(z, z, z, gq, gkv, lng, lnb, wuq, wiq)


def _sortable(x):
    bits = pltpu.bitcast(x, I32)
    key = jnp.where(bits < 0, bits ^ 0x7FFFFFFF, bits)
    return jnp.where(x == 0.0, 0, key)


_KEY_NEG = int(np.array(NEG, np.float32).view(np.int32)) ^ 0x7FFFFFFF


def _dsa_kernel(ki_ref, kv_ref, kvt_ref, qi_ref, q_ref, wi_ref, bias_ref, wuv_ref, o_ref,
                key_ref, madd_ref, lg_ref, acc_ref, *, front, topk, Tp):
    QB = LANES
    j = pl.program_id(1)
    nkb = j + 1
    nblk = Tp // QB
    npair = (nkb + 1) // 2
    s_loc = lax.broadcasted_iota(I32, (QB, QB), 0)
    t_glob = j * QB + lax.broadcasted_iota(I32, (QB, QB), 1)
    wi = wi_ref[0] * (IDX_H ** -0.5)
    n_rest = jnp.maximum(Tp - 2 * npair * QB, 0)

    def blk(kb):
        return pl.ds(pl.multiple_of(kb * QB, QB), QB)

    def score_block(kb):
        ki = ki_ref[0, blk(jnp.minimum(kb, nblk - 1)), :]
        acc = jnp.zeros((QB, QB), F32)
        for hh in range(IDX_H):
            s = lax.dot_general(ki, qi_ref[0, hh], NT_DIMS, preferred_element_type=F32)
            acc = acc + jnp.maximum(s, 0.0) * wi[hh:hh + 1, :]
        s_glob = kb * QB + s_loc
        key = _sortable(jnp.where(s_glob <= t_glob, acc, NEG))
        key_ref[blk(kb), :] = jnp.where((s_glob >= front) & (s_glob < Tp), key, INT_MIN)

    def score_body(i, c):
        score_block(2 * i)
        score_block(2 * i + 1)
        return c

    lax.fori_loop(0, npair, score_body, 0)

    def count(pred_fn):
        def body(i, acc):
            hit = (pred_fn(key_ref[blk(2 * i), :], 2 * i).astype(I32)
                   + pred_fn(key_ref[blk(2 * i + 1), :], 2 * i + 1).astype(I32))
            return acc + jnp.sum(hit.reshape(QB // SUBLANES, SUBLANES, QB), axis=0)
        acc = lax.fori_loop(0, npair, body, jnp.zeros((SUBLANES, QB), I32))
        return jnp.sum(acc, axis=0, keepdims=True)

    def count_ge(cand):
        return count(lambda kblk, kb: kblk >= cand) + jnp.where(_KEY_NEG >= cand, n_rest, 0)

    base = jnp.where(count_ge(jnp.zeros((1, QB), I32)) >= topk, 0, INT_MIN).astype(I32)

    def bit_body(i, base):
        cand = base | jnp.left_shift(jnp.int32(1), 30 - i)
        return jnp.where(count_ge(cand) >= topk, cand, base)

    tau = lax.fori_loop(0, 31, bit_body, base)

    def valid_of(kb):
        s_glob = kb * QB + s_loc
        return (s_glob >= front) & (s_glob <= t_glob)

    c_gt = count(lambda kblk, kb: kblk > tau) + jnp.where(_KEY_NEG > tau, n_rest, 0)
    c_eq = count(lambda kblk, kb: (kblk == tau) & valid_of(kb))
    need = topk - c_gt

    @pl.when(jnp.max((c_eq > need).astype(I32)) > 0)
    def _():
        lstrict = (s_loc > lax.broadcasted_iota(I32, (QB, QB), 1)).astype(MXU_DT)
        needf = need.astype(F32)

        def tie_body(kb, seen):
            kblk = key_ref[blk(kb), :]
            tie = (kblk == tau) & valid_of(kb)
            tief = tie.astype(F32)
            rank = seen + jnp.dot(lstrict, tief.astype(MXU_DT), preferred_element_type=F32)
            key_ref[blk(kb), :] = jnp.where(tie & (rank >= needf), INT_MIN, kblk)
            return seen + jnp.sum(tief, axis=0, keepdims=True)

        lax.fori_loop(0, nkb, tie_body, jnp.zeros((1, QB), F32))

    def mask_body(kb, c):
        sel = (key_ref[blk(kb), :] >= tau) & valid_of(kb)
        madd_ref[blk(kb), :] = jnp.where(sel, 0.0, NEG)
        return c

    lax.fori_loop(0, nkb, mask_body, 0)

    NP = DSA_H // 2
    grp = (QB // SUBLANES, SUBLANES, 2 * QB)

    def logit_body(kb, mxs):
        kvb = kv_ref[0, blk(kb), :]
        ma = madd_ref[blk(kb), :]
        ma2 = jnp.concatenate([ma, ma], axis=1)
        rel = jnp.minimum(j - kb, 2)
        out = []
        for p in range(NP):
            q2 = q_ref[0, 2 * p:2 * p + 2].reshape(2 * QB, DSA_KV_LAT)
            lg = lax.dot_general(kvb, q2, NT_DIMS, preferred_element_type=F32)
            lg = lg + bias_ref[p, rel] + ma2
            lg_ref[p, blk(kb), :] = lg
            out.append(jnp.maximum(mxs[p], jnp.max(lg.reshape(grp), axis=0)))
        return tuple(out)

    mxs = lax.fori_loop(0, nkb, logit_body, tuple(jnp.full((SUBLANES, 2 * QB), -jnp.inf, F32) for _ in range(NP)))
    mxs = [jnp.max(m8, axis=0, keepdims=True) for m8 in mxs]
    acc_ref[...] = jnp.zeros_like(acc_ref)

    def pv_body(kb, lsums):
        kvt = kvt_ref[0, kb]
        out = []
        for p in range(NP):
            pm = jnp.exp(lg_ref[p, blk(kb), :] - mxs[p])
            out.append(lsums[p] + jnp.sum(pm.reshape(grp), axis=0))
            acc_ref[p] += jnp.dot(kvt, pm.astype(MXU_DT), preferred_element_type=F32)
        return tuple(out)

    lsums = lax.fori_loop(0, nkb, pv_body, tuple(jnp.zeros((SUBLANES, 2 * QB), F32) for _ in range(NP)))
    for p in range(NP):
        o_t = acc_ref[p] / jnp.sum(lsums[p], axis=0, keepdims=True)
        outs = []
        for hh in range(2):
            o_h = jnp.transpose(o_t[:, hh * QB:(hh + 1) * QB])
            outs.append(_mm(o_h, wuv_ref[2 * p + hh]))
        o_ref[0, :, 2 * p * DSA_DH:(2 * p + 2) * DSA_DH] = jnp.concatenate(outs, axis=1).astype(o_ref.dtype)


def _dsa(ki, kv, kvt, qi, q, wi_t, bias_tiles, wuv, *, front, topk):
    B, Tp, _ = kv.shape
    nblk = Tp // LANES
    return pl.pallas_call(
        functools.partial(_dsa_kernel, front=front, topk=topk, Tp=Tp),
        grid=(B, nblk),
        in_specs=[pl.BlockSpec((1, Tp, IDX_D), lambda b, j: (b, 0, 0)),
                  pl.BlockSpec((1, Tp, DSA_KV_LAT), lambda b, j: (b, 0, 0)),
                  pl.BlockSpec((1, nblk, DSA_KV_LAT, LANES), lambda b, j: (b, 0, 0, 0)),
                  pl.BlockSpec((1, IDX_H, LANES, IDX_D), lambda b, j: (b, 0, j, 0)),
                  pl.BlockSpec((1, DSA_H, LANES, DSA_KV_LAT), lambda b, j: (b, 0, j, 0)),
                  pl.BlockSpec((1, IDX_H, LANES), lambda b, j: (b, 0, j)),
                  pl.BlockSpec((DSA_H // 2, 3, LANES, 2 * LANES), lambda b, j: (0, 0, 0, 0)),
                  pl.BlockSpec((DSA_H, DSA_KV_LAT, DSA_DH), lambda b, j: (0, 0, 0))],
        out_specs=pl.BlockSpec((1, LANES, DSA_H * DSA_DH), lambda b, j: (b, j, 0)),
        out_shape=jax.ShapeDtypeStruct((B, Tp, DSA_H * DSA_DH), ACT_DT),
        scratch_shapes=[pltpu.VMEM((Tp + LANES, LANES), I32), pltpu.VMEM((Tp, LANES), F32),
                        pltpu.VMEM((DSA_H // 2, Tp, 2 * LANES), F32),
                        pltpu.VMEM((DSA_H // 2, DSA_KV_LAT, 2 * LANES), F32)],
        compiler_params=_params(("parallel", "arbitrary")),
        name="dsa_attn",
    )(ki, kv, kvt, qi, q, wi_t, bias_tiles, wuv)


def _t5_bucket(rel):
    n = jnp.maximum(rel, 0)
    max_exact = T5_BUCKETS // 2
    nf = jnp.maximum(n, 1).astype(F32)
    large = max_exact + (jnp.log(nf / max_exact) / math.log(T5_MAX_DIST / max_exact)
                         * (T5_BUCKETS - max_exact)).astype(I32)
    large = jnp.minimum(large, T5_BUCKETS - 1)
    return jnp.where(n < max_exact, n, large)


def _t5_tiles(t5_bias):
    s = jnp.arange(LANES, dtype=I32)[:, None]
    t = jnp.arange(LANES, dtype=I32)[None, :]
    tiles = []
    for shift in (0, LANES, 2 * LANES):
        tiles.append(t5_bias[_t5_bucket(shift + t - s)])
    tiles = jnp.stack(tiles, axis=0)
    tiles = tiles.transpose(3, 0, 1, 2).reshape(DSA_H // 2, 2, 3, LANES, LANES)
    return tiles.transpose(0, 2, 3, 1, 4).reshape(DSA_H // 2, 3, LANES, 2 * LANES).astype(F32)


def _pack_in_proj(w_in, b_in):
    splits = np.cumsum([GLA_H * GLA_DK, GLA_H * GLA_DK, GLA_H * GLA_DV, GLA_H * GLA_DV, GLA_LR,
                        DSA_Q_LORA, DSA_KV_LAT, IDX_D, IDX_H,
                        ML_H * ML_DK, ML_H * ML_DK, ML_H * ML_DV, ML_H * ML_DV, ML_H, ML_H,
                        D_MODEL, D_MODEL])[:].tolist()

    def pack(a):
        (gq, gk, gv, gr, ga, cq, ckv, ik, iw, mq, mk, mv, mo, mi, mf, a_, b_, c_) = jnp.split(a, splits, axis=-1)
        pad = lambda n: jnp.zeros(a.shape[:-1] + (n,), a.dtype)
        return jnp.concatenate([gq, gk, gv, gr, mq, mk, mv, mo, a_, b_, c_, cq, ckv,
                                ga, pad(LANES - GLA_LR),
                                ik, iw, pad(LANES - IDX_D - IDX_H),
                                mi, mf, pad(LANES - 2 * ML_H)], axis=-1)

    return pack(w_in), pack(b_in)


def kernel(x, meta_tokens, t5_bias, g_pre_mix, w_in, b_in, w_gla_a2, b_gla_a, g_gla_head, w_br_gla, g_dsa_q, w_dsa_uq, w_idx_q, g_dsa_kv, w_dsa_uv, idx_ln_g, idx_ln_b, w_br_dsa, ml_conv, ml_f_bias, g_ml_head, w_br_ml, w_out, g_post_mix, g_pre_mlp, w_up, w_down, g_post_mlp):
    B, S, D = x.shape
    L = w_in.shape[0]
    T = S + N_META
    front = (-T) % LANES
    Tp = T + front
    R = B * Tp
    nblk = Tp // LANES
    topk = min(TOPK_MAX, (T - N_META) // 4)
    tm_big = 1024 if R % 1024 == 0 else (512 if R % 512 == 0 else LANES)
    tm_mid = 512 if R % 512 == 0 else LANES
    tm_small = 256 if R % 256 == 0 else LANES

    w_in_p, b_in_p = _pack_in_proj(w_in, b_in)
    wa2 = jnp.pad(w_gla_a2, ((0, 0), (0, LANES - GLA_LR), (0, 0)))
    wa2 = wa2.reshape(L, LANES, GLA_H, GLA_DK).transpose(0, 2, 1, 3).astype(MXU_DT)
    fb_row = jnp.pad(ml_f_bias, ((0, 0), (ML_H, LANES - 2 * ML_H)))[:, None, :]
    layers = dict(
        g_pre_mix=g_pre_mix[:, None, :], w_in=w_in_p.astype(MXU_DT), b_in=b_in_p[:, None, :],
        wa2=wa2, ba=b_gla_a.reshape(L, GLA_H, 1, GLA_DK), g_gla_head=g_gla_head[:, None, :],
        w_br_gla=w_br_gla.astype(MXU_DT),
        g_dsa_q=g_dsa_q[:, None, :], w_dsa_uq=w_dsa_uq.astype(MXU_DT),
        w_idx_q=w_idx_q.reshape(L, DSA_Q_LORA, IDX_H, IDX_D).transpose(0, 2, 1, 3).astype(MXU_DT),
        g_dsa_kv=g_dsa_kv[:, None, :], w_dsa_uv=w_dsa_uv.astype(MXU_DT),
        idx_ln_g=idx_ln_g[:, None, :], idx_ln_b=idx_ln_b[:, None, :], w_br_dsa=w_br_dsa.astype(MXU_DT),
        ml_conv=ml_conv, fb_row=fb_row, ml_f_bias=ml_f_bias, g_ml_head=g_ml_head[:, None, :],
        w_br_ml=w_br_ml.astype(MXU_DT), w_out=w_out.astype(MXU_DT), g_post_mix=g_post_mix[:, None, :],
        g_pre_mlp=g_pre_mlp[:, None, :], w_up=w_up.astype(MXU_DT), w_down=w_down.astype(MXU_DT),
        g_post_mlp=g_post_mlp[:, None, :],
    )
    bias_tiles = _t5_tiles(t5_bias)
    zero_ff = jnp.zeros((1, D_FF), F32)

    meta = jnp.broadcast_to(meta_tokens.astype(x.dtype)[None], (B, N_META, D))
    h0 = jnp.concatenate([jnp.zeros((B, front, D), x.dtype), meta, x], axis=1).reshape(R, D)

    def layer(h, p):
        z = _norm_matmul(h, p["g_pre_mix"], p["w_in"], p["b_in"], tm=tm_big, tn=NZ_TILE,
                         relu2=False, out_dtype=F32)
        z3 = z.reshape(B, Tp, NZ)
        o_gla = _gla(z3, p["wa2"], p["ba"], p["g_gla_head"], front=front)
        gates_row = z3[:, :, Z_SM2:Z_SM2 + 2 * ML_H].transpose(0, 2, 1).reshape(B, 2 * ML_H, nblk, LANES)
        o_ml = _mlstm(z3, gates_row, p["ml_conv"], p["fb_row"], p["ml_f_bias"], p["g_ml_head"], front=front)
        q, qi, kv, ki = _dsa_prep(z3, p["g_dsa_q"], p["g_dsa_kv"], p["idx_ln_g"], p["idx_ln_b"],
                                  p["w_dsa_uq"], p["w_idx_q"])
        kvt = kv.reshape(B, nblk, LANES, DSA_KV_LAT).transpose(0, 1, 3, 2)
        wi_t = z3[:, :, Z_SM1 + IDX_D:Z_SM1 + IDX_D + IDX_H].transpose(0, 2, 1)
        o_dsa = _dsa(ki, kv, kvt, qi, q, wi_t, bias_tiles, p["w_dsa_uv"], front=front, topk=topk)
        h = _mix(o_gla.reshape(R, D), o_dsa.reshape(R, D), o_ml.reshape(R, D), z,
                 p["w_br_gla"], p["w_br_dsa"], p["w_br_ml"], p["w_out"], p["g_post_mix"], h, tm=tm_small)
        u = _norm_matmul(h, p["g_pre_mlp"], p["w_up"], zero_ff, tm=tm_mid, tn=1024,
                         relu2=True, out_dtype=ACT_DT)
        h = _matmul_norm_res(u, p["w_down"], p["g_post_mlp"], h, tm=tm_mid)
        return h, None

    h, _ = lax.scan(layer, h0, layers)
    return h.reshape(B, Tp, D)[:, front + N_META:]
```

```python
import functools
import math

import numpy as np
import jax
import jax.numpy as jnp
from jax import lax
from jax.experimental import pallas as pl
from jax.experimental.pallas import tpu as pltpu

D_MODEL = 1024
N_META = 16
GLA_H, GLA_DK, GLA_DV, GLA_LR, GLA_TAU = 4, 128, 256, 16, 16.0
DSA_H, DSA_DH, DSA_Q_LORA, DSA_KV_LAT = 16, 64, 256, 128
IDX_H, IDX_D, TOPK_MAX = 8, 64, 256
ML_H, ML_DK, ML_DV, ML_CONV = 4, 128, 256, 4
T5_BUCKETS, T5_MAX_DIST = 32, 128
D_FF = 4 * D_MODEL
EPS = 1e-6
NEG = -1e30
LOG2E = math.log2(math.e)

LANES = 128
SUBLANES = 8
VMEM_LIMIT = 56 * 1024 * 1024

MXU_DT = jnp.bfloat16
ACT_DT = jnp.bfloat16

F32 = jnp.float32
I32 = jnp.int32
INT_MIN = -2147483648
HI = lax.Precision.HIGHEST
NT_DIMS = (((1,), (1,)), ((), ()))

Z_GLA_Q, Z_GLA_K, Z_GLA_V, Z_GLA_R = 0, 512, 1024, 2048
Z_ML_Q, Z_ML_K, Z_ML_V, Z_ML_O = 3072, 3584, 4096, 5120
Z_GATE_A, Z_GATE_B, Z_GATE_C = 6144, 7168, 8192
Z_CQ, Z_CKV = 9216, 9472
Z_SM0, Z_SM1, Z_SM2 = 9600, 9728, 9856
NZ = 9984
NZ_TILE = 768
KVT_ONES = 16
MIX_CHUNKS = 6


def _params(sem):
    return pltpu.CompilerParams(dimension_semantics=sem, vmem_limit_bytes=VMEM_LIMIT)


def _mm(a, b):
    return jnp.dot(a.astype(MXU_DT), b.astype(MXU_DT), preferred_element_type=F32)


def _mm_nt(a, b):
    return lax.dot_general(a.astype(MXU_DT), b.astype(MXU_DT), NT_DIMS, preferred_element_type=F32)


def _rms(x, g):
    return x * lax.rsqrt(jnp.mean(x * x, axis=-1, keepdims=True) + EPS) * g


def _log_sigmoid(x):
    return jnp.minimum(x, 0.0) - jnp.log1p(jnp.exp(-jnp.abs(x)))


def _sigmoid(x):
    return 1.0 / (1.0 + jnp.exp(-x))


def _norm_matmul_kernel(x_ref, g_ref, w_ref, b_ref, o_ref, xn_ref, *, relu2):
    @pl.when(pl.program_id(1) == 0)
    def _():
        xn_ref[...] = _rms(x_ref[...], g_ref[...]).astype(xn_ref.dtype)

    acc = jnp.dot(xn_ref[...], w_ref[...], preferred_element_type=F32) + b_ref[...]
    if relu2:
        acc = jnp.square(jnp.maximum(acc, 0.0))
    o_ref[...] = acc.astype(o_ref.dtype)


def _norm_matmul(x, g, w, b, *, tm, tn, relu2, out_dtype):
    R, K = x.shape
    N = w.shape[1]
    return pl.pallas_call(
        functools.partial(_norm_matmul_kernel, relu2=relu2),
        grid=(R // tm, N // tn),
        in_specs=[pl.BlockSpec((tm, K), lambda i, j: (i, 0)),
                  pl.BlockSpec((1, K), lambda i, j: (0, 0)),
                  pl.BlockSpec((K, tn), lambda i, j: (0, j)),
                  pl.BlockSpec((1, tn), lambda i, j: (0, j))],
        out_specs=pl.BlockSpec((tm, tn), lambda i, j: (i, j)),
        out_shape=jax.ShapeDtypeStruct((R, N), out_dtype),
        scratch_shapes=[pltpu.VMEM((tm, K), MXU_DT)],
        compiler_params=_params(("parallel", "arbitrary")),
        name="norm_matmul_relu2" if relu2 else "norm_matmul",
    )(x, g, w, b)


def _matmul_norm_res_kernel(u_ref, w_ref, g_ref, h_ref, o_ref):
    y = jnp.dot(u_ref[...], w_ref[...], preferred_element_type=F32)
    o_ref[...] = h_ref[...] + _rms(y, g_ref[...])


def _matmul_norm_res(u, w, g, h, *, tm):
    R, K = u.shape
    N = w.shape[1]
    return pl.pallas_call(
        _matmul_norm_res_kernel,
        grid=(R // tm,),
        in_specs=[pl.BlockSpec((tm, K), lambda i: (i, 0)),
                  pl.BlockSpec((K, N), lambda i: (0, 0)),
                  pl.BlockSpec((1, N), lambda i: (0, 0)),
                  pl.BlockSpec((tm, N), lambda i: (i, 0))],
        out_specs=pl.BlockSpec((tm, N), lambda i: (i, 0)),
        out_shape=jax.ShapeDtypeStruct((R, N), F32),
        compiler_params=_params(("parallel",)),
        name="mlp_down",
    )(u, w, g, h)


def _mix_kernel(oa_ref, ob_ref, oc_ref, ga_ref, gb_ref, gc_ref, wa_ref, wb_ref, wc_ref,
                wo_ref, g_ref, h_ref, o_ref):
    ya = jnp.dot(oa_ref[...], wa_ref[...], preferred_element_type=F32)
    yb = jnp.dot(ob_ref[...], wb_ref[...], preferred_element_type=F32)
    yc = jnp.dot(oc_ref[...], wc_ref[...], preferred_element_type=F32)
    mix = _sigmoid(ga_ref[...]) * ya + _sigmoid(gb_ref[...]) * yb + _sigmoid(gc_ref[...]) * yc
    y = _mm(mix, wo_ref[...])
    o_ref[...] = h_ref[...] + _rms(y, g_ref[...])


def _mix(oa, ob, oc, z, wa, wb, wc, wo, g, h, *, tm):
    R, D = h.shape
    row = lambda i: (i, 0)
    const = lambda i: (0, 0)
    gate = lambda off: pl.BlockSpec((tm, D), lambda i: (i, off // D))
    return pl.pallas_call(
        _mix_kernel,
        grid=(R // tm,),
        in_specs=[pl.BlockSpec((tm, D), row), pl.BlockSpec((tm, D), row), pl.BlockSpec((tm, D), row),
                  gate(Z_GATE_A), gate(Z_GATE_B), gate(Z_GATE_C),
                  pl.BlockSpec((D, D), const), pl.BlockSpec((D, D), const), pl.BlockSpec((D, D), const),
                  pl.BlockSpec((D, D), const), pl.BlockSpec((1, D), const), pl.BlockSpec((tm, D), row)],
        out_specs=pl.BlockSpec((tm, D), row),
        out_shape=jax.ShapeDtypeStruct((R, D), F32),
        compiler_params=_params(("parallel",)),
        name="mix_out",
    )(oa, ob, oc, z, z, z, wa, wb, wc, wo, g, h)


def _gla_kernel(q_ref, k_ref, v_ref, r_ref, a_ref, wa_ref, ba_ref, gh_ref, o_ref, s_ref, *, front, nchunks):
    C = LANES
    tb = pl.program_id(1)

    @pl.when(tb == 0)
    def _():
        s_ref[...] = jnp.zeros_like(s_ref)

    r0 = lax.broadcasted_iota(I32, (C, C), 0)
    r1 = lax.broadcasted_iota(I32, (C, C), 1)
    causal = r0 >= r1
    tril = causal.astype(F32)
    rowi = lax.broadcasted_iota(I32, (C, 1), 0)

    def body(c, carry):
        sl = pl.ds(pl.multiple_of(c * C, C), C)
        live = ((tb * MIX_CHUNKS + c) * C + rowi) >= front
        a = a_ref[0, sl, :].astype(MXU_DT)
        for h in range(GLA_H):
            kq = slice(h * GLA_DK, (h + 1) * GLA_DK)
            kv = slice(h * GLA_DV, (h + 1) * GLA_DV)
            q = q_ref[0, sl, kq] * (GLA_DK ** -0.5)
            k = jnp.where(live, k_ref[0, sl, kq], 0.0)
            v = jnp.where(live, v_ref[0, sl, kv], 0.0)
            g = _log_sigmoid(jnp.dot(a, wa_ref[h], preferred_element_type=F32) + ba_ref[h]) / GLA_TAU
            G = jnp.dot(tril, g, precision=HI, preferred_element_type=F32)
            g_last = G[C - 1:C, :]
            g_mid = G[C // 2 - 1:C // 2, :]
            A = _mm_nt(q * jnp.exp(G - g_mid), k * jnp.exp(g_mid - G))
            A = jnp.where(causal, A, 0.0)
            S = s_ref[h]
            o = _mm(q * jnp.exp(G), S) + _mm(A, v)
            kd_t = jnp.transpose(k * jnp.exp(g_last - G))
            dec_col = jnp.transpose(jnp.broadcast_to(jnp.exp(g_last), (C, GLA_DK)))
            s_ref[h] = jnp.concatenate([dec_col, dec_col], axis=1) * S + _mm(kd_t, v)
            r = r_ref[0, sl, kv]
            o = _rms(o, gh_ref[...]) * (r * _sigmoid(r))
            o_ref[0, sl, kv] = o.astype(o_ref.dtype)
        return carry

    lax.fori_loop(0, jnp.minimum(MIX_CHUNKS, nchunks - tb * MIX_CHUNKS), body, 0)


def _gla(z, wa, ba, gh, *, front):
    B, Tp, _ = z.shape
    nchunks = Tp // LANES
    TB = min(MIX_CHUNKS * LANES, Tp)
    WK, WV = GLA_H * GLA_DK, GLA_H * GLA_DV
    return pl.pallas_call(
        functools.partial(_gla_kernel, front=front, nchunks=nchunks),
        grid=(B, pl.cdiv(Tp, TB)),
        in_specs=[pl.BlockSpec((1, TB, WK), lambda b, t: (b, t, Z_GLA_Q // WK)),
                  pl.BlockSpec((1, TB, WK), lambda b, t: (b, t, Z_GLA_K // WK)),
                  pl.BlockSpec((1, TB, WV), lambda b, t: (b, t, Z_GLA_V // WV)),
                  pl.BlockSpec((1, TB, WV), lambda b, t: (b, t, Z_GLA_R // WV)),
                  pl.BlockSpec((1, TB, LANES), lambda b, t: (b, t, Z_SM0 // LANES)),
                  pl.BlockSpec((GLA_H, LANES, GLA_DK), lambda b, t: (0, 0, 0)),
                  pl.BlockSpec((GLA_H, 1, GLA_DK), lambda b, t: (0, 0, 0)),
                  pl.BlockSpec((1, GLA_DV), lambda b, t: (0, 0))],
        out_specs=pl.BlockSpec((1, TB, WV), lambda b, t: (b, t, 0)),
        out_shape=jax.ShapeDtypeStruct((B, Tp, WV), ACT_DT),
        scratch_shapes=[pltpu.VMEM((GLA_H, GLA_DK, GLA_DV), F32)],
        compiler_params=_params(("parallel", "arbitrary")),
        name="gla",
    )(z, z, z, z, z, wa, ba, gh)


def _mlstm_kernel(q_ref, k_ref, v_ref, og_ref, gcol_ref, grow_ref, cw_ref, fbrow_ref, fb_ref, gh_ref, o_ref,
                  s_ref, m_ref, pq_ref, pk_ref, *, front, nchunks):
    C = LANES
    WK = ML_H * ML_DK
    tb = pl.program_id(1)

    @pl.when(tb == 0)
    def _():
        s_ref[...] = jnp.zeros_like(s_ref)
        m_ref[...] = jnp.zeros_like(m_ref)
        pq_ref[...] = jnp.zeros_like(pq_ref)
        pk_ref[...] = jnp.zeros_like(pk_ref)

    r0 = lax.broadcasted_iota(I32, (C, C), 0)
    r1 = lax.broadcasted_iota(I32, (C, C), 1)
    causal = r0 >= r1
    tril = causal.astype(F32)
    triu = (r0 <= r1).astype(F32)
    rowi = lax.broadcasted_iota(I32, (C, 1), 0)

    def conv_silu(x, xp, w):
        acc = x * w[ML_CONV - 1:ML_CONV, :]
        for d in range(1, ML_CONV):
            sh = jnp.where(rowi >= d, pltpu.roll(x, d, 0), pltpu.roll(xp, d, 0))
            acc = acc + sh * w[ML_CONV - 1 - d:ML_CONV - d, :]
        return acc * _sigmoid(acc)

    def body(c, carry):
        sl = pl.ds(pl.multiple_of(c * C, C), C)
        cg = tb * MIX_CHUNKS + c
        live = (cg * C + rowi) >= front
        xq = jnp.where(live, q_ref[0, sl, :], 0.0)
        xk = jnp.where(live, k_ref[0, sl, :], 0.0)
        q_all = conv_silu(xq, pq_ref[...], cw_ref[:, :WK]) * (ML_DK ** -0.5)
        k_all = conv_silu(xk, pk_ref[...], cw_ref[:, WK:])
        pq_ref[...] = xq
        pk_ref[...] = xk
        lf_col = _log_sigmoid(gcol_ref[0, sl, :] + fbrow_ref[...])
        for h in range(ML_H):
            q = q_all[:, h * ML_DK:(h + 1) * ML_DK]
            k = k_all[:, h * ML_DK:(h + 1) * ML_DK]
            vs = slice(h * ML_DV, (h + 1) * ML_DV)
            v = jnp.where(live, v_ref[0, sl, vs], 0.0)
            v_aug = jnp.concatenate([v, jnp.ones((C, LANES), F32)], axis=1)
            m_prev = m_ref[h, 0:1, :]
            pick_f = (r0 == ML_H + h).astype(F32)
            b_bc = jnp.dot(tril, jnp.dot(lf_col, pick_f, precision=HI, preferred_element_type=F32),
                           precision=HI, preferred_element_type=F32)
            li_row = grow_ref[0, h, pl.ds(cg, 1), :]
            lf_row = _log_sigmoid(grow_ref[0, ML_H + h, pl.ds(cg, 1), :] + fb_ref[h])
            b_row = jnp.dot(lf_row, triu, precision=HI, preferred_element_type=F32)
            b_last = b_row[:, C - 1:C]
            Dm = jnp.where(causal, b_bc - (b_row - li_row), -jnp.inf)
            m_inter = b_bc + m_prev
            m = jnp.maximum(m_inter, jnp.max(Dm, axis=1, keepdims=True))
            w_inter = jnp.exp(m_inter - m)
            P = jnp.exp(Dm - m) * _mm_nt(q, k)
            S = s_ref[h]
            qs = _mm(q, S)
            w3 = jnp.concatenate([w_inter, w_inter, w_inter], axis=1)
            num = w3 * qs + _mm(P, v_aug)
            den = num[:, ML_DV:]
            dd = jnp.maximum(jnp.abs(den), jnp.exp(-m))
            hval = num[:, :ML_DV] / jnp.concatenate([dd, dd], axis=1)
            dec = b_last - b_row + li_row
            m_new = jnp.maximum(b_last + m_prev, jnp.max(dec, axis=1, keepdims=True))
            w_old = jnp.exp(b_last + m_prev - m_new)
            w_new = jnp.exp(dec - m_new)
            kt_w = jnp.transpose(k) * w_new
            s_ref[h] = jnp.concatenate([w_old, w_old, w_old], axis=1) * S + _mm(kt_w, v_aug)
            m_ref[h] = jnp.broadcast_to(m_new, (SUBLANES, C))
            og = og_ref[0, sl, vs]
            o_ref[0, sl, vs] = (_rms(hval, gh_ref[...]) * _sigmoid(og)).astype(o_ref.dtype)
        return carry

    lax.fori_loop(0, jnp.minimum(MIX_CHUNKS, nchunks - tb * MIX_CHUNKS), body, 0)


def _mlstm(z, gates_row, conv_w, fb_row, fb, gh, *, front):
    B, Tp, _ = z.shape
    nchunks = Tp // LANES
    TB = min(MIX_CHUNKS * LANES, Tp)
    WK, WV = ML_H * ML_DK, ML_H * ML_DV
    return pl.pallas_call(
        functools.partial(_mlstm_kernel, front=front, nchunks=nchunks),
        grid=(B, pl.cdiv(Tp, TB)),
        in_specs=[pl.BlockSpec((1, TB, WK), lambda b, t: (b, t, Z_ML_Q // WK)),
                  pl.BlockSpec((1, TB, WK), lambda b, t: (b, t, Z_ML_K // WK)),
                  pl.BlockSpec((1, TB, WV), lambda b, t: (b, t, Z_ML_V // WV)),
                  pl.BlockSpec((1, TB, WV), lambda b, t: (b, t, Z_ML_O // WV)),
                  pl.BlockSpec((1, TB, LANES), lambda b, t: (b, t, Z_SM2 // LANES)),
                  pl.BlockSpec((1, 2 * ML_H, nchunks, LANES), lambda b, t: (b, 0, 0, 0)),
                  pl.BlockSpec((ML_CONV, 2 * WK), lambda b, t: (0, 0)),
                  pl.BlockSpec((1, LANES), lambda b, t: (0, 0)),
                  pl.BlockSpec(memory_space=pltpu.SMEM),
                  pl.BlockSpec((1, ML_DV), lambda b, t: (0, 0))],
        out_specs=pl.BlockSpec((1, TB, WV), lambda b, t: (b, t, 0)),
        out_shape=jax.ShapeDtypeStruct((B, Tp, WV), ACT_DT),
        scratch_shapes=[pltpu.VMEM((ML_H, ML_DK, ML_DV + LANES), F32), pltpu.VMEM((ML_H, SUBLANES, LANES), F32),
                        pltpu.VMEM((LANES, WK), F32), pltpu.VMEM((LANES, WK), F32)],
        compiler_params=_params(("parallel", "arbitrary")),
        name="mlstm",
    )(z, z, z, z, z, gates_row, conv_w, fb_row, fb, gh)


def _dsa_prep_kernel(cq_ref, ckv_ref, sm_ref, gq_ref, gkv_ref, lng_ref, lnb_ref, wuq_ref, wiq_ref,
                     q_ref, qi_ref, kv_ref, ki_ref):
    cq = _rms(cq_ref[0], gq_ref[...]).astype(MXU_DT)
    qf = jnp.dot(cq, wuq_ref[...], preferred_element_type=F32) * (DSA_KV_LAT ** -0.5 * LOG2E)
    for hh in range(DSA_H):
        q_ref[0, hh] = qf[:, hh * DSA_KV_LAT:(hh + 1) * DSA_KV_LAT].astype(q_ref.dtype)
    for hh in range(IDX_H):
        qi = jnp.dot(cq, wiq_ref[hh], preferred_element_type=F32) * (IDX_D ** -0.5)
        qi_ref[0, hh] = qi.astype(qi_ref.dtype)
    kv_ref[0] = _rms(ckv_ref[0], gkv_ref[...]).astype(kv_ref.dtype)
    x = sm_ref[0][:, :IDX_D]
    mu = jnp.mean(x, axis=-1, keepdims=True)
    var = jnp.mean(jnp.square(x - mu), axis=-1, keepdims=True)
    ki_ref[0] = ((x - mu) * lax.rsqrt(var + EPS) * lng_ref[...] + lnb_ref[...]).astype(ki_ref.dtype)


def _dsa_prep(z, gq, gkv, lng, lnb, wuq, wiq):
    B, Tp, _ = z.shape
    tm = LANES
    c2 = lambda b, i: (0, 0)
    return pl.pallas_call(
        _dsa_prep_kernel,
        grid=(B, Tp // tm),
        in_specs=[pl.BlockSpec((1, tm, DSA_Q_LORA), lambda b, i: (b, i, Z_CQ // DSA_Q_LORA)),
                  pl.BlockSpec((1, tm, DSA_KV_LAT), lambda b, i: (b, i, Z_CKV // DSA_KV_LAT)),
                  pl.BlockSpec((1, tm, LANES), lambda b, i: (b, i, Z_SM1 // LANES)),
                  pl.BlockSpec((1, DSA_Q_LORA), c2), pl.BlockSpec((1, DSA_KV_LAT), c2),
                  pl.BlockSpec((1, IDX_D), c2), pl.BlockSpec((1, IDX_D), c2),
                  pl.BlockSpec((DSA_Q_LORA, DSA_H * DSA_KV_LAT), c2),
                  pl.BlockSpec((IDX_H, DSA_Q_LORA, IDX_D), lambda b, i: (0, 0, 0))],
        out_specs=[pl.BlockSpec((1, DSA_H, tm, DSA_KV_LAT), lambda b, i: (b, 0, i, 0)),
                   pl.BlockSpec((1, IDX_H, tm, IDX_D), lambda b, i: (b, 0, i, 0)),
                   pl.BlockSpec((1, tm, DSA_KV_LAT), lambda b, i: (b, i, 0)),
                   pl.BlockSpec((1, tm, IDX_D), lambda b, i: (b, i, 0))],
        out_shape=[jax.ShapeDtypeStruct((B, DSA_H, Tp, DSA_KV_LAT), ACT_DT),
                   jax.ShapeDtypeStruct((B, IDX_H, Tp, IDX_D), ACT_DT),
                   jax.ShapeDtypeStruct((B, Tp, DSA_KV_LAT), ACT_DT),
                   jax.ShapeDtypeStruct((B, Tp, IDX_D), ACT_DT)],
        compiler_params=_params(("parallel", "parallel")),
        name="dsa_prep",
    )(z, z, z, gq, gkv, lng, lnb, wuq, wiq)


def _sortable(x):
    bits = pltpu.bitcast(x, I32)
    key = jnp.where(bits < 0, bits ^ 0x7FFFFFFF, bits)
    return jnp.where(x == 0.0, 0, key)


_KEY_NEG = int(np.array(NEG, np.float32).view(np.int32)) ^ 0x7FFFFFFF


def _dsa_kernel(ki_ref, kv_ref, kvt_ref, qi_ref, q_ref, wi_ref, bias_ref, wuv_ref, o_ref,
                key_ref, madd_ref, lg_ref, acc_ref, *, front, topk, Tp):
    QB = LANES
    j = pl.program_id(1)
    nkb = j + 1
    nblk = Tp // QB
    npair = (nkb + 1) // 2
    s_loc = lax.broadcasted_iota(I32, (QB, QB), 0)
    t_glob = j * QB + lax.broadcasted_iota(I32, (QB, QB), 1)
    wi = wi_ref[0] * (IDX_H ** -0.5)
    n_rest = jnp.maximum(Tp - 2 * npair * QB, 0)

    def blk(kb):
        return pl.ds(pl.multiple_of(kb * QB, QB), QB)

    def score_block(kb):
        ki = ki_ref[0, blk(jnp.minimum(kb, nblk - 1)), :]
        acc = jnp.zeros((QB, QB), F32)
        for hh in range(IDX_H):
            s = lax.dot_general(ki, qi_ref[0, hh], NT_DIMS, preferred_element_type=F32)
            acc = acc + jnp.maximum(s, 0.0) * wi[hh:hh + 1, :]
        s_glob = kb * QB + s_loc
        key = _sortable(jnp.where(s_glob <= t_glob, acc, NEG))
        key_ref[blk(kb), :] = jnp.where((s_glob >= front) & (s_glob < Tp), key, INT_MIN)

    def score_body(i, c):
        for u in range(4):
            score_block(4 * i + u)
        return c

    lax.fori_loop(0, (nkb + 3) // 4, score_body, 0)

    def count(pred_fn):
        def body(i, acc):
            hit = (pred_fn(key_ref[blk(2 * i), :], 2 * i).astype(I32)
                   + pred_fn(key_ref[blk(2 * i + 1), :], 2 * i + 1).astype(I32))
            return acc + jnp.sum(hit.reshape(QB // SUBLANES, SUBLANES, QB), axis=0)
        acc = lax.fori_loop(0, npair, body, jnp.zeros((SUBLANES, QB), I32))
        return jnp.sum(acc, axis=0, keepdims=True)

    def count_ge(cand):
        return count(lambda kblk, kb: kblk >= cand) + jnp.where(_KEY_NEG >= cand, n_rest, 0)

    base = jnp.where(count_ge(jnp.zeros((1, QB), I32)) >= topk, 0, INT_MIN).astype(I32)

    def bit_body(i, base):
        cand = base | jnp.left_shift(jnp.int32(1), 30 - i)
        return jnp.where(count_ge(cand) >= topk, cand, base)

    tau = lax.fori_loop(0, 31, bit_body, base)

    def valid_of(kb):
        s_glob = kb * QB + s_loc
        return (s_glob >= front) & (s_glob <= t_glob)

    c_gt = count(lambda kblk, kb: kblk > tau) + jnp.where(_KEY_NEG > tau, n_rest, 0)
    c_eq = count(lambda kblk, kb: (kblk == tau) & valid_of(kb))
    need = topk - c_gt

    @pl.when(jnp.max((c_eq > need).astype(I32)) > 0)
    def _():
        lstrict = (s_loc > lax.broadcasted_iota(I32, (QB, QB), 1)).astype(MXU_DT)
        needf = need.astype(F32)

        def tie_body(kb, seen):
            kblk = key_ref[blk(kb), :]
            tie = (kblk == tau) & valid_of(kb)
            tief = tie.astype(F32)
            rank = seen + jnp.dot(lstrict, tief.astype(MXU_DT), preferred_element_type=F32)
            key_ref[blk(kb), :] = jnp.where(tie & (rank >= needf), INT_MIN, kblk)
            return seen + jnp.sum(tief, axis=0, keepdims=True)

        lax.fori_loop(0, nkb, tie_body, jnp.zeros((1, QB), F32))

    def mask_body(kb, c):
        sel = (key_ref[blk(kb), :] >= tau) & valid_of(kb)
        madd_ref[blk(kb), :] = jnp.where(sel, 0.0, NEG)
        return c

    lax.fori_loop(0, nkb, mask_body, 0)

    NP = DSA_H // 2
    grp = (QB // SUBLANES, SUBLANES, 2 * QB)

    def logit_body(kb, mxs):
        kvb = kv_ref[0, blk(kb), :]
        ma = madd_ref[blk(kb), :]
        ma2 = jnp.concatenate([ma, ma], axis=1)
        rel = jnp.minimum(j - kb, 2)
        out = []
        for p in range(NP):
            q2 = q_ref[0, 2 * p:2 * p + 2].reshape(2 * QB, DSA_KV_LAT)
            lg = lax.dot_general(kvb, q2, NT_DIMS, preferred_element_type=F32)
            lg = lg + bias_ref[p, rel] + ma2
            lg_ref[p, blk(kb), :] = lg
            out.append(jnp.maximum(mxs[p], jnp.max(lg.reshape(grp), axis=0)))
        return tuple(out)

    mxs = lax.fori_loop(0, nkb, logit_body, tuple(jnp.full((SUBLANES, 2 * QB), -jnp.inf, F32) for _ in range(NP)))
    mxs = [jnp.max(m8, axis=0, keepdims=True) for m8 in mxs]
    acc_ref[...] = jnp.zeros_like(acc_ref)

    def pv_body(kb, c):
        kvt = kvt_ref[0, kb]
        for p in range(NP):
            pm = jnp.exp2(lg_ref[p, blk(kb), :] - mxs[p])
            acc_ref[p] += jnp.dot(kvt, pm.astype(MXU_DT), preferred_element_type=F32)
        return c

    lax.fori_loop(0, nkb, pv_body, 0)
    for p in range(NP):
        acc = acc_ref[p]
        o_t = acc[:DSA_KV_LAT] / acc[DSA_KV_LAT:DSA_KV_LAT + 1]
        outs = []
        for hh in range(2):
            o_h = jnp.transpose(o_t[:, hh * QB:(hh + 1) * QB])
            outs.append(_mm(o_h, wuv_ref[2 * p + hh]))
        o_ref[0, :, 2 * p * DSA_DH:(2 * p + 2) * DSA_DH] = jnp.concatenate(outs, axis=1).astype(o_ref.dtype)


def _dsa(ki, kv, kvt, qi, q, wi_t, bias_tiles, wuv, *, front, topk):
    B, Tp, _ = kv.shape
    nblk = Tp // LANES
    return pl.pallas_call(
        functools.partial(_dsa_kernel, front=front, topk=topk, Tp=Tp),
        grid=(B, nblk),
        in_specs=[pl.BlockSpec((1, Tp, IDX_D), lambda b, j: (b, 0, 0)),
                  pl.BlockSpec((1, Tp, DSA_KV_LAT), lambda b, j: (b, 0, 0)),
                  pl.BlockSpec((1, nblk, DSA_KV_LAT + KVT_ONES, LANES), lambda b, j: (b, 0, 0, 0)),
                  pl.BlockSpec((1, IDX_H, LANES, IDX_D), lambda b, j: (b, 0, j, 0)),
                  pl.BlockSpec((1, DSA_H, LANES, DSA_KV_LAT), lambda b, j: (b, 0, j, 0)),
                  pl.BlockSpec((1, IDX_H, LANES), lambda b, j: (b, 0, j)),
                  pl.BlockSpec((DSA_H // 2, 3, LANES, 2 * LANES), lambda b, j: (0, 0, 0, 0)),
                  pl.BlockSpec((DSA_H, DSA_KV_LAT, DSA_DH), lambda b, j: (0, 0, 0))],
        out_specs=pl.BlockSpec((1, LANES, DSA_H * DSA_DH), lambda b, j: (b, j, 0)),
        out_shape=jax.ShapeDtypeStruct((B, Tp, DSA_H * DSA_DH), ACT_DT),
        scratch_shapes=[pltpu.VMEM((Tp + 3 * LANES, LANES), I32), pltpu.VMEM((Tp, LANES), F32),
                        pltpu.VMEM((DSA_H // 2, Tp, 2 * LANES), F32),
                        pltpu.VMEM((DSA_H // 2, DSA_KV_LAT + KVT_ONES, 2 * LANES), F32)],
        compiler_params=_params(("parallel", "arbitrary")),
        name="dsa_attn",
    )(ki, kv, kvt, qi, q, wi_t, bias_tiles, wuv)


def _t5_bucket(rel):
    n = jnp.maximum(rel, 0)
    max_exact = T5_BUCKETS // 2
    nf = jnp.maximum(n, 1).astype(F32)
    large = max_exact + (jnp.log(nf / max_exact) / math.log(T5_MAX_DIST / max_exact)
                         * (T5_BUCKETS - max_exact)).astype(I32)
    large = jnp.minimum(large, T5_BUCKETS - 1)
    return jnp.where(n < max_exact, n, large)


def _t5_tiles(t5_bias):
    s = jnp.arange(LANES, dtype=I32)[:, None]
    t = jnp.arange(LANES, dtype=I32)[None, :]
    tiles = []
    for shift in (0, LANES, 2 * LANES):
        tiles.append(t5_bias[_t5_bucket(shift + t - s)])
    tiles = jnp.stack(tiles, axis=0)
    tiles = tiles.transpose(3, 0, 1, 2).reshape(DSA_H // 2, 2, 3, LANES, LANES)
    return (tiles.transpose(0, 2, 3, 1, 4).reshape(DSA_H // 2, 3, LANES, 2 * LANES) * LOG2E).astype(F32)


def _pack_in_proj(w_in, b_in):
    splits = np.cumsum([GLA_H * GLA_DK, GLA_H * GLA_DK, GLA_H * GLA_DV, GLA_H * GLA_DV, GLA_LR,
                        DSA_Q_LORA, DSA_KV_LAT, IDX_D, IDX_H,
                        ML_H * ML_DK, ML_H * ML_DK, ML_H * ML_DV, ML_H * ML_DV, ML_H, ML_H,
                        D_MODEL, D_MODEL])[:].tolist()

    def pack(a):
        (gq, gk, gv, gr, ga, cq, ckv, ik, iw, mq, mk, mv, mo, mi, mf, a_, b_, c_) = jnp.split(a, splits, axis=-1)
        pad = lambda n: jnp.zeros(a.shape[:-1] + (n,), a.dtype)
        return jnp.concatenate([gq, gk, gv, gr, mq, mk, mv, mo, a_, b_, c_, cq, ckv,
                                ga, pad(LANES - GLA_LR),
                                ik, iw, pad(LANES - IDX_D - IDX_H),
                                mi, mf, pad(LANES - 2 * ML_H)], axis=-1)

    return pack(w_in), pack(b_in)


def kernel(x, meta_tokens, t5_bias, g_pre_mix, w_in, b_in, w_gla_a2, b_gla_a, g_gla_head, w_br_gla, g_dsa_q, w_dsa_uq, w_idx_q, g_dsa_kv, w_dsa_uv, idx_ln_g, idx_ln_b, w_br_dsa, ml_conv, ml_f_bias, g_ml_head, w_br_ml, w_out, g_post_mix, g_pre_mlp, w_up, w_down, g_post_mlp):
    B, S, D = x.shape
    L = w_in.shape[0]
    T = S + N_META
    front = (-T) % LANES
    Tp = T + front
    R = B * Tp
    nblk = Tp // LANES
    topk = min(TOPK_MAX, (T - N_META) // 4)
    tm_big = 1024 if R % 1024 == 0 else (512 if R % 512 == 0 else LANES)
    tm_mid = 512 if R % 512 == 0 else LANES
    tm_small = 256 if R % 256 == 0 else LANES

    w_in_p, b_in_p = _pack_in_proj(w_in, b_in)
    wa2 = jnp.pad(w_gla_a2, ((0, 0), (0, LANES - GLA_LR), (0, 0)))
    wa2 = wa2.reshape(L, LANES, GLA_H, GLA_DK).transpose(0, 2, 1, 3).astype(MXU_DT)
    fb_row = jnp.pad(ml_f_bias, ((0, 0), (ML_H, LANES - 2 * ML_H)))[:, None, :]
    layers = dict(
        g_pre_mix=g_pre_mix[:, None, :], w_in=w_in_p.astype(MXU_DT), b_in=b_in_p[:, None, :],
        wa2=wa2, ba=b_gla_a.reshape(L, GLA_H, 1, GLA_DK), g_gla_head=g_gla_head[:, None, :],
        w_br_gla=w_br_gla.astype(MXU_DT),
        g_dsa_q=g_dsa_q[:, None, :], w_dsa_uq=w_dsa_uq.astype(MXU_DT),
        w_idx_q=w_idx_q.reshape(L, DSA_Q_LORA, IDX_H, IDX_D).transpose(0, 2, 1, 3).astype(MXU_DT),
        g_dsa_kv=g_dsa_kv[:, None, :], w_dsa_uv=w_dsa_uv.astype(MXU_DT),
        idx_ln_g=idx_ln_g[:, None, :], idx_ln_b=idx_ln_b[:, None, :], w_br_dsa=w_br_dsa.astype(MXU_DT),
        ml_conv=ml_conv, fb_row=fb_row, ml_f_bias=ml_f_bias, g_ml_head=g_ml_head[:, None, :],
        w_br_ml=w_br_ml.astype(MXU_DT), w_out=w_out.astype(MXU_DT), g_post_mix=g_post_mix[:, None, :],
        g_pre_mlp=g_pre_mlp[:, None, :], w_up=w_up.astype(MXU_DT), w_down=w_down.astype(MXU_DT),
        g_post_mlp=g_post_mlp[:, None, :],
    )
    bias_tiles = _t5_tiles(t5_bias)
    zero_ff = jnp.zeros((1, D_FF), F32)

    meta = jnp.broadcast_to(meta_tokens.astype(x.dtype)[None], (B, N_META, D))
    h0 = jnp.concatenate([jnp.zeros((B, front, D), x.dtype), meta, x], axis=1).reshape(R, D)

    def layer(h, p):
        z = _norm_matmul(h, p["g_pre_mix"], p["w_in"], p["b_in"], tm=tm_big, tn=NZ_TILE,
                         relu2=False, out_dtype=F32)
        z3 = z.reshape(B, Tp, NZ)
        o_gla = _gla(z3, p["wa2"], p["ba"], p["g_gla_head"], front=front)
        gates_row = z3[:, :, Z_SM2:Z_SM2 + 2 * ML_H].transpose(0, 2, 1).reshape(B, 2 * ML_H, nblk, LANES)
        o_ml = _mlstm(z3, gates_row, p["ml_conv"], p["fb_row"], p["ml_f_bias"], p["g_ml_head"], front=front)
        q, qi, kv, ki = _dsa_prep(z3, p["g_dsa_q"], p["g_dsa_kv"], p["idx_ln_g"], p["idx_ln_b"],
                                  p["w_dsa_uq"], p["w_idx_q"])
        kvt = kv.reshape(B, nblk, LANES, DSA_KV_LAT).transpose(0, 1, 3, 2)
        kvt = jnp.concatenate([kvt, jnp.ones((B, nblk, KVT_ONES, LANES), kvt.dtype)], axis=2)
        wi_t = z3[:, :, Z_SM1 + IDX_D:Z_SM1 + IDX_D + IDX_H].transpose(0, 2, 1)
        o_dsa = _dsa(ki, kv, kvt, qi, q, wi_t, bias_tiles, p["w_dsa_uv"], front=front, topk=topk)
        h = _mix(o_gla.reshape(R, D), o_dsa.reshape(R, D), o_ml.reshape(R, D), z,
                 p["w_br_gla"], p["w_br_dsa"], p["w_br_ml"], p["w_out"], p["g_post_mix"], h, tm=tm_small)
        u = _norm_matmul(h, p["g_pre_mlp"], p["w_up"], zero_ff, tm=tm_mid, tn=1024,
                         relu2=True, out_dtype=ACT_DT)
        h = _matmul_norm_res(u, p["w_down"], p["g_post_mlp"], h, tm=tm_mid)
        return h, None

    h, _ = lax.scan(layer, h0, layers)
    return h.reshape(B, Tp, D)[:, front + N_META:]
```

```python
import functools
import math

import numpy as np
import jax
import jax.numpy as jnp
from jax import lax
from jax.experimental import pallas as pl
from jax.experimental.pallas import tpu as pltpu

D_MODEL = 1024
N_META = 16
GLA_H, GLA_DK, GLA_DV, GLA_LR, GLA_TAU = 4, 128, 256, 16, 16.0
DSA_H, DSA_DH, DSA_Q_LORA, DSA_KV_LAT = 16, 64, 256, 128
IDX_H, IDX_D, TOPK_MAX = 8, 64, 256
ML_H, ML_DK, ML_DV, ML_CONV = 4, 128, 256, 4
T5_BUCKETS, T5_MAX_DIST = 32, 128
D_FF = 4 * D_MODEL
EPS = 1e-6
NEG = -1e30
LOG2E = math.log2(math.e)

LANES = 128
SUBLANES = 8
VMEM_LIMIT = 56 * 1024 * 1024

MXU_DT = jnp.bfloat16
ACT_DT = jnp.bfloat16

F32 = jnp.float32
I32 = jnp.int32
INT_MIN = -2147483648
HI = lax.Precision.HIGHEST
NT_DIMS = (((1,), (1,)), ((), ()))

Z_GLA_Q, Z_GLA_K, Z_GLA_V, Z_GLA_R = 0, 512, 1024, 2048
Z_ML_Q, Z_ML_K, Z_ML_V, Z_ML_O = 3072, 3584, 4096, 5120
NZL = 6144
Z_GATE_A, Z_GATE_B, Z_GATE_C = 0, 1024, 2048
Z_CQ, Z_CKV = 3072, 3328
Z_SM0, Z_SM1, Z_SM2 = 3456, 3584, 3712
NZH = 3840
KVT_ONES = 16
MIX_CHUNKS = 6


def _params(sem):
    return pltpu.CompilerParams(dimension_semantics=sem, vmem_limit_bytes=VMEM_LIMIT)


def _mm(a, b):
    return jnp.dot(a.astype(MXU_DT), b.astype(MXU_DT), preferred_element_type=F32)


def _mm_nt(a, b):
    return lax.dot_general(a.astype(MXU_DT), b.astype(MXU_DT), NT_DIMS, preferred_element_type=F32)


def _rms(x, g):
    return x * lax.rsqrt(jnp.mean(x * x, axis=-1, keepdims=True) + EPS) * g


def _log_sigmoid(x):
    return jnp.minimum(x, 0.0) - jnp.log1p(jnp.exp(-jnp.abs(x)))


def _sigmoid(x):
    return 1.0 / (1.0 + jnp.exp(-x))


def _norm_matmul_kernel(x_ref, g_ref, w_ref, b_ref, o_ref, *, relu2, tn):
    xn = _rms(x_ref[...], g_ref[...]).astype(MXU_DT)
    for n0 in range(0, o_ref.shape[1], tn):
        acc = jnp.dot(xn, w_ref[:, n0:n0 + tn], preferred_element_type=F32) + b_ref[:, n0:n0 + tn]
        if relu2:
            acc = jnp.square(jnp.maximum(acc, 0.0))
        o_ref[:, n0:n0 + tn] = acc.astype(o_ref.dtype)


def _norm_matmul(x, g, w, b, *, tm, tn, relu2, out_dtype, name):
    R, K = x.shape
    N = w.shape[1]
    return pl.pallas_call(
        functools.partial(_norm_matmul_kernel, relu2=relu2, tn=tn),
        grid=(R // tm,),
        in_specs=[pl.BlockSpec((tm, K), lambda i: (i, 0)),
                  pl.BlockSpec((1, K), lambda i: (0, 0)),
                  pl.BlockSpec((K, N), lambda i: (0, 0)),
                  pl.BlockSpec((1, N), lambda i: (0, 0))],
        out_specs=pl.BlockSpec((tm, N), lambda i: (i, 0)),
        out_shape=jax.ShapeDtypeStruct((R, N), out_dtype),
        compiler_params=_params(("parallel",)),
        name=name,
    )(x, g, w, b)


def _matmul_norm_res_kernel(u_ref, w_ref, g_ref, h_ref, o_ref):
    y = jnp.dot(u_ref[...], w_ref[...], preferred_element_type=F32)
    o_ref[...] = h_ref[...] + _rms(y, g_ref[...])


def _matmul_norm_res(u, w, g, h, *, tm):
    R, K = u.shape
    N = w.shape[1]
    return pl.pallas_call(
        _matmul_norm_res_kernel,
        grid=(R // tm,),
        in_specs=[pl.BlockSpec((tm, K), lambda i: (i, 0)),
                  pl.BlockSpec((K, N), lambda i: (0, 0)),
                  pl.BlockSpec((1, N), lambda i: (0, 0)),
                  pl.BlockSpec((tm, N), lambda i: (i, 0))],
        out_specs=pl.BlockSpec((tm, N), lambda i: (i, 0)),
        out_shape=jax.ShapeDtypeStruct((R, N), F32),
        compiler_params=_params(("parallel",)),
        name="mlp_down",
    )(u, w, g, h)


def _mix_kernel(oa_ref, ob_ref, oc_ref, ga_ref, gb_ref, gc_ref, wa_ref, wb_ref, wc_ref,
                wo_ref, g_ref, h_ref, o_ref):
    ya = jnp.dot(oa_ref[...], wa_ref[...], preferred_element_type=F32)
    yb = jnp.dot(ob_ref[...], wb_ref[...], preferred_element_type=F32)
    yc = jnp.dot(oc_ref[...], wc_ref[...], preferred_element_type=F32)
    mix = _sigmoid(ga_ref[...]) * ya + _sigmoid(gb_ref[...]) * yb + _sigmoid(gc_ref[...]) * yc
    y = _mm(mix, wo_ref[...])
    o_ref[...] = h_ref[...] + _rms(y, g_ref[...])


def _mix(oa, ob, oc, z, wa, wb, wc, wo, g, h, *, tm):
    R, D = h.shape
    row = lambda i: (i, 0)
    const = lambda i: (0, 0)
    gate = lambda off: pl.BlockSpec((tm, D), lambda i: (i, off // D))
    return pl.pallas_call(
        _mix_kernel,
        grid=(R // tm,),
        in_specs=[pl.BlockSpec((tm, D), row), pl.BlockSpec((tm, D), row), pl.BlockSpec((tm, D), row),
                  gate(Z_GATE_A), gate(Z_GATE_B), gate(Z_GATE_C),
                  pl.BlockSpec((D, D), const), pl.BlockSpec((D, D), const), pl.BlockSpec((D, D), const),
                  pl.BlockSpec((D, D), const), pl.BlockSpec((1, D), const), pl.BlockSpec((tm, D), row)],
        out_specs=pl.BlockSpec((tm, D), row),
        out_shape=jax.ShapeDtypeStruct((R, D), F32),
        compiler_params=_params(("parallel",)),
        name="mix_out",
    )(oa, ob, oc, z, z, z, wa, wb, wc, wo, g, h)


def _gla_kernel(q_ref, k_ref, v_ref, r_ref, a_ref, wa_ref, ba_ref, gh_ref, o_ref, s_ref, *, front, nchunks):
    C = LANES
    tb = pl.program_id(1)

    @pl.when(tb == 0)
    def _():
        s_ref[...] = jnp.zeros_like(s_ref)

    r0 = lax.broadcasted_iota(I32, (C, C), 0)
    r1 = lax.broadcasted_iota(I32, (C, C), 1)
    causal = r0 >= r1
    tril = causal.astype(F32)
    rowi = lax.broadcasted_iota(I32, (C, 1), 0)

    def body(c, carry):
        sl = pl.ds(pl.multiple_of(c * C, C), C)
        live = ((tb * MIX_CHUNKS + c) * C + rowi) >= front
        a = a_ref[0, sl, :].astype(MXU_DT)
        for h in range(GLA_H):
            kq = slice(h * GLA_DK, (h + 1) * GLA_DK)
            kv = slice(h * GLA_DV, (h + 1) * GLA_DV)
            q = q_ref[0, sl, kq].astype(F32) * (GLA_DK ** -0.5)
            k = jnp.where(live, k_ref[0, sl, kq].astype(F32), 0.0)
            v = jnp.where(live, v_ref[0, sl, kv].astype(F32), 0.0)
            g = _log_sigmoid(jnp.dot(a, wa_ref[h], preferred_element_type=F32) + ba_ref[h]) / GLA_TAU
            G = jnp.dot(tril, g, precision=HI, preferred_element_type=F32)
            g_last = G[C - 1:C, :]
            g_mid = G[C // 2 - 1:C // 2, :]
            A = _mm_nt(q * jnp.exp(G - g_mid), k * jnp.exp(g_mid - G))
            A = jnp.where(causal, A, 0.0)
            S = s_ref[h]
            o = _mm(q * jnp.exp(G), S) + _mm(A, v)
            kd_t = jnp.transpose(k * jnp.exp(g_last - G))
            dec_col = jnp.transpose(jnp.broadcast_to(jnp.exp(g_last), (C, GLA_DK)))
            s_ref[h] = jnp.concatenate([dec_col, dec_col], axis=1) * S + _mm(kd_t, v)
            r = r_ref[0, sl, kv].astype(F32)
            o = _rms(o, gh_ref[...]) * (r * _sigmoid(r))
            o_ref[0, sl, kv] = o.astype(o_ref.dtype)
        return carry

    lax.fori_loop(0, jnp.minimum(MIX_CHUNKS, nchunks - tb * MIX_CHUNKS), body, 0)


def _gla(zl, zh, wa, ba, gh, *, front):
    B, Tp, _ = zl.shape
    nchunks = Tp // LANES
    TB = min(MIX_CHUNKS * LANES, Tp)
    WK, WV = GLA_H * GLA_DK, GLA_H * GLA_DV
    return pl.pallas_call(
        functools.partial(_gla_kernel, front=front, nchunks=nchunks),
        grid=(B, pl.cdiv(Tp, TB)),
        in_specs=[pl.BlockSpec((1, TB, WK), lambda b, t: (b, t, Z_GLA_Q // WK)),
                  pl.BlockSpec((1, TB, WK), lambda b, t: (b, t, Z_GLA_K // WK)),
                  pl.BlockSpec((1, TB, WV), lambda b, t: (b, t, Z_GLA_V // WV)),
                  pl.BlockSpec((1, TB, WV), lambda b, t: (b, t, Z_GLA_R // WV)),
                  pl.BlockSpec((1, TB, LANES), lambda b, t: (b, t, Z_SM0 // LANES)),
                  pl.BlockSpec((GLA_H, LANES, GLA_DK), lambda b, t: (0, 0, 0)),
                  pl.BlockSpec((GLA_H, 1, GLA_DK), lambda b, t: (0, 0, 0)),
                  pl.BlockSpec((1, GLA_DV), lambda b, t: (0, 0))],
        out_specs=pl.BlockSpec((1, TB, WV), lambda b, t: (b, t, 0)),
        out_shape=jax.ShapeDtypeStruct((B, Tp, WV), ACT_DT),
        scratch_shapes=[pltpu.VMEM((GLA_H, GLA_DK, GLA_DV), F32)],
        compiler_params=_params(("parallel", "arbitrary")),
        name="gla",
    )(zl, zl, zl, zl, zh, wa, ba, gh)


def _mlstm_kernel(q_ref, k_ref, v_ref, og_ref, gcol_ref, grow_ref, cw_ref, fbrow_ref, fb_ref, gh_ref, o_ref,
                  s_ref, m_ref, pq_ref, pk_ref, *, front, nchunks):
    C = LANES
    WK = ML_H * ML_DK
    tb = pl.program_id(1)

    @pl.when(tb == 0)
    def _():
        s_ref[...] = jnp.zeros_like(s_ref)
        m_ref[...] = jnp.zeros_like(m_ref)
        pq_ref[...] = jnp.zeros_like(pq_ref)
        pk_ref[...] = jnp.zeros_like(pk_ref)

    r0 = lax.broadcasted_iota(I32, (C, C), 0)
    r1 = lax.broadcasted_iota(I32, (C, C), 1)
    causal = r0 >= r1
    tril = causal.astype(F32)
    triu = (r0 <= r1).astype(F32)
    rowi = lax.broadcasted_iota(I32, (C, 1), 0)

    def conv_silu(x, xp, w):
        acc = x * w[ML_CONV - 1:ML_CONV, :]
        for d in range(1, ML_CONV):
            sh = jnp.where(rowi >= d, pltpu.roll(x, d, 0), pltpu.roll(xp, d, 0))
            acc = acc + sh * w[ML_CONV - 1 - d:ML_CONV - d, :]
        return acc * _sigmoid(acc)

    def body(c, carry):
        sl = pl.ds(pl.multiple_of(c * C, C), C)
        cg = tb * MIX_CHUNKS + c
        live = (cg * C + rowi) >= front
        xq = jnp.where(live, q_ref[0, sl, :].astype(F32), 0.0)
        xk = jnp.where(live, k_ref[0, sl, :].astype(F32), 0.0)
        q_all = conv_silu(xq, pq_ref[...], cw_ref[:, :WK]) * (ML_DK ** -0.5)
        k_all = conv_silu(xk, pk_ref[...], cw_ref[:, WK:])
        pq_ref[...] = xq
        pk_ref[...] = xk
        lf_col = _log_sigmoid(gcol_ref[0, sl, :] + fbrow_ref[...])
        for h in range(ML_H):
            q = q_all[:, h * ML_DK:(h + 1) * ML_DK]
            k = k_all[:, h * ML_DK:(h + 1) * ML_DK]
            vs = slice(h * ML_DV, (h + 1) * ML_DV)
            v = jnp.where(live, v_ref[0, sl, vs].astype(F32), 0.0)
            v_aug = jnp.concatenate([v, jnp.ones((C, LANES), F32)], axis=1)
            m_prev = m_ref[h, 0:1, :]
            pick_f = (r0 == ML_H + h).astype(F32)
            b_bc = jnp.dot(tril, jnp.dot(lf_col, pick_f, precision=HI, preferred_element_type=F32),
                           precision=HI, preferred_element_type=F32)
            li_row = grow_ref[0, h, pl.ds(cg, 1), :]
            lf_row = _log_sigmoid(grow_ref[0, ML_H + h, pl.ds(cg, 1), :] + fb_ref[h])
            b_row = jnp.dot(lf_row, triu, precision=HI, preferred_element_type=F32)
            b_last = b_row[:, C - 1:C]
            Dm = jnp.where(causal, b_bc - (b_row - li_row), -jnp.inf)
            m_inter = b_bc + m_prev
            m = jnp.maximum(m_inter, jnp.max(Dm, axis=1, keepdims=True))
            w_inter = jnp.exp(m_inter - m)
            P = jnp.exp(Dm - m) * _mm_nt(q, k)
            S = s_ref[h]
            qs = _mm(q, S)
            w3 = jnp.concatenate([w_inter, w_inter, w_inter], axis=1)
            num = w3 * qs + _mm(P, v_aug)
            den = num[:, ML_DV:]
            dd = jnp.maximum(jnp.abs(den), jnp.exp(-m))
            hval = num[:, :ML_DV] / jnp.concatenate([dd, dd], axis=1)
            dec = b_last - b_row + li_row
            m_new = jnp.maximum(b_last + m_prev, jnp.max(dec, axis=1, keepdims=True))
            w_old = jnp.exp(b_last + m_prev - m_new)
            w_new = jnp.exp(dec - m_new)
            kt_w = jnp.transpose(k) * w_new
            s_ref[h] = jnp.concatenate([w_old, w_old, w_old], axis=1) * S + _mm(kt_w, v_aug)
            m_ref[h] = jnp.broadcast_to(m_new, (SUBLANES, C))
            og = og_ref[0, sl, vs].astype(F32)
            o_ref[0, sl, vs] = (_rms(hval, gh_ref[...]) * _sigmoid(og)).astype(o_ref.dtype)
        return carry

    lax.fori_loop(0, jnp.minimum(MIX_CHUNKS, nchunks - tb * MIX_CHUNKS), body, 0)


def _mlstm(zl, zh, gates_row, conv_w, fb_row, fb, gh, *, front):
    B, Tp, _ = zl.shape
    nchunks = Tp // LANES
    TB = min(MIX_CHUNKS * LANES, Tp)
    WK, WV = ML_H * ML_DK, ML_H * ML_DV
    return pl.pallas_call(
        functools.partial(_mlstm_kernel, front=front, nchunks=nchunks),
        grid=(B, pl.cdiv(Tp, TB)),
        in_specs=[pl.BlockSpec((1, TB, WK), lambda b, t: (b, t, Z_ML_Q // WK)),
                  pl.BlockSpec((1, TB, WK), lambda b, t: (b, t, Z_ML_K // WK)),
                  pl.BlockSpec((1, TB, WV), lambda b, t: (b, t, Z_ML_V // WV)),
                  pl.BlockSpec((1, TB, WV), lambda b, t: (b, t, Z_ML_O // WV)),
                  pl.BlockSpec((1, TB, LANES), lambda b, t: (b, t, Z_SM2 // LANES)),
                  pl.BlockSpec((1, 2 * ML_H, nchunks, LANES), lambda b, t: (b, 0, 0, 0)),
                  pl.BlockSpec((ML_CONV, 2 * WK), lambda b, t: (0, 0)),
                  pl.BlockSpec((1, LANES), lambda b, t: (0, 0)),
                  pl.BlockSpec(memory_space=pltpu.SMEM),
                  pl.BlockSpec((1, ML_DV), lambda b, t: (0, 0))],
        out_specs=pl.BlockSpec((1, TB, WV), lambda b, t: (b, t, 0)),
        out_shape=jax.ShapeDtypeStruct((B, Tp, WV), ACT_DT),
        scratch_shapes=[pltpu.VMEM((ML_H, ML_DK, ML_DV + LANES), F32), pltpu.VMEM((ML_H, SUBLANES, LANES), F32),
                        pltpu.VMEM((LANES, WK), F32), pltpu.VMEM((LANES, WK), F32)],
        compiler_params=_params(("parallel", "arbitrary")),
        name="mlstm",
    )(zl, zl, zl, zl, zh, gates_row, conv_w, fb_row, fb, gh)


def _dsa_prep_kernel(cq_ref, ckv_ref, sm_ref, gq_ref, gkv_ref, lng_ref, lnb_ref, wuq_ref, wiq_ref,
                     q_ref, qi_ref, kv_ref, ki_ref):
    cq = _rms(cq_ref[0], gq_ref[...]).astype(MXU_DT)
    qf = jnp.dot(cq, wuq_ref[...], preferred_element_type=F32) * (DSA_KV_LAT ** -0.5 * LOG2E)
    for hh in range(DSA_H):
        q_ref[0, hh] = qf[:, hh * DSA_KV_LAT:(hh + 1) * DSA_KV_LAT].astype(q_ref.dtype)
    for hh in range(IDX_H):
        qi = jnp.dot(cq, wiq_ref[hh], preferred_element_type=F32) * (IDX_D ** -0.5)
        qi_ref[0, hh] = qi.astype(qi_ref.dtype)
    kv_ref[0] = _rms(ckv_ref[0], gkv_ref[...]).astype(kv_ref.dtype)
    x = sm_ref[0][:, :IDX_D]
    mu = jnp.mean(x, axis=-1, keepdims=True)
    var = jnp.mean(jnp.square(x - mu), axis=-1, keepdims=True)
    ki_ref[0] = ((x - mu) * lax.rsqrt(var + EPS) * lng_ref[...] + lnb_ref[...]).astype(ki_ref.dtype)


def _dsa_prep(z, gq, gkv, lng, lnb, wuq, wiq):
    B, Tp, _ = z.shape
    tm = LANES
    c2 = lambda b, i: (0, 0)
    return pl.pallas_call(
        _dsa_prep_kernel,
        grid=(B, Tp // tm),
        in_specs=[pl.BlockSpec((1, tm, DSA_Q_LORA), lambda b, i: (b, i, Z_CQ // DSA_Q_LORA)),
                  pl.BlockSpec((1, tm, DSA_KV_LAT), lambda b, i: (b, i, Z_CKV // DSA_KV_LAT)),
                  pl.BlockSpec((1, tm, LANES), lambda b, i: (b, i, Z_SM1 // LANES)),
                  pl.BlockSpec((1, DSA_Q_LORA), c2), pl.BlockSpec((1, DSA_KV_LAT), c2),
                  pl.BlockSpec((1, IDX_D), c2), pl.BlockSpec((1, IDX_D), c2),
                  pl.BlockSpec((DSA_Q_LORA, DSA_H * DSA_KV_LAT), c2),
                  pl.BlockSpec((IDX_H, DSA_Q_LORA, IDX_D), lambda b, i: (0, 0, 0))],
        out_specs=[pl.BlockSpec((1, DSA_H, tm, DSA_KV_LAT), lambda b, i: (b, 0, i, 0)),
                   pl.BlockSpec((1, IDX_H, tm, IDX_D), lambda b, i: (b, 0, i, 0)),
                   pl.BlockSpec((1, tm, DSA_KV_LAT), lambda b, i: (b, i, 0)),
                   pl.BlockSpec((1, tm, IDX_D), lambda b, i: (b, i, 0))],
        out_shape=[jax.ShapeDtypeStruct((B, DSA_H, Tp, DSA_KV_LAT), ACT_DT),
                   jax.ShapeDtypeStruct((B, IDX_H, Tp, IDX_D), ACT_DT),
                   jax.ShapeDtypeStruct((B, Tp, DSA_KV_LAT), ACT_DT),
                   jax.ShapeDtypeStruct((B, Tp, IDX_D), ACT_DT)],
        compiler_params=_params(("parallel", "parallel")),
        name="dsa_prep",
    )(z, z, z, gq, gkv, lng, lnb, wuq, wiq)


def _sortable(x):
    bits = pltpu.bitcast(x, I32)
    key = jnp.where(bits < 0, bits ^ 0x7FFFFFFF, bits)
    return jnp.where(x == 0.0, 0, key)


_KEY_NEG = int(np.array(NEG, np.float32).view(np.int32)) ^ 0x7FFFFFFF


def _dsa_kernel(ki_ref, kv_ref, kvt_ref, qi_ref, q_ref, wi_ref, bias_ref, wuv_ref, o_ref,
                key_ref, madd_ref, lg_ref, acc_ref, *, front, topk, Tp):
    QB = LANES
    j = pl.program_id(1)
    nkb = j + 1
    nblk = Tp // QB
    npair = (nkb + 1) // 2
    s_loc = lax.broadcasted_iota(I32, (QB, QB), 0)
    t_glob = j * QB + lax.broadcasted_iota(I32, (QB, QB), 1)
    wi = wi_ref[0] * (IDX_H ** -0.5)
    n_rest = jnp.maximum(Tp - 2 * npair * QB, 0)

    def blk(kb):
        return pl.ds(pl.multiple_of(kb * QB, QB), QB)

    def score_block(kb):
        ki = ki_ref[0, blk(jnp.minimum(kb, nblk - 1)), :]
        acc = jnp.zeros((QB, QB), F32)
        for hh in range(IDX_H):
            s = lax.dot_general(ki, qi_ref[0, hh], NT_DIMS, preferred_element_type=F32)
            acc = acc + jnp.maximum(s, 0.0) * wi[hh:hh + 1, :]
        s_glob = kb * QB + s_loc
        key = _sortable(jnp.where(s_glob <= t_glob, acc, NEG))
        key_ref[blk(kb), :] = jnp.where((s_glob >= front) & (s_glob < Tp), key, INT_MIN)

    def score_body(i, c):
        for u in range(4):
            score_block(4 * i + u)
        return c

    lax.fori_loop(0, (nkb + 3) // 4, score_body, 0)

    def count(pred_fn):
        def body(i, acc):
            hit = (pred_fn(key_ref[blk(2 * i), :], 2 * i).astype(I32)
                   + pred_fn(key_ref[blk(2 * i + 1), :], 2 * i + 1).astype(I32))
            return acc + jnp.sum(hit.reshape(QB // SUBLANES, SUBLANES, QB), axis=0)
        acc = lax.fori_loop(0, npair, body, jnp.zeros((SUBLANES, QB), I32))
        return jnp.sum(acc, axis=0, keepdims=True)

    def count_ge(cand):
        return count(lambda kblk, kb: kblk >= cand) + jnp.where(_KEY_NEG >= cand, n_rest, 0)

    base = jnp.where(count_ge(jnp.zeros((1, QB), I32)) >= topk, 0, INT_MIN).astype(I32)

    def bit_body(i, base):
        cand = base | jnp.left_shift(jnp.int32(1), 30 - i)
        return jnp.where(count_ge(cand) >= topk, cand, base)

    tau = lax.fori_loop(0, 31, bit_body, base)

    def valid_of(kb):
        s_glob = kb * QB + s_loc
        return (s_glob >= front) & (s_glob <= t_glob)

    c_gt = count(lambda kblk, kb: kblk > tau) + jnp.where(_KEY_NEG > tau, n_rest, 0)
    c_eq = count(lambda kblk, kb: (kblk == tau) & valid_of(kb))
    need = topk - c_gt

    @pl.when(jnp.max((c_eq > need).astype(I32)) > 0)
    def _():
        lstrict = (s_loc > lax.broadcasted_iota(I32, (QB, QB), 1)).astype(MXU_DT)
        needf = need.astype(F32)

        def tie_body(kb, seen):
            kblk = key_ref[blk(kb), :]
            tie = (kblk == tau) & valid_of(kb)
            tief = tie.astype(F32)
            rank = seen + jnp.dot(lstrict, tief.astype(MXU_DT), preferred_element_type=F32)
            key_ref[blk(kb), :] = jnp.where(tie & (rank >= needf), INT_MIN, kblk)
            return seen + jnp.sum(tief, axis=0, keepdims=True)

        lax.fori_loop(0, nkb, tie_body, jnp.zeros((1, QB), F32))

    def mask_body(kb, c):
        sel = (key_ref[blk(kb), :] >= tau) & valid_of(kb)
        madd_ref[blk(kb), :] = jnp.where(sel, 0.0, NEG)
        return c

    lax.fori_loop(0, nkb, mask_body, 0)

    NP = DSA_H // 2
    grp = (QB // SUBLANES, SUBLANES, 2 * QB)
    acc_ref[...] = jnp.zeros_like(acc_ref)

    def attn_body(kb, ms):
        kvb = kv_ref[0, blk(kb), :]
        kvt = kvt_ref[0, kb]
        ma = madd_ref[blk(kb), :]
        ma2 = jnp.concatenate([ma, ma], axis=1)
        rel = jnp.minimum(j - kb, 2)
        out = []
        for p in range(NP):
            q2 = q_ref[0, 2 * p:2 * p + 2].reshape(2 * QB, DSA_KV_LAT)
            lg = lax.dot_general(kvb, q2, NT_DIMS, preferred_element_type=F32)
            lg = lg + bias_ref[p, rel] + ma2
            lg_ref[p] = lg
            out.append(jnp.maximum(ms[p], jnp.max(jnp.max(lg.reshape(grp), axis=0), axis=0, keepdims=True)))
        for p in range(NP):
            pm = jnp.exp2(lg_ref[p] - out[p])
            acc_ref[p] = (jnp.exp2(ms[p] - out[p]) * acc_ref[p]
                          + jnp.dot(kvt, pm.astype(MXU_DT), preferred_element_type=F32))
        return tuple(out)

    lax.fori_loop(0, nkb, attn_body, tuple(jnp.full((1, 2 * QB), -jnp.inf, F32) for _ in range(NP)))
    for p in range(NP):
        acc = acc_ref[p]
        o_t = acc[:DSA_KV_LAT] / acc[DSA_KV_LAT:DSA_KV_LAT + 1]
        outs = []
        for hh in range(2):
            o_h = jnp.transpose(o_t[:, hh * QB:(hh + 1) * QB])
            outs.append(_mm(o_h, wuv_ref[2 * p + hh]))
        o_ref[0, :, 2 * p * DSA_DH:(2 * p + 2) * DSA_DH] = jnp.concatenate(outs, axis=1).astype(o_ref.dtype)


def _dsa(ki, kv, kvt, qi, q, wi_t, bias_tiles, wuv, *, front, topk):
    B, Tp, _ = kv.shape
    nblk = Tp // LANES
    return pl.pallas_call(
        functools.partial(_dsa_kernel, front=front, topk=topk, Tp=Tp),
        grid=(B, nblk),
        in_specs=[pl.BlockSpec((1, Tp, IDX_D), lambda b, j: (b, 0, 0)),
                  pl.BlockSpec((1, Tp, DSA_KV_LAT), lambda b, j: (b, 0, 0)),
                  pl.BlockSpec((1, nblk, DSA_KV_LAT + KVT_ONES, LANES), lambda b, j: (b, 0, 0, 0)),
                  pl.BlockSpec((1, IDX_H, LANES, IDX_D), lambda b, j: (b, 0, j, 0)),
                  pl.BlockSpec((1, DSA_H, LANES, DSA_KV_LAT), lambda b, j: (b, 0, j, 0)),
                  pl.BlockSpec((1, IDX_H, LANES), lambda b, j: (b, 0, j)),
                  pl.BlockSpec((DSA_H // 2, 3, LANES, 2 * LANES), lambda b, j: (0, 0, 0, 0)),
                  pl.BlockSpec((DSA_H, DSA_KV_LAT, DSA_DH), lambda b, j: (0, 0, 0))],
        out_specs=pl.BlockSpec((1, LANES, DSA_H * DSA_DH), lambda b, j: (b, j, 0)),
        out_shape=jax.ShapeDtypeStruct((B, Tp, DSA_H * DSA_DH), ACT_DT),
        scratch_shapes=[pltpu.VMEM((Tp + 3 * LANES, LANES), I32), pltpu.VMEM((Tp, LANES), F32),
                        pltpu.VMEM((DSA_H // 2, LANES, 2 * LANES), F32),
                        pltpu.VMEM((DSA_H // 2, DSA_KV_LAT + KVT_ONES, 2 * LANES), F32)],
        compiler_params=_params(("parallel", "arbitrary")),
        name="dsa_attn",
    )(ki, kv, kvt, qi, q, wi_t, bias_tiles, wuv)


def _t5_bucket(rel):
    n = jnp.maximum(rel, 0)
    max_exact = T5_BUCKETS // 2
    nf = jnp.maximum(n, 1).astype(F32)
    large = max_exact + (jnp.log(nf / max_exact) / math.log(T5_MAX_DIST / max_exact)
                         * (T5_BUCKETS - max_exact)).astype(I32)
    large = jnp.minimum(large, T5_BUCKETS - 1)
    return jnp.where(n < max_exact, n, large)


def _t5_tiles(t5_bias):
    s = jnp.arange(LANES, dtype=I32)[:, None]
    t = jnp.arange(LANES, dtype=I32)[None, :]
    tiles = []
    for shift in (0, LANES, 2 * LANES):
        tiles.append(t5_bias[_t5_bucket(shift + t - s)])
    tiles = jnp.stack(tiles, axis=0)
    tiles = tiles.transpose(3, 0, 1, 2).reshape(DSA_H // 2, 2, 3, LANES, LANES)
    return (tiles.transpose(0, 2, 3, 1, 4).reshape(DSA_H // 2, 3, LANES, 2 * LANES) * LOG2E).astype(F32)


def _pack_in_proj(w_in, b_in):
    splits = np.cumsum([GLA_H * GLA_DK, GLA_H * GLA_DK, GLA_H * GLA_DV, GLA_H * GLA_DV, GLA_LR,
                        DSA_Q_LORA, DSA_KV_LAT, IDX_D, IDX_H,
                        ML_H * ML_DK, ML_H * ML_DK, ML_H * ML_DV, ML_H * ML_DV, ML_H, ML_H,
                        D_MODEL, D_MODEL])[:].tolist()

    def pack(a):
        (gq, gk, gv, gr, ga, cq, ckv, ik, iw, mq, mk, mv, mo, mi, mf, a_, b_, c_) = jnp.split(a, splits, axis=-1)
        pad = lambda n: jnp.zeros(a.shape[:-1] + (n,), a.dtype)
        lo = jnp.concatenate([gq, gk, gv, gr, mq, mk, mv, mo], axis=-1)
        hi = jnp.concatenate([a_, b_, c_, cq, ckv,
                              ga, pad(LANES - GLA_LR),
                              ik, iw, pad(LANES - IDX_D - IDX_H),
                              mi, mf, pad(LANES - 2 * ML_H)], axis=-1)
        return lo, hi

    return pack(w_in), pack(b_in)


def kernel(x, meta_tokens, t5_bias, g_pre_mix, w_in, b_in, w_gla_a2, b_gla_a, g_gla_head, w_br_gla, g_dsa_q, w_dsa_uq, w_idx_q, g_dsa_kv, w_dsa_uv, idx_ln_g, idx_ln_b, w_br_dsa, ml_conv, ml_f_bias, g_ml_head, w_br_ml, w_out, g_post_mix, g_pre_mlp, w_up, w_down, g_post_mlp):
    B, S, D = x.shape
    L = w_in.shape[0]
    T = S + N_META
    front = (-T) % LANES
    Tp = T + front
    R = B * Tp
    nblk = Tp // LANES
    topk = min(TOPK_MAX, (T - N_META) // 4)
    tm_mid = 512 if R % 512 == 0 else LANES
    tm_small = 256 if R % 256 == 0 else LANES

    (w_lo, w_hi), (b_lo, b_hi) = _pack_in_proj(w_in, b_in)
    wa2 = jnp.pad(w_gla_a2, ((0, 0), (0, LANES - GLA_LR), (0, 0)))
    wa2 = wa2.reshape(L, LANES, GLA_H, GLA_DK).transpose(0, 2, 1, 3).astype(MXU_DT)
    fb_row = jnp.pad(ml_f_bias, ((0, 0), (ML_H, LANES - 2 * ML_H)))[:, None, :]
    layers = dict(
        g_pre_mix=g_pre_mix[:, None, :], w_lo=w_lo.astype(MXU_DT), b_lo=b_lo[:, None, :],
        w_hi=w_hi.astype(MXU_DT), b_hi=b_hi[:, None, :],
        wa2=wa2, ba=b_gla_a.reshape(L, GLA_H, 1, GLA_DK), g_gla_head=g_gla_head[:, None, :],
        w_br_gla=w_br_gla.astype(MXU_DT),
        g_dsa_q=g_dsa_q[:, None, :], w_dsa_uq=w_dsa_uq.astype(MXU_DT),
        w_idx_q=w_idx_q.reshape(L, DSA_Q_LORA, IDX_H, IDX_D).transpose(0, 2, 1, 3).astype(MXU_DT),
        g_dsa_kv=g_dsa_kv[:, None, :], w_dsa_uv=w_dsa_uv.astype(MXU_DT),
        idx_ln_g=idx_ln_g[:, None, :], idx_ln_b=idx_ln_b[:, None, :], w_br_dsa=w_br_dsa.astype(MXU_DT),
        ml_conv=ml_conv, fb_row=fb_row, ml_f_bias=ml_f_bias, g_ml_head=g_ml_head[:, None, :],
        w_br_ml=w_br_ml.astype(MXU_DT), w_out=w_out.astype(MXU_DT), g_post_mix=g_post_mix[:, None, :],
        g_pre_mlp=g_pre_mlp[:, None, :], w_up=w_up.astype(MXU_DT), w_down=w_down.astype(MXU_DT),
        g_post_mlp=g_post_mlp[:, None, :],
    )
    bias_tiles = _t5_tiles(t5_bias)
    zero_ff = jnp.zeros((1, D_FF), F32)

    meta = jnp.broadcast_to(meta_tokens.astype(x.dtype)[None], (B, N_META, D))
    h0 = jnp.concatenate([jnp.zeros((B, front, D), x.dtype), meta, x], axis=1).reshape(R, D)

    def layer(h, p):
        zl = _norm_matmul(h, p["g_pre_mix"], p["w_lo"], p["b_lo"], tm=tm_mid, tn=1024, relu2=False,
                          out_dtype=ACT_DT, name="in_proj_lo").reshape(B, Tp, NZL)
        zh = _norm_matmul(h, p["g_pre_mix"], p["w_hi"], p["b_hi"], tm=tm_mid, tn=768, relu2=False,
                          out_dtype=F32, name="in_proj_hi")
        zh3 = zh.reshape(B, Tp, NZH)
        o_gla = _gla(zl, zh3, p["wa2"], p["ba"], p["g_gla_head"], front=front)
        gates_row = zh3[:, :, Z_SM2:Z_SM2 + 2 * ML_H].transpose(0, 2, 1).reshape(B, 2 * ML_H, nblk, LANES)
        o_ml = _mlstm(zl, zh3, gates_row, p["ml_conv"], p["fb_row"], p["ml_f_bias"], p["g_ml_head"], front=front)
        q, qi, kv, ki = _dsa_prep(zh3, p["g_dsa_q"], p["g_dsa_kv"], p["idx_ln_g"], p["idx_ln_b"],
                                  p["w_dsa_uq"], p["w_idx_q"])
        kvt = kv.reshape(B, nblk, LANES, DSA_KV_LAT).transpose(0, 1, 3, 2)
        kvt = jnp.concatenate([kvt, jnp.ones((B, nblk, KVT_ONES, LANES), kvt.dtype)], axis=2)
        wi_t = zh3[:, :, Z_SM1 + IDX_D:Z_SM1 + IDX_D + IDX_H].transpose(0, 2, 1)
        o_dsa = _dsa(ki, kv, kvt, qi, q, wi_t, bias_tiles, p["w_dsa_uv"], front=front, topk=topk)
        h = _mix(o_gla.reshape(R, D), o_dsa.reshape(R, D), o_ml.reshape(R, D), zh,
                 p["w_br_gla"], p["w_br_dsa"], p["w_br_ml"], p["w_out"], p["g_post_mix"], h, tm=tm_small)
        u = _norm_matmul(h, p["g_pre_mlp"], p["w_up"], zero_ff, tm=tm_mid, tn=1024,
                         relu2=True, out_dtype=ACT_DT, name="mlp_up")
        h = _matmul_norm_res(u, p["w_down"], p["g_post_mlp"], h, tm=tm_mid)
        return h, None

    h, _ = lax.scan(layer, h0, layers)
    return h.reshape(B, Tp, D)[:, front + N_META:]
```

```python
import functools
import math

import numpy as np
import jax
import jax.numpy as jnp
from jax import lax
from jax.experimental import pallas as pl
from jax.experimental.pallas import tpu as pltpu

D_MODEL = 1024
N_META = 16
GLA_H, GLA_DK, GLA_DV, GLA_LR, GLA_TAU = 4, 128, 256, 16, 16.0
DSA_H, DSA_DH, DSA_Q_LORA, DSA_KV_LAT = 16, 64, 256, 128
IDX_H, IDX_D, TOPK_MAX = 8, 64, 256
ML_H, ML_DK, ML_DV, ML_CONV = 4, 128, 256, 4
T5_BUCKETS, T5_MAX_DIST = 32, 128
D_FF = 4 * D_MODEL
EPS = 1e-6
NEG = -1e30
LOG2E = math.log2(math.e)

LANES = 128
SUBLANES = 8
VMEM_LIMIT = 56 * 1024 * 1024

MXU_DT = jnp.bfloat16
ACT_DT = jnp.bfloat16

F32 = jnp.float32
I32 = jnp.int32
INT_MIN = -2147483648
HI = lax.Precision.HIGHEST
NT_DIMS = (((1,), (1,)), ((), ()))

Z_GLA_Q, Z_GLA_K, Z_GLA_V, Z_GLA_R = 0, 512, 1024, 2048
Z_ML_Q, Z_ML_K, Z_ML_V, Z_ML_O = 3072, 3584, 4096, 5120
NZL = 6144
Z_GATE_A, Z_GATE_B, Z_GATE_C = 0, 1024, 2048
Z_CQ, Z_CKV = 3072, 3328
Z_SM0, Z_SM1, Z_SM2 = 3456, 3584, 3712
NZH = 3840
KVT_ONES = 16
KB_GROUP = 4
MIX_CHUNKS = 6


def _params(sem):
    return pltpu.CompilerParams(dimension_semantics=sem, vmem_limit_bytes=VMEM_LIMIT)


def _mm(a, b):
    return jnp.dot(a.astype(MXU_DT), b.astype(MXU_DT), preferred_element_type=F32)


def _mm_nt(a, b):
    return lax.dot_general(a.astype(MXU_DT), b.astype(MXU_DT), NT_DIMS, preferred_element_type=F32)


def _rms(x, g):
    return x * lax.rsqrt(jnp.mean(x * x, axis=-1, keepdims=True) + EPS) * g


def _log_sigmoid(x):
    return jnp.minimum(x, 0.0) - jnp.log1p(jnp.exp(-jnp.abs(x)))


def _sigmoid(x):
    return 1.0 / (1.0 + jnp.exp(-x))


def _norm_matmul_kernel(x_ref, g_ref, w_ref, b_ref, o_ref, *, relu2, tn):
    xn = _rms(x_ref[...], g_ref[...]).astype(MXU_DT)
    for n0 in range(0, o_ref.shape[1], tn):
        acc = jnp.dot(xn, w_ref[:, n0:n0 + tn], preferred_element_type=F32) + b_ref[:, n0:n0 + tn]
        if relu2:
            acc = jnp.square(jnp.maximum(acc, 0.0))
        o_ref[:, n0:n0 + tn] = acc.astype(o_ref.dtype)


def _norm_matmul(x, g, w, b, *, tm, tn, relu2, out_dtype, name):
    R, K = x.shape
    N = w.shape[1]
    return pl.pallas_call(
        functools.partial(_norm_matmul_kernel, relu2=relu2, tn=tn),
        grid=(R // tm,),
        in_specs=[pl.BlockSpec((tm, K), lambda i: (i, 0)),
                  pl.BlockSpec((1, K), lambda i: (0, 0)),
                  pl.BlockSpec((K, N), lambda i: (0, 0)),
                  pl.BlockSpec((1, N), lambda i: (0, 0))],
        out_specs=pl.BlockSpec((tm, N), lambda i: (i, 0)),
        out_shape=jax.ShapeDtypeStruct((R, N), out_dtype),
        compiler_params=_params(("parallel",)),
        name=name,
    )(x, g, w, b)


def _matmul_norm_res_kernel(u_ref, w_ref, g_ref, h_ref, o_ref):
    y = jnp.dot(u_ref[...], w_ref[...], preferred_element_type=F32)
    o_ref[...] = h_ref[...] + _rms(y, g_ref[...])


def _matmul_norm_res(u, w, g, h, *, tm):
    R, K = u.shape
    N = w.shape[1]
    return pl.pallas_call(
        _matmul_norm_res_kernel,
        grid=(R // tm,),
        in_specs=[pl.BlockSpec((tm, K), lambda i: (i, 0)),
                  pl.BlockSpec((K, N), lambda i: (0, 0)),
                  pl.BlockSpec((1, N), lambda i: (0, 0)),
                  pl.BlockSpec((tm, N), lambda i: (i, 0))],
        out_specs=pl.BlockSpec((tm, N), lambda i: (i, 0)),
        out_shape=jax.ShapeDtypeStruct((R, N), F32),
        compiler_params=_params(("parallel",)),
        name="mlp_down",
    )(u, w, g, h)


def _mix_kernel(oa_ref, ob_ref, oc_ref, ga_ref, gb_ref, gc_ref, wa_ref, wb_ref, wc_ref,
                wo_ref, g_ref, h_ref, o_ref):
    ya = jnp.dot(oa_ref[...], wa_ref[...], preferred_element_type=F32)
    yb = jnp.dot(ob_ref[...], wb_ref[...], preferred_element_type=F32)
    yc = jnp.dot(oc_ref[...], wc_ref[...], preferred_element_type=F32)
    mix = _sigmoid(ga_ref[...]) * ya + _sigmoid(gb_ref[...]) * yb + _sigmoid(gc_ref[...]) * yc
    y = _mm(mix, wo_ref[...])
    o_ref[...] = h_ref[...] + _rms(y, g_ref[...])


def _mix(oa, ob, oc, z, wa, wb, wc, wo, g, h, *, tm):
    R, D = h.shape
    row = lambda i: (i, 0)
    const = lambda i: (0, 0)
    gate = lambda off: pl.BlockSpec((tm, D), lambda i: (i, off // D))
    return pl.pallas_call(
        _mix_kernel,
        grid=(R // tm,),
        in_specs=[pl.BlockSpec((tm, D), row), pl.BlockSpec((tm, D), row), pl.BlockSpec((tm, D), row),
                  gate(Z_GATE_A), gate(Z_GATE_B), gate(Z_GATE_C),
                  pl.BlockSpec((D, D), const), pl.BlockSpec((D, D), const), pl.BlockSpec((D, D), const),
                  pl.BlockSpec((D, D), const), pl.BlockSpec((1, D), const), pl.BlockSpec((tm, D), row)],
        out_specs=pl.BlockSpec((tm, D), row),
        out_shape=jax.ShapeDtypeStruct((R, D), F32),
        compiler_params=_params(("parallel",)),
        name="mix_out",
    )(oa, ob, oc, z, z, z, wa, wb, wc, wo, g, h)


def _gla_kernel(q_ref, k_ref, v_ref, r_ref, a_ref, wa_ref, ba_ref, gh_ref, o_ref, s_ref, *, front, nchunks):
    C = LANES
    tb = pl.program_id(1)

    @pl.when(tb == 0)
    def _():
        s_ref[...] = jnp.zeros_like(s_ref)

    r0 = lax.broadcasted_iota(I32, (C, C), 0)
    r1 = lax.broadcasted_iota(I32, (C, C), 1)
    causal = r0 >= r1
    tril = causal.astype(F32)
    rowi = lax.broadcasted_iota(I32, (C, 1), 0)

    def body(c, carry):
        sl = pl.ds(pl.multiple_of(c * C, C), C)
        live = ((tb * MIX_CHUNKS + c) * C + rowi) >= front
        a = a_ref[0, sl, :].astype(MXU_DT)
        H = range(GLA_H)
        kq = [slice(h * GLA_DK, (h + 1) * GLA_DK) for h in H]
        kv = [slice(h * GLA_DV, (h + 1) * GLA_DV) for h in H]
        g = [_log_sigmoid(jnp.dot(a, wa_ref[h], preferred_element_type=F32) + ba_ref[h]) / GLA_TAU for h in H]
        G = [jnp.dot(tril, g[h], precision=HI, preferred_element_type=F32) for h in H]
        q = [q_ref[0, sl, kq[h]].astype(F32) * (GLA_DK ** -0.5) for h in H]
        k = [jnp.where(live, k_ref[0, sl, kq[h]].astype(F32), 0.0) for h in H]
        v = [jnp.where(live, v_ref[0, sl, kv[h]].astype(F32), 0.0).astype(MXU_DT) for h in H]
        S = [s_ref[h] for h in H]
        g_last = [G[h][C - 1:C, :] for h in H]
        g_mid = [G[h][C // 2 - 1:C // 2, :] for h in H]
        o_inter = [_mm(q[h] * jnp.exp(G[h]), S[h]) for h in H]
        A = [jnp.where(causal, _mm_nt(q[h] * jnp.exp(G[h] - g_mid[h]), k[h] * jnp.exp(g_mid[h] - G[h])), 0.0) for h in H]
        o_intra = [_mm(A[h], v[h]) for h in H]
        kd_t = [jnp.transpose(k[h] * jnp.exp(g_last[h] - G[h])) for h in H]
        dec_col = [jnp.transpose(jnp.broadcast_to(jnp.exp(g_last[h]), (C, GLA_DK))) for h in H]
        for h in H:
            s_ref[h] = jnp.concatenate([dec_col[h], dec_col[h]], axis=1) * S[h] + _mm(kd_t[h], v[h])
        for h in H:
            r = r_ref[0, sl, kv[h]].astype(F32)
            o = _rms(o_inter[h] + o_intra[h], gh_ref[...]) * (r * _sigmoid(r))
            o_ref[0, sl, kv[h]] = o.astype(o_ref.dtype)
        return carry

    lax.fori_loop(0, jnp.minimum(MIX_CHUNKS, nchunks - tb * MIX_CHUNKS), body, 0)


def _gla(zl, zh, wa, ba, gh, *, front):
    B, Tp, _ = zl.shape
    nchunks = Tp // LANES
    TB = min(MIX_CHUNKS * LANES, Tp)
    WK, WV = GLA_H * GLA_DK, GLA_H * GLA_DV
    return pl.pallas_call(
        functools.partial(_gla_kernel, front=front, nchunks=nchunks),
        grid=(B, pl.cdiv(Tp, TB)),
        in_specs=[pl.BlockSpec((1, TB, WK), lambda b, t: (b, t, Z_GLA_Q // WK)),
                  pl.BlockSpec((1, TB, WK), lambda b, t: (b, t, Z_GLA_K // WK)),
                  pl.BlockSpec((1, TB, WV), lambda b, t: (b, t, Z_GLA_V // WV)),
                  pl.BlockSpec((1, TB, WV), lambda b, t: (b, t, Z_GLA_R // WV)),
                  pl.BlockSpec((1, TB, LANES), lambda b, t: (b, t, Z_SM0 // LANES)),
                  pl.BlockSpec((GLA_H, LANES, GLA_DK), lambda b, t: (0, 0, 0)),
                  pl.BlockSpec((GLA_H, 1, GLA_DK), lambda b, t: (0, 0, 0)),
                  pl.BlockSpec((1, GLA_DV), lambda b, t: (0, 0))],
        out_specs=pl.BlockSpec((1, TB, WV), lambda b, t: (b, t, 0)),
        out_shape=jax.ShapeDtypeStruct((B, Tp, WV), ACT_DT),
        scratch_shapes=[pltpu.VMEM((GLA_H, GLA_DK, GLA_DV), F32)],
        compiler_params=_params(("parallel", "arbitrary")),
        name="gla",
    )(zl, zl, zl, zl, zh, wa, ba, gh)


def _mlstm_kernel(q_ref, k_ref, v_ref, og_ref, gcol_ref, grow_ref, cw_ref, fbrow_ref, fb_ref, gh_ref, o_ref,
                  s_ref, m_ref, pq_ref, pk_ref, *, front, nchunks):
    C = LANES
    WK = ML_H * ML_DK
    tb = pl.program_id(1)

    @pl.when(tb == 0)
    def _():
        s_ref[...] = jnp.zeros_like(s_ref)
        m_ref[...] = jnp.zeros_like(m_ref)
        pq_ref[...] = jnp.zeros_like(pq_ref)
        pk_ref[...] = jnp.zeros_like(pk_ref)

    r0 = lax.broadcasted_iota(I32, (C, C), 0)
    r1 = lax.broadcasted_iota(I32, (C, C), 1)
    causal = r0 >= r1
    tril = causal.astype(F32)
    triu = (r0 <= r1).astype(F32)
    rowi = lax.broadcasted_iota(I32, (C, 1), 0)

    def conv_silu(x, tail, w):
        acc = x * w[ML_CONV - 1:ML_CONV, :]
        for d in range(1, ML_CONV):
            sh = pltpu.roll(x, d, 0)
            head = jnp.where(rowi[:SUBLANES] >= d, sh[:SUBLANES], pltpu.roll(tail, d, 0))
            sh = jnp.concatenate([head, sh[SUBLANES:]], axis=0)
            acc = acc + sh * w[ML_CONV - 1 - d:ML_CONV - d, :]
        return acc * _sigmoid(acc)

    def body(c, carry):
        sl = pl.ds(pl.multiple_of(c * C, C), C)
        cg = tb * MIX_CHUNKS + c
        live = (cg * C + rowi) >= front
        xq = jnp.where(live, q_ref[0, sl, :].astype(F32), 0.0)
        xk = jnp.where(live, k_ref[0, sl, :].astype(F32), 0.0)
        q_all = conv_silu(xq, pq_ref[...], cw_ref[:, :WK]) * (ML_DK ** -0.5)
        k_all = conv_silu(xk, pk_ref[...], cw_ref[:, WK:])
        pq_ref[...] = xq[C - SUBLANES:, :]
        pk_ref[...] = xk[C - SUBLANES:, :]
        lf_col = _log_sigmoid(gcol_ref[0, sl, :] + fbrow_ref[...])
        H = range(ML_H)
        vs = [slice(h * ML_DV, (h + 1) * ML_DV) for h in H]
        q = [q_all[:, h * ML_DK:(h + 1) * ML_DK].astype(MXU_DT) for h in H]
        k = [k_all[:, h * ML_DK:(h + 1) * ML_DK] for h in H]
        ones = jnp.ones((C, LANES), MXU_DT)
        v_aug = [jnp.concatenate([jnp.where(live, v_ref[0, sl, vs[h]].astype(F32), 0.0).astype(MXU_DT), ones], axis=1)
                 for h in H]
        m_prev = [m_ref[h, 0:1, :] for h in H]
        S = [s_ref[h] for h in H]
        sel = [jnp.dot(lf_col, (r0 == ML_H + h).astype(F32), precision=HI, preferred_element_type=F32) for h in H]
        b_bc = [jnp.dot(tril, sel[h], precision=HI, preferred_element_type=F32) for h in H]
        li_row = [grow_ref[0, h, pl.ds(cg, 1), :] for h in H]
        lf_row = [_log_sigmoid(grow_ref[0, ML_H + h, pl.ds(cg, 1), :] + fb_ref[h]) for h in H]
        b_row = [jnp.dot(lf_row[h], triu, precision=HI, preferred_element_type=F32) for h in H]
        b_last = [b_row[h][:, C - 1:C] for h in H]
        qk = [_mm_nt(q[h], k[h]) for h in H]
        qs = [_mm(q[h], S[h]) for h in H]
        Dm = [jnp.where(causal, b_bc[h] - (b_row[h] - li_row[h]), -jnp.inf) for h in H]
        m_inter = [b_bc[h] + m_prev[h] for h in H]
        m = [jnp.maximum(m_inter[h], jnp.max(Dm[h], axis=1, keepdims=True)) for h in H]
        pv = [_mm(jnp.exp(Dm[h] - m[h]) * qk[h], v_aug[h]) for h in H]
        dec = [b_last[h] - b_row[h] + li_row[h] for h in H]
        m_new = [jnp.maximum(b_last[h] + m_prev[h], jnp.max(dec[h], axis=1, keepdims=True)) for h in H]
        kt_w = [jnp.transpose(k[h]) * jnp.exp(dec[h] - m_new[h]) for h in H]
        for h in H:
            w_old = jnp.exp(b_last[h] + m_prev[h] - m_new[h])
            s_ref[h] = jnp.concatenate([w_old, w_old, w_old], axis=1) * S[h] + _mm(kt_w[h], v_aug[h])
            m_ref[h] = jnp.broadcast_to(m_new[h], (SUBLANES, C))
        for h in H:
            w_inter = jnp.exp(m_inter[h] - m[h])
            num = jnp.concatenate([w_inter, w_inter, w_inter], axis=1) * qs[h] + pv[h]
            den = num[:, ML_DV:]
            dd = jnp.maximum(jnp.abs(den), jnp.exp(-m[h]))
            hval = num[:, :ML_DV] / jnp.concatenate([dd, dd], axis=1)
            og = og_ref[0, sl, vs[h]].astype(F32)
            o_ref[0, sl, vs[h]] = (_rms(hval, gh_ref[...]) * _sigmoid(og)).astype(o_ref.dtype)
        return carry

    lax.fori_loop(0, jnp.minimum(MIX_CHUNKS, nchunks - tb * MIX_CHUNKS), body, 0)


def _mlstm(zl, zh, gates_row, conv_w, fb_row, fb, gh, *, front):
    B, Tp, _ = zl.shape
    nchunks = Tp // LANES
    TB = min(MIX_CHUNKS * LANES, Tp)
    WK, WV = ML_H * ML_DK, ML_H * ML_DV
    return pl.pallas_call(
        functools.partial(_mlstm_kernel, front=front, nchunks=nchunks),
        grid=(B, pl.cdiv(Tp, TB)),
        in_specs=[pl.BlockSpec((1, TB, WK), lambda b, t: (b, t, Z_ML_Q // WK)),
                  pl.BlockSpec((1, TB, WK), lambda b, t: (b, t, Z_ML_K // WK)),
                  pl.BlockSpec((1, TB, WV), lambda b, t: (b, t, Z_ML_V // WV)),
                  pl.BlockSpec((1, TB, WV), lambda b, t: (b, t, Z_ML_O // WV)),
                  pl.BlockSpec((1, TB, LANES), lambda b, t: (b, t, Z_SM2 // LANES)),
                  pl.BlockSpec((1, 2 * ML_H, nchunks, LANES), lambda b, t: (b, 0, 0, 0)),
                  pl.BlockSpec((ML_CONV, 2 * WK), lambda b, t: (0, 0)),
                  pl.BlockSpec((1, LANES), lambda b, t: (0, 0)),
                  pl.BlockSpec(memory_space=pltpu.SMEM),
                  pl.BlockSpec((1, ML_DV), lambda b, t: (0, 0))],
        out_specs=pl.BlockSpec((1, TB, WV), lambda b, t: (b, t, 0)),
        out_shape=jax.ShapeDtypeStruct((B, Tp, WV), ACT_DT),
        scratch_shapes=[pltpu.VMEM((ML_H, ML_DK, ML_DV + LANES), F32), pltpu.VMEM((ML_H, SUBLANES, LANES), F32),
                        pltpu.VMEM((SUBLANES, WK), F32), pltpu.VMEM((SUBLANES, WK), F32)],
        compiler_params=_params(("parallel", "arbitrary")),
        name="mlstm",
    )(zl, zl, zl, zl, zh, gates_row, conv_w, fb_row, fb, gh)


def _dsa_prep_kernel(cq_ref, ckv_ref, sm_ref, gq_ref, gkv_ref, lng_ref, lnb_ref, wuq_ref, wiq_ref,
                     q_ref, qi_ref, kv_ref, ki_ref):
    cq = _rms(cq_ref[0], gq_ref[...]).astype(MXU_DT)
    qf = jnp.dot(cq, wuq_ref[...], preferred_element_type=F32) * (DSA_KV_LAT ** -0.5 * LOG2E)
    for hh in range(DSA_H):
        q_ref[0, hh] = qf[:, hh * DSA_KV_LAT:(hh + 1) * DSA_KV_LAT].astype(q_ref.dtype)
    for hh in range(IDX_H):
        qi = jnp.dot(cq, wiq_ref[hh], preferred_element_type=F32) * (IDX_D ** -0.5)
        qi_ref[0, hh] = qi.astype(qi_ref.dtype)
    kv_ref[0] = _rms(ckv_ref[0], gkv_ref[...]).astype(kv_ref.dtype)
    x = sm_ref[0][:, :IDX_D]
    mu = jnp.mean(x, axis=-1, keepdims=True)
    var = jnp.mean(jnp.square(x - mu), axis=-1, keepdims=True)
    ki_ref[0] = ((x - mu) * lax.rsqrt(var + EPS) * lng_ref[...] + lnb_ref[...]).astype(ki_ref.dtype)


def _dsa_prep(z, gq, gkv, lng, lnb, wuq, wiq):
    B, Tp, _ = z.shape
    tm = LANES
    c2 = lambda b, i: (0, 0)
    return pl.pallas_call(
        _dsa_prep_kernel,
        grid=(B, Tp // tm),
        in_specs=[pl.BlockSpec((1, tm, DSA_Q_LORA), lambda b, i: (b, i, Z_CQ // DSA_Q_LORA)),
                  pl.BlockSpec((1, tm, DSA_KV_LAT), lambda b, i: (b, i, Z_CKV // DSA_KV_LAT)),
                  pl.BlockSpec((1, tm, LANES), lambda b, i: (b, i, Z_SM1 // LANES)),
                  pl.BlockSpec((1, DSA_Q_LORA), c2), pl.BlockSpec((1, DSA_KV_LAT), c2),
                  pl.BlockSpec((1, IDX_D), c2), pl.BlockSpec((1, IDX_D), c2),
                  pl.BlockSpec((DSA_Q_LORA, DSA_H * DSA_KV_LAT), c2),
                  pl.BlockSpec((IDX_H, DSA_Q_LORA, IDX_D), lambda b, i: (0, 0, 0))],
        out_specs=[pl.BlockSpec((1, DSA_H, tm, DSA_KV_LAT), lambda b, i: (b, 0, i, 0)),
                   pl.BlockSpec((1, IDX_H, tm, IDX_D), lambda b, i: (b, 0, i, 0)),
                   pl.BlockSpec((1, tm, DSA_KV_LAT), lambda b, i: (b, i, 0)),
                   pl.BlockSpec((1, tm, IDX_D), lambda b, i: (b, i, 0))],
        out_shape=[jax.ShapeDtypeStruct((B, DSA_H, Tp, DSA_KV_LAT), ACT_DT),
                   jax.ShapeDtypeStruct((B, IDX_H, Tp, IDX_D), ACT_DT),
                   jax.ShapeDtypeStruct((B, Tp, DSA_KV_LAT), ACT_DT),
                   jax.ShapeDtypeStruct((B, Tp, IDX_D), ACT_DT)],
        compiler_params=_params(("parallel", "parallel")),
        name="dsa_prep",
    )(z, z, z, gq, gkv, lng, lnb, wuq, wiq)


def _sortable(x):
    bits = pltpu.bitcast(x, I32)
    key = jnp.where(bits < 0, bits ^ 0x7FFFFFFF, bits)
    return jnp.where(x == 0.0, 0, key)


_KEY_NEG = int(np.array(NEG, np.float32).view(np.int32)) ^ 0x7FFFFFFF


def _dsa_kernel(ki_ref, kv_ref, kvt_ref, qi_ref, q_ref, wi_ref, bias_ref, wuv_ref, o_ref,
                key_ref, madd_ref, lg_ref, acc_ref, *, front, topk, Tp):
    QB = LANES
    j = pl.program_id(1)
    nkb = j + 1
    ngrp = nkb // KB_GROUP
    s_loc = lax.broadcasted_iota(I32, (QB, QB), 0)
    t_glob = j * QB + lax.broadcasted_iota(I32, (QB, QB), 1)
    wi = wi_ref[0] * (IDX_H ** -0.5)
    n_rest = Tp - nkb * QB

    def blk(kb):
        return pl.ds(pl.multiple_of(kb * QB, QB), QB)

    def score_block(kb):
        ki = ki_ref[0, blk(kb), :]
        acc = jnp.zeros((QB, QB), F32)
        for hp in range(IDX_H // 2):
            qi2 = qi_ref[0, 2 * hp:2 * hp + 2].reshape(2 * QB, IDX_D)
            s = lax.dot_general(ki, qi2, NT_DIMS, preferred_element_type=F32)
            acc = acc + jnp.maximum(s[:, :QB], 0.0) * wi[2 * hp:2 * hp + 1, :]
            acc = acc + jnp.maximum(s[:, QB:], 0.0) * wi[2 * hp + 1:2 * hp + 2, :]
        s_glob = kb * QB + s_loc
        key = _sortable(jnp.where(s_glob <= t_glob, acc, NEG))
        key_ref[blk(kb), :] = jnp.where(s_glob >= front, key, INT_MIN)

    def score_group(i, c):
        for u in range(KB_GROUP):
            score_block(KB_GROUP * i + u)
        return c

    def score_single(kb, c):
        score_block(kb)
        return c

    lax.fori_loop(0, ngrp, score_group, 0)
    lax.fori_loop(ngrp * KB_GROUP, nkb, score_single, 0)

    def count(pred_fn):
        def fold(hit):
            return jnp.sum(hit.reshape(QB // SUBLANES, SUBLANES, QB), axis=0)

        def hit_of(kb):
            return pred_fn(key_ref[blk(kb), :], kb).astype(I32)

        def group(i, acc):
            hit = hit_of(KB_GROUP * i)
            for u in range(1, KB_GROUP):
                hit = hit + hit_of(KB_GROUP * i + u)
            return acc + fold(hit)

        acc = lax.fori_loop(0, ngrp, group, jnp.zeros((SUBLANES, QB), I32))
        acc = lax.fori_loop(ngrp * KB_GROUP, nkb, lambda kb, a: a + fold(hit_of(kb)), acc)
        return jnp.sum(acc, axis=0, keepdims=True)

    def count_ge(cand):
        return count(lambda kblk, kb: kblk >= cand) + jnp.where(_KEY_NEG >= cand, n_rest, 0)

    base = jnp.where(count_ge(jnp.zeros((1, QB), I32)) >= topk, 0, INT_MIN).astype(I32)

    def bit_body(i, base):
        cand = base | jnp.left_shift(jnp.int32(1), 30 - i)
        return jnp.where(count_ge(cand) >= topk, cand, base)

    tau = lax.fori_loop(0, 31, bit_body, base)

    def valid_of(kb):
        s_glob = kb * QB + s_loc
        return (s_glob >= front) & (s_glob <= t_glob)

    c_gt = count(lambda kblk, kb: kblk > tau) + jnp.where(_KEY_NEG > tau, n_rest, 0)
    c_eq = count(lambda kblk, kb: (kblk == tau) & valid_of(kb))
    need = topk - c_gt

    @pl.when(jnp.max((c_eq > need).astype(I32)) > 0)
    def _():
        lstrict = (s_loc > lax.broadcasted_iota(I32, (QB, QB), 1)).astype(MXU_DT)
        needf = need.astype(F32)

        def tie_body(kb, seen):
            kblk = key_ref[blk(kb), :]
            tie = (kblk == tau) & valid_of(kb)
            tief = tie.astype(F32)
            rank = seen + jnp.dot(lstrict, tief.astype(MXU_DT), preferred_element_type=F32)
            key_ref[blk(kb), :] = jnp.where(tie & (rank >= needf), INT_MIN, kblk)
            return seen + jnp.sum(tief, axis=0, keepdims=True)

        lax.fori_loop(0, nkb, tie_body, jnp.zeros((1, QB), F32))

    def mask_body(kb, c):
        sel = (key_ref[blk(kb), :] >= tau) & valid_of(kb)
        madd_ref[blk(kb), :] = jnp.where(sel, 0.0, NEG)
        return c

    lax.fori_loop(0, nkb, mask_body, 0)

    NP = DSA_H // 2
    grp = (QB // SUBLANES, SUBLANES, 2 * QB)
    acc_ref[...] = jnp.zeros_like(acc_ref)

    def score_inputs(kb):
        ma = madd_ref[blk(kb), :]
        return kv_ref[0, blk(kb), :], jnp.concatenate([ma, ma], axis=1), jnp.minimum(j - kb, 2)

    def stage1(p, slot, inputs):
        kvb, ma2, rel = inputs
        q2 = q_ref[0, 2 * p:2 * p + 2].reshape(2 * QB, DSA_KV_LAT)
        lg = lax.dot_general(kvb, q2, NT_DIMS, preferred_element_type=F32) + bias_ref[p, rel] + ma2
        lg_ref[slot, p] = lg
        return jnp.max(jnp.max(lg.reshape(grp), axis=0), axis=0, keepdims=True)

    def stage2(p, slot, kvt, m_old, bm):
        m_new = jnp.maximum(m_old, bm)
        pm = jnp.exp2(lg_ref[slot, p] - m_new)
        acc_ref[p] = (jnp.exp2(m_old - m_new) * acc_ref[p]
                      + jnp.dot(kvt, pm.astype(MXU_DT), preferred_element_type=F32))
        return m_new

    inp0 = score_inputs(0)
    bm0 = tuple(stage1(p, 0, inp0) for p in range(NP))

    def attn_body(i, carry):
        ms, bm_a = carry
        a = 2 * i
        inp_b = score_inputs(a + 1)
        inp_c = score_inputs(jnp.minimum(a + 2, nkb - 1))
        kvt_a = kvt_ref[0, a]
        kvt_b = kvt_ref[0, a + 1]
        ms_a, bm_b = [], []
        for p in range(NP):
            bm_b.append(stage1(p, 1, inp_b))
            ms_a.append(stage2(p, 0, kvt_a, ms[p], bm_a[p]))
        ms_b, bm_c = [], []
        for p in range(NP):
            bm_c.append(stage1(p, 0, inp_c))
            ms_b.append(stage2(p, 1, kvt_b, ms_a[p], bm_b[p]))
        return tuple(ms_b), tuple(bm_c)

    ms, bm_last = lax.fori_loop(0, nkb // 2, attn_body,
                                (tuple(jnp.full((1, 2 * QB), -jnp.inf, F32) for _ in range(NP)), bm0))

    @pl.when(nkb % 2 == 1)
    def _():
        kvt = kvt_ref[0, nkb - 1]
        for p in range(NP):
            stage2(p, 0, kvt, ms[p], bm_last[p])

    for p in range(NP):
        acc = acc_ref[p]
        o_t = acc[:DSA_KV_LAT] / acc[DSA_KV_LAT:DSA_KV_LAT + 1]
        outs = []
        for hh in range(2):
            o_h = jnp.transpose(o_t[:, hh * QB:(hh + 1) * QB])
            outs.append(_mm(o_h, wuv_ref[2 * p + hh]))
        o_ref[0, :, 2 * p * DSA_DH:(2 * p + 2) * DSA_DH] = jnp.concatenate(outs, axis=1).astype(o_ref.dtype)


def _dsa(ki, kv, kvt, qi, q, wi_t, bias_tiles, wuv, *, front, topk):
    B, Tp, _ = kv.shape
    nblk = Tp // LANES
    return pl.pallas_call(
        functools.partial(_dsa_kernel, front=front, topk=topk, Tp=Tp),
        grid=(B, nblk),
        in_specs=[pl.BlockSpec((1, Tp, IDX_D), lambda b, j: (b, 0, 0)),
                  pl.BlockSpec((1, Tp, DSA_KV_LAT), lambda b, j: (b, 0, 0)),
                  pl.BlockSpec((1, nblk, DSA_KV_LAT + KVT_ONES, LANES), lambda b, j: (b, 0, 0, 0)),
                  pl.BlockSpec((1, IDX_H, LANES, IDX_D), lambda b, j: (b, 0, j, 0)),
                  pl.BlockSpec((1, DSA_H, LANES, DSA_KV_LAT), lambda b, j: (b, 0, j, 0)),
                  pl.BlockSpec((1, IDX_H, LANES), lambda b, j: (b, 0, j)),
                  pl.BlockSpec((DSA_H // 2, 3, LANES, 2 * LANES), lambda b, j: (0, 0, 0, 0)),
                  pl.BlockSpec((DSA_H, DSA_KV_LAT, DSA_DH), lambda b, j: (0, 0, 0))],
        out_specs=pl.BlockSpec((1, LANES, DSA_H * DSA_DH), lambda b, j: (b, j, 0)),
        out_shape=jax.ShapeDtypeStruct((B, Tp, DSA_H * DSA_DH), ACT_DT),
        scratch_shapes=[pltpu.VMEM((Tp, LANES), I32), pltpu.VMEM((Tp, LANES), F32),
                        pltpu.VMEM((2, DSA_H // 2, LANES, 2 * LANES), F32),
                        pltpu.VMEM((DSA_H // 2, DSA_KV_LAT + KVT_ONES, 2 * LANES), F32)],
        compiler_params=_params(("parallel", "arbitrary")),
        name="dsa_attn",
    )(ki, kv, kvt, qi, q, wi_t, bias_tiles, wuv)


def _t5_bucket(rel):
    n = jnp.maximum(rel, 0)
    max_exact = T5_BUCKETS // 2
    nf = jnp.maximum(n, 1).astype(F32)
    large = max_exact + (jnp.log(nf / max_exact) / math.log(T5_MAX_DIST / max_exact)
                         * (T5_BUCKETS - max_exact)).astype(I32)
    large = jnp.minimum(large, T5_BUCKETS - 1)
    return jnp.where(n < max_exact, n, large)


def _t5_tiles(t5_bias):
    s = jnp.arange(LANES, dtype=I32)[:, None]
    t = jnp.arange(LANES, dtype=I32)[None, :]
    tiles = []
    for shift in (0, LANES, 2 * LANES):
        tiles.append(t5_bias[_t5_bucket(shift + t - s)])
    tiles = jnp.stack(tiles, axis=0)
    tiles = tiles.transpose(3, 0, 1, 2).reshape(DSA_H // 2, 2, 3, LANES, LANES)
    return (tiles.transpose(0, 2, 3, 1, 4).reshape(DSA_H // 2, 3, LANES, 2 * LANES) * LOG2E).astype(F32)


def _pack_in_proj(w_in, b_in):
    splits = np.cumsum([GLA_H * GLA_DK, GLA_H * GLA_DK, GLA_H * GLA_DV, GLA_H * GLA_DV, GLA_LR,
                        DSA_Q_LORA, DSA_KV_LAT, IDX_D, IDX_H,
                        ML_H * ML_DK, ML_H * ML_DK, ML_H * ML_DV, ML_H * ML_DV, ML_H, ML_H,
                        D_MODEL, D_MODEL])[:].tolist()

    def pack(a):
        (gq, gk, gv, gr, ga, cq, ckv, ik, iw, mq, mk, mv, mo, mi, mf, a_, b_, c_) = jnp.split(a, splits, axis=-1)
        pad = lambda n: jnp.zeros(a.shape[:-1] + (n,), a.dtype)
        lo = jnp.concatenate([gq, gk, gv, gr, mq, mk, mv, mo], axis=-1)
        hi = jnp.concatenate([a_, b_, c_, cq, ckv,
                              ga, pad(LANES - GLA_LR),
                              ik, iw, pad(LANES - IDX_D - IDX_H),
                              mi, mf, pad(LANES - 2 * ML_H)], axis=-1)
        return lo, hi

    return pack(w_in), pack(b_in)


def kernel(x, meta_tokens, t5_bias, g_pre_mix, w_in, b_in, w_gla_a2, b_gla_a, g_gla_head, w_br_gla, g_dsa_q, w_dsa_uq, w_idx_q, g_dsa_kv, w_dsa_uv, idx_ln_g, idx_ln_b, w_br_dsa, ml_conv, ml_f_bias, g_ml_head, w_br_ml, w_out, g_post_mix, g_pre_mlp, w_up, w_down, g_post_mlp):
    B, S, D = x.shape
    L = w_in.shape[0]
    T = S + N_META
    front = (-T) % LANES
    Tp = T + front
    R = B * Tp
    nblk = Tp // LANES
    topk = min(TOPK_MAX, (T - N_META) // 4)
    tm_mid = 512 if R % 512 == 0 else LANES
    tm_small = 256 if R % 256 == 0 else LANES

    (w_lo, w_hi), (b_lo, b_hi) = _pack_in_proj(w_in, b_in)
    wa2 = jnp.pad(w_gla_a2, ((0, 0), (0, LANES - GLA_LR), (0, 0)))
    wa2 = wa2.reshape(L, LANES, GLA_H, GLA_DK).transpose(0, 2, 1, 3).astype(MXU_DT)
    fb_row = jnp.pad(ml_f_bias, ((0, 0), (ML_H, LANES - 2 * ML_H)))[:, None, :]
    layers = dict(
        g_pre_mix=g_pre_mix[:, None, :], w_lo=w_lo.astype(MXU_DT), b_lo=b_lo[:, None, :],
        w_hi=w_hi.astype(MXU_DT), b_hi=b_hi[:, None, :],
        wa2=wa2, ba=b_gla_a.reshape(L, GLA_H, 1, GLA_DK), g_gla_head=g_gla_head[:, None, :],
        w_br_gla=w_br_gla.astype(MXU_DT),
        g_dsa_q=g_dsa_q[:, None, :], w_dsa_uq=w_dsa_uq.astype(MXU_DT),
        w_idx_q=w_idx_q.reshape(L, DSA_Q_LORA, IDX_H, IDX_D).transpose(0, 2, 1, 3).astype(MXU_DT),
        g_dsa_kv=g_dsa_kv[:, None, :], w_dsa_uv=w_dsa_uv.astype(MXU_DT),
        idx_ln_g=idx_ln_g[:, None, :], idx_ln_b=idx_ln_b[:, None, :], w_br_dsa=w_br_dsa.astype(MXU_DT),
        ml_conv=ml_conv, fb_row=fb_row, ml_f_bias=ml_f_bias, g_ml_head=g_ml_head[:, None, :],
        w_br_ml=w_br_ml.astype(MXU_DT), w_out=w_out.astype(MXU_DT), g_post_mix=g_post_mix[:, None, :],
        g_pre_mlp=g_pre_mlp[:, None, :], w_up=w_up.astype(MXU_DT), w_down=w_down.astype(MXU_DT),
        g_post_mlp=g_post_mlp[:, None, :],
    )
    bias_tiles = _t5_tiles(t5_bias)
    zero_ff = jnp.zeros((1, D_FF), F32)

    meta = jnp.broadcast_to(meta_tokens.astype(x.dtype)[None], (B, N_META, D))
    h0 = jnp.concatenate([jnp.zeros((B, front, D), x.dtype), meta, x], axis=1).reshape(R, D)

    def layer(h, p):
        zl = _norm_matmul(h, p["g_pre_mix"], p["w_lo"], p["b_lo"], tm=tm_mid, tn=1024, relu2=False,
                          out_dtype=ACT_DT, name="in_proj_lo").reshape(B, Tp, NZL)
        zh = _norm_matmul(h, p["g_pre_mix"], p["w_hi"], p["b_hi"], tm=tm_mid, tn=768, relu2=False,
                          out_dtype=F32, name="in_proj_hi")
        zh3 = zh.reshape(B, Tp, NZH)
        o_gla = _gla(zl, zh3, p["wa2"], p["ba"], p["g_gla_head"], front=front)
        gates_row = zh3[:, :, Z_SM2:Z_SM2 + 2 * ML_H].transpose(0, 2, 1).reshape(B, 2 * ML_H, nblk, LANES)
        o_ml = _mlstm(zl, zh3, gates_row, p["ml_conv"], p["fb_row"], p["ml_f_bias"], p["g_ml_head"], front=front)
        q, qi, kv, ki = _dsa_prep(zh3, p["g_dsa_q"], p["g_dsa_kv"], p["idx_ln_g"], p["idx_ln_b"],
                                  p["w_dsa_uq"], p["w_idx_q"])
        kvt = kv.reshape(B, nblk, LANES, DSA_KV_LAT).transpose(0, 1, 3, 2)
        kvt = jnp.concatenate([kvt, jnp.ones((B, nblk, KVT_ONES, LANES), kvt.dtype)], axis=2)
        wi_t = zh3[:, :, Z_SM1 + IDX_D:Z_SM1 + IDX_D + IDX_H].transpose(0, 2, 1)
        o_dsa = _dsa(ki, kv, kvt, qi, q, wi_t, bias_tiles, p["w_dsa_uv"], front=front, topk=topk)
        h = _mix(o_gla.reshape(R, D), o_dsa.reshape(R, D), o_ml.reshape(R, D), zh,
                 p["w_br_gla"], p["w_br_dsa"], p["w_br_ml"], p["w_out"], p["g_post_mix"], h, tm=tm_small)
        u = _norm_matmul(h, p["g_pre_mlp"], p["w_up"], zero_ff, tm=tm_mid, tn=1024,
                         relu2=True, out_dtype=ACT_DT, name="mlp_up")
        h = _matmul_norm_res(u, p["w_down"], p["g_post_mlp"], h, tm=tm_mid)
        return h, None

    h, _ = lax.scan(layer, h0, layers)
    return h.reshape(B, Tp, D)[:, front + N_META:]
```

```python
import functools
import math

import numpy as np
import jax
import jax.numpy as jnp
from jax import lax
from jax.experimental import pallas as pl
from jax.experimental.pallas import tpu as pltpu

D_MODEL = 1024
N_META = 16
GLA_H, GLA_DK, GLA_DV, GLA_LR, GLA_TAU = 4, 128, 256, 16, 16.0
DSA_H, DSA_DH, DSA_Q_LORA, DSA_KV_LAT = 16, 64, 256, 128
IDX_H, IDX_D, TOPK_MAX = 8, 64, 256
ML_H, ML_DK, ML_DV, ML_CONV = 4, 128, 256, 4
T5_BUCKETS, T5_MAX_DIST = 32, 128
D_FF = 4 * D_MODEL
EPS = 1e-6
NEG = -1e30
LOG2E = math.log2(math.e)

LANES = 128
SUBLANES = 8
VMEM_LIMIT = 56 * 1024 * 1024

MXU_DT = jnp.bfloat16
ACT_DT = jnp.bfloat16

F32 = jnp.float32
I32 = jnp.int32
INT_MIN = -2147483648
HI = lax.Precision.HIGHEST
NT_DIMS = (((1,), (1,)), ((), ()))

Z_GLA_Q, Z_GLA_K, Z_GLA_V, Z_GLA_R = 0, 512, 1024, 2048
Z_ML_Q, Z_ML_K, Z_ML_V, Z_ML_O = 3072, 3584, 4096, 5120
NZL = 6144
Z_GATE_A, Z_GATE_B, Z_GATE_C = 0, 1024, 2048
Z_CQ, Z_CKV = 3072, 3328
Z_SM0, Z_SM1, Z_SM2 = 3456, 3584, 3712
NZH = 3840
KVT_ONES = 16
PREP_ROWS_MAX = 1088
KB_GROUP = 4
MIX_CHUNKS = 6


def _params(sem):
    return pltpu.CompilerParams(dimension_semantics=sem, vmem_limit_bytes=VMEM_LIMIT)


def _mm(a, b):
    return jnp.dot(a.astype(MXU_DT), b.astype(MXU_DT), preferred_element_type=F32)


def _mm_nt(a, b):
    return lax.dot_general(a.astype(MXU_DT), b.astype(MXU_DT), NT_DIMS, preferred_element_type=F32)


def _rms(x, g):
    return x * lax.rsqrt(jnp.mean(x * x, axis=-1, keepdims=True) + EPS) * g


def _log_sigmoid(x):
    return jnp.minimum(x, 0.0) - jnp.log1p(jnp.exp(-jnp.abs(x)))


def _sigmoid(x):
    return 1.0 / (1.0 + jnp.exp(-x))


def _norm_matmul_kernel(x_ref, g_ref, w_ref, b_ref, o_ref, *, relu2, tn):
    xn = _rms(x_ref[...], g_ref[...]).astype(MXU_DT)
    for n0 in range(0, o_ref.shape[1], tn):
        acc = jnp.dot(xn, w_ref[:, n0:n0 + tn], preferred_element_type=F32) + b_ref[:, n0:n0 + tn]
        if relu2:
            acc = jnp.square(jnp.maximum(acc, 0.0))
        o_ref[:, n0:n0 + tn] = acc.astype(o_ref.dtype)


def _norm_matmul(x, g, w, b, *, tm, tn, relu2, out_dtype, name):
    R, K = x.shape
    N = w.shape[1]
    return pl.pallas_call(
        functools.partial(_norm_matmul_kernel, relu2=relu2, tn=tn),
        grid=(R // tm,),
        in_specs=[pl.BlockSpec((tm, K), lambda i: (i, 0)),
                  pl.BlockSpec((1, K), lambda i: (0, 0)),
                  pl.BlockSpec((K, N), lambda i: (0, 0)),
                  pl.BlockSpec((1, N), lambda i: (0, 0))],
        out_specs=pl.BlockSpec((tm, N), lambda i: (i, 0)),
        out_shape=jax.ShapeDtypeStruct((R, N), out_dtype),
        compiler_params=_params(("parallel",)),
        name=name,
    )(x, g, w, b)


def _matmul_norm_res_kernel(u_ref, w_ref, g_ref, h_ref, o_ref):
    y = jnp.dot(u_ref[...], w_ref[...], preferred_element_type=F32)
    o_ref[...] = h_ref[...] + _rms(y, g_ref[...])


def _matmul_norm_res(u, w, g, h, *, tm):
    R, K = u.shape
    N = w.shape[1]
    return pl.pallas_call(
        _matmul_norm_res_kernel,
        grid=(R // tm,),
        in_specs=[pl.BlockSpec((tm, K), lambda i: (i, 0)),
                  pl.BlockSpec((K, N), lambda i: (0, 0)),
                  pl.BlockSpec((1, N), lambda i: (0, 0)),
                  pl.BlockSpec((tm, N), lambda i: (i, 0))],
        out_specs=pl.BlockSpec((tm, N), lambda i: (i, 0)),
        out_shape=jax.ShapeDtypeStruct((R, N), F32),
        compiler_params=_params(("parallel",)),
        name="mlp_down",
    )(u, w, g, h)


def _mix_kernel(oa_ref, ob_ref, oc_ref, ga_ref, gb_ref, gc_ref, wa_ref, wb_ref, wc_ref,
                wo_ref, g_ref, h_ref, o_ref):
    ya = jnp.dot(oa_ref[...], wa_ref[...], preferred_element_type=F32)
    yb = jnp.dot(ob_ref[...], wb_ref[...], preferred_element_type=F32)
    yc = jnp.dot(oc_ref[...], wc_ref[...], preferred_element_type=F32)
    mix = _sigmoid(ga_ref[...]) * ya + _sigmoid(gb_ref[...]) * yb + _sigmoid(gc_ref[...]) * yc
    y = _mm(mix, wo_ref[...])
    o_ref[...] = h_ref[...] + _rms(y, g_ref[...])


def _mix(oa, ob, oc, z, wa, wb, wc, wo, g, h, *, tm):
    R, D = h.shape
    row = lambda i: (i, 0)
    const = lambda i: (0, 0)
    gate = lambda off: pl.BlockSpec((tm, D), lambda i: (i, off // D))
    return pl.pallas_call(
        _mix_kernel,
        grid=(R // tm,),
        in_specs=[pl.BlockSpec((tm, D), row), pl.BlockSpec((tm, D), row), pl.BlockSpec((tm, D), row),
                  gate(Z_GATE_A), gate(Z_GATE_B), gate(Z_GATE_C),
                  pl.BlockSpec((D, D), const), pl.BlockSpec((D, D), const), pl.BlockSpec((D, D), const),
                  pl.BlockSpec((D, D), const), pl.BlockSpec((1, D), const), pl.BlockSpec((tm, D), row)],
        out_specs=pl.BlockSpec((tm, D), row),
        out_shape=jax.ShapeDtypeStruct((R, D), F32),
        compiler_params=_params(("parallel",)),
        name="mix_out",
    )(oa, ob, oc, z, z, z, wa, wb, wc, wo, g, h)


def _gla_kernel(q_ref, k_ref, v_ref, r_ref, a_ref, wa_ref, ba_ref, gh_ref, o_ref, s_ref, *, front, nchunks):
    C = LANES
    tb = pl.program_id(1)

    @pl.when(tb == 0)
    def _():
        s_ref[...] = jnp.zeros_like(s_ref)

    r0 = lax.broadcasted_iota(I32, (C, C), 0)
    r1 = lax.broadcasted_iota(I32, (C, C), 1)
    causal = r0 >= r1
    tril = causal.astype(F32)
    rowi = lax.broadcasted_iota(I32, (C, 1), 0)

    def body(c, carry):
        sl = pl.ds(pl.multiple_of(c * C, C), C)
        live = ((tb * MIX_CHUNKS + c) * C + rowi) >= front
        a = a_ref[0, sl, :].astype(MXU_DT)
        H = range(GLA_H)
        kq = [slice(h * GLA_DK, (h + 1) * GLA_DK) for h in H]
        kv = [slice(h * GLA_DV, (h + 1) * GLA_DV) for h in H]
        g = [_log_sigmoid(jnp.dot(a, wa_ref[h], preferred_element_type=F32) + ba_ref[h]) / GLA_TAU for h in H]
        G = [jnp.dot(tril, g[h], precision=HI, preferred_element_type=F32) for h in H]
        q = [q_ref[0, sl, kq[h]].astype(F32) * (GLA_DK ** -0.5) for h in H]
        k = [jnp.where(live, k_ref[0, sl, kq[h]].astype(F32), 0.0) for h in H]
        v = [jnp.where(live, v_ref[0, sl, kv[h]].astype(F32), 0.0).astype(MXU_DT) for h in H]
        S = [s_ref[h] for h in H]
        g_last = [G[h][C - 1:C, :] for h in H]
        g_mid = [G[h][C // 2 - 1:C // 2, :] for h in H]
        o_inter = [_mm(q[h] * jnp.exp(G[h]), S[h]) for h in H]
        A = [jnp.where(causal, _mm_nt(q[h] * jnp.exp(G[h] - g_mid[h]), k[h] * jnp.exp(g_mid[h] - G[h])), 0.0) for h in H]
        o_intra = [_mm(A[h], v[h]) for h in H]
        kd_t = [jnp.transpose(k[h] * jnp.exp(g_last[h] - G[h])) for h in H]
        dec_col = [jnp.transpose(jnp.broadcast_to(jnp.exp(g_last[h]), (C, GLA_DK))) for h in H]
        for h in H:
            s_ref[h] = jnp.concatenate([dec_col[h], dec_col[h]], axis=1) * S[h] + _mm(kd_t[h], v[h])
        for h in H:
            r = r_ref[0, sl, kv[h]].astype(F32)
            o = _rms(o_inter[h] + o_intra[h], gh_ref[...]) * (r * _sigmoid(r))
            o_ref[0, sl, kv[h]] = o.astype(o_ref.dtype)
        return carry

    lax.fori_loop(0, jnp.minimum(MIX_CHUNKS, nchunks - tb * MIX_CHUNKS), body, 0)


def _gla(zl, zh, wa, ba, gh, *, front):
    B, Tp, _ = zl.shape
    nchunks = Tp // LANES
    TB = min(MIX_CHUNKS * LANES, Tp)
    WK, WV = GLA_H * GLA_DK, GLA_H * GLA_DV
    return pl.pallas_call(
        functools.partial(_gla_kernel, front=front, nchunks=nchunks),
        grid=(B, pl.cdiv(Tp, TB)),
        in_specs=[pl.BlockSpec((1, TB, WK), lambda b, t: (b, t, Z_GLA_Q // WK)),
                  pl.BlockSpec((1, TB, WK), lambda b, t: (b, t, Z_GLA_K // WK)),
                  pl.BlockSpec((1, TB, WV), lambda b, t: (b, t, Z_GLA_V // WV)),
                  pl.BlockSpec((1, TB, WV), lambda b, t: (b, t, Z_GLA_R // WV)),
                  pl.BlockSpec((1, TB, LANES), lambda b, t: (b, t, Z_SM0 // LANES)),
                  pl.BlockSpec((GLA_H, LANES, GLA_DK), lambda b, t: (0, 0, 0)),
                  pl.BlockSpec((GLA_H, 1, GLA_DK), lambda b, t: (0, 0, 0)),
                  pl.BlockSpec((1, GLA_DV), lambda b, t: (0, 0))],
        out_specs=pl.BlockSpec((1, TB, WV), lambda b, t: (b, t, 0)),
        out_shape=jax.ShapeDtypeStruct((B, Tp, WV), ACT_DT),
        scratch_shapes=[pltpu.VMEM((GLA_H, GLA_DK, GLA_DV), F32)],
        compiler_params=_params(("parallel", "arbitrary")),
        name="gla",
    )(zl, zl, zl, zl, zh, wa, ba, gh)


def _mlstm_kernel(q_ref, k_ref, v_ref, og_ref, gcol_ref, grow_ref, cw_ref, fbrow_ref, fb_ref, gh_ref, o_ref,
                  s_ref, m_ref, pq_ref, pk_ref, *, front, nchunks):
    C = LANES
    WK = ML_H * ML_DK
    tb = pl.program_id(1)

    @pl.when(tb == 0)
    def _():
        s_ref[...] = jnp.zeros_like(s_ref)
        m_ref[...] = jnp.zeros_like(m_ref)
        pq_ref[...] = jnp.zeros_like(pq_ref)
        pk_ref[...] = jnp.zeros_like(pk_ref)

    r0 = lax.broadcasted_iota(I32, (C, C), 0)
    r1 = lax.broadcasted_iota(I32, (C, C), 1)
    causal = r0 >= r1
    tril = causal.astype(F32)
    triu = (r0 <= r1).astype(F32)
    rowi = lax.broadcasted_iota(I32, (C, 1), 0)

    def conv_silu(x, tail, w):
        acc = x * w[ML_CONV - 1:ML_CONV, :]
        for d in range(1, ML_CONV):
            sh = pltpu.roll(x, d, 0)
            head = jnp.where(rowi[:SUBLANES] >= d, sh[:SUBLANES], pltpu.roll(tail, d, 0))
            sh = jnp.concatenate([head, sh[SUBLANES:]], axis=0)
            acc = acc + sh * w[ML_CONV - 1 - d:ML_CONV - d, :]
        return acc * _sigmoid(acc)

    def body(c, carry):
        sl = pl.ds(pl.multiple_of(c * C, C), C)
        cg = tb * MIX_CHUNKS + c
        live = (cg * C + rowi) >= front
        xq = jnp.where(live, q_ref[0, sl, :].astype(F32), 0.0)
        xk = jnp.where(live, k_ref[0, sl, :].astype(F32), 0.0)
        q_all = conv_silu(xq, pq_ref[...], cw_ref[:, :WK]) * (ML_DK ** -0.5)
        k_all = conv_silu(xk, pk_ref[...], cw_ref[:, WK:])
        pq_ref[...] = xq[C - SUBLANES:, :]
        pk_ref[...] = xk[C - SUBLANES:, :]
        lf_col = _log_sigmoid(gcol_ref[0, sl, :] + fbrow_ref[...])
        H = range(ML_H)
        vs = [slice(h * ML_DV, (h + 1) * ML_DV) for h in H]
        q = [q_all[:, h * ML_DK:(h + 1) * ML_DK].astype(MXU_DT) for h in H]
        k = [k_all[:, h * ML_DK:(h + 1) * ML_DK] for h in H]
        ones = jnp.ones((C, LANES), MXU_DT)
        v_aug = [jnp.concatenate([jnp.where(live, v_ref[0, sl, vs[h]].astype(F32), 0.0).astype(MXU_DT), ones], axis=1)
                 for h in H]
        m_prev = [m_ref[h, 0:1, :] for h in H]
        S = [s_ref[h] for h in H]
        sel = [jnp.dot(lf_col, (r0 == ML_H + h).astype(F32), precision=HI, preferred_element_type=F32) for h in H]
        b_bc = [jnp.dot(tril, sel[h], precision=HI, preferred_element_type=F32) for h in H]
        li_row = [grow_ref[0, h, pl.ds(cg, 1), :] for h in H]
        lf_row = [_log_sigmoid(grow_ref[0, ML_H + h, pl.ds(cg, 1), :] + fb_ref[h]) for h in H]
        b_row = [jnp.dot(lf_row[h], triu, precision=HI, preferred_element_type=F32) for h in H]
        b_last = [b_row[h][:, C - 1:C] for h in H]
        qk = [_mm_nt(q[h], k[h]) for h in H]
        qs = [_mm(q[h], S[h]) for h in H]
        Dm = [jnp.where(causal, b_bc[h] - (b_row[h] - li_row[h]), -jnp.inf) for h in H]
        m_inter = [b_bc[h] + m_prev[h] for h in H]
        m = [jnp.maximum(m_inter[h], jnp.max(Dm[h], axis=1, keepdims=True)) for h in H]
        pv = [_mm(jnp.exp(Dm[h] - m[h]) * qk[h], v_aug[h]) for h in H]
        dec = [b_last[h] - b_row[h] + li_row[h] for h in H]
        m_new = [jnp.maximum(b_last[h] + m_prev[h], jnp.max(dec[h], axis=1, keepdims=True)) for h in H]
        kt_w = [jnp.transpose(k[h]) * jnp.exp(dec[h] - m_new[h]) for h in H]
        for h in H:
            w_old = jnp.exp(b_last[h] + m_prev[h] - m_new[h])
            s_ref[h] = jnp.concatenate([w_old, w_old, w_old], axis=1) * S[h] + _mm(kt_w[h], v_aug[h])
            m_ref[h] = jnp.broadcast_to(m_new[h], (SUBLANES, C))
        for h in H:
            w_inter = jnp.exp(m_inter[h] - m[h])
            num = jnp.concatenate([w_inter, w_inter, w_inter], axis=1) * qs[h] + pv[h]
            den = num[:, ML_DV:]
            dd = jnp.maximum(jnp.abs(den), jnp.exp(-m[h]))
            hval = num[:, :ML_DV] / jnp.concatenate([dd, dd], axis=1)
            og = og_ref[0, sl, vs[h]].astype(F32)
            o_ref[0, sl, vs[h]] = (_rms(hval, gh_ref[...]) * _sigmoid(og)).astype(o_ref.dtype)
        return carry

    lax.fori_loop(0, jnp.minimum(MIX_CHUNKS, nchunks - tb * MIX_CHUNKS), body, 0)


def _mlstm(zl, zh, gates_row, conv_w, fb_row, fb, gh, *, front):
    B, Tp, _ = zl.shape
    nchunks = Tp // LANES
    TB = min(MIX_CHUNKS * LANES, Tp)
    WK, WV = ML_H * ML_DK, ML_H * ML_DV
    return pl.pallas_call(
        functools.partial(_mlstm_kernel, front=front, nchunks=nchunks),
        grid=(B, pl.cdiv(Tp, TB)),
        in_specs=[pl.BlockSpec((1, TB, WK), lambda b, t: (b, t, Z_ML_Q // WK)),
                  pl.BlockSpec((1, TB, WK), lambda b, t: (b, t, Z_ML_K // WK)),
                  pl.BlockSpec((1, TB, WV), lambda b, t: (b, t, Z_ML_V // WV)),
                  pl.BlockSpec((1, TB, WV), lambda b, t: (b, t, Z_ML_O // WV)),
                  pl.BlockSpec((1, TB, LANES), lambda b, t: (b, t, Z_SM2 // LANES)),
                  pl.BlockSpec((1, 2 * ML_H, nchunks, LANES), lambda b, t: (b, 0, 0, 0)),
                  pl.BlockSpec((ML_CONV, 2 * WK), lambda b, t: (0, 0)),
                  pl.BlockSpec((1, LANES), lambda b, t: (0, 0)),
                  pl.BlockSpec(memory_space=pltpu.SMEM),
                  pl.BlockSpec((1, ML_DV), lambda b, t: (0, 0))],
        out_specs=pl.BlockSpec((1, TB, WV), lambda b, t: (b, t, 0)),
        out_shape=jax.ShapeDtypeStruct((B, Tp, WV), ACT_DT),
        scratch_shapes=[pltpu.VMEM((ML_H, ML_DK, ML_DV + LANES), F32), pltpu.VMEM((ML_H, SUBLANES, LANES), F32),
                        pltpu.VMEM((SUBLANES, WK), F32), pltpu.VMEM((SUBLANES, WK), F32)],
        compiler_params=_params(("parallel", "arbitrary")),
        name="mlstm",
    )(zl, zl, zl, zl, zh, gates_row, conv_w, fb_row, fb, gh)


def _dsa_prep_kernel(cq_ref, ckv_ref, sm_ref, gq_ref, gkv_ref, lng_ref, lnb_ref, wuq_ref, wiq_ref,
                     q_ref, qi_ref, kv_ref, ki_ref):
    cq = _rms(cq_ref[0], gq_ref[...]).astype(MXU_DT)
    for hp in range(DSA_H // 2):
        w2 = wuq_ref[:, 2 * hp * DSA_KV_LAT:(2 * hp + 2) * DSA_KV_LAT]
        qf = jnp.dot(cq, w2, preferred_element_type=F32) * (DSA_KV_LAT ** -0.5 * LOG2E)
        q_ref[0, 2 * hp] = qf[:, :DSA_KV_LAT].astype(q_ref.dtype)
        q_ref[0, 2 * hp + 1] = qf[:, DSA_KV_LAT:].astype(q_ref.dtype)
    for hh in range(IDX_H):
        qi = jnp.dot(cq, wiq_ref[hh], preferred_element_type=F32) * (IDX_D ** -0.5)
        qi_ref[0, hh] = qi.astype(qi_ref.dtype)
    kv_ref[0] = _rms(ckv_ref[0], gkv_ref[...]).astype(kv_ref.dtype)
    x = sm_ref[0][:, :IDX_D]
    mu = jnp.mean(x, axis=-1, keepdims=True)
    var = jnp.mean(jnp.square(x - mu), axis=-1, keepdims=True)
    ki_ref[0] = ((x - mu) * lax.rsqrt(var + EPS) * lng_ref[...] + lnb_ref[...]).astype(ki_ref.dtype)


def _dsa_prep(z, gq, gkv, lng, lnb, wuq, wiq):
    B, Tp, _ = z.shape
    tm = max(t for t in range(2 * SUBLANES, PREP_ROWS_MAX + 1, 2 * SUBLANES) if Tp % t == 0)
    c2 = lambda b, i: (0, 0)
    return pl.pallas_call(
        _dsa_prep_kernel,
        grid=(B, Tp // tm),
        in_specs=[pl.BlockSpec((1, tm, DSA_Q_LORA), lambda b, i: (b, i, Z_CQ // DSA_Q_LORA)),
                  pl.BlockSpec((1, tm, DSA_KV_LAT), lambda b, i: (b, i, Z_CKV // DSA_KV_LAT)),
                  pl.BlockSpec((1, tm, LANES), lambda b, i: (b, i, Z_SM1 // LANES)),
                  pl.BlockSpec((1, DSA_Q_LORA), c2), pl.BlockSpec((1, DSA_KV_LAT), c2),
                  pl.BlockSpec((1, IDX_D), c2), pl.BlockSpec((1, IDX_D), c2),
                  pl.BlockSpec((DSA_Q_LORA, DSA_H * DSA_KV_LAT), c2),
                  pl.BlockSpec((IDX_H, DSA_Q_LORA, IDX_D), lambda b, i: (0, 0, 0))],
        out_specs=[pl.BlockSpec((1, DSA_H, tm, DSA_KV_LAT), lambda b, i: (b, 0, i, 0)),
                   pl.BlockSpec((1, IDX_H, tm, IDX_D), lambda b, i: (b, 0, i, 0)),
                   pl.BlockSpec((1, tm, DSA_KV_LAT), lambda b, i: (b, i, 0)),
                   pl.BlockSpec((1, tm, IDX_D), lambda b, i: (b, i, 0))],
        out_shape=[jax.ShapeDtypeStruct((B, DSA_H, Tp, DSA_KV_LAT), ACT_DT),
                   jax.ShapeDtypeStruct((B, IDX_H, Tp, IDX_D), ACT_DT),
                   jax.ShapeDtypeStruct((B, Tp, DSA_KV_LAT), ACT_DT),
                   jax.ShapeDtypeStruct((B, Tp, IDX_D), ACT_DT)],
        compiler_params=_params(("parallel", "parallel")),
        name="dsa_prep",
    )(z, z, z, gq, gkv, lng, lnb, wuq, wiq)


def _sortable(x):
    bits = pltpu.bitcast(x, I32)
    key = jnp.where(bits < 0, bits ^ 0x7FFFFFFF, bits)
    return jnp.where(x == 0.0, 0, key)


_KEY_NEG = int(np.array(NEG, np.float32).view(np.int32)) ^ 0x7FFFFFFF


def _dsa_kernel(ki_ref, kv_ref, kvt_ref, qi_ref, q_ref, wi_ref, bias_ref, wuv_ref, o_ref,
                key_ref, hi_ref, lo_ref, madd_ref, lg_ref, acc_ref, *, front, topk, Tp):
    QB = LANES
    j = pl.program_id(1)
    nkb = j + 1
    ngrp = nkb // KB_GROUP
    s_loc = lax.broadcasted_iota(I32, (QB, QB), 0)
    t_glob = j * QB + lax.broadcasted_iota(I32, (QB, QB), 1)
    wi = wi_ref[0] * (IDX_H ** -0.5)
    n_rest = Tp - nkb * QB

    def blk(kb):
        return pl.ds(pl.multiple_of(kb * QB, QB), QB)

    def score_block(kb):
        ki = ki_ref[0, blk(kb), :]
        acc = jnp.zeros((QB, QB), F32)
        for hp in range(IDX_H // 2):
            qi2 = qi_ref[0, 2 * hp:2 * hp + 2].reshape(2 * QB, IDX_D)
            s = lax.dot_general(ki, qi2, NT_DIMS, preferred_element_type=F32)
            acc = acc + jnp.maximum(s[:, :QB], 0.0) * wi[2 * hp:2 * hp + 1, :]
            acc = acc + jnp.maximum(s[:, QB:], 0.0) * wi[2 * hp + 1:2 * hp + 2, :]
        s_glob = kb * QB + s_loc
        key = _sortable(jnp.where(s_glob <= t_glob, acc, NEG))
        key = jnp.where(s_glob >= front, key, INT_MIN)
        key_ref[blk(kb), :] = key
        hi_ref[blk(kb), :] = (key >> 16).astype(jnp.int16)

    def score_group(i, c):
        for u in range(KB_GROUP):
            score_block(KB_GROUP * i + u)
        return c

    def score_single(kb, c):
        score_block(kb)
        return c

    lax.fori_loop(0, ngrp, score_group, 0)
    lax.fori_loop(ngrp * KB_GROUP, nkb, score_single, 0)

    def fold32(hit):
        return jnp.sum(hit.reshape(QB // SUBLANES, SUBLANES, QB), axis=0)

    def count(pred_fn):
        def hit_of(kb):
            return pred_fn(key_ref[blk(kb), :], kb).astype(I32)

        def group(i, acc):
            hit = hit_of(KB_GROUP * i)
            for u in range(1, KB_GROUP):
                hit = hit + hit_of(KB_GROUP * i + u)
            return acc + fold32(hit)

        acc = lax.fori_loop(0, ngrp, group, jnp.zeros((SUBLANES, QB), I32))
        acc = lax.fori_loop(ngrp * KB_GROUP, nkb, lambda kb, a: a + fold32(hit_of(kb)), acc)
        return jnp.sum(acc, axis=0, keepdims=True)

    PK = 2 * SUBLANES

    def count16(ref, cand):
        def tree(parts):
            while len(parts) > 1:
                parts = [parts[i] + parts[i + 1] for i in range(0, len(parts), 2)]
            return parts[0]

        def hits(kb):
            hit = jnp.where(ref[blk(kb), :] >= cand, jnp.bfloat16(1), jnp.bfloat16(0)).reshape(QB // PK, PK, QB)
            return tree([hit[u] for u in range(QB // PK)])

        def group(i, acc):
            return acc + tree([hits(KB_GROUP * i + u) for u in range(KB_GROUP)])

        acc = lax.fori_loop(0, ngrp, group, jnp.zeros((PK, QB), jnp.bfloat16))
        acc = lax.fori_loop(ngrp * KB_GROUP, nkb, lambda kb, a: a + hits(kb), acc)
        return jnp.sum(acc.astype(F32), axis=0, keepdims=True).astype(I32)

    hi_neg, lo_neg = _KEY_NEG >> 16, (_KEY_NEG & 0xFFFF) - 0x8000

    def ge_hi(cand):
        return count16(hi_ref, cand.astype(jnp.int16)) + jnp.where(hi_neg >= cand, n_rest, 0)

    p_hi = jnp.where(ge_hi(jnp.zeros((1, QB), I32)) >= topk, 0, -0x8000).astype(I32)

    def hi_body(i, p):
        cand = p | jnp.left_shift(jnp.int32(1), 14 - i)
        return jnp.where(ge_hi(cand) >= topk, cand, p)

    p_hi = lax.fori_loop(0, 15, hi_body, p_hi)
    p_hi16 = p_hi.astype(jnp.int16)

    def lo_block(kb, c):
        lo = ((key_ref[blk(kb), :] & 0xFFFF) - 0x8000).astype(jnp.int16)
        hi = hi_ref[blk(kb), :]
        lo_ref[blk(kb), :] = jnp.where(hi == p_hi16, lo, jnp.where(hi > p_hi16, jnp.int16(0x7FFF), jnp.int16(-0x8000)))
        return c

    lax.fori_loop(0, nkb, lo_block, 0)

    def ge_lo(cand_u):
        cand = cand_u - 0x8000
        rest = jnp.where(hi_neg > p_hi, n_rest, jnp.where((hi_neg == p_hi) & (lo_neg >= cand), n_rest, 0))
        return count16(lo_ref, cand.astype(jnp.int16)) + rest

    def lo_body(i, p):
        cand = p | jnp.left_shift(jnp.int32(1), 15 - i)
        return jnp.where(ge_lo(cand) >= topk, cand, p)

    p_lo = lax.fori_loop(0, 16, lo_body, jnp.zeros((1, QB), I32))
    tau = p_hi * 0x10000 + p_lo

    def valid_of(kb):
        s_glob = kb * QB + s_loc
        return (s_glob >= front) & (s_glob <= t_glob)

    c_gt = count(lambda kblk, kb: kblk > tau) + jnp.where(_KEY_NEG > tau, n_rest, 0)
    c_eq = count(lambda kblk, kb: (kblk == tau) & valid_of(kb))
    need = topk - c_gt

    @pl.when(jnp.max((c_eq > need).astype(I32)) > 0)
    def _():
        lstrict = (s_loc > lax.broadcasted_iota(I32, (QB, QB), 1)).astype(MXU_DT)
        needf = need.astype(F32)

        def tie_body(kb, seen):
            kblk = key_ref[blk(kb), :]
            tie = (kblk == tau) & valid_of(kb)
            tief = tie.astype(F32)
            rank = seen + jnp.dot(lstrict, tief.astype(MXU_DT), preferred_element_type=F32)
            key_ref[blk(kb), :] = jnp.where(tie & (rank >= needf), INT_MIN, kblk)
            return seen + jnp.sum(tief, axis=0, keepdims=True)

        lax.fori_loop(0, nkb, tie_body, jnp.zeros((1, QB), F32))

    def mask_body(kb, c):
        sel = (key_ref[blk(kb), :] >= tau) & valid_of(kb)
        madd_ref[blk(kb), :] = jnp.where(sel, 0.0, NEG)
        return c

    lax.fori_loop(0, nkb, mask_body, 0)

    NP = DSA_H // 2
    grp = (QB // SUBLANES, SUBLANES, 2 * QB)
    acc_ref[...] = jnp.zeros_like(acc_ref)

    def score_inputs(kb):
        ma = madd_ref[blk(kb), :]
        return kv_ref[0, blk(kb), :], jnp.concatenate([ma, ma], axis=1), jnp.minimum(j - kb, 2)

    def stage1(p, slot, inputs):
        kvb, ma2, rel = inputs
        q2 = q_ref[0, 2 * p:2 * p + 2].reshape(2 * QB, DSA_KV_LAT)
        lg = lax.dot_general(kvb, q2, NT_DIMS, preferred_element_type=F32) + bias_ref[p, rel] + ma2
        lg_ref[slot, p] = lg
        return jnp.max(jnp.max(lg.reshape(grp), axis=0), axis=0, keepdims=True)

    def stage2(p, slot, kvt, m_old, bm):
        m_new = jnp.maximum(m_old, bm)
        pm = jnp.exp2(lg_ref[slot, p] - m_new)
        acc_ref[p] = (jnp.exp2(m_old - m_new) * acc_ref[p]
                      + jnp.dot(kvt, pm.astype(MXU_DT), preferred_element_type=F32))
        return m_new

    inp0 = score_inputs(0)
    bm0 = tuple(stage1(p, 0, inp0) for p in range(NP))

    def attn_body(i, carry):
        ms, bm_a = carry
        a = 2 * i
        inp_b = score_inputs(a + 1)
        inp_c = score_inputs(jnp.minimum(a + 2, nkb - 1))
        kvt_a = kvt_ref[0, a]
        kvt_b = kvt_ref[0, a + 1]
        ms_a, bm_b = [], []
        for p in range(NP):
            bm_b.append(stage1(p, 1, inp_b))
            ms_a.append(stage2(p, 0, kvt_a, ms[p], bm_a[p]))
        ms_b, bm_c = [], []
        for p in range(NP):
            bm_c.append(stage1(p, 0, inp_c))
            ms_b.append(stage2(p, 1, kvt_b, ms_a[p], bm_b[p]))
        return tuple(ms_b), tuple(bm_c)

    ms, bm_last = lax.fori_loop(0, nkb // 2, attn_body,
                                (tuple(jnp.full((1, 2 * QB), -jnp.inf, F32) for _ in range(NP)), bm0))

    @pl.when(nkb % 2 == 1)
    def _():
        kvt = kvt_ref[0, nkb - 1]
        for p in range(NP):
            stage2(p, 0, kvt, ms[p], bm_last[p])

    for p in range(NP):
        acc = acc_ref[p]
        o_t = acc[:DSA_KV_LAT] / acc[DSA_KV_LAT:DSA_KV_LAT + 1]
        outs = []
        for hh in range(2):
            o_h = jnp.transpose(o_t[:, hh * QB:(hh + 1) * QB])
            outs.append(_mm(o_h, wuv_ref[2 * p + hh]))
        o_ref[0, :, 2 * p * DSA_DH:(2 * p + 2) * DSA_DH] = jnp.concatenate(outs, axis=1).astype(o_ref.dtype)


def _dsa(ki, kv, kvt, qi, q, wi_t, bias_tiles, wuv, *, front, topk):
    B, Tp, _ = kv.shape
    nblk = Tp // LANES
    return pl.pallas_call(
        functools.partial(_dsa_kernel, front=front, topk=topk, Tp=Tp),
        grid=(B, nblk),
        in_specs=[pl.BlockSpec((1, Tp, IDX_D), lambda b, j: (b, 0, 0)),
                  pl.BlockSpec((1, Tp, DSA_KV_LAT), lambda b, j: (b, 0, 0)),
                  pl.BlockSpec((1, nblk, DSA_KV_LAT + KVT_ONES, LANES), lambda b, j: (b, 0, 0, 0)),
                  pl.BlockSpec((1, IDX_H, LANES, IDX_D), lambda b, j: (b, 0, j, 0)),
                  pl.BlockSpec((1, DSA_H, LANES, DSA_KV_LAT), lambda b, j: (b, 0, j, 0)),
                  pl.BlockSpec((1, IDX_H, LANES), lambda b, j: (b, 0, j)),
                  pl.BlockSpec((DSA_H // 2, 3, LANES, 2 * LANES), lambda b, j: (0, 0, 0, 0)),
                  pl.BlockSpec((DSA_H, DSA_KV_LAT, DSA_DH), lambda b, j: (0, 0, 0))],
        out_specs=pl.BlockSpec((1, LANES, DSA_H * DSA_DH), lambda b, j: (b, j, 0)),
        out_shape=jax.ShapeDtypeStruct((B, Tp, DSA_H * DSA_DH), ACT_DT),
        scratch_shapes=[pltpu.VMEM((Tp, LANES), I32), pltpu.VMEM((Tp, LANES), jnp.int16),
                        pltpu.VMEM((Tp, LANES), jnp.int16), pltpu.VMEM((Tp, LANES), F32),
                        pltpu.VMEM((2, DSA_H // 2, LANES, 2 * LANES), F32),
                        pltpu.VMEM((DSA_H // 2, DSA_KV_LAT + KVT_ONES, 2 * LANES), F32)],
        compiler_params=_params(("parallel", "arbitrary")),
        name="dsa_attn",
    )(ki, kv, kvt, qi, q, wi_t, bias_tiles, wuv)


def _t5_bucket(rel):
    n = jnp.maximum(rel, 0)
    max_exact = T5_BUCKETS // 2
    nf = jnp.maximum(n, 1).astype(F32)
    large = max_exact + (jnp.log(nf / max_exact) / math.log(T5_MAX_DIST / max_exact)
                         * (T5_BUCKETS - max_exact)).astype(I32)
    large = jnp.minimum(large, T5_BUCKETS - 1)
    return jnp.where(n < max_exact, n, large)


def _t5_tiles(t5_bias):
    s = jnp.arange(LANES, dtype=I32)[:, None]
    t = jnp.arange(LANES, dtype=I32)[None, :]
    tiles = []
    for shift in (0, LANES, 2 * LANES):
        tiles.append(t5_bias[_t5_bucket(shift + t - s)])
    tiles = jnp.stack(tiles, axis=0)
    tiles = tiles.transpose(3, 0, 1, 2).reshape(DSA_H // 2, 2, 3, LANES, LANES)
    return (tiles.transpose(0, 2, 3, 1, 4).reshape(DSA_H // 2, 3, LANES, 2 * LANES) * LOG2E).astype(F32)


def _pack_in_proj(w_in, b_in):
    splits = np.cumsum([GLA_H * GLA_DK, GLA_H * GLA_DK, GLA_H * GLA_DV, GLA_H * GLA_DV, GLA_LR,
                        DSA_Q_LORA, DSA_KV_LAT, IDX_D, IDX_H,
                        ML_H * ML_DK, ML_H * ML_DK, ML_H * ML_DV, ML_H * ML_DV, ML_H, ML_H,
                        D_MODEL, D_MODEL])[:].tolist()

    def pack(a):
        (gq, gk, gv, gr, ga, cq, ckv, ik, iw, mq, mk, mv, mo, mi, mf, a_, b_, c_) = jnp.split(a, splits, axis=-1)
        pad = lambda n: jnp.zeros(a.shape[:-1] + (n,), a.dtype)
        lo = jnp.concatenate([gq, gk, gv, gr, mq, mk, mv, mo], axis=-1)
        hi = jnp.concatenate([a_, b_, c_, cq, ckv,
                              ga, pad(LANES - GLA_LR),
                              ik, iw, pad(LANES - IDX_D - IDX_H),
                              mi, mf, pad(LANES - 2 * ML_H)], axis=-1)
        return lo, hi

    return pack(w_in), pack(b_in)


def kernel(x, meta_tokens, t5_bias, g_pre_mix, w_in, b_in, w_gla_a2, b_gla_a, g_gla_head, w_br_gla, g_dsa_q, w_dsa_uq, w_idx_q, g_dsa_kv, w_dsa_uv, idx_ln_g, idx_ln_b, w_br_dsa, ml_conv, ml_f_bias, g_ml_head, w_br_ml, w_out, g_post_mix, g_pre_mlp, w_up, w_down, g_post_mlp):
    B, S, D = x.shape
    L = w_in.shape[0]
    T = S + N_META
    front = (-T) % LANES
    Tp = T + front
    R = B * Tp
    nblk = Tp // LANES
    topk = min(TOPK_MAX, (T - N_META) // 4)
    tm_mid = 512 if R % 512 == 0 else LANES
    tm_small = 256 if R % 256 == 0 else LANES

    (w_lo, w_hi), (b_lo, b_hi) = _pack_in_proj(w_in, b_in)
    wa2 = jnp.pad(w_gla_a2, ((0, 0), (0, LANES - GLA_LR), (0, 0)))
    wa2 = wa2.reshape(L, LANES, GLA_H, GLA_DK).transpose(0, 2, 1, 3).astype(MXU_DT)
    fb_row = jnp.pad(ml_f_bias, ((0, 0), (ML_H, LANES - 2 * ML_H)))[:, None, :]
    layers = dict(
        g_pre_mix=g_pre_mix[:, None, :], w_lo=w_lo.astype(MXU_DT), b_lo=b_lo[:, None, :],
        w_hi=w_hi.astype(MXU_DT), b_hi=b_hi[:, None, :],
        wa2=wa2, ba=b_gla_a.reshape(L, GLA_H, 1, GLA_DK), g_gla_head=g_gla_head[:, None, :],
        w_br_gla=w_br_gla.astype(MXU_DT),
        g_dsa_q=g_dsa_q[:, None, :], w_dsa_uq=w_dsa_uq.astype(MXU_DT),
        w_idx_q=w_idx_q.reshape(L, DSA_Q_LORA, IDX_H, IDX_D).transpose(0, 2, 1, 3).astype(MXU_DT),
        g_dsa_kv=g_dsa_kv[:, None, :], w_dsa_uv=w_dsa_uv.astype(MXU_DT),
        idx_ln_g=idx_ln_g[:, None, :], idx_ln_b=idx_ln_b[:, None, :], w_br_dsa=w_br_dsa.astype(MXU_DT),
        ml_conv=ml_conv, fb_row=fb_row, ml_f_bias=ml_f_bias, g_ml_head=g_ml_head[:, None, :],
        w_br_ml=w_br_ml.astype(MXU_DT), w_out=w_out.astype(MXU_DT), g_post_mix=g_post_mix[:, None, :],
        g_pre_mlp=g_pre_mlp[:, None, :], w_up=w_up.astype(MXU_DT), w_down=w_down.astype(MXU_DT),
        g_post_mlp=g_post_mlp[:, None, :],
    )
    bias_tiles = _t5_tiles(t5_bias)
    zero_ff = jnp.zeros((1, D_FF), F32)

    meta = jnp.broadcast_to(meta_tokens.astype(x.dtype)[None], (B, N_META, D))
    h0 = jnp.concatenate([jnp.zeros((B, front, D), x.dtype), meta, x], axis=1).reshape(R, D)

    def layer(h, p):
        zl = _norm_matmul(h, p["g_pre_mix"], p["w_lo"], p["b_lo"], tm=tm_mid, tn=1024, relu2=False,
                          out_dtype=ACT_DT, name="in_proj_lo").reshape(B, Tp, NZL)
        zh = _norm_matmul(h, p["g_pre_mix"], p["w_hi"], p["b_hi"], tm=tm_mid, tn=768, relu2=False,
                          out_dtype=F32, name="in_proj_hi")
        zh3 = zh.reshape(B, Tp, NZH)
        o_gla = _gla(zl, zh3, p["wa2"], p["ba"], p["g_gla_head"], front=front)
        gates_row = zh3[:, :, Z_SM2:Z_SM2 + 2 * ML_H].transpose(0, 2, 1).reshape(B, 2 * ML_H, nblk, LANES)
        o_ml = _mlstm(zl, zh3, gates_row, p["ml_conv"], p["fb_row"], p["ml_f_bias"], p["g_ml_head"], front=front)
        q, qi, kv, ki = _dsa_prep(zh3, p["g_dsa_q"], p["g_dsa_kv"], p["idx_ln_g"], p["idx_ln_b"],
                                  p["w_dsa_uq"], p["w_idx_q"])
        kvt = kv.reshape(B, nblk, LANES, DSA_KV_LAT).transpose(0, 1, 3, 2)
        kvt = jnp.concatenate([kvt, jnp.ones((B, nblk, KVT_ONES, LANES), kvt.dtype)], axis=2)
        wi_t = zh3[:, :, Z_SM1 + IDX_D:Z_SM1 + IDX_D + IDX_H].transpose(0, 2, 1)
        o_dsa = _dsa(ki, kv, kvt, qi, q, wi_t, bias_tiles, p["w_dsa_uv"], front=front, topk=topk)
        h = _mix(o_gla.reshape(R, D), o_dsa.reshape(R, D), o_ml.reshape(R, D), zh,
                 p["w_br_gla"], p["w_br_dsa"], p["w_br_ml"], p["w_out"], p["g_post_mix"], h, tm=tm_small)
        u = _norm_matmul(h, p["g_pre_mlp"], p["w_up"], zero_ff, tm=tm_mid, tn=1024,
                         relu2=True, out_dtype=ACT_DT, name="mlp_up")
        h = _matmul_norm_res(u, p["w_down"], p["g_post_mlp"], h, tm=tm_mid)
        return h, None

    h, _ = lax.scan(layer, h0, layers)
    return h.reshape(B, Tp, D)[:, front + N_META:]
```

```python
import functools
import math

import numpy as np
import jax
import jax.numpy as jnp
from jax import lax
from jax.experimental import pallas as pl
from jax.experimental.pallas import tpu as pltpu

D_MODEL = 1024
N_META = 16
GLA_H, GLA_DK, GLA_DV, GLA_LR, GLA_TAU = 4, 128, 256, 16, 16.0
DSA_H, DSA_DH, DSA_Q_LORA, DSA_KV_LAT = 16, 64, 256, 128
IDX_H, IDX_D, TOPK_MAX = 8, 64, 256
ML_H, ML_DK, ML_DV, ML_CONV = 4, 128, 256, 4
T5_BUCKETS, T5_MAX_DIST = 32, 128
D_FF = 4 * D_MODEL
EPS = 1e-6
NEG = -1e30
LOG2E = math.log2(math.e)

LANES = 128
SUBLANES = 8
VMEM_LIMIT = 56 * 1024 * 1024

MXU_DT = jnp.bfloat16
ACT_DT = jnp.bfloat16

F32 = jnp.float32
I32 = jnp.int32
INT_MIN = -2147483648
HI = lax.Precision.HIGHEST
NT_DIMS = (((1,), (1,)), ((), ()))

Z_GLA_Q, Z_GLA_K, Z_GLA_V, Z_GLA_R = 0, 512, 1024, 2048
Z_ML_Q, Z_ML_K, Z_ML_V, Z_ML_O = 3072, 3584, 4096, 5120
NZL = 6144
Z_GATE_A, Z_GATE_B, Z_GATE_C = 0, 1024, 2048
Z_CQ, Z_CKV = 3072, 3328
Z_SM0, Z_SM1, Z_SM2 = 3456, 3584, 3712
NZH = 3840
KVT_ONES = 16
PREP_ROWS_MAX = 1088
KB_GROUP = 4
MIX_CHUNKS = 6


def _params(sem):
    return pltpu.CompilerParams(dimension_semantics=sem, vmem_limit_bytes=VMEM_LIMIT)


def _mm(a, b):
    return jnp.dot(a.astype(MXU_DT), b.astype(MXU_DT), preferred_element_type=F32)


def _mm_nt(a, b):
    return lax.dot_general(a.astype(MXU_DT), b.astype(MXU_DT), NT_DIMS, preferred_element_type=F32)


def _rms(x, g):
    return x * lax.rsqrt(jnp.mean(x * x, axis=-1, keepdims=True) + EPS) * g


def _log_sigmoid(x):
    return jnp.minimum(x, 0.0) - jnp.log1p(jnp.exp(-jnp.abs(x)))


def _sigmoid(x):
    return 1.0 / (1.0 + jnp.exp(-x))


def _norm_matmul_kernel(x_ref, g_ref, w_ref, b_ref, o_ref, *, relu2, tn):
    xn = _rms(x_ref[...], g_ref[...]).astype(MXU_DT)
    for n0 in range(0, o_ref.shape[1], tn):
        acc = jnp.dot(xn, w_ref[:, n0:n0 + tn], preferred_element_type=F32) + b_ref[:, n0:n0 + tn]
        if relu2:
            acc = jnp.square(jnp.maximum(acc, 0.0))
        o_ref[:, n0:n0 + tn] = acc.astype(o_ref.dtype)


def _norm_matmul(x, g, w, b, *, tm, tn, relu2, out_dtype, name):
    R, K = x.shape
    N = w.shape[1]
    return pl.pallas_call(
        functools.partial(_norm_matmul_kernel, relu2=relu2, tn=tn),
        grid=(R // tm,),
        in_specs=[pl.BlockSpec((tm, K), lambda i: (i, 0)),
                  pl.BlockSpec((1, K), lambda i: (0, 0)),
                  pl.BlockSpec((K, N), lambda i: (0, 0)),
                  pl.BlockSpec((1, N), lambda i: (0, 0))],
        out_specs=pl.BlockSpec((tm, N), lambda i: (i, 0)),
        out_shape=jax.ShapeDtypeStruct((R, N), out_dtype),
        compiler_params=_params(("parallel",)),
        name=name,
    )(x, g, w, b)


def _matmul_norm_res_kernel(u_ref, w_ref, g_ref, h_ref, o_ref):
    y = jnp.dot(u_ref[...], w_ref[...], preferred_element_type=F32)
    o_ref[...] = h_ref[...] + _rms(y, g_ref[...])


def _matmul_norm_res(u, w, g, h, *, tm):
    R, K = u.shape
    N = w.shape[1]
    return pl.pallas_call(
        _matmul_norm_res_kernel,
        grid=(R // tm,),
        in_specs=[pl.BlockSpec((tm, K), lambda i: (i, 0)),
                  pl.BlockSpec((K, N), lambda i: (0, 0)),
                  pl.BlockSpec((1, N), lambda i: (0, 0)),
                  pl.BlockSpec((tm, N), lambda i: (i, 0))],
        out_specs=pl.BlockSpec((tm, N), lambda i: (i, 0)),
        out_shape=jax.ShapeDtypeStruct((R, N), F32),
        compiler_params=_params(("parallel",)),
        name="mlp_down",
    )(u, w, g, h)


def _mix_kernel(oa_ref, ob_ref, oc_ref, ga_ref, gb_ref, gc_ref, wa_ref, wb_ref, wc_ref,
                wo_ref, g_ref, h_ref, o_ref):
    ya = jnp.dot(oa_ref[...], wa_ref[...], preferred_element_type=F32)
    yb = jnp.dot(ob_ref[...], wb_ref[...], preferred_element_type=F32)
    yc = jnp.dot(oc_ref[...], wc_ref[...], preferred_element_type=F32)
    mix = _sigmoid(ga_ref[...]) * ya + _sigmoid(gb_ref[...]) * yb + _sigmoid(gc_ref[...]) * yc
    y = _mm(mix, wo_ref[...])
    o_ref[...] = h_ref[...] + _rms(y, g_ref[...])


def _mix(oa, ob, oc, z, wa, wb, wc, wo, g, h, *, tm):
    R, D = h.shape
    row = lambda i: (i, 0)
    const = lambda i: (0, 0)
    gate = lambda off: pl.BlockSpec((tm, D), lambda i: (i, off // D))
    return pl.pallas_call(
        _mix_kernel,
        grid=(R // tm,),
        in_specs=[pl.BlockSpec((tm, D), row), pl.BlockSpec((tm, D), row), pl.BlockSpec((tm, D), row),
                  gate(Z_GATE_A), gate(Z_GATE_B), gate(Z_GATE_C),
                  pl.BlockSpec((D, D), const), pl.BlockSpec((D, D), const), pl.BlockSpec((D, D), const),
                  pl.BlockSpec((D, D), const), pl.BlockSpec((1, D), const), pl.BlockSpec((tm, D), row)],
        out_specs=pl.BlockSpec((tm, D), row),
        out_shape=jax.ShapeDtypeStruct((R, D), F32),
        compiler_params=_params(("parallel",)),
        name="mix_out",
    )(oa, ob, oc, z, z, z, wa, wb, wc, wo, g, h)


def _gla_kernel(q_ref, k_ref, v_ref, r_ref, a_ref, wa_ref, ba_ref, gh_ref, o_ref, s_ref, *, front, nchunks):
    C = LANES
    tb = pl.program_id(1)

    @pl.when(tb == 0)
    def _():
        s_ref[...] = jnp.zeros_like(s_ref)

    r0 = lax.broadcasted_iota(I32, (C, C), 0)
    r1 = lax.broadcasted_iota(I32, (C, C), 1)
    causal = r0 >= r1
    tril = causal.astype(F32)
    rowi = lax.broadcasted_iota(I32, (C, 1), 0)

    def body(c, carry):
        sl = pl.ds(pl.multiple_of(c * C, C), C)
        live = ((tb * MIX_CHUNKS + c) * C + rowi) >= front
        a = a_ref[0, sl, :].astype(MXU_DT)
        H = range(GLA_H)
        kq = [slice(h * GLA_DK, (h + 1) * GLA_DK) for h in H]
        kv = [slice(h * GLA_DV, (h + 1) * GLA_DV) for h in H]
        g = [_log_sigmoid(jnp.dot(a, wa_ref[h], preferred_element_type=F32) + ba_ref[h]) / GLA_TAU for h in H]
        G = [jnp.dot(tril, g[h], precision=HI, preferred_element_type=F32) for h in H]
        q = [q_ref[0, sl, kq[h]].astype(F32) * (GLA_DK ** -0.5) for h in H]
        k = [jnp.where(live, k_ref[0, sl, kq[h]].astype(F32), 0.0) for h in H]
        v = [jnp.where(live, v_ref[0, sl, kv[h]].astype(F32), 0.0).astype(MXU_DT) for h in H]
        S = [s_ref[h] for h in H]
        g_last = [G[h][C - 1:C, :] for h in H]
        g_mid = [G[h][C // 2 - 1:C // 2, :] for h in H]
        o_inter = [_mm(q[h] * jnp.exp(G[h]), S[h]) for h in H]
        A = [jnp.where(causal, _mm_nt(q[h] * jnp.exp(G[h] - g_mid[h]), k[h] * jnp.exp(g_mid[h] - G[h])), 0.0) for h in H]
        o_intra = [_mm(A[h], v[h]) for h in H]
        kd_t = [jnp.transpose(k[h] * jnp.exp(g_last[h] - G[h])) for h in H]
        dec_col = [jnp.transpose(jnp.broadcast_to(jnp.exp(g_last[h]), (C, GLA_DK))) for h in H]
        for h in H:
            s_ref[h] = jnp.concatenate([dec_col[h], dec_col[h]], axis=1) * S[h] + _mm(kd_t[h], v[h])
        for h in H:
            r = r_ref[0, sl, kv[h]].astype(F32)
            o = _rms(o_inter[h] + o_intra[h], gh_ref[...]) * (r * _sigmoid(r))
            o_ref[0, sl, kv[h]] = o.astype(o_ref.dtype)
        return carry

    lax.fori_loop(0, jnp.minimum(MIX_CHUNKS, nchunks - tb * MIX_CHUNKS), body, 0)


def _gla(zl, zh, wa, ba, gh, *, front):
    B, Tp, _ = zl.shape
    nchunks = Tp // LANES
    TB = min(MIX_CHUNKS * LANES, Tp)
    WK, WV = GLA_H * GLA_DK, GLA_H * GLA_DV
    return pl.pallas_call(
        functools.partial(_gla_kernel, front=front, nchunks=nchunks),
        grid=(B, pl.cdiv(Tp, TB)),
        in_specs=[pl.BlockSpec((1, TB, WK), lambda b, t: (b, t, Z_GLA_Q // WK)),
                  pl.BlockSpec((1, TB, WK), lambda b, t: (b, t, Z_GLA_K // WK)),
                  pl.BlockSpec((1, TB, WV), lambda b, t: (b, t, Z_GLA_V // WV)),
                  pl.BlockSpec((1, TB, WV), lambda b, t: (b, t, Z_GLA_R // WV)),
                  pl.BlockSpec((1, TB, LANES), lambda b, t: (b, t, Z_SM0 // LANES)),
                  pl.BlockSpec((GLA_H, LANES, GLA_DK), lambda b, t: (0, 0, 0)),
                  pl.BlockSpec((GLA_H, 1, GLA_DK), lambda b, t: (0, 0, 0)),
                  pl.BlockSpec((1, GLA_DV), lambda b, t: (0, 0))],
        out_specs=pl.BlockSpec((1, TB, WV), lambda b, t: (b, t, 0)),
        out_shape=jax.ShapeDtypeStruct((B, Tp, WV), ACT_DT),
        scratch_shapes=[pltpu.VMEM((GLA_H, GLA_DK, GLA_DV), F32)],
        compiler_params=_params(("parallel", "arbitrary")),
        name="gla",
    )(zl, zl, zl, zl, zh, wa, ba, gh)


def _mlstm_kernel(q_ref, k_ref, v_ref, og_ref, gcol_ref, grow_ref, cw_ref, fbrow_ref, fb_ref, gh_ref, o_ref,
                  s_ref, m_ref, pq_ref, pk_ref, *, front, nchunks):
    C = LANES
    WK = ML_H * ML_DK
    tb = pl.program_id(1)

    @pl.when(tb == 0)
    def _():
        s_ref[...] = jnp.zeros_like(s_ref)
        m_ref[...] = jnp.zeros_like(m_ref)
        pq_ref[...] = jnp.zeros_like(pq_ref)
        pk_ref[...] = jnp.zeros_like(pk_ref)

    r0 = lax.broadcasted_iota(I32, (C, C), 0)
    r1 = lax.broadcasted_iota(I32, (C, C), 1)
    causal = r0 >= r1
    tril = causal.astype(F32)
    triu = (r0 <= r1).astype(F32)
    rowi = lax.broadcasted_iota(I32, (C, 1), 0)

    def conv_silu(x, tail, w):
        acc = x * w[ML_CONV - 1:ML_CONV, :]
        for d in range(1, ML_CONV):
            sh = pltpu.roll(x, d, 0)
            head = jnp.where(rowi[:SUBLANES] >= d, sh[:SUBLANES], pltpu.roll(tail, d, 0))
            sh = jnp.concatenate([head, sh[SUBLANES:]], axis=0)
            acc = acc + sh * w[ML_CONV - 1 - d:ML_CONV - d, :]
        return acc * _sigmoid(acc)

    def body(c, carry):
        sl = pl.ds(pl.multiple_of(c * C, C), C)
        cg = tb * MIX_CHUNKS + c
        live = (cg * C + rowi) >= front
        xq = jnp.where(live, q_ref[0, sl, :].astype(F32), 0.0)
        xk = jnp.where(live, k_ref[0, sl, :].astype(F32), 0.0)
        q_all = conv_silu(xq, pq_ref[...], cw_ref[:, :WK]) * (ML_DK ** -0.5)
        k_all = conv_silu(xk, pk_ref[...], cw_ref[:, WK:])
        pq_ref[...] = xq[C - SUBLANES:, :]
        pk_ref[...] = xk[C - SUBLANES:, :]
        lf_col = _log_sigmoid(gcol_ref[0, sl, :] + fbrow_ref[...])
        H = range(ML_H)
        vs = [slice(h * ML_DV, (h + 1) * ML_DV) for h in H]
        q = [q_all[:, h * ML_DK:(h + 1) * ML_DK].astype(MXU_DT) for h in H]
        k = [k_all[:, h * ML_DK:(h + 1) * ML_DK] for h in H]
        ones = jnp.ones((C, LANES), MXU_DT)
        v_aug = [jnp.concatenate([jnp.where(live, v_ref[0, sl, vs[h]].astype(F32), 0.0).astype(MXU_DT), ones], axis=1)
                 for h in H]
        m_prev = [m_ref[h, 0:1, :] for h in H]
        S = [s_ref[h] for h in H]
        sel = [jnp.dot(lf_col, (r0 == ML_H + h).astype(F32), precision=HI, preferred_element_type=F32) for h in H]
        b_bc = [jnp.dot(tril, sel[h], precision=HI, preferred_element_type=F32) for h in H]
        li_row = [grow_ref[0, h, pl.ds(cg, 1), :] for h in H]
        lf_row = [_log_sigmoid(grow_ref[0, ML_H + h, pl.ds(cg, 1), :] + fb_ref[h]) for h in H]
        b_row = [jnp.dot(lf_row[h], triu, precision=HI, preferred_element_type=F32) for h in H]
        b_last = [b_row[h][:, C - 1:C] for h in H]
        qk = [_mm_nt(q[h], k[h]) for h in H]
        qs = [_mm(q[h], S[h]) for h in H]
        Dm = [jnp.where(causal, b_bc[h] - (b_row[h] - li_row[h]), -jnp.inf) for h in H]
        m_inter = [b_bc[h] + m_prev[h] for h in H]
        m = [jnp.maximum(m_inter[h], jnp.max(Dm[h], axis=1, keepdims=True)) for h in H]
        pv = [_mm(jnp.exp(Dm[h] - m[h]) * qk[h], v_aug[h]) for h in H]
        dec = [b_last[h] - b_row[h] + li_row[h] for h in H]
        m_new = [jnp.maximum(b_last[h] + m_prev[h], jnp.max(dec[h], axis=1, keepdims=True)) for h in H]
        kt_w = [jnp.transpose(k[h]) * jnp.exp(dec[h] - m_new[h]) for h in H]
        for h in H:
            w_old = jnp.exp(b_last[h] + m_prev[h] - m_new[h])
            s_ref[h] = jnp.concatenate([w_old, w_old, w_old], axis=1) * S[h] + _mm(kt_w[h], v_aug[h])
            m_ref[h] = jnp.broadcast_to(m_new[h], (SUBLANES, C))
        for h in H:
            w_inter = jnp.exp(m_inter[h] - m[h])
            num = jnp.concatenate([w_inter, w_inter, w_inter], axis=1) * qs[h] + pv[h]
            den = num[:, ML_DV:]
            dd = jnp.maximum(jnp.abs(den), jnp.exp(-m[h]))
            hval = num[:, :ML_DV] / jnp.concatenate([dd, dd], axis=1)
            og = og_ref[0, sl, vs[h]].astype(F32)
            o_ref[0, sl, vs[h]] = (_rms(hval, gh_ref[...]) * _sigmoid(og)).astype(o_ref.dtype)
        return carry

    lax.fori_loop(0, jnp.minimum(MIX_CHUNKS, nchunks - tb * MIX_CHUNKS), body, 0)


def _mlstm(zl, zh, gates_row, conv_w, fb_row, fb, gh, *, front):
    B, Tp, _ = zl.shape
    nchunks = Tp // LANES
    TB = min(MIX_CHUNKS * LANES, Tp)
    WK, WV = ML_H * ML_DK, ML_H * ML_DV
    return pl.pallas_call(
        functools.partial(_mlstm_kernel, front=front, nchunks=nchunks),
        grid=(B, pl.cdiv(Tp, TB)),
        in_specs=[pl.BlockSpec((1, TB, WK), lambda b, t: (b, t, Z_ML_Q // WK)),
                  pl.BlockSpec((1, TB, WK), lambda b, t: (b, t, Z_ML_K // WK)),
                  pl.BlockSpec((1, TB, WV), lambda b, t: (b, t, Z_ML_V // WV)),
                  pl.BlockSpec((1, TB, WV), lambda b, t: (b, t, Z_ML_O // WV)),
                  pl.BlockSpec((1, TB, LANES), lambda b, t: (b, t, Z_SM2 // LANES)),
                  pl.BlockSpec((1, 2 * ML_H, nchunks, LANES), lambda b, t: (b, 0, 0, 0)),
                  pl.BlockSpec((ML_CONV, 2 * WK), lambda b, t: (0, 0)),
                  pl.BlockSpec((1, LANES), lambda b, t: (0, 0)),
                  pl.BlockSpec(memory_space=pltpu.SMEM),
                  pl.BlockSpec((1, ML_DV), lambda b, t: (0, 0))],
        out_specs=pl.BlockSpec((1, TB, WV), lambda b, t: (b, t, 0)),
        out_shape=jax.ShapeDtypeStruct((B, Tp, WV), ACT_DT),
        scratch_shapes=[pltpu.VMEM((ML_H, ML_DK, ML_DV + LANES), F32), pltpu.VMEM((ML_H, SUBLANES, LANES), F32),
                        pltpu.VMEM((SUBLANES, WK), F32), pltpu.VMEM((SUBLANES, WK), F32)],
        compiler_params=_params(("parallel", "arbitrary")),
        name="mlstm",
    )(zl, zl, zl, zl, zh, gates_row, conv_w, fb_row, fb, gh)


def _dsa_prep_kernel(cq_ref, ckv_ref, sm_ref, gq_ref, gkv_ref, lng_ref, lnb_ref, wuq_ref, wiq_ref,
                     q_ref, qi_ref, kv_ref, ki_ref):
    cq = _rms(cq_ref[0], gq_ref[...]).astype(MXU_DT)
    for hp in range(DSA_H // 2):
        w2 = wuq_ref[:, 2 * hp * DSA_KV_LAT:(2 * hp + 2) * DSA_KV_LAT]
        qf = jnp.dot(cq, w2, preferred_element_type=F32) * (DSA_KV_LAT ** -0.5 * LOG2E)
        q_ref[0, 2 * hp] = qf[:, :DSA_KV_LAT].astype(q_ref.dtype)
        q_ref[0, 2 * hp + 1] = qf[:, DSA_KV_LAT:].astype(q_ref.dtype)
    for hh in range(IDX_H):
        qi = jnp.dot(cq, wiq_ref[hh], preferred_element_type=F32) * (IDX_D ** -0.5)
        qi_ref[0, hh] = qi.astype(qi_ref.dtype)
    kv_ref[0] = _rms(ckv_ref[0], gkv_ref[...]).astype(kv_ref.dtype)
    x = sm_ref[0][:, :IDX_D]
    mu = jnp.mean(x, axis=-1, keepdims=True)
    var = jnp.mean(jnp.square(x - mu), axis=-1, keepdims=True)
    ki_ref[0] = ((x - mu) * lax.rsqrt(var + EPS) * lng_ref[...] + lnb_ref[...]).astype(ki_ref.dtype)


def _dsa_prep(z, gq, gkv, lng, lnb, wuq, wiq):
    B, Tp, _ = z.shape
    tm = max(t for t in range(2 * SUBLANES, PREP_ROWS_MAX + 1, 2 * SUBLANES) if Tp % t == 0)
    c2 = lambda b, i: (0, 0)
    return pl.pallas_call(
        _dsa_prep_kernel,
        grid=(B, Tp // tm),
        in_specs=[pl.BlockSpec((1, tm, DSA_Q_LORA), lambda b, i: (b, i, Z_CQ // DSA_Q_LORA)),
                  pl.BlockSpec((1, tm, DSA_KV_LAT), lambda b, i: (b, i, Z_CKV // DSA_KV_LAT)),
                  pl.BlockSpec((1, tm, LANES), lambda b, i: (b, i, Z_SM1 // LANES)),
                  pl.BlockSpec((1, DSA_Q_LORA), c2), pl.BlockSpec((1, DSA_KV_LAT), c2),
                  pl.BlockSpec((1, IDX_D), c2), pl.BlockSpec((1, IDX_D), c2),
                  pl.BlockSpec((DSA_Q_LORA, DSA_H * DSA_KV_LAT), c2),
                  pl.BlockSpec((IDX_H, DSA_Q_LORA, IDX_D), lambda b, i: (0, 0, 0))],
        out_specs=[pl.BlockSpec((1, DSA_H, tm, DSA_KV_LAT), lambda b, i: (b, 0, i, 0)),
                   pl.BlockSpec((1, IDX_H, tm, IDX_D), lambda b, i: (b, 0, i, 0)),
                   pl.BlockSpec((1, tm, DSA_KV_LAT), lambda b, i: (b, i, 0)),
                   pl.BlockSpec((1, tm, IDX_D), lambda b, i: (b, i, 0))],
        out_shape=[jax.ShapeDtypeStruct((B, DSA_H, Tp, DSA_KV_LAT), ACT_DT),
                   jax.ShapeDtypeStruct((B, IDX_H, Tp, IDX_D), ACT_DT),
                   jax.ShapeDtypeStruct((B, Tp, DSA_KV_LAT), ACT_DT),
                   jax.ShapeDtypeStruct((B, Tp, IDX_D), ACT_DT)],
        compiler_params=_params(("parallel", "parallel")),
        name="dsa_prep",
    )(z, z, z, gq, gkv, lng, lnb, wuq, wiq)


def _sortable(x):
    bits = pltpu.bitcast(x, I32)
    key = jnp.where(bits < 0, bits ^ 0x7FFFFFFF, bits)
    return jnp.where(x == 0.0, 0, key)


_KEY_NEG = int(np.array(NEG, np.float32).view(np.int32)) ^ 0x7FFFFFFF


def _dsa_kernel(ki_ref, kv_ref, kvt_ref, qi_ref, q_ref, wi_ref, bias_ref, wuv_ref, o_ref,
                key_ref, madd_ref, lg_ref, acc_ref, *, front, topk, Tp):
    QB = LANES
    j = pl.program_id(1)
    nkb = j + 1
    ngrp = nkb // KB_GROUP
    s_loc = lax.broadcasted_iota(I32, (QB, QB), 0)
    t_glob = j * QB + lax.broadcasted_iota(I32, (QB, QB), 1)
    wi = wi_ref[0] * (IDX_H ** -0.5)
    n_rest = Tp - nkb * QB

    def blk(kb):
        return pl.ds(pl.multiple_of(kb * QB, QB), QB)

    def score_block(kb):
        ki = ki_ref[0, blk(kb), :]
        acc = jnp.zeros((QB, QB), F32)
        for hp in range(IDX_H // 2):
            qi2 = qi_ref[0, 2 * hp:2 * hp + 2].reshape(2 * QB, IDX_D)
            s = lax.dot_general(ki, qi2, NT_DIMS, preferred_element_type=F32)
            acc = acc + jnp.maximum(s[:, :QB], 0.0) * wi[2 * hp:2 * hp + 1, :]
            acc = acc + jnp.maximum(s[:, QB:], 0.0) * wi[2 * hp + 1:2 * hp + 2, :]
        s_glob = kb * QB + s_loc
        key = _sortable(jnp.where(s_glob <= t_glob, acc, NEG))
        key_ref[blk(kb), :] = jnp.where(s_glob >= front, key, INT_MIN)

    def score_group(i, c):
        for u in range(KB_GROUP):
            score_block(KB_GROUP * i + u)
        return c

    def score_single(kb, c):
        score_block(kb)
        return c

    lax.fori_loop(0, ngrp, score_group, 0)
    lax.fori_loop(ngrp * KB_GROUP, nkb, score_single, 0)

    def fold32(hit):
        return jnp.sum(hit.reshape(QB // SUBLANES, SUBLANES, QB), axis=0)

    def count(pred_fn):
        def hit_of(kb):
            return pred_fn(key_ref[blk(kb), :], kb).astype(I32)

        def group(i, acc):
            hit = hit_of(KB_GROUP * i)
            for u in range(1, KB_GROUP):
                hit = hit + hit_of(KB_GROUP * i + u)
            return acc + fold32(hit)

        acc = lax.fori_loop(0, ngrp, group, jnp.zeros((SUBLANES, QB), I32))
        acc = lax.fori_loop(ngrp * KB_GROUP, nkb, lambda kb, a: a + fold32(hit_of(kb)), acc)
        return jnp.sum(acc, axis=0, keepdims=True)

    def count_ge(cand):
        return count(lambda kblk, kb: kblk >= cand) + jnp.where(_KEY_NEG >= cand, n_rest, 0)

    base = jnp.where(count_ge(jnp.zeros((1, QB), I32)) >= topk, 0, INT_MIN).astype(I32)

    def bit_body(i, base):
        cand = base | jnp.left_shift(jnp.int32(1), 30 - i)
        return jnp.where(count_ge(cand) >= topk, cand, base)

    tau = lax.fori_loop(0, 31, bit_body, base)

    def valid_of(kb):
        s_glob = kb * QB + s_loc
        return (s_glob >= front) & (s_glob <= t_glob)

    c_gt = count(lambda kblk, kb: kblk > tau) + jnp.where(_KEY_NEG > tau, n_rest, 0)
    c_eq = count(lambda kblk, kb: (kblk == tau) & valid_of(kb))
    need = topk - c_gt

    @pl.when(jnp.max((c_eq > need).astype(I32)) > 0)
    def _():
        lstrict = (s_loc > lax.broadcasted_iota(I32, (QB, QB), 1)).astype(MXU_DT)
        needf = need.astype(F32)

        def tie_body(kb, seen):
            kblk = key_ref[blk(kb), :]
            tie = (kblk == tau) & valid_of(kb)
            tief = tie.astype(F32)
            rank = seen + jnp.dot(lstrict, tief.astype(MXU_DT), preferred_element_type=F32)
            key_ref[blk(kb), :] = jnp.where(tie & (rank >= needf), INT_MIN, kblk)
            return seen + jnp.sum(tief, axis=0, keepdims=True)

        lax.fori_loop(0, nkb, tie_body, jnp.zeros((1, QB), F32))

    def mask_body(kb, c):
        sel = (key_ref[blk(kb), :] >= tau) & valid_of(kb)
        madd_ref[blk(kb), :] = jnp.where(sel, 0.0, NEG)
        return c

    lax.fori_loop(0, nkb, mask_body, 0)

    NP = DSA_H // 2
    grp = (QB // SUBLANES, SUBLANES, 2 * QB)
    acc_ref[...] = jnp.zeros_like(acc_ref)

    def score_inputs(kb):
        ma = madd_ref[blk(kb), :]
        return kv_ref[0, blk(kb), :], jnp.concatenate([ma, ma], axis=1), jnp.minimum(j - kb, 2)

    def stage1(p, slot, inputs):
        kvb, ma2, rel = inputs
        q2 = q_ref[0, 2 * p:2 * p + 2].reshape(2 * QB, DSA_KV_LAT)
        lg = lax.dot_general(kvb, q2, NT_DIMS, preferred_element_type=F32) + bias_ref[p, rel] + ma2
        lg_ref[slot, p] = lg
        return jnp.max(jnp.max(lg.reshape(grp), axis=0), axis=0, keepdims=True)

    def stage2(p, slot, kvt, m_old, bm):
        m_new = jnp.maximum(m_old, bm)
        pm = jnp.exp2(lg_ref[slot, p] - m_new)
        acc_ref[p] = (jnp.exp2(m_old - m_new) * acc_ref[p]
                      + jnp.dot(kvt, pm.astype(MXU_DT), preferred_element_type=F32))
        return m_new

    inp0 = score_inputs(0)
    bm0 = tuple(stage1(p, 0, inp0) for p in range(NP))

    def attn_body(i, carry):
        ms, bm_a = carry
        a = 2 * i
        inp_b = score_inputs(a + 1)
        inp_c = score_inputs(jnp.minimum(a + 2, nkb - 1))
        kvt_a = kvt_ref[0, a]
        kvt_b = kvt_ref[0, a + 1]
        ms_a, bm_b = [], []
        for p in range(NP):
            bm_b.append(stage1(p, 1, inp_b))
            ms_a.append(stage2(p, 0, kvt_a, ms[p], bm_a[p]))
        ms_b, bm_c = [], []
        for p in range(NP):
            bm_c.append(stage1(p, 0, inp_c))
            ms_b.append(stage2(p, 1, kvt_b, ms_a[p], bm_b[p]))
        return tuple(ms_b), tuple(bm_c)

    ms, bm_last = lax.fori_loop(0, nkb // 2, attn_body,
                                (tuple(jnp.full((1, 2 * QB), -jnp.inf, F32) for _ in range(NP)), bm0))

    @pl.when(nkb % 2 == 1)
    def _():
        kvt = kvt_ref[0, nkb - 1]
        for p in range(NP):
            stage2(p, 0, kvt, ms[p], bm_last[p])

    for p in range(NP):
        acc = acc_ref[p]
        o_t = acc[:DSA_KV_LAT] / acc[DSA_KV_LAT:DSA_KV_LAT + 1]
        outs = []
        for hh in range(2):
            o_h = jnp.transpose(o_t[:, hh * QB:(hh + 1) * QB])
            outs.append(_mm(o_h, wuv_ref[2 * p + hh]))
        o_ref[0, :, 2 * p * DSA_DH:(2 * p + 2) * DSA_DH] = jnp.concatenate(outs, axis=1).astype(o_ref.dtype)


def _dsa(ki, kv, kvt, qi, q, wi_t, bias_tiles, wuv, *, front, topk):
    B, Tp, _ = kv.shape
    nblk = Tp // LANES
    return pl.pallas_call(
        functools.partial(_dsa_kernel, front=front, topk=topk, Tp=Tp),
        grid=(B, nblk),
        in_specs=[pl.BlockSpec((1, Tp, IDX_D), lambda b, j: (b, 0, 0)),
                  pl.BlockSpec((1, Tp, DSA_KV_LAT), lambda b, j: (b, 0, 0)),
                  pl.BlockSpec((1, nblk, DSA_KV_LAT + KVT_ONES, LANES), lambda b, j: (b, 0, 0, 0)),
                  pl.BlockSpec((1, IDX_H, LANES, IDX_D), lambda b, j: (b, 0, j, 0)),
                  pl.BlockSpec((1, DSA_H, LANES, DSA_KV_LAT), lambda b, j: (b, 0, j, 0)),
                  pl.BlockSpec((1, IDX_H, LANES), lambda b, j: (b, 0, j)),
                  pl.BlockSpec((DSA_H // 2, 3, LANES, 2 * LANES), lambda b, j: (0, 0, 0, 0)),
                  pl.BlockSpec((DSA_H, DSA_KV_LAT, DSA_DH), lambda b, j: (0, 0, 0))],
        out_specs=pl.BlockSpec((1, LANES, DSA_H * DSA_DH), lambda b, j: (b, j, 0)),
        out_shape=jax.ShapeDtypeStruct((B, Tp, DSA_H * DSA_DH), ACT_DT),
        scratch_shapes=[pltpu.VMEM((Tp, LANES), I32), pltpu.VMEM((Tp, LANES), F32),
                        pltpu.VMEM((2, DSA_H // 2, LANES, 2 * LANES), F32),
                        pltpu.VMEM((DSA_H // 2, DSA_KV_LAT + KVT_ONES, 2 * LANES), F32)],
        compiler_params=_params(("parallel", "arbitrary")),
        name="dsa_attn",
    )(ki, kv, kvt, qi, q, wi_t, bias_tiles, wuv)


def _t5_bucket(rel):
    n = jnp.maximum(rel, 0)
    max_exact = T5_BUCKETS // 2
    nf = jnp.maximum(n, 1).astype(F32)
    large = max_exact + (jnp.log(nf / max_exact) / math.log(T5_MAX_DIST / max_exact)
                         * (T5_BUCKETS - max_exact)).astype(I32)
    large = jnp.minimum(large, T5_BUCKETS - 1)
    return jnp.where(n < max_exact, n, large)


def _t5_tiles(t5_bias):
    s = jnp.arange(LANES, dtype=I32)[:, None]
    t = jnp.arange(LANES, dtype=I32)[None, :]
    tiles = []
    for shift in (0, LANES, 2 * LANES):
        tiles.append(t5_bias[_t5_bucket(shift + t - s)])
    tiles = jnp.stack(tiles, axis=0)
    tiles = tiles.transpose(3, 0, 1, 2).reshape(DSA_H // 2, 2, 3, LANES, LANES)
    return (tiles.transpose(0, 2, 3, 1, 4).reshape(DSA_H // 2, 3, LANES, 2 * LANES) * LOG2E).astype(F32)


def _pack_in_proj(w_in, b_in):
    splits = np.cumsum([GLA_H * GLA_DK, GLA_H * GLA_DK, GLA_H * GLA_DV, GLA_H * GLA_DV, GLA_LR,
                        DSA_Q_LORA, DSA_KV_LAT, IDX_D, IDX_H,
                        ML_H * ML_DK, ML_H * ML_DK, ML_H * ML_DV, ML_H * ML_DV, ML_H, ML_H,
                        D_MODEL, D_MODEL])[:].tolist()

    def pack(a):
        (gq, gk, gv, gr, ga, cq, ckv, ik, iw, mq, mk, mv, mo, mi, mf, a_, b_, c_) = jnp.split(a, splits, axis=-1)
        pad = lambda n: jnp.zeros(a.shape[:-1] + (n,), a.dtype)
        lo = jnp.concatenate([gq, gk, gv, gr, mq, mk, mv, mo], axis=-1)
        hi = jnp.concatenate([a_, b_, c_, cq, ckv,
                              ga, pad(LANES - GLA_LR),
                              ik, iw, pad(LANES - IDX_D - IDX_H),
                              mi, mf, pad(LANES - 2 * ML_H)], axis=-1)
        return lo, hi

    return pack(w_in), pack(b_in)


def kernel(x, meta_tokens, t5_bias, g_pre_mix, w_in, b_in, w_gla_a2, b_gla_a, g_gla_head, w_br_gla, g_dsa_q, w_dsa_uq, w_idx_q, g_dsa_kv, w_dsa_uv, idx_ln_g, idx_ln_b, w_br_dsa, ml_conv, ml_f_bias, g_ml_head, w_br_ml, w_out, g_post_mix, g_pre_mlp, w_up, w_down, g_post_mlp):
    B, S, D = x.shape
    L = w_in.shape[0]
    T = S + N_META
    front = (-T) % LANES
    Tp = T + front
    R = B * Tp
    nblk = Tp // LANES
    topk = min(TOPK_MAX, (T - N_META) // 4)
    tm_mid = 512 if R % 512 == 0 else LANES
    tm_small = 512 if R % 512 == 0 else LANES

    (w_lo, w_hi), (b_lo, b_hi) = _pack_in_proj(w_in, b_in)
    wa2 = jnp.pad(w_gla_a2, ((0, 0), (0, LANES - GLA_LR), (0, 0)))
    wa2 = wa2.reshape(L, LANES, GLA_H, GLA_DK).transpose(0, 2, 1, 3).astype(MXU_DT)
    fb_row = jnp.pad(ml_f_bias, ((0, 0), (ML_H, LANES - 2 * ML_H)))[:, None, :]
    layers = dict(
        g_pre_mix=g_pre_mix[:, None, :], w_lo=w_lo.astype(MXU_DT), b_lo=b_lo[:, None, :],
        w_hi=w_hi.astype(MXU_DT), b_hi=b_hi[:, None, :],
        wa2=wa2, ba=b_gla_a.reshape(L, GLA_H, 1, GLA_DK), g_gla_head=g_gla_head[:, None, :],
        w_br_gla=w_br_gla.astype(MXU_DT),
        g_dsa_q=g_dsa_q[:, None, :], w_dsa_uq=w_dsa_uq.astype(MXU_DT),
        w_idx_q=w_idx_q.reshape(L, DSA_Q_LORA, IDX_H, IDX_D).transpose(0, 2, 1, 3).astype(MXU_DT),
        g_dsa_kv=g_dsa_kv[:, None, :], w_dsa_uv=w_dsa_uv.astype(MXU_DT),
        idx_ln_g=idx_ln_g[:, None, :], idx_ln_b=idx_ln_b[:, None, :], w_br_dsa=w_br_dsa.astype(MXU_DT),
        ml_conv=ml_conv, fb_row=fb_row, ml_f_bias=ml_f_bias, g_ml_head=g_ml_head[:, None, :],
        w_br_ml=w_br_ml.astype(MXU_DT), w_out=w_out.astype(MXU_DT), g_post_mix=g_post_mix[:, None, :],
        g_pre_mlp=g_pre_mlp[:, None, :], w_up=w_up.astype(MXU_DT), w_down=w_down.astype(MXU_DT),
        g_post_mlp=g_post_mlp[:, None, :],
    )
    bias_tiles = _t5_tiles(t5_bias)
    zero_ff = jnp.zeros((1, D_FF), F32)

    meta = jnp.broadcast_to(meta_tokens.astype(x.dtype)[None], (B, N_META, D))
    h0 = jnp.concatenate([jnp.zeros((B, front, D), x.dtype), meta, x], axis=1).reshape(R, D)

    def layer(h, p):
        zl = _norm_matmul(h, p["g_pre_mix"], p["w_lo"], p["b_lo"], tm=tm_mid, tn=1024, relu2=False,
                          out_dtype=ACT_DT, name="in_proj_lo").reshape(B, Tp, NZL)
        zh = _norm_matmul(h, p["g_pre_mix"], p["w_hi"], p["b_hi"], tm=tm_mid, tn=768, relu2=False,
                          out_dtype=F32, name="in_proj_hi")
        zh3 = zh.reshape(B, Tp, NZH)
        o_gla = _gla(zl, zh3, p["wa2"], p["ba"], p["g_gla_head"], front=front)
        gates_row = zh3[:, :, Z_SM2:Z_SM2 + 2 * ML_H].transpose(0, 2, 1).reshape(B, 2 * ML_H, nblk, LANES)
        o_ml = _mlstm(zl, zh3, gates_row, p["ml_conv"], p["fb_row"], p["ml_f_bias"], p["g_ml_head"], front=front)
        q, qi, kv, ki = _dsa_prep(zh3, p["g_dsa_q"], p["g_dsa_kv"], p["idx_ln_g"], p["idx_ln_b"],
                                  p["w_dsa_uq"], p["w_idx_q"])
        kvt = kv.reshape(B, nblk, LANES, DSA_KV_LAT).transpose(0, 1, 3, 2)
        kvt = jnp.concatenate([kvt, jnp.ones((B, nblk, KVT_ONES, LANES), kvt.dtype)], axis=2)
        wi_t = zh3[:, :, Z_SM1 + IDX_D:Z_SM1 + IDX_D + IDX_H].transpose(0, 2, 1)
        o_dsa = _dsa(ki, kv, kvt, qi, q, wi_t, bias_tiles, p["w_dsa_uv"], front=front, topk=topk)
        h = _mix(o_gla.reshape(R, D), o_dsa.reshape(R, D), o_ml.reshape(R, D), zh,
                 p["w_br_gla"], p["w_br_dsa"], p["w_br_ml"], p["w_out"], p["g_post_mix"], h, tm=tm_small)
        u = _norm_matmul(h, p["g_pre_mlp"], p["w_up"], zero_ff, tm=tm_mid, tn=1024,
                         relu2=True, out_dtype=ACT_DT, name="mlp_up")
        h = _matmul_norm_res(u, p["w_down"], p["g_post_mlp"], h, tm=tm_mid)
        return h, None

    h, _ = lax.scan(layer, h0, layers)
    return h.reshape(B, Tp, D)[:, front + N_META:]
```

```python
import functools
import math

import numpy as np
import jax
import jax.numpy as jnp
from jax import lax
from jax.experimental import pallas as pl
from jax.experimental.pallas import tpu as pltpu

D_MODEL = 1024
N_META = 16
GLA_H, GLA_DK, GLA_DV, GLA_LR, GLA_TAU = 4, 128, 256, 16, 16.0
DSA_H, DSA_DH, DSA_Q_LORA, DSA_KV_LAT = 16, 64, 256, 128
IDX_H, IDX_D, TOPK_MAX = 8, 64, 256
ML_H, ML_DK, ML_DV, ML_CONV = 4, 128, 256, 4
T5_BUCKETS, T5_MAX_DIST = 32, 128
D_FF = 4 * D_MODEL
EPS = 1e-6
NEG = -1e30
LOG2E = math.log2(math.e)

LANES = 128
SUBLANES = 8
VMEM_LIMIT = 56 * 1024 * 1024

MXU_DT = jnp.bfloat16
ACT_DT = jnp.bfloat16

F32 = jnp.float32
I32 = jnp.int32
INT_MIN = -2147483648
HI = lax.Precision.HIGHEST
NT_DIMS = (((1,), (1,)), ((), ()))

Z_GLA_Q, Z_GLA_K, Z_GLA_V, Z_GLA_R = 0, 512, 1024, 2048
Z_ML_Q, Z_ML_K, Z_ML_V, Z_ML_O = 3072, 3584, 4096, 5120
NZL = 6144
Z_GATE_A, Z_GATE_B, Z_GATE_C = 0, 1024, 2048
Z_CQ, Z_CKV = 3072, 3328
Z_SM0, Z_SM1, Z_SM2 = 3456, 3584, 3712
NZH = 3840
KVT_ONES = 16
PREP_ROWS_MAX = 1088
DSA_STREAMS = 2
KB_GROUP = 4
MIX_CHUNKS = 6


def _params(sem):
    return pltpu.CompilerParams(dimension_semantics=sem, vmem_limit_bytes=VMEM_LIMIT)


def _mm(a, b):
    return jnp.dot(a.astype(MXU_DT), b.astype(MXU_DT), preferred_element_type=F32)


def _mm_nt(a, b):
    return lax.dot_general(a.astype(MXU_DT), b.astype(MXU_DT), NT_DIMS, preferred_element_type=F32)


def _rms(x, g):
    return x * lax.rsqrt(jnp.mean(x * x, axis=-1, keepdims=True) + EPS) * g


def _log_sigmoid(x):
    return jnp.minimum(x, 0.0) - jnp.log1p(jnp.exp(-jnp.abs(x)))


def _sigmoid(x):
    return 1.0 / (1.0 + jnp.exp(-x))


def _norm_matmul_kernel(x_ref, g_ref, w_ref, b_ref, o_ref, *, relu2, tn):
    xn = _rms(x_ref[...], g_ref[...]).astype(MXU_DT)
    for n0 in range(0, o_ref.shape[1], tn):
        acc = jnp.dot(xn, w_ref[:, n0:n0 + tn], preferred_element_type=F32) + b_ref[:, n0:n0 + tn]
        if relu2:
            acc = jnp.square(jnp.maximum(acc, 0.0))
        o_ref[:, n0:n0 + tn] = acc.astype(o_ref.dtype)


def _norm_matmul(x, g, w, b, *, tm, tn, relu2, out_dtype, name):
    R, K = x.shape
    N = w.shape[1]
    return pl.pallas_call(
        functools.partial(_norm_matmul_kernel, relu2=relu2, tn=tn),
        grid=(R // tm,),
        in_specs=[pl.BlockSpec((tm, K), lambda i: (i, 0)),
                  pl.BlockSpec((1, K), lambda i: (0, 0)),
                  pl.BlockSpec((K, N), lambda i: (0, 0)),
                  pl.BlockSpec((1, N), lambda i: (0, 0))],
        out_specs=pl.BlockSpec((tm, N), lambda i: (i, 0)),
        out_shape=jax.ShapeDtypeStruct((R, N), out_dtype),
        compiler_params=_params(("parallel",)),
        name=name,
    )(x, g, w, b)


def _matmul_norm_res_kernel(u_ref, w_ref, g_ref, h_ref, o_ref):
    y = jnp.dot(u_ref[...], w_ref[...], preferred_element_type=F32)
    o_ref[...] = h_ref[...] + _rms(y, g_ref[...])


def _matmul_norm_res(u, w, g, h, *, tm):
    R, K = u.shape
    N = w.shape[1]
    return pl.pallas_call(
        _matmul_norm_res_kernel,
        grid=(R // tm,),
        in_specs=[pl.BlockSpec((tm, K), lambda i: (i, 0)),
                  pl.BlockSpec((K, N), lambda i: (0, 0)),
                  pl.BlockSpec((1, N), lambda i: (0, 0)),
                  pl.BlockSpec((tm, N), lambda i: (i, 0))],
        out_specs=pl.BlockSpec((tm, N), lambda i: (i, 0)),
        out_shape=jax.ShapeDtypeStruct((R, N), F32),
        compiler_params=_params(("parallel",)),
        name="mlp_down",
    )(u, w, g, h)


def _mix_kernel(oa_ref, ob_ref, oc_ref, ga_ref, gb_ref, gc_ref, wa_ref, wb_ref, wc_ref,
                wo_ref, g_ref, h_ref, o_ref):
    ya = jnp.dot(oa_ref[...], wa_ref[...], preferred_element_type=F32)
    yb = jnp.dot(ob_ref[...], wb_ref[...], preferred_element_type=F32)
    yc = jnp.dot(oc_ref[...], wc_ref[...], preferred_element_type=F32)
    mix = _sigmoid(ga_ref[...]) * ya + _sigmoid(gb_ref[...]) * yb + _sigmoid(gc_ref[...]) * yc
    y = _mm(mix, wo_ref[...])
    o_ref[...] = h_ref[...] + _rms(y, g_ref[...])


def _mix(oa, ob, oc, z, wa, wb, wc, wo, g, h, *, tm):
    R, D = h.shape
    row = lambda i: (i, 0)
    const = lambda i: (0, 0)
    gate = lambda off: pl.BlockSpec((tm, D), lambda i: (i, off // D))
    return pl.pallas_call(
        _mix_kernel,
        grid=(R // tm,),
        in_specs=[pl.BlockSpec((tm, D), row), pl.BlockSpec((tm, D), row), pl.BlockSpec((tm, D), row),
                  gate(Z_GATE_A), gate(Z_GATE_B), gate(Z_GATE_C),
                  pl.BlockSpec((D, D), const), pl.BlockSpec((D, D), const), pl.BlockSpec((D, D), const),
                  pl.BlockSpec((D, D), const), pl.BlockSpec((1, D), const), pl.BlockSpec((tm, D), row)],
        out_specs=pl.BlockSpec((tm, D), row),
        out_shape=jax.ShapeDtypeStruct((R, D), F32),
        compiler_params=_params(("parallel",)),
        name="mix_out",
    )(oa, ob, oc, z, z, z, wa, wb, wc, wo, g, h)


def _gla_kernel(q_ref, k_ref, v_ref, r_ref, a_ref, wa_ref, ba_ref, gh_ref, o_ref, s_ref, *, front, nchunks):
    C = LANES
    tb = pl.program_id(1)

    @pl.when(tb == 0)
    def _():
        s_ref[...] = jnp.zeros_like(s_ref)

    r0 = lax.broadcasted_iota(I32, (C, C), 0)
    r1 = lax.broadcasted_iota(I32, (C, C), 1)
    causal = r0 >= r1
    tril = causal.astype(F32)
    rowi = lax.broadcasted_iota(I32, (C, 1), 0)

    def body(c, carry):
        sl = pl.ds(pl.multiple_of(c * C, C), C)
        live = ((tb * MIX_CHUNKS + c) * C + rowi) >= front
        a = a_ref[0, sl, :].astype(MXU_DT)
        H = range(GLA_H)
        kq = [slice(h * GLA_DK, (h + 1) * GLA_DK) for h in H]
        kv = [slice(h * GLA_DV, (h + 1) * GLA_DV) for h in H]
        g = [_log_sigmoid(jnp.dot(a, wa_ref[h], preferred_element_type=F32) + ba_ref[h]) / GLA_TAU for h in H]
        G = [jnp.dot(tril, g[h], precision=HI, preferred_element_type=F32) for h in H]
        q = [q_ref[0, sl, kq[h]].astype(F32) * (GLA_DK ** -0.5) for h in H]
        k = [jnp.where(live, k_ref[0, sl, kq[h]].astype(F32), 0.0) for h in H]
        v = [jnp.where(live, v_ref[0, sl, kv[h]].astype(F32), 0.0).astype(MXU_DT) for h in H]
        S = [s_ref[h] for h in H]
        g_last = [G[h][C - 1:C, :] for h in H]
        g_mid = [G[h][C // 2 - 1:C // 2, :] for h in H]
        o_inter = [_mm(q[h] * jnp.exp(G[h]), S[h]) for h in H]
        A = [jnp.where(causal, _mm_nt(q[h] * jnp.exp(G[h] - g_mid[h]), k[h] * jnp.exp(g_mid[h] - G[h])), 0.0) for h in H]
        o_intra = [_mm(A[h], v[h]) for h in H]
        kd_t = [jnp.transpose(k[h] * jnp.exp(g_last[h] - G[h])) for h in H]
        dec_col = [jnp.transpose(jnp.broadcast_to(jnp.exp(g_last[h]), (C, GLA_DK))) for h in H]
        for h in H:
            s_ref[h] = jnp.concatenate([dec_col[h], dec_col[h]], axis=1) * S[h] + _mm(kd_t[h], v[h])
        for h in H:
            r = r_ref[0, sl, kv[h]].astype(F32)
            o = _rms(o_inter[h] + o_intra[h], gh_ref[...]) * (r * _sigmoid(r))
            o_ref[0, sl, kv[h]] = o.astype(o_ref.dtype)
        return carry

    lax.fori_loop(0, jnp.minimum(MIX_CHUNKS, nchunks - tb * MIX_CHUNKS), body, 0)


def _gla(zl, zh, wa, ba, gh, *, front):
    B, Tp, _ = zl.shape
    nchunks = Tp // LANES
    TB = min(MIX_CHUNKS * LANES, Tp)
    WK, WV = GLA_H * GLA_DK, GLA_H * GLA_DV
    return pl.pallas_call(
        functools.partial(_gla_kernel, front=front, nchunks=nchunks),
        grid=(B, pl.cdiv(Tp, TB)),
        in_specs=[pl.BlockSpec((1, TB, WK), lambda b, t: (b, t, Z_GLA_Q // WK)),
                  pl.BlockSpec((1, TB, WK), lambda b, t: (b, t, Z_GLA_K // WK)),
                  pl.BlockSpec((1, TB, WV), lambda b, t: (b, t, Z_GLA_V // WV)),
                  pl.BlockSpec((1, TB, WV), lambda b, t: (b, t, Z_GLA_R // WV)),
                  pl.BlockSpec((1, TB, LANES), lambda b, t: (b, t, Z_SM0 // LANES)),
                  pl.BlockSpec((GLA_H, LANES, GLA_DK), lambda b, t: (0, 0, 0)),
                  pl.BlockSpec((GLA_H, 1, GLA_DK), lambda b, t: (0, 0, 0)),
                  pl.BlockSpec((1, GLA_DV), lambda b, t: (0, 0))],
        out_specs=pl.BlockSpec((1, TB, WV), lambda b, t: (b, t, 0)),
        out_shape=jax.ShapeDtypeStruct((B, Tp, WV), ACT_DT),
        scratch_shapes=[pltpu.VMEM((GLA_H, GLA_DK, GLA_DV), F32)],
        compiler_params=_params(("parallel", "arbitrary")),
        name="gla",
    )(zl, zl, zl, zl, zh, wa, ba, gh)


def _mlstm_kernel(q_ref, k_ref, v_ref, og_ref, gcol_ref, grow_ref, cw_ref, fbrow_ref, fb_ref, gh_ref, o_ref,
                  s_ref, m_ref, pq_ref, pk_ref, *, front, nchunks):
    C = LANES
    WK = ML_H * ML_DK
    tb = pl.program_id(1)

    @pl.when(tb == 0)
    def _():
        s_ref[...] = jnp.zeros_like(s_ref)
        m_ref[...] = jnp.zeros_like(m_ref)
        pq_ref[...] = jnp.zeros_like(pq_ref)
        pk_ref[...] = jnp.zeros_like(pk_ref)

    r0 = lax.broadcasted_iota(I32, (C, C), 0)
    r1 = lax.broadcasted_iota(I32, (C, C), 1)
    causal = r0 >= r1
    tril = causal.astype(F32)
    triu = (r0 <= r1).astype(F32)
    rowi = lax.broadcasted_iota(I32, (C, 1), 0)

    def conv_silu(x, tail, w):
        acc = x * w[ML_CONV - 1:ML_CONV, :]
        for d in range(1, ML_CONV):
            sh = pltpu.roll(x, d, 0)
            head = jnp.where(rowi[:SUBLANES] >= d, sh[:SUBLANES], pltpu.roll(tail, d, 0))
            sh = jnp.concatenate([head, sh[SUBLANES:]], axis=0)
            acc = acc + sh * w[ML_CONV - 1 - d:ML_CONV - d, :]
        return acc * _sigmoid(acc)

    def body(c, carry):
        sl = pl.ds(pl.multiple_of(c * C, C), C)
        cg = tb * MIX_CHUNKS + c
        live = (cg * C + rowi) >= front
        xq = jnp.where(live, q_ref[0, sl, :].astype(F32), 0.0)
        xk = jnp.where(live, k_ref[0, sl, :].astype(F32), 0.0)
        q_all = conv_silu(xq, pq_ref[...], cw_ref[:, :WK]) * (ML_DK ** -0.5)
        k_all = conv_silu(xk, pk_ref[...], cw_ref[:, WK:])
        pq_ref[...] = xq[C - SUBLANES:, :]
        pk_ref[...] = xk[C - SUBLANES:, :]
        lf_col = _log_sigmoid(gcol_ref[0, sl, :] + fbrow_ref[...])
        H = range(ML_H)
        vs = [slice(h * ML_DV, (h + 1) * ML_DV) for h in H]
        q = [q_all[:, h * ML_DK:(h + 1) * ML_DK].astype(MXU_DT) for h in H]
        k = [k_all[:, h * ML_DK:(h + 1) * ML_DK] for h in H]
        ones = jnp.ones((C, LANES), MXU_DT)
        v_aug = [jnp.concatenate([jnp.where(live, v_ref[0, sl, vs[h]].astype(F32), 0.0).astype(MXU_DT), ones], axis=1)
                 for h in H]
        m_prev = [m_ref[h, 0:1, :] for h in H]
        S = [s_ref[h] for h in H]
        sel = [jnp.dot(lf_col, (r0 == ML_H + h).astype(F32), precision=HI, preferred_element_type=F32) for h in H]
        b_bc = [jnp.dot(tril, sel[h], precision=HI, preferred_element_type=F32) for h in H]
        li_row = [grow_ref[0, h, pl.ds(cg, 1), :] for h in H]
        lf_row = [_log_sigmoid(grow_ref[0, ML_H + h, pl.ds(cg, 1), :] + fb_ref[h]) for h in H]
        b_row = [jnp.dot(lf_row[h], triu, precision=HI, preferred_element_type=F32) for h in H]
        b_last = [b_row[h][:, C - 1:C] for h in H]
        qk = [_mm_nt(q[h], k[h]) for h in H]
        qs = [_mm(q[h], S[h]) for h in H]
        Dm = [jnp.where(causal, b_bc[h] - (b_row[h] - li_row[h]), -jnp.inf) for h in H]
        m_inter = [b_bc[h] + m_prev[h] for h in H]
        m = [jnp.maximum(m_inter[h], jnp.max(Dm[h], axis=1, keepdims=True)) for h in H]
        pv = [_mm(jnp.exp(Dm[h] - m[h]) * qk[h], v_aug[h]) for h in H]
        dec = [b_last[h] - b_row[h] + li_row[h] for h in H]
        m_new = [jnp.maximum(b_last[h] + m_prev[h], jnp.max(dec[h], axis=1, keepdims=True)) for h in H]
        kt_w = [jnp.transpose(k[h]) * jnp.exp(dec[h] - m_new[h]) for h in H]
        for h in H:
            w_old = jnp.exp(b_last[h] + m_prev[h] - m_new[h])
            s_ref[h] = jnp.concatenate([w_old, w_old, w_old], axis=1) * S[h] + _mm(kt_w[h], v_aug[h])
            m_ref[h] = jnp.broadcast_to(m_new[h], (SUBLANES, C))
        for h in H:
            w_inter = jnp.exp(m_inter[h] - m[h])
            num = jnp.concatenate([w_inter, w_inter, w_inter], axis=1) * qs[h] + pv[h]
            den = num[:, ML_DV:]
            dd = jnp.maximum(jnp.abs(den), jnp.exp(-m[h]))
            hval = num[:, :ML_DV] / jnp.concatenate([dd, dd], axis=1)
            og = og_ref[0, sl, vs[h]].astype(F32)
            o_ref[0, sl, vs[h]] = (_rms(hval, gh_ref[...]) * _sigmoid(og)).astype(o_ref.dtype)
        return carry

    lax.fori_loop(0, jnp.minimum(MIX_CHUNKS, nchunks - tb * MIX_CHUNKS), body, 0)


def _mlstm(zl, zh, gates_row, conv_w, fb_row, fb, gh, *, front):
    B, Tp, _ = zl.shape
    nchunks = Tp // LANES
    TB = min(MIX_CHUNKS * LANES, Tp)
    WK, WV = ML_H * ML_DK, ML_H * ML_DV
    return pl.pallas_call(
        functools.partial(_mlstm_kernel, front=front, nchunks=nchunks),
        grid=(B, pl.cdiv(Tp, TB)),
        in_specs=[pl.BlockSpec((1, TB, WK), lambda b, t: (b, t, Z_ML_Q // WK)),
                  pl.BlockSpec((1, TB, WK), lambda b, t: (b, t, Z_ML_K // WK)),
                  pl.BlockSpec((1, TB, WV), lambda b, t: (b, t, Z_ML_V // WV)),
                  pl.BlockSpec((1, TB, WV), lambda b, t: (b, t, Z_ML_O // WV)),
                  pl.BlockSpec((1, TB, LANES), lambda b, t: (b, t, Z_SM2 // LANES)),
                  pl.BlockSpec((1, 2 * ML_H, nchunks, LANES), lambda b, t: (b, 0, 0, 0)),
                  pl.BlockSpec((ML_CONV, 2 * WK), lambda b, t: (0, 0)),
                  pl.BlockSpec((1, LANES), lambda b, t: (0, 0)),
                  pl.BlockSpec(memory_space=pltpu.SMEM),
                  pl.BlockSpec((1, ML_DV), lambda b, t: (0, 0))],
        out_specs=pl.BlockSpec((1, TB, WV), lambda b, t: (b, t, 0)),
        out_shape=jax.ShapeDtypeStruct((B, Tp, WV), ACT_DT),
        scratch_shapes=[pltpu.VMEM((ML_H, ML_DK, ML_DV + LANES), F32), pltpu.VMEM((ML_H, SUBLANES, LANES), F32),
                        pltpu.VMEM((SUBLANES, WK), F32), pltpu.VMEM((SUBLANES, WK), F32)],
        compiler_params=_params(("parallel", "arbitrary")),
        name="mlstm",
    )(zl, zl, zl, zl, zh, gates_row, conv_w, fb_row, fb, gh)


def _dsa_prep_kernel(cq_ref, ckv_ref, sm_ref, gq_ref, gkv_ref, lng_ref, lnb_ref, wuq_ref, wiq_ref,
                     q_ref, qi_ref, kv_ref, ki_ref):
    cq = _rms(cq_ref[0], gq_ref[...]).astype(MXU_DT)
    for hp in range(DSA_H // 2):
        w2 = wuq_ref[:, 2 * hp * DSA_KV_LAT:(2 * hp + 2) * DSA_KV_LAT]
        qf = jnp.dot(cq, w2, preferred_element_type=F32) * (DSA_KV_LAT ** -0.5 * LOG2E)
        q_ref[0, 2 * hp] = qf[:, :DSA_KV_LAT].astype(q_ref.dtype)
        q_ref[0, 2 * hp + 1] = qf[:, DSA_KV_LAT:].astype(q_ref.dtype)
    for hh in range(IDX_H):
        qi = jnp.dot(cq, wiq_ref[hh], preferred_element_type=F32) * (IDX_D ** -0.5)
        qi_ref[0, hh] = qi.astype(qi_ref.dtype)
    kv_ref[0] = _rms(ckv_ref[0], gkv_ref[...]).astype(kv_ref.dtype)
    x = sm_ref[0][:, :IDX_D]
    mu = jnp.mean(x, axis=-1, keepdims=True)
    var = jnp.mean(jnp.square(x - mu), axis=-1, keepdims=True)
    ki_ref[0] = ((x - mu) * lax.rsqrt(var + EPS) * lng_ref[...] + lnb_ref[...]).astype(ki_ref.dtype)


def _dsa_prep(z, gq, gkv, lng, lnb, wuq, wiq):
    B, Tp, _ = z.shape
    tm = max(t for t in range(2 * SUBLANES, PREP_ROWS_MAX + 1, 2 * SUBLANES) if Tp % t == 0)
    c2 = lambda b, i: (0, 0)
    return pl.pallas_call(
        _dsa_prep_kernel,
        grid=(B, Tp // tm),
        in_specs=[pl.BlockSpec((1, tm, DSA_Q_LORA), lambda b, i: (b, i, Z_CQ // DSA_Q_LORA)),
                  pl.BlockSpec((1, tm, DSA_KV_LAT), lambda b, i: (b, i, Z_CKV // DSA_KV_LAT)),
                  pl.BlockSpec((1, tm, LANES), lambda b, i: (b, i, Z_SM1 // LANES)),
                  pl.BlockSpec((1, DSA_Q_LORA), c2), pl.BlockSpec((1, DSA_KV_LAT), c2),
                  pl.BlockSpec((1, IDX_D), c2), pl.BlockSpec((1, IDX_D), c2),
                  pl.BlockSpec((DSA_Q_LORA, DSA_H * DSA_KV_LAT), c2),
                  pl.BlockSpec((IDX_H, DSA_Q_LORA, IDX_D), lambda b, i: (0, 0, 0))],
        out_specs=[pl.BlockSpec((1, DSA_H, tm, DSA_KV_LAT), lambda b, i: (b, 0, i, 0)),
                   pl.BlockSpec((1, IDX_H, tm, IDX_D), lambda b, i: (b, 0, i, 0)),
                   pl.BlockSpec((1, tm, DSA_KV_LAT), lambda b, i: (b, i, 0)),
                   pl.BlockSpec((1, tm, IDX_D), lambda b, i: (b, i, 0))],
        out_shape=[jax.ShapeDtypeStruct((B, DSA_H, Tp, DSA_KV_LAT), ACT_DT),
                   jax.ShapeDtypeStruct((B, IDX_H, Tp, IDX_D), ACT_DT),
                   jax.ShapeDtypeStruct((B, Tp, DSA_KV_LAT), ACT_DT),
                   jax.ShapeDtypeStruct((B, Tp, IDX_D), ACT_DT)],
        compiler_params=_params(("parallel", "parallel")),
        name="dsa_prep",
    )(z, z, z, gq, gkv, lng, lnb, wuq, wiq)


def _sortable(x):
    bits = pltpu.bitcast(x, I32)
    key = jnp.where(bits < 0, bits ^ 0x7FFFFFFF, bits)
    return jnp.where(x == 0.0, 0, key)


_KEY_NEG = int(np.array(NEG, np.float32).view(np.int32)) ^ 0x7FFFFFFF


def _dsa_kernel(ki_ref, kv_ref, kvt_ref, qi_ref, q_ref, wi_ref, bias_ref, wuv_ref, o_ref,
                key_ref, madd_ref, lg_ref, acc_ref, *, front, topk, Tp, ns):
    QB = LANES
    S = range(ns)
    j = pl.program_id(1)
    nkb = j + 1
    ngrp = nkb // KB_GROUP
    s_loc = lax.broadcasted_iota(I32, (QB, QB), 0)
    t_glob = j * QB + lax.broadcasted_iota(I32, (QB, QB), 1)
    wi = [wi_ref[s] * (IDX_H ** -0.5) for s in S]
    n_rest = Tp - nkb * QB

    def blk(kb):
        return pl.ds(pl.multiple_of(kb * QB, QB), QB)

    def score_block(kb, s):
        ki = ki_ref[s, blk(kb), :]
        acc = jnp.zeros((QB, QB), F32)
        for hp in range(IDX_H // 2):
            qi2 = qi_ref[s, 2 * hp:2 * hp + 2].reshape(2 * QB, IDX_D)
            sc = lax.dot_general(ki, qi2, NT_DIMS, preferred_element_type=F32)
            acc = acc + jnp.maximum(sc[:, :QB], 0.0) * wi[s][2 * hp:2 * hp + 1, :]
            acc = acc + jnp.maximum(sc[:, QB:], 0.0) * wi[s][2 * hp + 1:2 * hp + 2, :]
        s_glob = kb * QB + s_loc
        key = _sortable(jnp.where(s_glob <= t_glob, acc, NEG))
        key_ref[s, blk(kb), :] = jnp.where(s_glob >= front, key, INT_MIN)

    def score_group(i, c):
        for u in range(KB_GROUP):
            for s in S:
                score_block(KB_GROUP * i + u, s)
        return c

    def score_single(kb, c):
        for s in S:
            score_block(kb, s)
        return c

    lax.fori_loop(0, ngrp, score_group, 0)
    lax.fori_loop(ngrp * KB_GROUP, nkb, score_single, 0)

    def fold32(hit):
        return jnp.sum(hit.reshape(QB // SUBLANES, SUBLANES, QB), axis=0)

    def count(pred_fn):
        def hit_of(kb, s):
            return pred_fn(key_ref[s, blk(kb), :], kb, s).astype(I32)

        def group(i, accs):
            out = []
            for s in S:
                hit = hit_of(KB_GROUP * i, s)
                for u in range(1, KB_GROUP):
                    hit = hit + hit_of(KB_GROUP * i + u, s)
                out.append(accs[s] + fold32(hit))
            return tuple(out)

        accs = lax.fori_loop(0, ngrp, group, tuple(jnp.zeros((SUBLANES, QB), I32) for _ in S))
        accs = lax.fori_loop(ngrp * KB_GROUP, nkb,
                             lambda kb, a: tuple(a[s] + fold32(hit_of(kb, s)) for s in S), accs)
        return [jnp.sum(a, axis=0, keepdims=True) for a in accs]

    def count_ge(cands):
        cnt = count(lambda kblk, kb, s: kblk >= cands[s])
        return [cnt[s] + jnp.where(_KEY_NEG >= cands[s], n_rest, 0) for s in S]

    c0 = count_ge([jnp.zeros((1, QB), I32) for _ in S])
    base = tuple(jnp.where(c0[s] >= topk, 0, INT_MIN).astype(I32) for s in S)

    def bit_body(i, base):
        cands = [base[s] | jnp.left_shift(jnp.int32(1), 30 - i) for s in S]
        cnt = count_ge(cands)
        return tuple(jnp.where(cnt[s] >= topk, cands[s], base[s]) for s in S)

    tau = lax.fori_loop(0, 31, bit_body, base)

    def valid_of(kb):
        s_glob = kb * QB + s_loc
        return (s_glob >= front) & (s_glob <= t_glob)

    c_gt = count(lambda kblk, kb, s: kblk > tau[s])
    c_eq = count(lambda kblk, kb, s: (kblk == tau[s]) & valid_of(kb))
    need = [topk - (c_gt[s] + jnp.where(_KEY_NEG > tau[s], n_rest, 0)) for s in S]

    for s in S:
        @pl.when(jnp.max((c_eq[s] > need[s]).astype(I32)) > 0)
        def _(s=s):
            lstrict = (s_loc > lax.broadcasted_iota(I32, (QB, QB), 1)).astype(MXU_DT)
            needf = need[s].astype(F32)

            def tie_body(kb, seen):
                kblk = key_ref[s, blk(kb), :]
                tie = (kblk == tau[s]) & valid_of(kb)
                tief = tie.astype(F32)
                rank = seen + jnp.dot(lstrict, tief.astype(MXU_DT), preferred_element_type=F32)
                key_ref[s, blk(kb), :] = jnp.where(tie & (rank >= needf), INT_MIN, kblk)
                return seen + jnp.sum(tief, axis=0, keepdims=True)

            lax.fori_loop(0, nkb, tie_body, jnp.zeros((1, QB), F32))

    def mask_body(kb, c):
        for s in S:
            sel = (key_ref[s, blk(kb), :] >= tau[s]) & valid_of(kb)
            madd_ref[s, blk(kb), :] = jnp.where(sel, 0.0, NEG)
        return c

    lax.fori_loop(0, nkb, mask_body, 0)

    NP = DSA_H // 2
    grp = (QB // SUBLANES, SUBLANES, 2 * QB)

    def attend(s, c):
        acc_ref[...] = jnp.zeros_like(acc_ref)

        def score_inputs(kb):
            ma = madd_ref[s, blk(kb), :]
            return kv_ref[s, blk(kb), :], jnp.concatenate([ma, ma], axis=1), jnp.minimum(j - kb, 2)

        def stage1(p, slot, inputs):
            kvb, ma2, rel = inputs
            q2 = q_ref[s, 2 * p:2 * p + 2].reshape(2 * QB, DSA_KV_LAT)
            lg = lax.dot_general(kvb, q2, NT_DIMS, preferred_element_type=F32) + bias_ref[p, rel] + ma2
            lg_ref[slot, p] = lg
            return jnp.max(jnp.max(lg.reshape(grp), axis=0), axis=0, keepdims=True)

        def stage2(p, slot, kvt, m_old, bm):
            m_new = jnp.maximum(m_old, bm)
            pm = jnp.exp2(lg_ref[slot, p] - m_new)
            acc_ref[p] = (jnp.exp2(m_old - m_new) * acc_ref[p]
                          + jnp.dot(kvt, pm.astype(MXU_DT), preferred_element_type=F32))
            return m_new

        inp0 = score_inputs(0)
        bm0 = tuple(stage1(p, 0, inp0) for p in range(NP))

        def attn_body(i, carry):
            ms, bm_a = carry
            a = 2 * i
            inp_b = score_inputs(a + 1)
            inp_c = score_inputs(jnp.minimum(a + 2, nkb - 1))
            kvt_a = kvt_ref[s, a]
            kvt_b = kvt_ref[s, a + 1]
            ms_a, bm_b = [], []
            for p in range(NP):
                bm_b.append(stage1(p, 1, inp_b))
                ms_a.append(stage2(p, 0, kvt_a, ms[p], bm_a[p]))
            ms_b, bm_c = [], []
            for p in range(NP):
                bm_c.append(stage1(p, 0, inp_c))
                ms_b.append(stage2(p, 1, kvt_b, ms_a[p], bm_b[p]))
            return tuple(ms_b), tuple(bm_c)

        ms, bm_last = lax.fori_loop(0, nkb // 2, attn_body,
                                    (tuple(jnp.full((1, 2 * QB), -jnp.inf, F32) for _ in range(NP)), bm0))

        @pl.when(nkb % 2 == 1)
        def _():
            kvt = kvt_ref[s, nkb - 1]
            for p in range(NP):
                stage2(p, 0, kvt, ms[p], bm_last[p])

        for p in range(NP):
            acc = acc_ref[p]
            o_t = acc[:DSA_KV_LAT] / acc[DSA_KV_LAT:DSA_KV_LAT + 1]
            outs = []
            for hh in range(2):
                o_h = jnp.transpose(o_t[:, hh * QB:(hh + 1) * QB])
                outs.append(_mm(o_h, wuv_ref[2 * p + hh]))
            o_ref[s, :, 2 * p * DSA_DH:(2 * p + 2) * DSA_DH] = jnp.concatenate(outs, axis=1).astype(o_ref.dtype)
        return c

    lax.fori_loop(0, ns, attend, 0)


def _dsa(ki, kv, kvt, qi, q, wi_t, bias_tiles, wuv, *, front, topk):
    B, Tp, _ = kv.shape
    nblk = Tp // LANES
    ns = DSA_STREAMS if B % DSA_STREAMS == 0 else 1
    return pl.pallas_call(
        functools.partial(_dsa_kernel, front=front, topk=topk, Tp=Tp, ns=ns),
        grid=(B // ns, nblk),
        in_specs=[pl.BlockSpec((ns, Tp, IDX_D), lambda b, j: (b, 0, 0)),
                  pl.BlockSpec((ns, Tp, DSA_KV_LAT), lambda b, j: (b, 0, 0)),
                  pl.BlockSpec((ns, nblk, DSA_KV_LAT + KVT_ONES, LANES), lambda b, j: (b, 0, 0, 0)),
                  pl.BlockSpec((ns, IDX_H, LANES, IDX_D), lambda b, j: (b, 0, j, 0)),
                  pl.BlockSpec((ns, DSA_H, LANES, DSA_KV_LAT), lambda b, j: (b, 0, j, 0)),
                  pl.BlockSpec((ns, IDX_H, LANES), lambda b, j: (b, 0, j)),
                  pl.BlockSpec((DSA_H // 2, 3, LANES, 2 * LANES), lambda b, j: (0, 0, 0, 0)),
                  pl.BlockSpec((DSA_H, DSA_KV_LAT, DSA_DH), lambda b, j: (0, 0, 0))],
        out_specs=pl.BlockSpec((ns, LANES, DSA_H * DSA_DH), lambda b, j: (b, j, 0)),
        out_shape=jax.ShapeDtypeStruct((B, Tp, DSA_H * DSA_DH), ACT_DT),
        scratch_shapes=[pltpu.VMEM((ns, Tp, LANES), I32), pltpu.VMEM((ns, Tp, LANES), F32),
                        pltpu.VMEM((2, DSA_H // 2, LANES, 2 * LANES), F32),
                        pltpu.VMEM((DSA_H // 2, DSA_KV_LAT + KVT_ONES, 2 * LANES), F32)],
        compiler_params=_params(("parallel", "arbitrary")),
        name="dsa_attn",
    )(ki, kv, kvt, qi, q, wi_t, bias_tiles, wuv)


def _t5_bucket(rel):
    n = jnp.maximum(rel, 0)
    max_exact = T5_BUCKETS // 2
    nf = jnp.maximum(n, 1).astype(F32)
    large = max_exact + (jnp.log(nf / max_exact) / math.log(T5_MAX_DIST / max_exact)
                         * (T5_BUCKETS - max_exact)).astype(I32)
    large = jnp.minimum(large, T5_BUCKETS - 1)
    return jnp.where(n < max_exact, n, large)


def _t5_tiles(t5_bias):
    s = jnp.arange(LANES, dtype=I32)[:, None]
    t = jnp.arange(LANES, dtype=I32)[None, :]
    tiles = []
    for shift in (0, LANES, 2 * LANES):
        tiles.append(t5_bias[_t5_bucket(shift + t - s)])
    tiles = jnp.stack(tiles, axis=0)
    tiles = tiles.transpose(3, 0, 1, 2).reshape(DSA_H // 2, 2, 3, LANES, LANES)
    return (tiles.transpose(0, 2, 3, 1, 4).reshape(DSA_H // 2, 3, LANES, 2 * LANES) * LOG2E).astype(F32)


def _pack_in_proj(w_in, b_in):
    splits = np.cumsum([GLA_H * GLA_DK, GLA_H * GLA_DK, GLA_H * GLA_DV, GLA_H * GLA_DV, GLA_LR,
                        DSA_Q_LORA, DSA_KV_LAT, IDX_D, IDX_H,
                        ML_H * ML_DK, ML_H * ML_DK, ML_H * ML_DV, ML_H * ML_DV, ML_H, ML_H,
                        D_MODEL, D_MODEL])[:].tolist()

    def pack(a):
        (gq, gk, gv, gr, ga, cq, ckv, ik, iw, mq, mk, mv, mo, mi, mf, a_, b_, c_) = jnp.split(a, splits, axis=-1)
        pad = lambda n: jnp.zeros(a.shape[:-1] + (n,), a.dtype)
        lo = jnp.concatenate([gq, gk, gv, gr, mq, mk, mv, mo], axis=-1)
        hi = jnp.concatenate([a_, b_, c_, cq, ckv,
                              ga, pad(LANES - GLA_LR),
                              ik, iw, pad(LANES - IDX_D - IDX_H),
                              mi, mf, pad(LANES - 2 * ML_H)], axis=-1)
        return lo, hi

    return pack(w_in), pack(b_in)


def kernel(x, meta_tokens, t5_bias, g_pre_mix, w_in, b_in, w_gla_a2, b_gla_a, g_gla_head, w_br_gla, g_dsa_q, w_dsa_uq, w_idx_q, g_dsa_kv, w_dsa_uv, idx_ln_g, idx_ln_b, w_br_dsa, ml_conv, ml_f_bias, g_ml_head, w_br_ml, w_out, g_post_mix, g_pre_mlp, w_up, w_down, g_post_mlp):
    B, S, D = x.shape
    L = w_in.shape[0]
    T = S + N_META
    front = (-T) % LANES
    Tp = T + front
    R = B * Tp
    nblk = Tp // LANES
    topk = min(TOPK_MAX, (T - N_META) // 4)
    tm_mid = 512 if R % 512 == 0 else LANES
    tm_small = 512 if R % 512 == 0 else LANES

    (w_lo, w_hi), (b_lo, b_hi) = _pack_in_proj(w_in, b_in)
    wa2 = jnp.pad(w_gla_a2, ((0, 0), (0, LANES - GLA_LR), (0, 0)))
    wa2 = wa2.reshape(L, LANES, GLA_H, GLA_DK).transpose(0, 2, 1, 3).astype(MXU_DT)
    fb_row = jnp.pad(ml_f_bias, ((0, 0), (ML_H, LANES - 2 * ML_H)))[:, None, :]
    layers = dict(
        g_pre_mix=g_pre_mix[:, None, :], w_lo=w_lo.astype(MXU_DT), b_lo=b_lo[:, None, :],
        w_hi=w_hi.astype(MXU_DT), b_hi=b_hi[:, None, :],
        wa2=wa2, ba=b_gla_a.reshape(L, GLA_H, 1, GLA_DK), g_gla_head=g_gla_head[:, None, :],
        w_br_gla=w_br_gla.astype(MXU_DT),
        g_dsa_q=g_dsa_q[:, None, :], w_dsa_uq=w_dsa_uq.astype(MXU_DT),
        w_idx_q=w_idx_q.reshape(L, DSA_Q_LORA, IDX_H, IDX_D).transpose(0, 2, 1, 3).astype(MXU_DT),
        g_dsa_kv=g_dsa_kv[:, None, :], w_dsa_uv=w_dsa_uv.astype(MXU_DT),
        idx_ln_g=idx_ln_g[:, None, :], idx_ln_b=idx_ln_b[:, None, :], w_br_dsa=w_br_dsa.astype(MXU_DT),
        ml_conv=ml_conv, fb_row=fb_row, ml_f_bias=ml_f_bias, g_ml_head=g_ml_head[:, None, :],
        w_br_ml=w_br_ml.astype(MXU_DT), w_out=w_out.astype(MXU_DT), g_post_mix=g_post_mix[:, None, :],
        g_pre_mlp=g_pre_mlp[:, None, :], w_up=w_up.astype(MXU_DT), w_down=w_down.astype(MXU_DT),
        g_post_mlp=g_post_mlp[:, None, :],
    )
    bias_tiles = _t5_tiles(t5_bias)
    zero_ff = jnp.zeros((1, D_FF), F32)

    meta = jnp.broadcast_to(meta_tokens.astype(x.dtype)[None], (B, N_META, D))
    h0 = jnp.concatenate([jnp.zeros((B, front, D), x.dtype), meta, x], axis=1).reshape(R, D)

    def layer(h, p):
        zl = _norm_matmul(h, p["g_pre_mix"], p["w_lo"], p["b_lo"], tm=tm_mid, tn=1024, relu2=False,
                          out_dtype=ACT_DT, name="in_proj_lo").reshape(B, Tp, NZL)
        zh = _norm_matmul(h, p["g_pre_mix"], p["w_hi"], p["b_hi"], tm=tm_mid, tn=768, relu2=False,
                          out_dtype=F32, name="in_proj_hi")
        zh3 = zh.reshape(B, Tp, NZH)
        o_gla = _gla(zl, zh3, p["wa2"], p["ba"], p["g_gla_head"], front=front)
        gates_row = zh3[:, :, Z_SM2:Z_SM2 + 2 * ML_H].transpose(0, 2, 1).reshape(B, 2 * ML_H, nblk, LANES)
        o_ml = _mlstm(zl, zh3, gates_row, p["ml_conv"], p["fb_row"], p["ml_f_bias"], p["g_ml_head"], front=front)
        q, qi, kv, ki = _dsa_prep(zh3, p["g_dsa_q"], p["g_dsa_kv"], p["idx_ln_g"], p["idx_ln_b"],
                                  p["w_dsa_uq"], p["w_idx_q"])
        kvt = kv.reshape(B, nblk, LANES, DSA_KV_LAT).transpose(0, 1, 3, 2)
        kvt = jnp.concatenate([kvt, jnp.ones((B, nblk, KVT_ONES, LANES), kvt.dtype)], axis=2)
        wi_t = zh3[:, :, Z_SM1 + IDX_D:Z_SM1 + IDX_D + IDX_H].transpose(0, 2, 1)
        o_dsa = _dsa(ki, kv, kvt, qi, q, wi_t, bias_tiles, p["w_dsa_uv"], front=front, topk=topk)
        h = _mix(o_gla.reshape(R, D), o_dsa.reshape(R, D), o_ml.reshape(R, D), zh,
                 p["w_br_gla"], p["w_br_dsa"], p["w_br_ml"], p["w_out"], p["g_post_mix"], h, tm=tm_small)
        u = _norm_matmul(h, p["g_pre_mlp"], p["w_up"], zero_ff, tm=tm_mid, tn=1024,
                         relu2=True, out_dtype=ACT_DT, name="mlp_up")
        h = _matmul_norm_res(u, p["w_down"], p["g_post_mlp"], h, tm=tm_mid)
        return h, None

    h, _ = lax.scan(layer, h0, layers)
    return h.reshape(B, Tp, D)[:, front + N_META:]
```

```python
import functools
import math

import numpy as np
import jax
import jax.numpy as jnp
from jax import lax
from jax.experimental import pallas as pl
from jax.experimental.pallas import tpu as pltpu

D_MODEL = 1024
N_META = 16
GLA_H, GLA_DK, GLA_DV, GLA_LR, GLA_TAU = 4, 128, 256, 16, 16.0
DSA_H, DSA_DH, DSA_Q_LORA, DSA_KV_LAT = 16, 64, 256, 128
IDX_H, IDX_D, TOPK_MAX = 8, 64, 256
ML_H, ML_DK, ML_DV, ML_CONV = 4, 128, 256, 4
T5_BUCKETS, T5_MAX_DIST = 32, 128
D_FF = 4 * D_MODEL
EPS = 1e-6
NEG = -1e30
LOG2E = math.log2(math.e)

LANES = 128
SUBLANES = 8
VMEM_LIMIT = 56 * 1024 * 1024

MXU_DT = jnp.bfloat16
ACT_DT = jnp.bfloat16

F32 = jnp.float32
I32 = jnp.int32
INT_MIN = -2147483648
HI = lax.Precision.HIGHEST
NT_DIMS = (((1,), (1,)), ((), ()))

Z_GLA_Q, Z_GLA_K, Z_GLA_V, Z_GLA_R = 0, 512, 1024, 2048
Z_ML_Q, Z_ML_K, Z_ML_V, Z_ML_O = 3072, 3584, 4096, 5120
NZL = 6144
Z_GATE_A, Z_GATE_B, Z_GATE_C = 0, 1024, 2048
Z_CQ, Z_CKV = 3072, 3328
Z_SM0, Z_SM1, Z_SM2 = 3456, 3584, 3712
NZH = 3840
KVT_ONES = 16
PREP_ROWS_MAX = 1088
DSA_STREAMS = 4
ATTN_STREAMS = 2
KB_GROUP = 4
MIX_CHUNKS = 6


def _params(sem):
    return pltpu.CompilerParams(dimension_semantics=sem, vmem_limit_bytes=VMEM_LIMIT)


def _mm(a, b):
    return jnp.dot(a.astype(MXU_DT), b.astype(MXU_DT), preferred_element_type=F32)


def _mm_nt(a, b):
    return lax.dot_general(a.astype(MXU_DT), b.astype(MXU_DT), NT_DIMS, preferred_element_type=F32)


def _rms(x, g):
    return x * lax.rsqrt(jnp.mean(x * x, axis=-1, keepdims=True) + EPS) * g


def _log_sigmoid(x):
    return jnp.minimum(x, 0.0) - jnp.log1p(jnp.exp(-jnp.abs(x)))


def _sigmoid(x):
    return 1.0 / (1.0 + jnp.exp(-x))


def _norm_matmul_kernel(x_ref, g_ref, w_ref, b_ref, o_ref, *, relu2, tn):
    xn = _rms(x_ref[...], g_ref[...]).astype(MXU_DT)
    for n0 in range(0, o_ref.shape[1], tn):
        acc = jnp.dot(xn, w_ref[:, n0:n0 + tn], preferred_element_type=F32) + b_ref[:, n0:n0 + tn]
        if relu2:
            acc = jnp.square(jnp.maximum(acc, 0.0))
        o_ref[:, n0:n0 + tn] = acc.astype(o_ref.dtype)


def _norm_matmul(x, g, w, b, *, tm, tn, relu2, out_dtype, name):
    R, K = x.shape
    N = w.shape[1]
    return pl.pallas_call(
        functools.partial(_norm_matmul_kernel, relu2=relu2, tn=tn),
        grid=(R // tm,),
        in_specs=[pl.BlockSpec((tm, K), lambda i: (i, 0)),
                  pl.BlockSpec((1, K), lambda i: (0, 0)),
                  pl.BlockSpec((K, N), lambda i: (0, 0)),
                  pl.BlockSpec((1, N), lambda i: (0, 0))],
        out_specs=pl.BlockSpec((tm, N), lambda i: (i, 0)),
        out_shape=jax.ShapeDtypeStruct((R, N), out_dtype),
        compiler_params=_params(("parallel",)),
        name=name,
    )(x, g, w, b)


def _matmul_norm_res_kernel(u_ref, w_ref, g_ref, h_ref, o_ref):
    y = jnp.dot(u_ref[...], w_ref[...], preferred_element_type=F32)
    o_ref[...] = h_ref[...] + _rms(y, g_ref[...])


def _matmul_norm_res(u, w, g, h, *, tm):
    R, K = u.shape
    N = w.shape[1]
    return pl.pallas_call(
        _matmul_norm_res_kernel,
        grid=(R // tm,),
        in_specs=[pl.BlockSpec((tm, K), lambda i: (i, 0)),
                  pl.BlockSpec((K, N), lambda i: (0, 0)),
                  pl.BlockSpec((1, N), lambda i: (0, 0)),
                  pl.BlockSpec((tm, N), lambda i: (i, 0))],
        out_specs=pl.BlockSpec((tm, N), lambda i: (i, 0)),
        out_shape=jax.ShapeDtypeStruct((R, N), F32),
        compiler_params=_params(("parallel",)),
        name="mlp_down",
    )(u, w, g, h)


def _mix_kernel(oa_ref, ob_ref, oc_ref, ga_ref, gb_ref, gc_ref, wa_ref, wb_ref, wc_ref,
                wo_ref, g_ref, h_ref, o_ref):
    ya = jnp.dot(oa_ref[...], wa_ref[...], preferred_element_type=F32)
    yb = jnp.dot(ob_ref[...], wb_ref[...], preferred_element_type=F32)
    yc = jnp.dot(oc_ref[...], wc_ref[...], preferred_element_type=F32)
    mix = _sigmoid(ga_ref[...]) * ya + _sigmoid(gb_ref[...]) * yb + _sigmoid(gc_ref[...]) * yc
    y = _mm(mix, wo_ref[...])
    o_ref[...] = h_ref[...] + _rms(y, g_ref[...])


def _mix(oa, ob, oc, z, wa, wb, wc, wo, g, h, *, tm):
    R, D = h.shape
    row = lambda i: (i, 0)
    const = lambda i: (0, 0)
    gate = lambda off: pl.BlockSpec((tm, D), lambda i: (i, off // D))
    return pl.pallas_call(
        _mix_kernel,
        grid=(R // tm,),
        in_specs=[pl.BlockSpec((tm, D), row), pl.BlockSpec((tm, D), row), pl.BlockSpec((tm, D), row),
                  gate(Z_GATE_A), gate(Z_GATE_B), gate(Z_GATE_C),
                  pl.BlockSpec((D, D), const), pl.BlockSpec((D, D), const), pl.BlockSpec((D, D), const),
                  pl.BlockSpec((D, D), const), pl.BlockSpec((1, D), const), pl.BlockSpec((tm, D), row)],
        out_specs=pl.BlockSpec((tm, D), row),
        out_shape=jax.ShapeDtypeStruct((R, D), F32),
        compiler_params=_params(("parallel",)),
        name="mix_out",
    )(oa, ob, oc, z, z, z, wa, wb, wc, wo, g, h)


def _gla_kernel(q_ref, k_ref, v_ref, r_ref, a_ref, wa_ref, ba_ref, gh_ref, o_ref, s_ref, *, front, nchunks):
    C = LANES
    tb = pl.program_id(1)

    @pl.when(tb == 0)
    def _():
        s_ref[...] = jnp.zeros_like(s_ref)

    r0 = lax.broadcasted_iota(I32, (C, C), 0)
    r1 = lax.broadcasted_iota(I32, (C, C), 1)
    causal = r0 >= r1
    tril = causal.astype(F32)
    rowi = lax.broadcasted_iota(I32, (C, 1), 0)

    def body(c, carry):
        sl = pl.ds(pl.multiple_of(c * C, C), C)
        live = ((tb * MIX_CHUNKS + c) * C + rowi) >= front
        a = a_ref[0, sl, :].astype(MXU_DT)
        H = range(GLA_H)
        kq = [slice(h * GLA_DK, (h + 1) * GLA_DK) for h in H]
        kv = [slice(h * GLA_DV, (h + 1) * GLA_DV) for h in H]
        g = [_log_sigmoid(jnp.dot(a, wa_ref[h], preferred_element_type=F32) + ba_ref[h]) / GLA_TAU for h in H]
        G = [jnp.dot(tril, g[h], precision=HI, preferred_element_type=F32) for h in H]
        q = [q_ref[0, sl, kq[h]].astype(F32) * (GLA_DK ** -0.5) for h in H]
        k = [jnp.where(live, k_ref[0, sl, kq[h]].astype(F32), 0.0) for h in H]
        v = [jnp.where(live, v_ref[0, sl, kv[h]].astype(F32), 0.0).astype(MXU_DT) for h in H]
        S = [s_ref[h] for h in H]
        g_last = [G[h][C - 1:C, :] for h in H]
        g_mid = [G[h][C // 2 - 1:C // 2, :] for h in H]
        o_inter = [_mm(q[h] * jnp.exp(G[h]), S[h]) for h in H]
        A = [jnp.where(causal, _mm_nt(q[h] * jnp.exp(G[h] - g_mid[h]), k[h] * jnp.exp(g_mid[h] - G[h])), 0.0) for h in H]
        o_intra = [_mm(A[h], v[h]) for h in H]
        kd_t = [jnp.transpose(k[h] * jnp.exp(g_last[h] - G[h])) for h in H]
        dec_col = [jnp.transpose(jnp.broadcast_to(jnp.exp(g_last[h]), (C, GLA_DK))) for h in H]
        for h in H:
            s_ref[h] = jnp.concatenate([dec_col[h], dec_col[h]], axis=1) * S[h] + _mm(kd_t[h], v[h])
        for h in H:
            r = r_ref[0, sl, kv[h]].astype(F32)
            o = _rms(o_inter[h] + o_intra[h], gh_ref[...]) * (r * _sigmoid(r))
            o_ref[0, sl, kv[h]] = o.astype(o_ref.dtype)
        return carry

    lax.fori_loop(0, jnp.minimum(MIX_CHUNKS, nchunks - tb * MIX_CHUNKS), body, 0)


def _gla(zl, zh, wa, ba, gh, *, front):
    B, Tp, _ = zl.shape
    nchunks = Tp // LANES
    TB = min(MIX_CHUNKS * LANES, Tp)
    WK, WV = GLA_H * GLA_DK, GLA_H * GLA_DV
    return pl.pallas_call(
        functools.partial(_gla_kernel, front=front, nchunks=nchunks),
        grid=(B, pl.cdiv(Tp, TB)),
        in_specs=[pl.BlockSpec((1, TB, WK), lambda b, t: (b, t, Z_GLA_Q // WK)),
                  pl.BlockSpec((1, TB, WK), lambda b, t: (b, t, Z_GLA_K // WK)),
                  pl.BlockSpec((1, TB, WV), lambda b, t: (b, t, Z_GLA_V // WV)),
                  pl.BlockSpec((1, TB, WV), lambda b, t: (b, t, Z_GLA_R // WV)),
                  pl.BlockSpec((1, TB, LANES), lambda b, t: (b, t, Z_SM0 // LANES)),
                  pl.BlockSpec((GLA_H, LANES, GLA_DK), lambda b, t: (0, 0, 0)),
                  pl.BlockSpec((GLA_H, 1, GLA_DK), lambda b, t: (0, 0, 0)),
                  pl.BlockSpec((1, GLA_DV), lambda b, t: (0, 0))],
        out_specs=pl.BlockSpec((1, TB, WV), lambda b, t: (b, t, 0)),
        out_shape=jax.ShapeDtypeStruct((B, Tp, WV), ACT_DT),
        scratch_shapes=[pltpu.VMEM((GLA_H, GLA_DK, GLA_DV), F32)],
        compiler_params=_params(("parallel", "arbitrary")),
        name="gla",
    )(zl, zl, zl, zl, zh, wa, ba, gh)


def _mlstm_kernel(q_ref, k_ref, v_ref, og_ref, gcol_ref, grow_ref, cw_ref, fbrow_ref, fb_ref, gh_ref, o_ref,
                  s_ref, m_ref, pq_ref, pk_ref, *, front, nchunks):
    C = LANES
    WK = ML_H * ML_DK
    tb = pl.program_id(1)

    @pl.when(tb == 0)
    def _():
        s_ref[...] = jnp.zeros_like(s_ref)
        m_ref[...] = jnp.zeros_like(m_ref)
        pq_ref[...] = jnp.zeros_like(pq_ref)
        pk_ref[...] = jnp.zeros_like(pk_ref)

    r0 = lax.broadcasted_iota(I32, (C, C), 0)
    r1 = lax.broadcasted_iota(I32, (C, C), 1)
    causal = r0 >= r1
    tril = causal.astype(F32)
    triu = (r0 <= r1).astype(F32)
    rowi = lax.broadcasted_iota(I32, (C, 1), 0)

    def conv_silu(x, tail, w):
        acc = x * w[ML_CONV - 1:ML_CONV, :]
        for d in range(1, ML_CONV):
            sh = pltpu.roll(x, d, 0)
            head = jnp.where(rowi[:SUBLANES] >= d, sh[:SUBLANES], pltpu.roll(tail, d, 0))
            sh = jnp.concatenate([head, sh[SUBLANES:]], axis=0)
            acc = acc + sh * w[ML_CONV - 1 - d:ML_CONV - d, :]
        return acc * _sigmoid(acc)

    def body(c, carry):
        sl = pl.ds(pl.multiple_of(c * C, C), C)
        cg = tb * MIX_CHUNKS + c
        live = (cg * C + rowi) >= front
        xq = jnp.where(live, q_ref[0, sl, :].astype(F32), 0.0)
        xk = jnp.where(live, k_ref[0, sl, :].astype(F32), 0.0)
        q_all = conv_silu(xq, pq_ref[...], cw_ref[:, :WK]) * (ML_DK ** -0.5)
        k_all = conv_silu(xk, pk_ref[...], cw_ref[:, WK:])
        pq_ref[...] = xq[C - SUBLANES:, :]
        pk_ref[...] = xk[C - SUBLANES:, :]
        lf_col = _log_sigmoid(gcol_ref[0, sl, :] + fbrow_ref[...])
        H = range(ML_H)
        vs = [slice(h * ML_DV, (h + 1) * ML_DV) for h in H]
        q = [q_all[:, h * ML_DK:(h + 1) * ML_DK].astype(MXU_DT) for h in H]
        k = [k_all[:, h * ML_DK:(h + 1) * ML_DK] for h in H]
        ones = jnp.ones((C, LANES), MXU_DT)
        v_aug = [jnp.concatenate([jnp.where(live, v_ref[0, sl, vs[h]].astype(F32), 0.0).astype(MXU_DT), ones], axis=1)
                 for h in H]
        m_prev = [m_ref[h, 0:1, :] for h in H]
        S = [s_ref[h] for h in H]
        sel = [jnp.dot(lf_col, (r0 == ML_H + h).astype(F32), precision=HI, preferred_element_type=F32) for h in H]
        b_bc = [jnp.dot(tril, sel[h], precision=HI, preferred_element_type=F32) for h in H]
        li_row = [grow_ref[0, h, pl.ds(cg, 1), :] for h in H]
        lf_row = [_log_sigmoid(grow_ref[0, ML_H + h, pl.ds(cg, 1), :] + fb_ref[h]) for h in H]
        b_row = [jnp.dot(lf_row[h], triu, precision=HI, preferred_element_type=F32) for h in H]
        b_last = [b_row[h][:, C - 1:C] for h in H]
        qk = [_mm_nt(q[h], k[h]) for h in H]
        qs = [_mm(q[h], S[h]) for h in H]
        Dm = [jnp.where(causal, b_bc[h] - (b_row[h] - li_row[h]), -jnp.inf) for h in H]
        m_inter = [b_bc[h] + m_prev[h] for h in H]
        m = [jnp.maximum(m_inter[h], jnp.max(Dm[h], axis=1, keepdims=True)) for h in H]
        pv = [_mm(jnp.exp(Dm[h] - m[h]) * qk[h], v_aug[h]) for h in H]
        dec = [b_last[h] - b_row[h] + li_row[h] for h in H]
        m_new = [jnp.maximum(b_last[h] + m_prev[h], jnp.max(dec[h], axis=1, keepdims=True)) for h in H]
        kt_w = [jnp.transpose(k[h]) * jnp.exp(dec[h] - m_new[h]) for h in H]
        for h in H:
            w_old = jnp.exp(b_last[h] + m_prev[h] - m_new[h])
            s_ref[h] = jnp.concatenate([w_old, w_old, w_old], axis=1) * S[h] + _mm(kt_w[h], v_aug[h])
            m_ref[h] = jnp.broadcast_to(m_new[h], (SUBLANES, C))
        for h in H:
            w_inter = jnp.exp(m_inter[h] - m[h])
            num = jnp.concatenate([w_inter, w_inter, w_inter], axis=1) * qs[h] + pv[h]
            den = num[:, ML_DV:]
            dd = jnp.maximum(jnp.abs(den), jnp.exp(-m[h]))
            hval = num[:, :ML_DV] / jnp.concatenate([dd, dd], axis=1)
            og = og_ref[0, sl, vs[h]].astype(F32)
            o_ref[0, sl, vs[h]] = (_rms(hval, gh_ref[...]) * _sigmoid(og)).astype(o_ref.dtype)
        return carry

    lax.fori_loop(0, jnp.minimum(MIX_CHUNKS, nchunks - tb * MIX_CHUNKS), body, 0)


def _mlstm(zl, zh, gates_row, conv_w, fb_row, fb, gh, *, front):
    B, Tp, _ = zl.shape
    nchunks = Tp // LANES
    TB = min(MIX_CHUNKS * LANES, Tp)
    WK, WV = ML_H * ML_DK, ML_H * ML_DV
    return pl.pallas_call(
        functools.partial(_mlstm_kernel, front=front, nchunks=nchunks),
        grid=(B, pl.cdiv(Tp, TB)),
        in_specs=[pl.BlockSpec((1, TB, WK), lambda b, t: (b, t, Z_ML_Q // WK)),
                  pl.BlockSpec((1, TB, WK), lambda b, t: (b, t, Z_ML_K // WK)),
                  pl.BlockSpec((1, TB, WV), lambda b, t: (b, t, Z_ML_V // WV)),
                  pl.BlockSpec((1, TB, WV), lambda b, t: (b, t, Z_ML_O // WV)),
                  pl.BlockSpec((1, TB, LANES), lambda b, t: (b, t, Z_SM2 // LANES)),
                  pl.BlockSpec((1, 2 * ML_H, nchunks, LANES), lambda b, t: (b, 0, 0, 0)),
                  pl.BlockSpec((ML_CONV, 2 * WK), lambda b, t: (0, 0)),
                  pl.BlockSpec((1, LANES), lambda b, t: (0, 0)),
                  pl.BlockSpec(memory_space=pltpu.SMEM),
                  pl.BlockSpec((1, ML_DV), lambda b, t: (0, 0))],
        out_specs=pl.BlockSpec((1, TB, WV), lambda b, t: (b, t, 0)),
        out_shape=jax.ShapeDtypeStruct((B, Tp, WV), ACT_DT),
        scratch_shapes=[pltpu.VMEM((ML_H, ML_DK, ML_DV + LANES), F32), pltpu.VMEM((ML_H, SUBLANES, LANES), F32),
                        pltpu.VMEM((SUBLANES, WK), F32), pltpu.VMEM((SUBLANES, WK), F32)],
        compiler_params=_params(("parallel", "arbitrary")),
        name="mlstm",
    )(zl, zl, zl, zl, zh, gates_row, conv_w, fb_row, fb, gh)


def _dsa_prep_kernel(cq_ref, ckv_ref, sm_ref, gq_ref, gkv_ref, lng_ref, lnb_ref, wuq_ref, wiq_ref,
                     q_ref, qi_ref, kv_ref, ki_ref):
    cq = _rms(cq_ref[0], gq_ref[...]).astype(MXU_DT)
    for hp in range(DSA_H // 2):
        w2 = wuq_ref[:, 2 * hp * DSA_KV_LAT:(2 * hp + 2) * DSA_KV_LAT]
        qf = jnp.dot(cq, w2, preferred_element_type=F32) * (DSA_KV_LAT ** -0.5 * LOG2E)
        q_ref[0, 2 * hp] = qf[:, :DSA_KV_LAT].astype(q_ref.dtype)
        q_ref[0, 2 * hp + 1] = qf[:, DSA_KV_LAT:].astype(q_ref.dtype)
    for hh in range(IDX_H):
        qi = jnp.dot(cq, wiq_ref[hh], preferred_element_type=F32) * (IDX_D ** -0.5)
        qi_ref[0, hh] = qi.astype(qi_ref.dtype)
    kv_ref[0] = _rms(ckv_ref[0], gkv_ref[...]).astype(kv_ref.dtype)
    x = sm_ref[0][:, :IDX_D]
    mu = jnp.mean(x, axis=-1, keepdims=True)
    var = jnp.mean(jnp.square(x - mu), axis=-1, keepdims=True)
    ki_ref[0] = ((x - mu) * lax.rsqrt(var + EPS) * lng_ref[...] + lnb_ref[...]).astype(ki_ref.dtype)


def _dsa_prep(z, gq, gkv, lng, lnb, wuq, wiq):
    B, Tp, _ = z.shape
    tm = max(t for t in range(2 * SUBLANES, PREP_ROWS_MAX + 1, 2 * SUBLANES) if Tp % t == 0)
    c2 = lambda b, i: (0, 0)
    return pl.pallas_call(
        _dsa_prep_kernel,
        grid=(B, Tp // tm),
        in_specs=[pl.BlockSpec((1, tm, DSA_Q_LORA), lambda b, i: (b, i, Z_CQ // DSA_Q_LORA)),
                  pl.BlockSpec((1, tm, DSA_KV_LAT), lambda b, i: (b, i, Z_CKV // DSA_KV_LAT)),
                  pl.BlockSpec((1, tm, LANES), lambda b, i: (b, i, Z_SM1 // LANES)),
                  pl.BlockSpec((1, DSA_Q_LORA), c2), pl.BlockSpec((1, DSA_KV_LAT), c2),
                  pl.BlockSpec((1, IDX_D), c2), pl.BlockSpec((1, IDX_D), c2),
                  pl.BlockSpec((DSA_Q_LORA, DSA_H * DSA_KV_LAT), c2),
                  pl.BlockSpec((IDX_H, DSA_Q_LORA, IDX_D), lambda b, i: (0, 0, 0))],
        out_specs=[pl.BlockSpec((1, DSA_H, tm, DSA_KV_LAT), lambda b, i: (b, 0, i, 0)),
                   pl.BlockSpec((1, IDX_H, tm, IDX_D), lambda b, i: (b, 0, i, 0)),
                   pl.BlockSpec((1, tm, DSA_KV_LAT), lambda b, i: (b, i, 0)),
                   pl.BlockSpec((1, tm, IDX_D), lambda b, i: (b, i, 0))],
        out_shape=[jax.ShapeDtypeStruct((B, DSA_H, Tp, DSA_KV_LAT), ACT_DT),
                   jax.ShapeDtypeStruct((B, IDX_H, Tp, IDX_D), ACT_DT),
                   jax.ShapeDtypeStruct((B, Tp, DSA_KV_LAT), ACT_DT),
                   jax.ShapeDtypeStruct((B, Tp, IDX_D), ACT_DT)],
        compiler_params=_params(("parallel", "parallel")),
        name="dsa_prep",
    )(z, z, z, gq, gkv, lng, lnb, wuq, wiq)


def _sortable(x):
    bits = pltpu.bitcast(x, I32)
    key = jnp.where(bits < 0, bits ^ 0x7FFFFFFF, bits)
    return jnp.where(x == 0.0, 0, key)


_KEY_NEG = int(np.array(NEG, np.float32).view(np.int32)) ^ 0x7FFFFFFF


def _dsa_kernel(ki_ref, kv_ref, kvt_ref, qi_ref, q_ref, wi_ref, bias_ref, wuv_ref, o_ref,
                key_ref, madd_ref, lg_ref, acc_ref, *, front, topk, Tp, ns):
    QB = LANES
    S = range(ns)
    j = pl.program_id(1)
    nkb = j + 1
    ngrp = nkb // KB_GROUP
    s_loc = lax.broadcasted_iota(I32, (QB, QB), 0)
    t_glob = j * QB + lax.broadcasted_iota(I32, (QB, QB), 1)
    wi = [wi_ref[s] * (IDX_H ** -0.5) for s in S]
    n_rest = Tp - nkb * QB

    def blk(kb):
        return pl.ds(pl.multiple_of(kb * QB, QB), QB)

    def score_block(kb, s):
        ki = ki_ref[s, blk(kb), :]
        acc = jnp.zeros((QB, QB), F32)
        for hp in range(IDX_H // 2):
            qi2 = qi_ref[s, 2 * hp:2 * hp + 2].reshape(2 * QB, IDX_D)
            sc = lax.dot_general(ki, qi2, NT_DIMS, preferred_element_type=F32)
            acc = acc + jnp.maximum(sc[:, :QB], 0.0) * wi[s][2 * hp:2 * hp + 1, :]
            acc = acc + jnp.maximum(sc[:, QB:], 0.0) * wi[s][2 * hp + 1:2 * hp + 2, :]
        s_glob = kb * QB + s_loc
        key = _sortable(jnp.where(s_glob <= t_glob, acc, NEG))
        key_ref[s, blk(kb), :] = jnp.where(s_glob >= front, key, INT_MIN)

    def score_group(i, c):
        for u in range(KB_GROUP):
            for s in S:
                score_block(KB_GROUP * i + u, s)
        return c

    def score_single(kb, c):
        for s in S:
            score_block(kb, s)
        return c

    lax.fori_loop(0, ngrp, score_group, 0)
    lax.fori_loop(ngrp * KB_GROUP, nkb, score_single, 0)

    def fold32(hit):
        return jnp.sum(hit.reshape(QB // SUBLANES, SUBLANES, QB), axis=0)

    def count(pred_fn):
        def hit_of(kb, s):
            return pred_fn(key_ref[s, blk(kb), :], kb, s).astype(I32)

        def group(i, accs):
            out = []
            for s in S:
                hit = hit_of(KB_GROUP * i, s)
                for u in range(1, KB_GROUP):
                    hit = hit + hit_of(KB_GROUP * i + u, s)
                out.append(accs[s] + fold32(hit))
            return tuple(out)

        accs = lax.fori_loop(0, ngrp, group, tuple(jnp.zeros((SUBLANES, QB), I32) for _ in S))
        accs = lax.fori_loop(ngrp * KB_GROUP, nkb,
                             lambda kb, a: tuple(a[s] + fold32(hit_of(kb, s)) for s in S), accs)
        return [jnp.sum(a, axis=0, keepdims=True) for a in accs]

    def count_ge(cands):
        cnt = count(lambda kblk, kb, s: kblk >= cands[s])
        return [cnt[s] + jnp.where(_KEY_NEG >= cands[s], n_rest, 0) for s in S]

    c0 = count_ge([jnp.zeros((1, QB), I32) for _ in S])
    base = tuple(jnp.where(c0[s] >= topk, 0, INT_MIN).astype(I32) for s in S)

    def bit_body(i, base):
        cands = [base[s] | jnp.left_shift(jnp.int32(1), 30 - i) for s in S]
        cnt = count_ge(cands)
        return tuple(jnp.where(cnt[s] >= topk, cands[s], base[s]) for s in S)

    tau = lax.fori_loop(0, 31, bit_body, base)

    def valid_of(kb):
        s_glob = kb * QB + s_loc
        return (s_glob >= front) & (s_glob <= t_glob)

    c_gt = count(lambda kblk, kb, s: kblk > tau[s])
    c_eq = count(lambda kblk, kb, s: (kblk == tau[s]) & valid_of(kb))
    need = [topk - (c_gt[s] + jnp.where(_KEY_NEG > tau[s], n_rest, 0)) for s in S]

    for s in S:
        @pl.when(jnp.max((c_eq[s] > need[s]).astype(I32)) > 0)
        def _(s=s):
            lstrict = (s_loc > lax.broadcasted_iota(I32, (QB, QB), 1)).astype(MXU_DT)
            needf = need[s].astype(F32)

            def tie_body(kb, seen):
                kblk = key_ref[s, blk(kb), :]
                tie = (kblk == tau[s]) & valid_of(kb)
                tief = tie.astype(F32)
                rank = seen + jnp.dot(lstrict, tief.astype(MXU_DT), preferred_element_type=F32)
                key_ref[s, blk(kb), :] = jnp.where(tie & (rank >= needf), INT_MIN, kblk)
                return seen + jnp.sum(tief, axis=0, keepdims=True)

            lax.fori_loop(0, nkb, tie_body, jnp.zeros((1, QB), F32))

    def mask_body(kb, c):
        for s in S:
            sel = (key_ref[s, blk(kb), :] >= tau[s]) & valid_of(kb)
            madd_ref[s, blk(kb), :] = jnp.where(sel, 0.0, NEG)
        return c

    lax.fori_loop(0, nkb, mask_body, 0)

    NP = DSA_H // 2
    grp = (QB // SUBLANES, SUBLANES, 2 * QB)

    AS = ATTN_STREAMS if ns % ATTN_STREAMS == 0 else 1

    def attend(g, c):
        ss = [g * AS + u for u in range(AS)]
        U = range(AS)
        acc_ref[...] = jnp.zeros_like(acc_ref)

        def score_inputs(kb):
            out = []
            for u in U:
                ma = madd_ref[ss[u], blk(kb), :]
                out.append((kv_ref[ss[u], blk(kb), :], jnp.concatenate([ma, ma], axis=1)))
            return out, jnp.minimum(j - kb, 2)

        def stage1(p, slot, inputs):
            per, rel = inputs
            out = []
            for u in U:
                kvb, ma2 = per[u]
                q2 = q_ref[ss[u], 2 * p:2 * p + 2].reshape(2 * QB, DSA_KV_LAT)
                lg = lax.dot_general(kvb, q2, NT_DIMS, preferred_element_type=F32) + bias_ref[p, rel] + ma2
                lg_ref[u, slot, p] = lg
                out.append(jnp.max(jnp.max(lg.reshape(grp), axis=0), axis=0, keepdims=True))
            return out

        def stage2(p, slot, kvts, m_old, bm):
            out = []
            for u in U:
                m_new = jnp.maximum(m_old[u], bm[u])
                pm = jnp.exp2(lg_ref[u, slot, p] - m_new)
                acc_ref[u, p] = (jnp.exp2(m_old[u] - m_new) * acc_ref[u, p]
                                 + jnp.dot(kvts[u], pm.astype(MXU_DT), preferred_element_type=F32))
                out.append(m_new)
            return out

        def kvt_of(kb):
            return [kvt_ref[ss[u], kb] for u in U]

        inp0 = score_inputs(0)
        bm0 = tuple(tuple(stage1(p, 0, inp0)) for p in range(NP))

        def attn_body(i, carry):
            ms, bm_a = carry
            a = 2 * i
            inp_b = score_inputs(a + 1)
            inp_c = score_inputs(jnp.minimum(a + 2, nkb - 1))
            kvt_a = kvt_of(a)
            kvt_b = kvt_of(a + 1)
            ms_a, bm_b = [], []
            for p in range(NP):
                bm_b.append(tuple(stage1(p, 1, inp_b)))
                ms_a.append(tuple(stage2(p, 0, kvt_a, ms[p], bm_a[p])))
            ms_b, bm_c = [], []
            for p in range(NP):
                bm_c.append(tuple(stage1(p, 0, inp_c)))
                ms_b.append(tuple(stage2(p, 1, kvt_b, ms_a[p], bm_b[p])))
            return tuple(ms_b), tuple(bm_c)

        ninf = tuple(tuple(jnp.full((1, 2 * QB), -jnp.inf, F32) for _ in U) for _ in range(NP))
        ms, bm_last = lax.fori_loop(0, nkb // 2, attn_body, (ninf, bm0))

        @pl.when(nkb % 2 == 1)
        def _():
            kvts = kvt_of(nkb - 1)
            for p in range(NP):
                stage2(p, 0, kvts, ms[p], bm_last[p])

        for u in U:
            for p in range(NP):
                acc = acc_ref[u, p]
                o_t = acc[:DSA_KV_LAT] / acc[DSA_KV_LAT:DSA_KV_LAT + 1]
                outs = []
                for hh in range(2):
                    o_h = jnp.transpose(o_t[:, hh * QB:(hh + 1) * QB])
                    outs.append(_mm(o_h, wuv_ref[2 * p + hh]))
                o_ref[ss[u], :, 2 * p * DSA_DH:(2 * p + 2) * DSA_DH] = jnp.concatenate(outs, axis=1).astype(o_ref.dtype)
        return c

    lax.fori_loop(0, ns // AS, attend, 0)


def _dsa(ki, kv, kvt, qi, q, wi_t, bias_tiles, wuv, *, front, topk):
    B, Tp, _ = kv.shape
    nblk = Tp // LANES
    ns = DSA_STREAMS if B % DSA_STREAMS == 0 else 1
    return pl.pallas_call(
        functools.partial(_dsa_kernel, front=front, topk=topk, Tp=Tp, ns=ns),
        grid=(B // ns, nblk),
        in_specs=[pl.BlockSpec((ns, Tp, IDX_D), lambda b, j: (b, 0, 0)),
                  pl.BlockSpec((ns, Tp, DSA_KV_LAT), lambda b, j: (b, 0, 0)),
                  pl.BlockSpec((ns, nblk, DSA_KV_LAT + KVT_ONES, LANES), lambda b, j: (b, 0, 0, 0)),
                  pl.BlockSpec((ns, IDX_H, LANES, IDX_D), lambda b, j: (b, 0, j, 0)),
                  pl.BlockSpec((ns, DSA_H, LANES, DSA_KV_LAT), lambda b, j: (b, 0, j, 0)),
                  pl.BlockSpec((ns, IDX_H, LANES), lambda b, j: (b, 0, j)),
                  pl.BlockSpec((DSA_H // 2, 3, LANES, 2 * LANES), lambda b, j: (0, 0, 0, 0)),
                  pl.BlockSpec((DSA_H, DSA_KV_LAT, DSA_DH), lambda b, j: (0, 0, 0))],
        out_specs=pl.BlockSpec((ns, LANES, DSA_H * DSA_DH), lambda b, j: (b, j, 0)),
        out_shape=jax.ShapeDtypeStruct((B, Tp, DSA_H * DSA_DH), ACT_DT),
        scratch_shapes=[pltpu.VMEM((ns, Tp, LANES), I32), pltpu.VMEM((ns, Tp, LANES), F32),
                        pltpu.VMEM((ATTN_STREAMS, 2, DSA_H // 2, LANES, 2 * LANES), F32),
                        pltpu.VMEM((ATTN_STREAMS, DSA_H // 2, DSA_KV_LAT + KVT_ONES, 2 * LANES), F32)],
        compiler_params=_params(("parallel", "arbitrary")),
        name="dsa_attn",
    )(ki, kv, kvt, qi, q, wi_t, bias_tiles, wuv)


def _t5_bucket(rel):
    n = jnp.maximum(rel, 0)
    max_exact = T5_BUCKETS // 2
    nf = jnp.maximum(n, 1).astype(F32)
    large = max_exact + (jnp.log(nf / max_exact) / math.log(T5_MAX_DIST / max_exact)
                         * (T5_BUCKETS - max_exact)).astype(I32)
    large = jnp.minimum(large, T5_BUCKETS - 1)
    return jnp.where(n < max_exact, n, large)


def _t5_tiles(t5_bias):
    s = jnp.arange(LANES, dtype=I32)[:, None]
    t = jnp.arange(LANES, dtype=I32)[None, :]
    tiles = []
    for shift in (0, LANES, 2 * LANES):
        tiles.append(t5_bias[_t5_bucket(shift + t - s)])
    tiles = jnp.stack(tiles, axis=0)
    tiles = tiles.transpose(3, 0, 1, 2).reshape(DSA_H // 2, 2, 3, LANES, LANES)
    return (tiles.transpose(0, 2, 3, 1, 4).reshape(DSA_H // 2, 3, LANES, 2 * LANES) * LOG2E).astype(F32)


def _pack_in_proj(w_in, b_in):
    splits = np.cumsum([GLA_H * GLA_DK, GLA_H * GLA_DK, GLA_H * GLA_DV, GLA_H * GLA_DV, GLA_LR,
                        DSA_Q_LORA, DSA_KV_LAT, IDX_D, IDX_H,
                        ML_H * ML_DK, ML_H * ML_DK, ML_H * ML_DV, ML_H * ML_DV, ML_H, ML_H,
                        D_MODEL, D_MODEL])[:].tolist()

    def pack(a):
        (gq, gk, gv, gr, ga, cq, ckv, ik, iw, mq, mk, mv, mo, mi, mf, a_, b_, c_) = jnp.split(a, splits, axis=-1)
        pad = lambda n: jnp.zeros(a.shape[:-1] + (n,), a.dtype)
        lo = jnp.concatenate([gq, gk, gv, gr, mq, mk, mv, mo], axis=-1)
        hi = jnp.concatenate([a_, b_, c_, cq, ckv,
                              ga, pad(LANES - GLA_LR),
                              ik, iw, pad(LANES - IDX_D - IDX_H),
                              mi, mf, pad(LANES - 2 * ML_H)], axis=-1)
        return lo, hi

    return pack(w_in), pack(b_in)


def kernel(x, meta_tokens, t5_bias, g_pre_mix, w_in, b_in, w_gla_a2, b_gla_a, g_gla_head, w_br_gla, g_dsa_q, w_dsa_uq, w_idx_q, g_dsa_kv, w_dsa_uv, idx_ln_g, idx_ln_b, w_br_dsa, ml_conv, ml_f_bias, g_ml_head, w_br_ml, w_out, g_post_mix, g_pre_mlp, w_up, w_down, g_post_mlp):
    B, S, D = x.shape
    L = w_in.shape[0]
    T = S + N_META
    front = (-T) % LANES
    Tp = T + front
    R = B * Tp
    nblk = Tp // LANES
    topk = min(TOPK_MAX, (T - N_META) // 4)
    tm_mid = 512 if R % 512 == 0 else LANES
    tm_small = 512 if R % 512 == 0 else LANES

    (w_lo, w_hi), (b_lo, b_hi) = _pack_in_proj(w_in, b_in)
    wa2 = jnp.pad(w_gla_a2, ((0, 0), (0, LANES - GLA_LR), (0, 0)))
    wa2 = wa2.reshape(L, LANES, GLA_H, GLA_DK).transpose(0, 2, 1, 3).astype(MXU_DT)
    fb_row = jnp.pad(ml_f_bias, ((0, 0), (ML_H, LANES - 2 * ML_H)))[:, None, :]
    layers = dict(
        g_pre_mix=g_pre_mix[:, None, :], w_lo=w_lo.astype(MXU_DT), b_lo=b_lo[:, None, :],
        w_hi=w_hi.astype(MXU_DT), b_hi=b_hi[:, None, :],
        wa2=wa2, ba=b_gla_a.reshape(L, GLA_H, 1, GLA_DK), g_gla_head=g_gla_head[:, None, :],
        w_br_gla=w_br_gla.astype(MXU_DT),
        g_dsa_q=g_dsa_q[:, None, :], w_dsa_uq=w_dsa_uq.astype(MXU_DT),
        w_idx_q=w_idx_q.reshape(L, DSA_Q_LORA, IDX_H, IDX_D).transpose(0, 2, 1, 3).astype(MXU_DT),
        g_dsa_kv=g_dsa_kv[:, None, :], w_dsa_uv=w_dsa_uv.astype(MXU_DT),
        idx_ln_g=idx_ln_g[:, None, :], idx_ln_b=idx_ln_b[:, None, :], w_br_dsa=w_br_dsa.astype(MXU_DT),
        ml_conv=ml_conv, fb_row=fb_row, ml_f_bias=ml_f_bias, g_ml_head=g_ml_head[:, None, :],
        w_br_ml=w_br_ml.astype(MXU_DT), w_out=w_out.astype(MXU_DT), g_post_mix=g_post_mix[:, None, :],
        g_pre_mlp=g_pre_mlp[:, None, :], w_up=w_up.astype(MXU_DT), w_down=w_down.astype(MXU_DT),
        g_post_mlp=g_post_mlp[:, None, :],
    )
    bias_tiles = _t5_tiles(t5_bias)
    zero_ff = jnp.zeros((1, D_FF), F32)

    meta = jnp.broadcast_to(meta_tokens.astype(x.dtype)[None], (B, N_META, D))
    h0 = jnp.concatenate([jnp.zeros((B, front, D), x.dtype), meta, x], axis=1).reshape(R, D)

    def layer(h, p):
        zl = _norm_matmul(h, p["g_pre_mix"], p["w_lo"], p["b_lo"], tm=tm_mid, tn=1024, relu2=False,
                          out_dtype=ACT_DT, name="in_proj_lo").reshape(B, Tp, NZL)
        zh = _norm_matmul(h, p["g_pre_mix"], p["w_hi"], p["b_hi"], tm=tm_mid, tn=768, relu2=False,
                          out_dtype=F32, name="in_proj_hi")
        zh3 = zh.reshape(B, Tp, NZH)
        o_gla = _gla(zl, zh3, p["wa2"], p["ba"], p["g_gla_head"], front=front)
        gates_row = zh3[:, :, Z_SM2:Z_SM2 + 2 * ML_H].transpose(0, 2, 1).reshape(B, 2 * ML_H, nblk, LANES)
        o_ml = _mlstm(zl, zh3, gates_row, p["ml_conv"], p["fb_row"], p["ml_f_bias"], p["g_ml_head"], front=front)
        q, qi, kv, ki = _dsa_prep(zh3, p["g_dsa_q"], p["g_dsa_kv"], p["idx_ln_g"], p["idx_ln_b"],
                                  p["w_dsa_uq"], p["w_idx_q"])
        kvt = kv.reshape(B, nblk, LANES, DSA_KV_LAT).transpose(0, 1, 3, 2)
        kvt = jnp.concatenate([kvt, jnp.ones((B, nblk, KVT_ONES, LANES), kvt.dtype)], axis=2)
        wi_t = zh3[:, :, Z_SM1 + IDX_D:Z_SM1 + IDX_D + IDX_H].transpose(0, 2, 1)
        o_dsa = _dsa(ki, kv, kvt, qi, q, wi_t, bias_tiles, p["w_dsa_uv"], front=front, topk=topk)
        h = _mix(o_gla.reshape(R, D), o_dsa.reshape(R, D), o_ml.reshape(R, D), zh,
                 p["w_br_gla"], p["w_br_dsa"], p["w_br_ml"], p["w_out"], p["g_post_mix"], h, tm=tm_small)
        u = _norm_matmul(h, p["g_pre_mlp"], p["w_up"], zero_ff, tm=tm_mid, tn=1024,
                         relu2=True, out_dtype=ACT_DT, name="mlp_up")
        h = _matmul_norm_res(u, p["w_down"], p["g_post_mlp"], h, tm=tm_mid)
        return h, None

    h, _ = lax.scan(layer, h0, layers)
    return h.reshape(B, Tp, D)[:, front + N_META:]
```

```python
import functools
import math

import numpy as np
import jax
import jax.numpy as jnp
from jax import lax
from jax.experimental import pallas as pl
from jax.experimental.pallas import tpu as pltpu

D_MODEL = 1024
N_META = 16
GLA_H, GLA_DK, GLA_DV, GLA_LR, GLA_TAU = 4, 128, 256, 16, 16.0
DSA_H, DSA_DH, DSA_Q_LORA, DSA_KV_LAT = 16, 64, 256, 128
IDX_H, IDX_D, TOPK_MAX = 8, 64, 256
ML_H, ML_DK, ML_DV, ML_CONV = 4, 128, 256, 4
T5_BUCKETS, T5_MAX_DIST = 32, 128
D_FF = 4 * D_MODEL
EPS = 1e-6
NEG = -1e30
LOG2E = math.log2(math.e)

LANES = 128
SUBLANES = 8
VMEM_LIMIT = 56 * 1024 * 1024

MXU_DT = jnp.bfloat16
ACT_DT = jnp.bfloat16

F32 = jnp.float32
I32 = jnp.int32
INT_MIN = -2147483648
HI = lax.Precision.HIGHEST
NT_DIMS = (((1,), (1,)), ((), ()))

Z_GLA_Q, Z_GLA_K, Z_GLA_V, Z_GLA_R = 0, 512, 1024, 2048
Z_ML_Q, Z_ML_K, Z_ML_V, Z_ML_O = 3072, 3584, 4096, 5120
NZL = 6144
Z_GATE_A, Z_GATE_B, Z_GATE_C = 0, 1024, 2048
Z_CQ, Z_CKV = 3072, 3328
Z_SM0, Z_SM1, Z_SM2 = 3456, 3584, 3712
NZH = 3840
KVT_ONES = 16
PREP_ROWS_MAX = 1088
DSA_STREAMS = 4
ATTN_STREAMS = 2
KB_GROUP = 4
GLA_ROWS = 2
MIX_CHUNKS = 6


def _params(sem):
    return pltpu.CompilerParams(dimension_semantics=sem, vmem_limit_bytes=VMEM_LIMIT)


def _mm(a, b):
    return jnp.dot(a.astype(MXU_DT), b.astype(MXU_DT), preferred_element_type=F32)


def _mm_nt(a, b):
    return lax.dot_general(a.astype(MXU_DT), b.astype(MXU_DT), NT_DIMS, preferred_element_type=F32)


def _rms(x, g):
    return x * lax.rsqrt(jnp.mean(x * x, axis=-1, keepdims=True) + EPS) * g


def _log_sigmoid(x):
    return jnp.minimum(x, 0.0) - jnp.log1p(jnp.exp(-jnp.abs(x)))


def _sigmoid(x):
    return 1.0 / (1.0 + jnp.exp(-x))


def _norm_matmul_kernel(x_ref, g_ref, w_ref, b_ref, o_ref, *, relu2, tn):
    xn = _rms(x_ref[...], g_ref[...]).astype(MXU_DT)
    for n0 in range(0, o_ref.shape[1], tn):
        acc = jnp.dot(xn, w_ref[:, n0:n0 + tn], preferred_element_type=F32) + b_ref[:, n0:n0 + tn]
        if relu2:
            acc = jnp.square(jnp.maximum(acc, 0.0))
        o_ref[:, n0:n0 + tn] = acc.astype(o_ref.dtype)


def _norm_matmul(x, g, w, b, *, tm, tn, relu2, out_dtype, name):
    R, K = x.shape
    N = w.shape[1]
    return pl.pallas_call(
        functools.partial(_norm_matmul_kernel, relu2=relu2, tn=tn),
        grid=(R // tm,),
        in_specs=[pl.BlockSpec((tm, K), lambda i: (i, 0)),
                  pl.BlockSpec((1, K), lambda i: (0, 0)),
                  pl.BlockSpec((K, N), lambda i: (0, 0)),
                  pl.BlockSpec((1, N), lambda i: (0, 0))],
        out_specs=pl.BlockSpec((tm, N), lambda i: (i, 0)),
        out_shape=jax.ShapeDtypeStruct((R, N), out_dtype),
        compiler_params=_params(("parallel",)),
        name=name,
    )(x, g, w, b)


def _matmul_norm_res_kernel(u_ref, w_ref, g_ref, h_ref, o_ref):
    y = jnp.dot(u_ref[...], w_ref[...], preferred_element_type=F32)
    o_ref[...] = h_ref[...] + _rms(y, g_ref[...])


def _matmul_norm_res(u, w, g, h, *, tm):
    R, K = u.shape
    N = w.shape[1]
    return pl.pallas_call(
        _matmul_norm_res_kernel,
        grid=(R // tm,),
        in_specs=[pl.BlockSpec((tm, K), lambda i: (i, 0)),
                  pl.BlockSpec((K, N), lambda i: (0, 0)),
                  pl.BlockSpec((1, N), lambda i: (0, 0)),
                  pl.BlockSpec((tm, N), lambda i: (i, 0))],
        out_specs=pl.BlockSpec((tm, N), lambda i: (i, 0)),
        out_shape=jax.ShapeDtypeStruct((R, N), F32),
        compiler_params=_params(("parallel",)),
        name="mlp_down",
    )(u, w, g, h)


def _mix_kernel(oa_ref, ob_ref, oc_ref, ga_ref, gb_ref, gc_ref, wa_ref, wb_ref, wc_ref,
                wo_ref, g_ref, h_ref, o_ref):
    ya = jnp.dot(oa_ref[...], wa_ref[...], preferred_element_type=F32)
    yb = jnp.dot(ob_ref[...], wb_ref[...], preferred_element_type=F32)
    yc = jnp.dot(oc_ref[...], wc_ref[...], preferred_element_type=F32)
    mix = _sigmoid(ga_ref[...]) * ya + _sigmoid(gb_ref[...]) * yb + _sigmoid(gc_ref[...]) * yc
    y = _mm(mix, wo_ref[...])
    o_ref[...] = h_ref[...] + _rms(y, g_ref[...])


def _mix(oa, ob, oc, z, wa, wb, wc, wo, g, h, *, tm):
    R, D = h.shape
    row = lambda i: (i, 0)
    const = lambda i: (0, 0)
    gate = lambda off: pl.BlockSpec((tm, D), lambda i: (i, off // D))
    return pl.pallas_call(
        _mix_kernel,
        grid=(R // tm,),
        in_specs=[pl.BlockSpec((tm, D), row), pl.BlockSpec((tm, D), row), pl.BlockSpec((tm, D), row),
                  gate(Z_GATE_A), gate(Z_GATE_B), gate(Z_GATE_C),
                  pl.BlockSpec((D, D), const), pl.BlockSpec((D, D), const), pl.BlockSpec((D, D), const),
                  pl.BlockSpec((D, D), const), pl.BlockSpec((1, D), const), pl.BlockSpec((tm, D), row)],
        out_specs=pl.BlockSpec((tm, D), row),
        out_shape=jax.ShapeDtypeStruct((R, D), F32),
        compiler_params=_params(("parallel",)),
        name="mix_out",
    )(oa, ob, oc, z, z, z, wa, wb, wc, wo, g, h)


def _gla_kernel(q_ref, k_ref, v_ref, r_ref, a_ref, wa_ref, ba_ref, gh_ref, o_ref, s_ref, *, front, nchunks, nb):
    C = LANES
    tb = pl.program_id(1)

    @pl.when(tb == 0)
    def _():
        s_ref[...] = jnp.zeros_like(s_ref)

    r0 = lax.broadcasted_iota(I32, (C, C), 0)
    r1 = lax.broadcasted_iota(I32, (C, C), 1)
    causal = r0 >= r1
    tril = causal.astype(F32)
    rowi = lax.broadcasted_iota(I32, (C, 1), 0)
    X = [(b, h) for b in range(nb) for h in range(GLA_H)]
    N = range(len(X))
    kq = [slice(h * GLA_DK, (h + 1) * GLA_DK) for _, h in X]
    kv = [slice(h * GLA_DV, (h + 1) * GLA_DV) for _, h in X]

    def body(c, carry):
        sl = pl.ds(pl.multiple_of(c * C, C), C)
        live = ((tb * MIX_CHUNKS + c) * C + rowi) >= front
        a = [a_ref[b, sl, :].astype(MXU_DT) for b in range(nb)]
        g = [_log_sigmoid(jnp.dot(a[b], wa_ref[h], preferred_element_type=F32) + ba_ref[h]) / GLA_TAU for b, h in X]
        G = [jnp.dot(tril, g[i], precision=HI, preferred_element_type=F32) for i in N]
        q = [q_ref[b, sl, kq[i]].astype(F32) * (GLA_DK ** -0.5) for i, (b, h) in enumerate(X)]
        k = [jnp.where(live, k_ref[b, sl, kq[i]].astype(F32), 0.0) for i, (b, h) in enumerate(X)]
        v = [jnp.where(live, v_ref[b, sl, kv[i]].astype(F32), 0.0).astype(MXU_DT) for i, (b, h) in enumerate(X)]
        S = [s_ref[b, h] for b, h in X]
        g_last = [G[i][C - 1:C, :] for i in N]
        g_mid = [G[i][C // 2 - 1:C // 2, :] for i in N]
        o_inter = [_mm(q[i] * jnp.exp(G[i]), S[i]) for i in N]
        A = [jnp.where(causal, _mm_nt(q[i] * jnp.exp(G[i] - g_mid[i]), k[i] * jnp.exp(g_mid[i] - G[i])), 0.0) for i in N]
        o_intra = [_mm(A[i], v[i]) for i in N]
        kd_t = [jnp.transpose(k[i] * jnp.exp(g_last[i] - G[i])) for i in N]
        dec_col = [jnp.transpose(jnp.broadcast_to(jnp.exp(g_last[i]), (C, GLA_DK))) for i in N]
        for i, (b, h) in enumerate(X):
            s_ref[b, h] = jnp.concatenate([dec_col[i], dec_col[i]], axis=1) * S[i] + _mm(kd_t[i], v[i])
        for i, (b, h) in enumerate(X):
            r = r_ref[b, sl, kv[i]].astype(F32)
            o = _rms(o_inter[i] + o_intra[i], gh_ref[...]) * (r * _sigmoid(r))
            o_ref[b, sl, kv[i]] = o.astype(o_ref.dtype)
        return carry

    lax.fori_loop(0, jnp.minimum(MIX_CHUNKS, nchunks - tb * MIX_CHUNKS), body, 0)


def _gla(zl, zh, wa, ba, gh, *, front):
    B, Tp, _ = zl.shape
    nchunks = Tp // LANES
    TB = min(MIX_CHUNKS * LANES, Tp)
    WK, WV = GLA_H * GLA_DK, GLA_H * GLA_DV
    nb = GLA_ROWS if B % GLA_ROWS == 0 else 1
    return pl.pallas_call(
        functools.partial(_gla_kernel, front=front, nchunks=nchunks, nb=nb),
        grid=(B // nb, pl.cdiv(Tp, TB)),
        in_specs=[pl.BlockSpec((nb, TB, WK), lambda b, t: (b, t, Z_GLA_Q // WK)),
                  pl.BlockSpec((nb, TB, WK), lambda b, t: (b, t, Z_GLA_K // WK)),
                  pl.BlockSpec((nb, TB, WV), lambda b, t: (b, t, Z_GLA_V // WV)),
                  pl.BlockSpec((nb, TB, WV), lambda b, t: (b, t, Z_GLA_R // WV)),
                  pl.BlockSpec((nb, TB, LANES), lambda b, t: (b, t, Z_SM0 // LANES)),
                  pl.BlockSpec((GLA_H, LANES, GLA_DK), lambda b, t: (0, 0, 0)),
                  pl.BlockSpec((GLA_H, 1, GLA_DK), lambda b, t: (0, 0, 0)),
                  pl.BlockSpec((1, GLA_DV), lambda b, t: (0, 0))],
        out_specs=pl.BlockSpec((nb, TB, WV), lambda b, t: (b, t, 0)),
        out_shape=jax.ShapeDtypeStruct((B, Tp, WV), ACT_DT),
        scratch_shapes=[pltpu.VMEM((nb, GLA_H, GLA_DK, GLA_DV), F32)],
        compiler_params=_params(("parallel", "arbitrary")),
        name="gla",
    )(zl, zl, zl, zl, zh, wa, ba, gh)


def _mlstm_kernel(q_ref, k_ref, v_ref, og_ref, gcol_ref, grow_ref, cw_ref, fbrow_ref, fb_ref, gh_ref, o_ref,
                  s_ref, m_ref, pq_ref, pk_ref, *, front, nchunks):
    C = LANES
    WK = ML_H * ML_DK
    tb = pl.program_id(1)

    @pl.when(tb == 0)
    def _():
        s_ref[...] = jnp.zeros_like(s_ref)
        m_ref[...] = jnp.zeros_like(m_ref)
        pq_ref[...] = jnp.zeros_like(pq_ref)
        pk_ref[...] = jnp.zeros_like(pk_ref)

    r0 = lax.broadcasted_iota(I32, (C, C), 0)
    r1 = lax.broadcasted_iota(I32, (C, C), 1)
    causal = r0 >= r1
    tril = causal.astype(F32)
    triu = (r0 <= r1).astype(F32)
    rowi = lax.broadcasted_iota(I32, (C, 1), 0)

    def conv_silu(x, tail, w):
        acc = x * w[ML_CONV - 1:ML_CONV, :]
        for d in range(1, ML_CONV):
            sh = pltpu.roll(x, d, 0)
            head = jnp.where(rowi[:SUBLANES] >= d, sh[:SUBLANES], pltpu.roll(tail, d, 0))
            sh = jnp.concatenate([head, sh[SUBLANES:]], axis=0)
            acc = acc + sh * w[ML_CONV - 1 - d:ML_CONV - d, :]
        return acc * _sigmoid(acc)

    def body(c, carry):
        sl = pl.ds(pl.multiple_of(c * C, C), C)
        cg = tb * MIX_CHUNKS + c
        live = (cg * C + rowi) >= front
        xq = jnp.where(live, q_ref[0, sl, :].astype(F32), 0.0)
        xk = jnp.where(live, k_ref[0, sl, :].astype(F32), 0.0)
        q_all = conv_silu(xq, pq_ref[...], cw_ref[:, :WK]) * (ML_DK ** -0.5)
        k_all = conv_silu(xk, pk_ref[...], cw_ref[:, WK:])
        pq_ref[...] = xq[C - SUBLANES:, :]
        pk_ref[...] = xk[C - SUBLANES:, :]
        lf_col = _log_sigmoid(gcol_ref[0, sl, :] + fbrow_ref[...])
        H = range(ML_H)
        vs = [slice(h * ML_DV, (h + 1) * ML_DV) for h in H]
        q = [q_all[:, h * ML_DK:(h + 1) * ML_DK].astype(MXU_DT) for h in H]
        k = [k_all[:, h * ML_DK:(h + 1) * ML_DK] for h in H]
        ones = jnp.ones((C, LANES), MXU_DT)
        v_aug = [jnp.concatenate([jnp.where(live, v_ref[0, sl, vs[h]].astype(F32), 0.0).astype(MXU_DT), ones], axis=1)
                 for h in H]
        m_prev = [m_ref[h, 0:1, :] for h in H]
        S = [s_ref[h] for h in H]
        sel = [jnp.dot(lf_col, (r0 == ML_H + h).astype(F32), precision=HI, preferred_element_type=F32) for h in H]
        b_bc = [jnp.dot(tril, sel[h], precision=HI, preferred_element_type=F32) for h in H]
        li_row = [grow_ref[0, h, pl.ds(cg, 1), :] for h in H]
        lf_row = [_log_sigmoid(grow_ref[0, ML_H + h, pl.ds(cg, 1), :] + fb_ref[h]) for h in H]
        b_row = [jnp.dot(lf_row[h], triu, precision=HI, preferred_element_type=F32) for h in H]
        b_last = [b_row[h][:, C - 1:C] for h in H]
        qk = [_mm_nt(q[h], k[h]) for h in H]
        qs = [_mm(q[h], S[h]) for h in H]
        Dm = [jnp.where(causal, b_bc[h] - (b_row[h] - li_row[h]), -jnp.inf) for h in H]
        m_inter = [b_bc[h] + m_prev[h] for h in H]
        m = [jnp.maximum(m_inter[h], jnp.max(Dm[h], axis=1, keepdims=True)) for h in H]
        pv = [_mm(jnp.exp(Dm[h] - m[h]) * qk[h], v_aug[h]) for h in H]
        dec = [b_last[h] - b_row[h] + li_row[h] for h in H]
        m_new = [jnp.maximum(b_last[h] + m_prev[h], jnp.max(dec[h], axis=1, keepdims=True)) for h in H]
        kt_w = [jnp.transpose(k[h]) * jnp.exp(dec[h] - m_new[h]) for h in H]
        for h in H:
            w_old = jnp.exp(b_last[h] + m_prev[h] - m_new[h])
            s_ref[h] = jnp.concatenate([w_old, w_old, w_old], axis=1) * S[h] + _mm(kt_w[h], v_aug[h])
            m_ref[h] = jnp.broadcast_to(m_new[h], (SUBLANES, C))
        for h in H:
            w_inter = jnp.exp(m_inter[h] - m[h])
            num = jnp.concatenate([w_inter, w_inter, w_inter], axis=1) * qs[h] + pv[h]
            den = num[:, ML_DV:]
            dd = jnp.maximum(jnp.abs(den), jnp.exp(-m[h]))
            hval = num[:, :ML_DV] / jnp.concatenate([dd, dd], axis=1)
            og = og_ref[0, sl, vs[h]].astype(F32)
            o_ref[0, sl, vs[h]] = (_rms(hval, gh_ref[...]) * _sigmoid(og)).astype(o_ref.dtype)
        return carry

    lax.fori_loop(0, jnp.minimum(MIX_CHUNKS, nchunks - tb * MIX_CHUNKS), body, 0)


def _mlstm(zl, zh, gates_row, conv_w, fb_row, fb, gh, *, front):
    B, Tp, _ = zl.shape
    nchunks = Tp // LANES
    TB = min(MIX_CHUNKS * LANES, Tp)
    WK, WV = ML_H * ML_DK, ML_H * ML_DV
    return pl.pallas_call(
        functools.partial(_mlstm_kernel, front=front, nchunks=nchunks),
        grid=(B, pl.cdiv(Tp, TB)),
        in_specs=[pl.BlockSpec((1, TB, WK), lambda b, t: (b, t, Z_ML_Q // WK)),
                  pl.BlockSpec((1, TB, WK), lambda b, t: (b, t, Z_ML_K // WK)),
                  pl.BlockSpec((1, TB, WV), lambda b, t: (b, t, Z_ML_V // WV)),
                  pl.BlockSpec((1, TB, WV), lambda b, t: (b, t, Z_ML_O // WV)),
                  pl.BlockSpec((1, TB, LANES), lambda b, t: (b, t, Z_SM2 // LANES)),
                  pl.BlockSpec((1, 2 * ML_H, nchunks, LANES), lambda b, t: (b, 0, 0, 0)),
                  pl.BlockSpec((ML_CONV, 2 * WK), lambda b, t: (0, 0)),
                  pl.BlockSpec((1, LANES), lambda b, t: (0, 0)),
                  pl.BlockSpec(memory_space=pltpu.SMEM),
                  pl.BlockSpec((1, ML_DV), lambda b, t: (0, 0))],
        out_specs=pl.BlockSpec((1, TB, WV), lambda b, t: (b, t, 0)),
        out_shape=jax.ShapeDtypeStruct((B, Tp, WV), ACT_DT),
        scratch_shapes=[pltpu.VMEM((ML_H, ML_DK, ML_DV + LANES), F32), pltpu.VMEM((ML_H, SUBLANES, LANES), F32),
                        pltpu.VMEM((SUBLANES, WK), F32), pltpu.VMEM((SUBLANES, WK), F32)],
        compiler_params=_params(("parallel", "arbitrary")),
        name="mlstm",
    )(zl, zl, zl, zl, zh, gates_row, conv_w, fb_row, fb, gh)


def _dsa_prep_kernel(cq_ref, ckv_ref, sm_ref, gq_ref, gkv_ref, lng_ref, lnb_ref, wuq_ref, wiq_ref,
                     q_ref, qi_ref, kv_ref, ki_ref):
    cq = _rms(cq_ref[0], gq_ref[...]).astype(MXU_DT)
    for hp in range(DSA_H // 2):
        w2 = wuq_ref[:, 2 * hp * DSA_KV_LAT:(2 * hp + 2) * DSA_KV_LAT]
        qf = jnp.dot(cq, w2, preferred_element_type=F32) * (DSA_KV_LAT ** -0.5 * LOG2E)
        q_ref[0, 2 * hp] = qf[:, :DSA_KV_LAT].astype(q_ref.dtype)
        q_ref[0, 2 * hp + 1] = qf[:, DSA_KV_LAT:].astype(q_ref.dtype)
    for hh in range(IDX_H):
        qi = jnp.dot(cq, wiq_ref[hh], preferred_element_type=F32) * (IDX_D ** -0.5)
        qi_ref[0, hh] = qi.astype(qi_ref.dtype)
    kv_ref[0] = _rms(ckv_ref[0], gkv_ref[...]).astype(kv_ref.dtype)
    x = sm_ref[0][:, :IDX_D]
    mu = jnp.mean(x, axis=-1, keepdims=True)
    var = jnp.mean(jnp.square(x - mu), axis=-1, keepdims=True)
    ki_ref[0] = ((x - mu) * lax.rsqrt(var + EPS) * lng_ref[...] + lnb_ref[...]).astype(ki_ref.dtype)


def _dsa_prep(z, gq, gkv, lng, lnb, wuq, wiq):
    B, Tp, _ = z.shape
    tm = max(t for t in range(2 * SUBLANES, PREP_ROWS_MAX + 1, 2 * SUBLANES) if Tp % t == 0)
    c2 = lambda b, i: (0, 0)
    return pl.pallas_call(
        _dsa_prep_kernel,
        grid=(B, Tp // tm),
        in_specs=[pl.BlockSpec((1, tm, DSA_Q_LORA), lambda b, i: (b, i, Z_CQ // DSA_Q_LORA)),
                  pl.BlockSpec((1, tm, DSA_KV_LAT), lambda b, i: (b, i, Z_CKV // DSA_KV_LAT)),
                  pl.BlockSpec((1, tm, LANES), lambda b, i: (b, i, Z_SM1 // LANES)),
                  pl.BlockSpec((1, DSA_Q_LORA), c2), pl.BlockSpec((1, DSA_KV_LAT), c2),
                  pl.BlockSpec((1, IDX_D), c2), pl.BlockSpec((1, IDX_D), c2),
                  pl.BlockSpec((DSA_Q_LORA, DSA_H * DSA_KV_LAT), c2),
                  pl.BlockSpec((IDX_H, DSA_Q_LORA, IDX_D), lambda b, i: (0, 0, 0))],
        out_specs=[pl.BlockSpec((1, DSA_H, tm, DSA_KV_LAT), lambda b, i: (b, 0, i, 0)),
                   pl.BlockSpec((1, IDX_H, tm, IDX_D), lambda b, i: (b, 0, i, 0)),
                   pl.BlockSpec((1, tm, DSA_KV_LAT), lambda b, i: (b, i, 0)),
                   pl.BlockSpec((1, tm, IDX_D), lambda b, i: (b, i, 0))],
        out_shape=[jax.ShapeDtypeStruct((B, DSA_H, Tp, DSA_KV_LAT), ACT_DT),
                   jax.ShapeDtypeStruct((B, IDX_H, Tp, IDX_D), ACT_DT),
                   jax.ShapeDtypeStruct((B, Tp, DSA_KV_LAT), ACT_DT),
                   jax.ShapeDtypeStruct((B, Tp, IDX_D), ACT_DT)],
        compiler_params=_params(("parallel", "parallel")),
        name="dsa_prep",
    )(z, z, z, gq, gkv, lng, lnb, wuq, wiq)


def _sortable(x):
    bits = pltpu.bitcast(x, I32)
    key = jnp.where(bits < 0, bits ^ 0x7FFFFFFF, bits)
    return jnp.where(x == 0.0, 0, key)


_KEY_NEG = int(np.array(NEG, np.float32).view(np.int32)) ^ 0x7FFFFFFF


def _dsa_kernel(ki_ref, kv_ref, kvt_ref, qi_ref, q_ref, wi_ref, bias_ref, wuv_ref, o_ref,
                key_ref, madd_ref, lg_ref, acc_ref, *, front, topk, Tp, ns):
    QB = LANES
    S = range(ns)
    j = pl.program_id(1)
    nkb = j + 1
    ngrp = nkb // KB_GROUP
    s_loc = lax.broadcasted_iota(I32, (QB, QB), 0)
    t_glob = j * QB + lax.broadcasted_iota(I32, (QB, QB), 1)
    wi = [wi_ref[s] * (IDX_H ** -0.5) for s in S]
    n_rest = Tp - nkb * QB

    def blk(kb):
        return pl.ds(pl.multiple_of(kb * QB, QB), QB)

    def score_block(kb, s):
        ki = ki_ref[s, blk(kb), :]
        acc = jnp.zeros((QB, QB), F32)
        for hp in range(IDX_H // 2):
            qi2 = qi_ref[s, 2 * hp:2 * hp + 2].reshape(2 * QB, IDX_D)
            sc = lax.dot_general(ki, qi2, NT_DIMS, preferred_element_type=F32)
            acc = acc + jnp.maximum(sc[:, :QB], 0.0) * wi[s][2 * hp:2 * hp + 1, :]
            acc = acc + jnp.maximum(sc[:, QB:], 0.0) * wi[s][2 * hp + 1:2 * hp + 2, :]
        s_glob = kb * QB + s_loc
        key = _sortable(jnp.where(s_glob <= t_glob, acc, NEG))
        key_ref[s, blk(kb), :] = jnp.where(s_glob >= front, key, INT_MIN)

    def score_group(i, c):
        for u in range(KB_GROUP):
            for s in S:
                score_block(KB_GROUP * i + u, s)
        return c

    def score_single(kb, c):
        for s in S:
            score_block(kb, s)
        return c

    lax.fori_loop(0, ngrp, score_group, 0)
    lax.fori_loop(ngrp * KB_GROUP, nkb, score_single, 0)

    def fold32(hit):
        return jnp.sum(hit.reshape(QB // SUBLANES, SUBLANES, QB), axis=0)

    def count(pred_fn):
        def hit_of(kb, s):
            return pred_fn(key_ref[s, blk(kb), :], kb, s).astype(I32)

        def group(i, accs):
            out = []
            for s in S:
                hit = hit_of(KB_GROUP * i, s)
                for u in range(1, KB_GROUP):
                    hit = hit + hit_of(KB_GROUP * i + u, s)
                out.append(accs[s] + fold32(hit))
            return tuple(out)

        accs = lax.fori_loop(0, ngrp, group, tuple(jnp.zeros((SUBLANES, QB), I32) for _ in S))
        accs = lax.fori_loop(ngrp * KB_GROUP, nkb,
                             lambda kb, a: tuple(a[s] + fold32(hit_of(kb, s)) for s in S), accs)
        return [jnp.sum(a, axis=0, keepdims=True) for a in accs]

    def count_ge(cands):
        cnt = count(lambda kblk, kb, s: kblk >= cands[s])
        return [cnt[s] + jnp.where(_KEY_NEG >= cands[s], n_rest, 0) for s in S]

    c0 = count_ge([jnp.zeros((1, QB), I32) for _ in S])
    base = tuple(jnp.where(c0[s] >= topk, 0, INT_MIN).astype(I32) for s in S)

    def bit_body(i, base):
        cands = [base[s] | jnp.left_shift(jnp.int32(1), 30 - i) for s in S]
        cnt = count_ge(cands)
        return tuple(jnp.where(cnt[s] >= topk, cands[s], base[s]) for s in S)

    tau = lax.fori_loop(0, 31, bit_body, base)

    def valid_of(kb):
        s_glob = kb * QB + s_loc
        return (s_glob >= front) & (s_glob <= t_glob)

    c_gt = count(lambda kblk, kb, s: kblk > tau[s])
    c_eq = count(lambda kblk, kb, s: (kblk == tau[s]) & valid_of(kb))
    need = [topk - (c_gt[s] + jnp.where(_KEY_NEG > tau[s], n_rest, 0)) for s in S]

    for s in S:
        @pl.when(jnp.max((c_eq[s] > need[s]).astype(I32)) > 0)
        def _(s=s):
            lstrict = (s_loc > lax.broadcasted_iota(I32, (QB, QB), 1)).astype(MXU_DT)
            needf = need[s].astype(F32)

            def tie_body(kb, seen):
                kblk = key_ref[s, blk(kb), :]
                tie = (kblk == tau[s]) & valid_of(kb)
                tief = tie.astype(F32)
                rank = seen + jnp.dot(lstrict, tief.astype(MXU_DT), preferred_element_type=F32)
                key_ref[s, blk(kb), :] = jnp.where(tie & (rank >= needf), INT_MIN, kblk)
                return seen + jnp.sum(tief, axis=0, keepdims=True)

            lax.fori_loop(0, nkb, tie_body, jnp.zeros((1, QB), F32))

    def mask_body(kb, c):
        for s in S:
            sel = (key_ref[s, blk(kb), :] >= tau[s]) & valid_of(kb)
            madd_ref[s, blk(kb), :] = jnp.where(sel, 0.0, NEG)
        return c

    lax.fori_loop(0, nkb, mask_body, 0)

    NP = DSA_H // 2
    grp = (QB // SUBLANES, SUBLANES, 2 * QB)

    AS = ATTN_STREAMS if ns % ATTN_STREAMS == 0 else 1

    def attend(g, c):
        ss = [g * AS + u for u in range(AS)]
        U = range(AS)
        acc_ref[...] = jnp.zeros_like(acc_ref)

        def score_inputs(kb):
            out = []
            for u in U:
                ma = madd_ref[ss[u], blk(kb), :]
                out.append((kv_ref[ss[u], blk(kb), :], jnp.concatenate([ma, ma], axis=1)))
            return out, jnp.minimum(j - kb, 2)

        def stage1(p, slot, inputs):
            per, rel = inputs
            out = []
            for u in U:
                kvb, ma2 = per[u]
                q2 = q_ref[ss[u], 2 * p:2 * p + 2].reshape(2 * QB, DSA_KV_LAT)
                lg = lax.dot_general(kvb, q2, NT_DIMS, preferred_element_type=F32) + bias_ref[p, rel] + ma2
                lg_ref[u, slot, p] = lg
                out.append(jnp.max(jnp.max(lg.reshape(grp), axis=0), axis=0, keepdims=True))
            return out

        def stage2(p, slot, kvts, m_old, bm):
            out = []
            for u in U:
                m_new = jnp.maximum(m_old[u], bm[u])
                pm = jnp.exp2(lg_ref[u, slot, p] - m_new)
                acc_ref[u, p] = (jnp.exp2(m_old[u] - m_new) * acc_ref[u, p]
                                 + jnp.dot(kvts[u], pm.astype(MXU_DT), preferred_element_type=F32))
                out.append(m_new)
            return out

        def kvt_of(kb):
            return [kvt_ref[ss[u], kb] for u in U]

        inp0 = score_inputs(0)
        bm0 = tuple(tuple(stage1(p, 0, inp0)) for p in range(NP))

        def attn_body(i, carry):
            ms, bm_a = carry
            a = 2 * i
            inp_b = score_inputs(a + 1)
            inp_c = score_inputs(jnp.minimum(a + 2, nkb - 1))
            kvt_a = kvt_of(a)
            kvt_b = kvt_of(a + 1)
            ms_a, bm_b = [], []
            for p in range(NP):
                bm_b.append(tuple(stage1(p, 1, inp_b)))
                ms_a.append(tuple(stage2(p, 0, kvt_a, ms[p], bm_a[p])))
            ms_b, bm_c = [], []
            for p in range(NP):
                bm_c.append(tuple(stage1(p, 0, inp_c)))
                ms_b.append(tuple(stage2(p, 1, kvt_b, ms_a[p], bm_b[p])))
            return tuple(ms_b), tuple(bm_c)

        ninf = tuple(tuple(jnp.full((1, 2 * QB), -jnp.inf, F32) for _ in U) for _ in range(NP))
        ms, bm_last = lax.fori_loop(0, nkb // 2, attn_body, (ninf, bm0))

        @pl.when(nkb % 2 == 1)
        def _():
            kvts = kvt_of(nkb - 1)
            for p in range(NP):
                stage2(p, 0, kvts, ms[p], bm_last[p])

        for u in U:
            for p in range(NP):
                acc = acc_ref[u, p]
                o_t = acc[:DSA_KV_LAT] / acc[DSA_KV_LAT:DSA_KV_LAT + 1]
                outs = []
                for hh in range(2):
                    o_h = jnp.transpose(o_t[:, hh * QB:(hh + 1) * QB])
                    outs.append(_mm(o_h, wuv_ref[2 * p + hh]))
                o_ref[ss[u], :, 2 * p * DSA_DH:(2 * p + 2) * DSA_DH] = jnp.concatenate(outs, axis=1).astype(o_ref.dtype)
        return c

    lax.fori_loop(0, ns // AS, attend, 0)


def _dsa(ki, kv, kvt, qi, q, wi_t, bias_tiles, wuv, *, front, topk):
    B, Tp, _ = kv.shape
    nblk = Tp // LANES
    ns = DSA_STREAMS if B % DSA_STREAMS == 0 else 1
    return pl.pallas_call(
        functools.partial(_dsa_kernel, front=front, topk=topk, Tp=Tp, ns=ns),
        grid=(B // ns, nblk),
        in_specs=[pl.BlockSpec((ns, Tp, IDX_D), lambda b, j: (b, 0, 0)),
                  pl.BlockSpec((ns, Tp, DSA_KV_LAT), lambda b, j: (b, 0, 0)),
                  pl.BlockSpec((ns, nblk, DSA_KV_LAT + KVT_ONES, LANES), lambda b, j: (b, 0, 0, 0)),
                  pl.BlockSpec((ns, IDX_H, LANES, IDX_D), lambda b, j: (b, 0, j, 0)),
                  pl.BlockSpec((ns, DSA_H, LANES, DSA_KV_LAT), lambda b, j: (b, 0, j, 0)),
                  pl.BlockSpec((ns, IDX_H, LANES), lambda b, j: (b, 0, j)),
                  pl.BlockSpec((DSA_H // 2, 3, LANES, 2 * LANES), lambda b, j: (0, 0, 0, 0)),
                  pl.BlockSpec((DSA_H, DSA_KV_LAT, DSA_DH), lambda b, j: (0, 0, 0))],
        out_specs=pl.BlockSpec((ns, LANES, DSA_H * DSA_DH), lambda b, j: (b, j, 0)),
        out_shape=jax.ShapeDtypeStruct((B, Tp, DSA_H * DSA_DH), ACT_DT),
        scratch_shapes=[pltpu.VMEM((ns, Tp, LANES), I32), pltpu.VMEM((ns, Tp, LANES), F32),
                        pltpu.VMEM((ATTN_STREAMS, 2, DSA_H // 2, LANES, 2 * LANES), F32),
                        pltpu.VMEM((ATTN_STREAMS, DSA_H // 2, DSA_KV_LAT + KVT_ONES, 2 * LANES), F32)],
        compiler_params=_params(("parallel", "arbitrary")),
        name="dsa_attn",
    )(ki, kv, kvt, qi, q, wi_t, bias_tiles, wuv)


def _t5_bucket(rel):
    n = jnp.maximum(rel, 0)
    max_exact = T5_BUCKETS // 2
    nf = jnp.maximum(n, 1).astype(F32)
    large = max_exact + (jnp.log(nf / max_exact) / math.log(T5_MAX_DIST / max_exact)
                         * (T5_BUCKETS - max_exact)).astype(I32)
    large = jnp.minimum(large, T5_BUCKETS - 1)
    return jnp.where(n < max_exact, n, large)


def _t5_tiles(t5_bias):
    s = jnp.arange(LANES, dtype=I32)[:, None]
    t = jnp.arange(LANES, dtype=I32)[None, :]
    tiles = []
    for shift in (0, LANES, 2 * LANES):
        tiles.append(t5_bias[_t5_bucket(shift + t - s)])
    tiles = jnp.stack(tiles, axis=0)
    tiles = tiles.transpose(3, 0, 1, 2).reshape(DSA_H // 2, 2, 3, LANES, LANES)
    return (tiles.transpose(0, 2, 3, 1, 4).reshape(DSA_H // 2, 3, LANES, 2 * LANES) * LOG2E).astype(F32)


def _pack_in_proj(w_in, b_in):
    splits = np.cumsum([GLA_H * GLA_DK, GLA_H * GLA_DK, GLA_H * GLA_DV, GLA_H * GLA_DV, GLA_LR,
                        DSA_Q_LORA, DSA_KV_LAT, IDX_D, IDX_H,
                        ML_H * ML_DK, ML_H * ML_DK, ML_H * ML_DV, ML_H * ML_DV, ML_H, ML_H,
                        D_MODEL, D_MODEL])[:].tolist()

    def pack(a):
        (gq, gk, gv, gr, ga, cq, ckv, ik, iw, mq, mk, mv, mo, mi, mf, a_, b_, c_) = jnp.split(a, splits, axis=-1)
        pad = lambda n: jnp.zeros(a.shape[:-1] + (n,), a.dtype)
        lo = jnp.concatenate([gq, gk, gv, gr, mq, mk, mv, mo], axis=-1)
        hi = jnp.concatenate([a_, b_, c_, cq, ckv,
                              ga, pad(LANES - GLA_LR),
                              ik, iw, pad(LANES - IDX_D - IDX_H),
                              mi, mf, pad(LANES - 2 * ML_H)], axis=-1)
        return lo, hi

    return pack(w_in), pack(b_in)


def kernel(x, meta_tokens, t5_bias, g_pre_mix, w_in, b_in, w_gla_a2, b_gla_a, g_gla_head, w_br_gla, g_dsa_q, w_dsa_uq, w_idx_q, g_dsa_kv, w_dsa_uv, idx_ln_g, idx_ln_b, w_br_dsa, ml_conv, ml_f_bias, g_ml_head, w_br_ml, w_out, g_post_mix, g_pre_mlp, w_up, w_down, g_post_mlp):
    B, S, D = x.shape
    L = w_in.shape[0]
    T = S + N_META
    front = (-T) % LANES
    Tp = T + front
    R = B * Tp
    nblk = Tp // LANES
    topk = min(TOPK_MAX, (T - N_META) // 4)
    tm_mid = 512 if R % 512 == 0 else LANES
    tm_small = 512 if R % 512 == 0 else LANES

    (w_lo, w_hi), (b_lo, b_hi) = _pack_in_proj(w_in, b_in)
    wa2 = jnp.pad(w_gla_a2, ((0, 0), (0, LANES - GLA_LR), (0, 0)))
    wa2 = wa2.reshape(L, LANES, GLA_H, GLA_DK).transpose(0, 2, 1, 3).astype(MXU_DT)
    fb_row = jnp.pad(ml_f_bias, ((0, 0), (ML_H, LANES - 2 * ML_H)))[:, None, :]
    layers = dict(
        g_pre_mix=g_pre_mix[:, None, :], w_lo=w_lo.astype(MXU_DT), b_lo=b_lo[:, None, :],
        w_hi=w_hi.astype(MXU_DT), b_hi=b_hi[:, None, :],
        wa2=wa2, ba=b_gla_a.reshape(L, GLA_H, 1, GLA_DK), g_gla_head=g_gla_head[:, None, :],
        w_br_gla=w_br_gla.astype(MXU_DT),
        g_dsa_q=g_dsa_q[:, None, :], w_dsa_uq=w_dsa_uq.astype(MXU_DT),
        w_idx_q=w_idx_q.reshape(L, DSA_Q_LORA, IDX_H, IDX_D).transpose(0, 2, 1, 3).astype(MXU_DT),
        g_dsa_kv=g_dsa_kv[:, None, :], w_dsa_uv=w_dsa_uv.astype(MXU_DT),
        idx_ln_g=idx_ln_g[:, None, :], idx_ln_b=idx_ln_b[:, None, :], w_br_dsa=w_br_dsa.astype(MXU_DT),
        ml_conv=ml_conv, fb_row=fb_row, ml_f_bias=ml_f_bias, g_ml_head=g_ml_head[:, None, :],
        w_br_ml=w_br_ml.astype(MXU_DT), w_out=w_out.astype(MXU_DT), g_post_mix=g_post_mix[:, None, :],
        g_pre_mlp=g_pre_mlp[:, None, :], w_up=w_up.astype(MXU_DT), w_down=w_down.astype(MXU_DT),
        g_post_mlp=g_post_mlp[:, None, :],
    )
    bias_tiles = _t5_tiles(t5_bias)
    zero_ff = jnp.zeros((1, D_FF), F32)

    meta = jnp.broadcast_to(meta_tokens.astype(x.dtype)[None], (B, N_META, D))
    h0 = jnp.concatenate([jnp.zeros((B, front, D), x.dtype), meta, x], axis=1).reshape(R, D)

    def layer(h, p):
        zl = _norm_matmul(h, p["g_pre_mix"], p["w_lo"], p["b_lo"], tm=tm_mid, tn=1024, relu2=False,
                          out_dtype=ACT_DT, name="in_proj_lo").reshape(B, Tp, NZL)
        zh = _norm_matmul(h, p["g_pre_mix"], p["w_hi"], p["b_hi"], tm=tm_mid, tn=768, relu2=False,
                          out_dtype=F32, name="in_proj_hi")
        zh3 = zh.reshape(B, Tp, NZH)
        o_gla = _gla(zl, zh3, p["wa2"], p["ba"], p["g_gla_head"], front=front)
        gates_row = zh3[:, :, Z_SM2:Z_SM2 + 2 * ML_H].transpose(0, 2, 1).reshape(B, 2 * ML_H, nblk, LANES)
        o_ml = _mlstm(zl, zh3, gates_row, p["ml_conv"], p["fb_row"], p["ml_f_bias"], p["g_ml_head"], front=front)
        q, qi, kv, ki = _dsa_prep(zh3, p["g_dsa_q"], p["g_dsa_kv"], p["idx_ln_g"], p["idx_ln_b"],
                                  p["w_dsa_uq"], p["w_idx_q"])
        kvt = kv.reshape(B, nblk, LANES, DSA_KV_LAT).transpose(0, 1, 3, 2)
        kvt = jnp.concatenate([kvt, jnp.ones((B, nblk, KVT_ONES, LANES), kvt.dtype)], axis=2)
        wi_t = zh3[:, :, Z_SM1 + IDX_D:Z_SM1 + IDX_D + IDX_H].transpose(0, 2, 1)
        o_dsa = _dsa(ki, kv, kvt, qi, q, wi_t, bias_tiles, p["w_dsa_uv"], front=front, topk=topk)
        h = _mix(o_gla.reshape(R, D), o_dsa.reshape(R, D), o_ml.reshape(R, D), zh,
                 p["w_br_gla"], p["w_br_dsa"], p["w_br_ml"], p["w_out"], p["g_post_mix"], h, tm=tm_small)
        u = _norm_matmul(h, p["g_pre_mlp"], p["w_up"], zero_ff, tm=tm_mid, tn=1024,
                         relu2=True, out_dtype=ACT_DT, name="mlp_up")
        h = _matmul_norm_res(u, p["w_down"], p["g_post_mlp"], h, tm=tm_mid)
        return h, None

    h, _ = lax.scan(layer, h0, layers)
    return h.reshape(B, Tp, D)[:, front + N_META:]
```

```python
import functools
import math

import numpy as np
import jax
import jax.numpy as jnp
from jax import lax
from jax.experimental import pallas as pl
from jax.experimental.pallas import tpu as pltpu

D_MODEL = 1024
N_META = 16
GLA_H, GLA_DK, GLA_DV, GLA_LR, GLA_TAU = 4, 128, 256, 16, 16.0
DSA_H, DSA_DH, DSA_Q_LORA, DSA_KV_LAT = 16, 64, 256, 128
IDX_H, IDX_D, TOPK_MAX = 8, 64, 256
ML_H, ML_DK, ML_DV, ML_CONV = 4, 128, 256, 4
T5_BUCKETS, T5_MAX_DIST = 32, 128
D_FF = 4 * D_MODEL
EPS = 1e-6
NEG = -1e30
LOG2E = math.log2(math.e)

LANES = 128
SUBLANES = 8
VMEM_LIMIT = 56 * 1024 * 1024

MXU_DT = jnp.bfloat16
ACT_DT = jnp.bfloat16

F32 = jnp.float32
I32 = jnp.int32
INT_MIN = -2147483648
HI = lax.Precision.HIGHEST
NT_DIMS = (((1,), (1,)), ((), ()))

Z_GLA_Q, Z_GLA_K, Z_GLA_V, Z_GLA_R = 0, 512, 1024, 2048
Z_ML_Q, Z_ML_K, Z_ML_V, Z_ML_O = 3072, 3584, 4096, 5120
NZL = 6144
Z_GATE_A, Z_GATE_B, Z_GATE_C = 0, 1024, 2048
Z_CQ, Z_CKV = 3072, 3328
Z_SM0, Z_SM1, Z_SM2 = 3456, 3584, 3712
NZH = 3840
KVT_ONES = 16
PREP_ROWS_MAX = 1088
DSA_STREAMS = 4
ATTN_STREAMS = 2
KB_GROUP = 4
GLA_ROWS = 2
MIX_CHUNKS = 6


def _params(sem):
    return pltpu.CompilerParams(dimension_semantics=sem, vmem_limit_bytes=VMEM_LIMIT)


def _mm(a, b):
    return jnp.dot(a.astype(MXU_DT), b.astype(MXU_DT), preferred_element_type=F32)


def _mm_nt(a, b):
    return lax.dot_general(a.astype(MXU_DT), b.astype(MXU_DT), NT_DIMS, preferred_element_type=F32)


def _rms(x, g):
    return x * lax.rsqrt(jnp.mean(x * x, axis=-1, keepdims=True) + EPS) * g


def _log_sigmoid(x):
    return jnp.minimum(x, 0.0) - jnp.log1p(jnp.exp(-jnp.abs(x)))


def _sigmoid(x):
    return 0.5 * jnp.tanh(0.5 * x) + 0.5


def _norm_matmul_kernel(x_ref, g_ref, w_ref, b_ref, o_ref, *, relu2, tn):
    xn = _rms(x_ref[...], g_ref[...]).astype(MXU_DT)
    for n0 in range(0, o_ref.shape[1], tn):
        acc = jnp.dot(xn, w_ref[:, n0:n0 + tn], preferred_element_type=F32) + b_ref[:, n0:n0 + tn]
        if relu2:
            acc = jnp.square(jnp.maximum(acc, 0.0))
        o_ref[:, n0:n0 + tn] = acc.astype(o_ref.dtype)


def _norm_matmul(x, g, w, b, *, tm, tn, relu2, out_dtype, name):
    R, K = x.shape
    N = w.shape[1]
    return pl.pallas_call(
        functools.partial(_norm_matmul_kernel, relu2=relu2, tn=tn),
        grid=(R // tm,),
        in_specs=[pl.BlockSpec((tm, K), lambda i: (i, 0)),
                  pl.BlockSpec((1, K), lambda i: (0, 0)),
                  pl.BlockSpec((K, N), lambda i: (0, 0)),
                  pl.BlockSpec((1, N), lambda i: (0, 0))],
        out_specs=pl.BlockSpec((tm, N), lambda i: (i, 0)),
        out_shape=jax.ShapeDtypeStruct((R, N), out_dtype),
        compiler_params=_params(("parallel",)),
        name=name,
    )(x, g, w, b)


def _matmul_norm_res_kernel(u_ref, w_ref, g_ref, h_ref, o_ref):
    y = jnp.dot(u_ref[...], w_ref[...], preferred_element_type=F32)
    o_ref[...] = h_ref[...] + _rms(y, g_ref[...])


def _matmul_norm_res(u, w, g, h, *, tm):
    R, K = u.shape
    N = w.shape[1]
    return pl.pallas_call(
        _matmul_norm_res_kernel,
        grid=(R // tm,),
        in_specs=[pl.BlockSpec((tm, K), lambda i: (i, 0)),
                  pl.BlockSpec((K, N), lambda i: (0, 0)),
                  pl.BlockSpec((1, N), lambda i: (0, 0)),
                  pl.BlockSpec((tm, N), lambda i: (i, 0))],
        out_specs=pl.BlockSpec((tm, N), lambda i: (i, 0)),
        out_shape=jax.ShapeDtypeStruct((R, N), F32),
        compiler_params=_params(("parallel",)),
        name="mlp_down",
    )(u, w, g, h)


def _mix_kernel(oa_ref, ob_ref, oc_ref, ga_ref, gb_ref, gc_ref, wa_ref, wb_ref, wc_ref,
                wo_ref, g_ref, h_ref, o_ref):
    ya = jnp.dot(oa_ref[...], wa_ref[...], preferred_element_type=F32)
    yb = jnp.dot(ob_ref[...], wb_ref[...], preferred_element_type=F32)
    yc = jnp.dot(oc_ref[...], wc_ref[...], preferred_element_type=F32)
    mix = _sigmoid(ga_ref[...]) * ya + _sigmoid(gb_ref[...]) * yb + _sigmoid(gc_ref[...]) * yc
    y = _mm(mix, wo_ref[...])
    o_ref[...] = h_ref[...] + _rms(y, g_ref[...])


def _mix(oa, ob, oc, z, wa, wb, wc, wo, g, h, *, tm):
    R, D = h.shape
    row = lambda i: (i, 0)
    const = lambda i: (0, 0)
    gate = lambda off: pl.BlockSpec((tm, D), lambda i: (i, off // D))
    return pl.pallas_call(
        _mix_kernel,
        grid=(R // tm,),
        in_specs=[pl.BlockSpec((tm, D), row), pl.BlockSpec((tm, D), row), pl.BlockSpec((tm, D), row),
                  gate(Z_GATE_A), gate(Z_GATE_B), gate(Z_GATE_C),
                  pl.BlockSpec((D, D), const), pl.BlockSpec((D, D), const), pl.BlockSpec((D, D), const),
                  pl.BlockSpec((D, D), const), pl.BlockSpec((1, D), const), pl.BlockSpec((tm, D), row)],
        out_specs=pl.BlockSpec((tm, D), row),
        out_shape=jax.ShapeDtypeStruct((R, D), F32),
        compiler_params=_params(("parallel",)),
        name="mix_out",
    )(oa, ob, oc, z, z, z, wa, wb, wc, wo, g, h)


def _gla_kernel(q_ref, k_ref, v_ref, r_ref, a_ref, wa_ref, ba_ref, gh_ref, o_ref, s_ref, *, front, nchunks, nb):
    C = LANES
    tb = pl.program_id(1)

    @pl.when(tb == 0)
    def _():
        s_ref[...] = jnp.zeros_like(s_ref)

    r0 = lax.broadcasted_iota(I32, (C, C), 0)
    r1 = lax.broadcasted_iota(I32, (C, C), 1)
    causal = r0 >= r1
    tril = causal.astype(F32)
    rowi = lax.broadcasted_iota(I32, (C, 1), 0)
    X = [(b, h) for b in range(nb) for h in range(GLA_H)]
    N = range(len(X))
    kq = [slice(h * GLA_DK, (h + 1) * GLA_DK) for _, h in X]
    kv = [slice(h * GLA_DV, (h + 1) * GLA_DV) for _, h in X]

    def body(c, carry):
        sl = pl.ds(pl.multiple_of(c * C, C), C)
        live = ((tb * MIX_CHUNKS + c) * C + rowi) >= front
        a = [a_ref[b, sl, :].astype(MXU_DT) for b in range(nb)]
        g = [_log_sigmoid(jnp.dot(a[b], wa_ref[h], preferred_element_type=F32) + ba_ref[h]) / GLA_TAU for b, h in X]
        G = [jnp.dot(tril, g[i], precision=HI, preferred_element_type=F32) for i in N]
        q = [q_ref[b, sl, kq[i]].astype(F32) * (GLA_DK ** -0.5) for i, (b, h) in enumerate(X)]
        k = [jnp.where(live, k_ref[b, sl, kq[i]].astype(F32), 0.0) for i, (b, h) in enumerate(X)]
        v = [jnp.where(live, v_ref[b, sl, kv[i]].astype(F32), 0.0).astype(MXU_DT) for i, (b, h) in enumerate(X)]
        S = [s_ref[b, h] for b, h in X]
        g_last = [G[i][C - 1:C, :] for i in N]
        g_mid = [G[i][C // 2 - 1:C // 2, :] for i in N]
        o_inter = [_mm(q[i] * jnp.exp(G[i]), S[i]) for i in N]
        A = [jnp.where(causal, _mm_nt(q[i] * jnp.exp(G[i] - g_mid[i]), k[i] * jnp.exp(g_mid[i] - G[i])), 0.0) for i in N]
        o_intra = [_mm(A[i], v[i]) for i in N]
        kd_t = [jnp.transpose(k[i] * jnp.exp(g_last[i] - G[i])) for i in N]
        dec_col = [jnp.transpose(jnp.broadcast_to(jnp.exp(g_last[i]), (C, GLA_DK))) for i in N]
        for i, (b, h) in enumerate(X):
            s_ref[b, h] = jnp.concatenate([dec_col[i], dec_col[i]], axis=1) * S[i] + _mm(kd_t[i], v[i])
        for i, (b, h) in enumerate(X):
            r = r_ref[b, sl, kv[i]].astype(F32)
            o = _rms(o_inter[i] + o_intra[i], gh_ref[...]) * (r * _sigmoid(r))
            o_ref[b, sl, kv[i]] = o.astype(o_ref.dtype)
        return carry

    lax.fori_loop(0, jnp.minimum(MIX_CHUNKS, nchunks - tb * MIX_CHUNKS), body, 0)


def _gla(zl, zh, wa, ba, gh, *, front):
    B, Tp, _ = zl.shape
    nchunks = Tp // LANES
    TB = min(MIX_CHUNKS * LANES, Tp)
    WK, WV = GLA_H * GLA_DK, GLA_H * GLA_DV
    nb = GLA_ROWS if B % GLA_ROWS == 0 else 1
    return pl.pallas_call(
        functools.partial(_gla_kernel, front=front, nchunks=nchunks, nb=nb),
        grid=(B // nb, pl.cdiv(Tp, TB)),
        in_specs=[pl.BlockSpec((nb, TB, WK), lambda b, t: (b, t, Z_GLA_Q // WK)),
                  pl.BlockSpec((nb, TB, WK), lambda b, t: (b, t, Z_GLA_K // WK)),
                  pl.BlockSpec((nb, TB, WV), lambda b, t: (b, t, Z_GLA_V // WV)),
                  pl.BlockSpec((nb, TB, WV), lambda b, t: (b, t, Z_GLA_R // WV)),
                  pl.BlockSpec((nb, TB, LANES), lambda b, t: (b, t, Z_SM0 // LANES)),
                  pl.BlockSpec((GLA_H, LANES, GLA_DK), lambda b, t: (0, 0, 0)),
                  pl.BlockSpec((GLA_H, 1, GLA_DK), lambda b, t: (0, 0, 0)),
                  pl.BlockSpec((1, GLA_DV), lambda b, t: (0, 0))],
        out_specs=pl.BlockSpec((nb, TB, WV), lambda b, t: (b, t, 0)),
        out_shape=jax.ShapeDtypeStruct((B, Tp, WV), ACT_DT),
        scratch_shapes=[pltpu.VMEM((nb, GLA_H, GLA_DK, GLA_DV), F32)],
        compiler_params=_params(("parallel", "arbitrary")),
        name="gla",
    )(zl, zl, zl, zl, zh, wa, ba, gh)


def _mlstm_kernel(q_ref, k_ref, v_ref, og_ref, gcol_ref, cw_ref, fbrow_ref, fb_ref, gh_ref, o_ref,
                  s_ref, m_ref, pq_ref, pk_ref, *, front, nchunks):
    C = LANES
    WK = ML_H * ML_DK
    tb = pl.program_id(1)

    @pl.when(tb == 0)
    def _():
        s_ref[...] = jnp.zeros_like(s_ref)
        m_ref[...] = jnp.zeros_like(m_ref)
        pq_ref[...] = jnp.zeros_like(pq_ref)
        pk_ref[...] = jnp.zeros_like(pk_ref)

    r0 = lax.broadcasted_iota(I32, (C, C), 0)
    r1 = lax.broadcasted_iota(I32, (C, C), 1)
    causal = r0 >= r1
    tril = causal.astype(F32)
    triu = (r0 <= r1).astype(F32)
    rowi = lax.broadcasted_iota(I32, (C, 1), 0)

    def conv_silu(x, tail, w):
        acc = x * w[ML_CONV - 1:ML_CONV, :]
        for d in range(1, ML_CONV):
            sh = pltpu.roll(x, d, 0)
            head = jnp.where(rowi[:SUBLANES] >= d, sh[:SUBLANES], pltpu.roll(tail, d, 0))
            sh = jnp.concatenate([head, sh[SUBLANES:]], axis=0)
            acc = acc + sh * w[ML_CONV - 1 - d:ML_CONV - d, :]
        return acc * _sigmoid(acc)

    def body(c, carry):
        sl = pl.ds(pl.multiple_of(c * C, C), C)
        cg = tb * MIX_CHUNKS + c
        live = (cg * C + rowi) >= front
        xq = jnp.where(live, q_ref[0, sl, :].astype(F32), 0.0)
        xk = jnp.where(live, k_ref[0, sl, :].astype(F32), 0.0)
        q_all = conv_silu(xq, pq_ref[...], cw_ref[:, :WK]) * (ML_DK ** -0.5)
        k_all = conv_silu(xk, pk_ref[...], cw_ref[:, WK:])
        pq_ref[...] = xq[C - SUBLANES:, :]
        pk_ref[...] = xk[C - SUBLANES:, :]
        lf_col = _log_sigmoid(gcol_ref[0, sl, :] + fbrow_ref[...])
        H = range(ML_H)
        vs = [slice(h * ML_DV, (h + 1) * ML_DV) for h in H]
        q = [q_all[:, h * ML_DK:(h + 1) * ML_DK].astype(MXU_DT) for h in H]
        k = [k_all[:, h * ML_DK:(h + 1) * ML_DK] for h in H]
        ones = jnp.ones((C, LANES), MXU_DT)
        v_aug = [jnp.concatenate([jnp.where(live, v_ref[0, sl, vs[h]].astype(F32), 0.0).astype(MXU_DT), ones], axis=1)
                 for h in H]
        m_prev = [m_ref[h, 0:1, :] for h in H]
        S = [s_ref[h] for h in H]
        sel = [jnp.dot(lf_col, (r0 == ML_H + h).astype(F32), precision=HI, preferred_element_type=F32) for h in H]
        b_bc = [jnp.dot(tril, sel[h], precision=HI, preferred_element_type=F32) for h in H]
        g_t = jnp.transpose(gcol_ref[0, sl, :])
        li_row = [g_t[h:h + 1, :] for h in H]
        lf_row = [_log_sigmoid(g_t[ML_H + h:ML_H + h + 1, :] + fb_ref[h]) for h in H]
        b_row = [jnp.dot(lf_row[h], triu, precision=HI, preferred_element_type=F32) for h in H]
        b_last = [b_row[h][:, C - 1:C] for h in H]
        qk = [_mm_nt(q[h], k[h]) for h in H]
        qs = [_mm(q[h], S[h]) for h in H]
        Dm = [jnp.where(causal, b_bc[h] - (b_row[h] - li_row[h]), -jnp.inf) for h in H]
        m_inter = [b_bc[h] + m_prev[h] for h in H]
        m = [jnp.maximum(m_inter[h], jnp.max(Dm[h], axis=1, keepdims=True)) for h in H]
        pv = [_mm(jnp.exp(Dm[h] - m[h]) * qk[h], v_aug[h]) for h in H]
        dec = [b_last[h] - b_row[h] + li_row[h] for h in H]
        m_new = [jnp.maximum(b_last[h] + m_prev[h], jnp.max(dec[h], axis=1, keepdims=True)) for h in H]
        kt_w = [jnp.transpose(k[h]) * jnp.exp(dec[h] - m_new[h]) for h in H]
        for h in H:
            w_old = jnp.exp(b_last[h] + m_prev[h] - m_new[h])
            s_ref[h] = jnp.concatenate([w_old, w_old, w_old], axis=1) * S[h] + _mm(kt_w[h], v_aug[h])
            m_ref[h] = jnp.broadcast_to(m_new[h], (SUBLANES, C))
        for h in H:
            w_inter = jnp.exp(m_inter[h] - m[h])
            num = jnp.concatenate([w_inter, w_inter, w_inter], axis=1) * qs[h] + pv[h]
            den = num[:, ML_DV:]
            dd = jnp.maximum(jnp.abs(den), jnp.exp(-m[h]))
            hval = num[:, :ML_DV] / jnp.concatenate([dd, dd], axis=1)
            og = og_ref[0, sl, vs[h]].astype(F32)
            o_ref[0, sl, vs[h]] = (_rms(hval, gh_ref[...]) * _sigmoid(og)).astype(o_ref.dtype)
        return carry

    lax.fori_loop(0, jnp.minimum(MIX_CHUNKS, nchunks - tb * MIX_CHUNKS), body, 0)


def _mlstm(zl, zh, conv_w, fb_row, fb, gh, *, front):
    B, Tp, _ = zl.shape
    nchunks = Tp // LANES
    TB = min(MIX_CHUNKS * LANES, Tp)
    WK, WV = ML_H * ML_DK, ML_H * ML_DV
    return pl.pallas_call(
        functools.partial(_mlstm_kernel, front=front, nchunks=nchunks),
        grid=(B, pl.cdiv(Tp, TB)),
        in_specs=[pl.BlockSpec((1, TB, WK), lambda b, t: (b, t, Z_ML_Q // WK)),
                  pl.BlockSpec((1, TB, WK), lambda b, t: (b, t, Z_ML_K // WK)),
                  pl.BlockSpec((1, TB, WV), lambda b, t: (b, t, Z_ML_V // WV)),
                  pl.BlockSpec((1, TB, WV), lambda b, t: (b, t, Z_ML_O // WV)),
                  pl.BlockSpec((1, TB, LANES), lambda b, t: (b, t, Z_SM2 // LANES)),
                  pl.BlockSpec((ML_CONV, 2 * WK), lambda b, t: (0, 0)),
                  pl.BlockSpec((1, LANES), lambda b, t: (0, 0)),
                  pl.BlockSpec(memory_space=pltpu.SMEM),
                  pl.BlockSpec((1, ML_DV), lambda b, t: (0, 0))],
        out_specs=pl.BlockSpec((1, TB, WV), lambda b, t: (b, t, 0)),
        out_shape=jax.ShapeDtypeStruct((B, Tp, WV), ACT_DT),
        scratch_shapes=[pltpu.VMEM((ML_H, ML_DK, ML_DV + LANES), F32), pltpu.VMEM((ML_H, SUBLANES, LANES), F32),
                        pltpu.VMEM((SUBLANES, WK), F32), pltpu.VMEM((SUBLANES, WK), F32)],
        compiler_params=_params(("parallel", "arbitrary")),
        name="mlstm",
    )(zl, zl, zl, zl, zh, conv_w, fb_row, fb, gh)


def _dsa_prep_kernel(cq_ref, ckv_ref, sm_ref, gq_ref, gkv_ref, lng_ref, lnb_ref, wuq_ref, wiq_ref,
                     q_ref, qi_ref, kv_ref, ki_ref):
    cq = _rms(cq_ref[0], gq_ref[...]).astype(MXU_DT)
    for hp in range(DSA_H // 2):
        w2 = wuq_ref[:, 2 * hp * DSA_KV_LAT:(2 * hp + 2) * DSA_KV_LAT]
        qf = jnp.dot(cq, w2, preferred_element_type=F32) * (DSA_KV_LAT ** -0.5 * LOG2E)
        q_ref[0, 2 * hp] = qf[:, :DSA_KV_LAT].astype(q_ref.dtype)
        q_ref[0, 2 * hp + 1] = qf[:, DSA_KV_LAT:].astype(q_ref.dtype)
    for hh in range(IDX_H):
        qi = jnp.dot(cq, wiq_ref[hh], preferred_element_type=F32) * (IDX_D ** -0.5)
        qi_ref[0, hh] = qi.astype(qi_ref.dtype)
    kv_ref[0] = _rms(ckv_ref[0], gkv_ref[...]).astype(kv_ref.dtype)
    x = sm_ref[0][:, :IDX_D]
    mu = jnp.mean(x, axis=-1, keepdims=True)
    var = jnp.mean(jnp.square(x - mu), axis=-1, keepdims=True)
    ki_ref[0] = ((x - mu) * lax.rsqrt(var + EPS) * lng_ref[...] + lnb_ref[...]).astype(ki_ref.dtype)


def _dsa_prep(z, gq, gkv, lng, lnb, wuq, wiq):
    B, Tp, _ = z.shape
    tm = max(t for t in range(2 * SUBLANES, PREP_ROWS_MAX + 1, 2 * SUBLANES) if Tp % t == 0)
    c2 = lambda b, i: (0, 0)
    return pl.pallas_call(
        _dsa_prep_kernel,
        grid=(B, Tp // tm),
        in_specs=[pl.BlockSpec((1, tm, DSA_Q_LORA), lambda b, i: (b, i, Z_CQ // DSA_Q_LORA)),
                  pl.BlockSpec((1, tm, DSA_KV_LAT), lambda b, i: (b, i, Z_CKV // DSA_KV_LAT)),
                  pl.BlockSpec((1, tm, LANES), lambda b, i: (b, i, Z_SM1 // LANES)),
                  pl.BlockSpec((1, DSA_Q_LORA), c2), pl.BlockSpec((1, DSA_KV_LAT), c2),
                  pl.BlockSpec((1, IDX_D), c2), pl.BlockSpec((1, IDX_D), c2),
                  pl.BlockSpec((DSA_Q_LORA, DSA_H * DSA_KV_LAT), c2),
                  pl.BlockSpec((IDX_H, DSA_Q_LORA, IDX_D), lambda b, i: (0, 0, 0))],
        out_specs=[pl.BlockSpec((1, DSA_H, tm, DSA_KV_LAT), lambda b, i: (b, 0, i, 0)),
                   pl.BlockSpec((1, IDX_H, tm, IDX_D), lambda b, i: (b, 0, i, 0)),
                   pl.BlockSpec((1, tm, DSA_KV_LAT), lambda b, i: (b, i, 0)),
                   pl.BlockSpec((1, tm, IDX_D), lambda b, i: (b, i, 0))],
        out_shape=[jax.ShapeDtypeStruct((B, DSA_H, Tp, DSA_KV_LAT), ACT_DT),
                   jax.ShapeDtypeStruct((B, IDX_H, Tp, IDX_D), ACT_DT),
                   jax.ShapeDtypeStruct((B, Tp, DSA_KV_LAT), ACT_DT),
                   jax.ShapeDtypeStruct((B, Tp, IDX_D), ACT_DT)],
        compiler_params=_params(("parallel", "parallel")),
        name="dsa_prep",
    )(z, z, z, gq, gkv, lng, lnb, wuq, wiq)


def _sortable(x):
    bits = pltpu.bitcast(x, I32)
    key = jnp.where(bits < 0, bits ^ 0x7FFFFFFF, bits)
    return jnp.where(x == 0.0, 0, key)


_KEY_NEG = int(np.array(NEG, np.float32).view(np.int32)) ^ 0x7FFFFFFF


def _dsa_kernel(ki_ref, kv_ref, kvt_ref, qi_ref, q_ref, wi_ref, bias_ref, wuv_ref, o_ref,
                key_ref, madd_ref, lg_ref, acc_ref, *, front, topk, Tp, ns):
    QB = LANES
    S = range(ns)
    j = pl.program_id(1)
    nkb = j + 1
    ngrp = nkb // KB_GROUP
    s_loc = lax.broadcasted_iota(I32, (QB, QB), 0)
    t_glob = j * QB + lax.broadcasted_iota(I32, (QB, QB), 1)
    wi = [jnp.transpose(wi_ref[s])[IDX_D:IDX_D + IDX_H, :] * (IDX_H ** -0.5) for s in S]
    n_rest = Tp - nkb * QB

    def blk(kb):
        return pl.ds(pl.multiple_of(kb * QB, QB), QB)

    def score_block(kb, s):
        ki = ki_ref[s, blk(kb), :]
        acc = jnp.zeros((QB, QB), F32)
        for hp in range(IDX_H // 2):
            qi2 = qi_ref[s, 2 * hp:2 * hp + 2].reshape(2 * QB, IDX_D)
            sc = lax.dot_general(ki, qi2, NT_DIMS, preferred_element_type=F32)
            acc = acc + jnp.maximum(sc[:, :QB], 0.0) * wi[s][2 * hp:2 * hp + 1, :]
            acc = acc + jnp.maximum(sc[:, QB:], 0.0) * wi[s][2 * hp + 1:2 * hp + 2, :]
        s_glob = kb * QB + s_loc
        key = _sortable(jnp.where(s_glob <= t_glob, acc, NEG))
        key_ref[s, blk(kb), :] = jnp.where(s_glob >= front, key, INT_MIN)

    def score_group(i, c):
        for u in range(KB_GROUP):
            for s in S:
                score_block(KB_GROUP * i + u, s)
        return c

    def score_single(kb, c):
        for s in S:
            score_block(kb, s)
        return c

    lax.fori_loop(0, ngrp, score_group, 0)
    lax.fori_loop(ngrp * KB_GROUP, nkb, score_single, 0)

    def fold32(hit):
        return jnp.sum(hit.reshape(QB // SUBLANES, SUBLANES, QB), axis=0)

    def count(pred_fn):
        def hit_of(kb, s):
            return pred_fn(key_ref[s, blk(kb), :], kb, s).astype(I32)

        def group(i, accs):
            out = []
            for s in S:
                hit = hit_of(KB_GROUP * i, s)
                for u in range(1, KB_GROUP):
                    hit = hit + hit_of(KB_GROUP * i + u, s)
                out.append(accs[s] + fold32(hit))
            return tuple(out)

        accs = lax.fori_loop(0, ngrp, group, tuple(jnp.zeros((SUBLANES, QB), I32) for _ in S))
        accs = lax.fori_loop(ngrp * KB_GROUP, nkb,
                             lambda kb, a: tuple(a[s] + fold32(hit_of(kb, s)) for s in S), accs)
        return [jnp.sum(a, axis=0, keepdims=True) for a in accs]

    def count_ge(cands):
        cnt = count(lambda kblk, kb, s: kblk >= cands[s])
        return [cnt[s] + jnp.where(_KEY_NEG >= cands[s], n_rest, 0) for s in S]

    c0 = count_ge([jnp.zeros((1, QB), I32) for _ in S])
    base = tuple(jnp.where(c0[s] >= topk, 0, INT_MIN).astype(I32) for s in S)

    def bit_body(i, base):
        cands = [base[s] | jnp.left_shift(jnp.int32(1), 30 - i) for s in S]
        cnt = count_ge(cands)
        return tuple(jnp.where(cnt[s] >= topk, cands[s], base[s]) for s in S)

    tau = lax.fori_loop(0, 31, bit_body, base)

    def valid_of(kb):
        s_glob = kb * QB + s_loc
        return (s_glob >= front) & (s_glob <= t_glob)

    c_gt = count(lambda kblk, kb, s: kblk > tau[s])
    c_eq = count(lambda kblk, kb, s: (kblk == tau[s]) & valid_of(kb))
    need = [topk - (c_gt[s] + jnp.where(_KEY_NEG > tau[s], n_rest, 0)) for s in S]

    for s in S:
        @pl.when(jnp.max((c_eq[s] > need[s]).astype(I32)) > 0)
        def _(s=s):
            lstrict = (s_loc > lax.broadcasted_iota(I32, (QB, QB), 1)).astype(MXU_DT)
            needf = need[s].astype(F32)

            def tie_body(kb, seen):
                kblk = key_ref[s, blk(kb), :]
                tie = (kblk == tau[s]) & valid_of(kb)
                tief = tie.astype(F32)
                rank = seen + jnp.dot(lstrict, tief.astype(MXU_DT), preferred_element_type=F32)
                key_ref[s, blk(kb), :] = jnp.where(tie & (rank >= needf), INT_MIN, kblk)
                return seen + jnp.sum(tief, axis=0, keepdims=True)

            lax.fori_loop(0, nkb, tie_body, jnp.zeros((1, QB), F32))

    def mask_body(kb, c):
        for s in S:
            sel = (key_ref[s, blk(kb), :] >= tau[s]) & valid_of(kb)
            madd_ref[s, blk(kb), :] = jnp.where(sel, 0.0, NEG)
        return c

    lax.fori_loop(0, nkb, mask_body, 0)

    NP = DSA_H // 2
    grp = (QB // SUBLANES, SUBLANES, 2 * QB)

    AS = ATTN_STREAMS if ns % ATTN_STREAMS == 0 else 1

    def attend(g, c):
        ss = [g * AS + u for u in range(AS)]
        U = range(AS)
        acc_ref[...] = jnp.zeros_like(acc_ref)

        def score_inputs(kb):
            out = []
            for u in U:
                ma = madd_ref[ss[u], blk(kb), :]
                out.append((kv_ref[ss[u], blk(kb), :], jnp.concatenate([ma, ma], axis=1)))
            return out, jnp.minimum(j - kb, 2)

        def stage1(p, slot, inputs):
            per, rel = inputs
            out = []
            for u in U:
                kvb, ma2 = per[u]
                q2 = q_ref[ss[u], 2 * p:2 * p + 2].reshape(2 * QB, DSA_KV_LAT)
                lg = lax.dot_general(kvb, q2, NT_DIMS, preferred_element_type=F32) + bias_ref[p, rel] + ma2
                lg_ref[u, slot, p] = lg
                out.append(jnp.max(jnp.max(lg.reshape(grp), axis=0), axis=0, keepdims=True))
            return out

        def stage2(p, slot, kvts, m_old, bm):
            out = []
            for u in U:
                m_new = jnp.maximum(m_old[u], bm[u])
                pm = jnp.exp2(lg_ref[u, slot, p] - m_new)
                acc_ref[u, p] = (jnp.exp2(m_old[u] - m_new) * acc_ref[u, p]
                                 + jnp.dot(kvts[u], pm.astype(MXU_DT), preferred_element_type=F32))
                out.append(m_new)
            return out

        def kvt_of(kb):
            return [kvt_ref[ss[u], kb] for u in U]

        inp0 = score_inputs(0)
        bm0 = tuple(tuple(stage1(p, 0, inp0)) for p in range(NP))

        def attn_body(i, carry):
            ms, bm_a = carry
            a = 2 * i
            inp_b = score_inputs(a + 1)
            inp_c = score_inputs(jnp.minimum(a + 2, nkb - 1))
            kvt_a = kvt_of(a)
            kvt_b = kvt_of(a + 1)
            ms_a, bm_b = [], []
            for p in range(NP):
                bm_b.append(tuple(stage1(p, 1, inp_b)))
                ms_a.append(tuple(stage2(p, 0, kvt_a, ms[p], bm_a[p])))
            ms_b, bm_c = [], []
            for p in range(NP):
                bm_c.append(tuple(stage1(p, 0, inp_c)))
                ms_b.append(tuple(stage2(p, 1, kvt_b, ms_a[p], bm_b[p])))
            return tuple(ms_b), tuple(bm_c)

        ninf = tuple(tuple(jnp.full((1, 2 * QB), -jnp.inf, F32) for _ in U) for _ in range(NP))
        ms, bm_last = lax.fori_loop(0, nkb // 2, attn_body, (ninf, bm0))

        @pl.when(nkb % 2 == 1)
        def _():
            kvts = kvt_of(nkb - 1)
            for p in range(NP):
                stage2(p, 0, kvts, ms[p], bm_last[p])

        for u in U:
            for p in range(NP):
                acc = acc_ref[u, p]
                o_t = acc[:DSA_KV_LAT] / acc[DSA_KV_LAT:DSA_KV_LAT + 1]
                outs = []
                for hh in range(2):
                    o_h = jnp.transpose(o_t[:, hh * QB:(hh + 1) * QB])
                    outs.append(_mm(o_h, wuv_ref[2 * p + hh]))
                o_ref[ss[u], :, 2 * p * DSA_DH:(2 * p + 2) * DSA_DH] = jnp.concatenate(outs, axis=1).astype(o_ref.dtype)
        return c

    lax.fori_loop(0, ns // AS, attend, 0)


def _dsa(ki, kv, kvt, qi, q, zh, bias_tiles, wuv, *, front, topk):
    B, Tp, _ = kv.shape
    nblk = Tp // LANES
    ns = DSA_STREAMS if B % DSA_STREAMS == 0 else 1
    return pl.pallas_call(
        functools.partial(_dsa_kernel, front=front, topk=topk, Tp=Tp, ns=ns),
        grid=(B // ns, nblk),
        in_specs=[pl.BlockSpec((ns, Tp, IDX_D), lambda b, j: (b, 0, 0)),
                  pl.BlockSpec((ns, Tp, DSA_KV_LAT), lambda b, j: (b, 0, 0)),
                  pl.BlockSpec((ns, nblk, DSA_KV_LAT + KVT_ONES, LANES), lambda b, j: (b, 0, 0, 0)),
                  pl.BlockSpec((ns, IDX_H, LANES, IDX_D), lambda b, j: (b, 0, j, 0)),
                  pl.BlockSpec((ns, DSA_H, LANES, DSA_KV_LAT), lambda b, j: (b, 0, j, 0)),
                  pl.BlockSpec((ns, LANES, LANES), lambda b, j: (b, j, Z_SM1 // LANES)),
                  pl.BlockSpec((DSA_H // 2, 3, LANES, 2 * LANES), lambda b, j: (0, 0, 0, 0)),
                  pl.BlockSpec((DSA_H, DSA_KV_LAT, DSA_DH), lambda b, j: (0, 0, 0))],
        out_specs=pl.BlockSpec((ns, LANES, DSA_H * DSA_DH), lambda b, j: (b, j, 0)),
        out_shape=jax.ShapeDtypeStruct((B, Tp, DSA_H * DSA_DH), ACT_DT),
        scratch_shapes=[pltpu.VMEM((ns, Tp, LANES), I32), pltpu.VMEM((ns, Tp, LANES), F32),
                        pltpu.VMEM((ATTN_STREAMS, 2, DSA_H // 2, LANES, 2 * LANES), F32),
                        pltpu.VMEM((ATTN_STREAMS, DSA_H // 2, DSA_KV_LAT + KVT_ONES, 2 * LANES), F32)],
        compiler_params=_params(("parallel", "arbitrary")),
        name="dsa_attn",
    )(ki, kv, kvt, qi, q, zh, bias_tiles, wuv)


def _t5_bucket(rel):
    n = jnp.maximum(rel, 0)
    max_exact = T5_BUCKETS // 2
    nf = jnp.maximum(n, 1).astype(F32)
    large = max_exact + (jnp.log(nf / max_exact) / math.log(T5_MAX_DIST / max_exact)
                         * (T5_BUCKETS - max_exact)).astype(I32)
    large = jnp.minimum(large, T5_BUCKETS - 1)
    return jnp.where(n < max_exact, n, large)


def _t5_tiles(t5_bias):
    s = jnp.arange(LANES, dtype=I32)[:, None]
    t = jnp.arange(LANES, dtype=I32)[None, :]
    buckets = jnp.stack([_t5_bucket(shift + t - s) for shift in (0, LANES, 2 * LANES)], axis=0)
    onehot = (buckets[..., None] == jnp.arange(T5_BUCKETS, dtype=I32)).astype(F32)
    tiles = jnp.einsum("kstb,bh->ksth", onehot, t5_bias.astype(F32), precision=HI)
    tiles = tiles.transpose(3, 0, 1, 2).reshape(DSA_H // 2, 2, 3, LANES, LANES)
    return (tiles.transpose(0, 2, 3, 1, 4).reshape(DSA_H // 2, 3, LANES, 2 * LANES) * LOG2E).astype(F32)


def _pack_in_proj(w_in, b_in):
    splits = np.cumsum([GLA_H * GLA_DK, GLA_H * GLA_DK, GLA_H * GLA_DV, GLA_H * GLA_DV, GLA_LR,
                        DSA_Q_LORA, DSA_KV_LAT, IDX_D, IDX_H,
                        ML_H * ML_DK, ML_H * ML_DK, ML_H * ML_DV, ML_H * ML_DV, ML_H, ML_H,
                        D_MODEL, D_MODEL])[:].tolist()

    def pack(a):
        (gq, gk, gv, gr, ga, cq, ckv, ik, iw, mq, mk, mv, mo, mi, mf, a_, b_, c_) = jnp.split(a, splits, axis=-1)
        pad = lambda n: jnp.zeros(a.shape[:-1] + (n,), a.dtype)
        lo = jnp.concatenate([gq, gk, gv, gr, mq, mk, mv, mo], axis=-1)
        hi = jnp.concatenate([a_, b_, c_, cq, ckv,
                              ga, pad(LANES - GLA_LR),
                              ik, iw, pad(LANES - IDX_D - IDX_H),
                              mi, mf, pad(LANES - 2 * ML_H)], axis=-1)
        return lo, hi

    return pack(w_in), pack(b_in)


def kernel(x, meta_tokens, t5_bias, g_pre_mix, w_in, b_in, w_gla_a2, b_gla_a, g_gla_head, w_br_gla, g_dsa_q, w_dsa_uq, w_idx_q, g_dsa_kv, w_dsa_uv, idx_ln_g, idx_ln_b, w_br_dsa, ml_conv, ml_f_bias, g_ml_head, w_br_ml, w_out, g_post_mix, g_pre_mlp, w_up, w_down, g_post_mlp):
    B, S, D = x.shape
    L = w_in.shape[0]
    T = S + N_META
    front = (-T) % LANES
    Tp = T + front
    R = B * Tp
    nblk = Tp // LANES
    topk = min(TOPK_MAX, (T - N_META) // 4)
    tm_mid = 512 if R % 512 == 0 else LANES
    tm_small = 512 if R % 512 == 0 else LANES

    (w_lo, w_hi), (b_lo, b_hi) = _pack_in_proj(w_in, b_in)
    wa2 = jnp.pad(w_gla_a2, ((0, 0), (0, LANES - GLA_LR), (0, 0)))
    wa2 = wa2.reshape(L, LANES, GLA_H, GLA_DK).transpose(0, 2, 1, 3).astype(MXU_DT)
    fb_row = jnp.pad(ml_f_bias, ((0, 0), (ML_H, LANES - 2 * ML_H)))[:, None, :]
    layers = dict(
        g_pre_mix=g_pre_mix[:, None, :], w_lo=w_lo.astype(MXU_DT), b_lo=b_lo[:, None, :],
        w_hi=w_hi.astype(MXU_DT), b_hi=b_hi[:, None, :],
        wa2=wa2, ba=b_gla_a.reshape(L, GLA_H, 1, GLA_DK), g_gla_head=g_gla_head[:, None, :],
        w_br_gla=w_br_gla.astype(MXU_DT),
        g_dsa_q=g_dsa_q[:, None, :], w_dsa_uq=w_dsa_uq.astype(MXU_DT),
        w_idx_q=w_idx_q.reshape(L, DSA_Q_LORA, IDX_H, IDX_D).transpose(0, 2, 1, 3).astype(MXU_DT),
        g_dsa_kv=g_dsa_kv[:, None, :], w_dsa_uv=w_dsa_uv.astype(MXU_DT),
        idx_ln_g=idx_ln_g[:, None, :], idx_ln_b=idx_ln_b[:, None, :], w_br_dsa=w_br_dsa.astype(MXU_DT),
        ml_conv=ml_conv, fb_row=fb_row, ml_f_bias=ml_f_bias, g_ml_head=g_ml_head[:, None, :],
        w_br_ml=w_br_ml.astype(MXU_DT), w_out=w_out.astype(MXU_DT), g_post_mix=g_post_mix[:, None, :],
        g_pre_mlp=g_pre_mlp[:, None, :], w_up=w_up.astype(MXU_DT), w_down=w_down.astype(MXU_DT),
        g_post_mlp=g_post_mlp[:, None, :],
    )
    bias_tiles = _t5_tiles(t5_bias)
    zero_ff = jnp.zeros((1, D_FF), F32)

    meta = jnp.broadcast_to(meta_tokens.astype(x.dtype)[None], (B, N_META, D))
    h0 = jnp.concatenate([jnp.zeros((B, front, D), x.dtype), meta, x], axis=1).reshape(R, D)

    def layer(h, p):
        zl = _norm_matmul(h, p["g_pre_mix"], p["w_lo"], p["b_lo"], tm=tm_mid, tn=1024, relu2=False,
                          out_dtype=ACT_DT, name="in_proj_lo").reshape(B, Tp, NZL)
        zh = _norm_matmul(h, p["g_pre_mix"], p["w_hi"], p["b_hi"], tm=tm_mid, tn=768, relu2=False,
                          out_dtype=F32, name="in_proj_hi")
        zh3 = zh.reshape(B, Tp, NZH)
        o_gla = _gla(zl, zh3, p["wa2"], p["ba"], p["g_gla_head"], front=front)
        o_ml = _mlstm(zl, zh3, p["ml_conv"], p["fb_row"], p["ml_f_bias"], p["g_ml_head"], front=front)
        q, qi, kv, ki = _dsa_prep(zh3, p["g_dsa_q"], p["g_dsa_kv"], p["idx_ln_g"], p["idx_ln_b"],
                                  p["w_dsa_uq"], p["w_idx_q"])
        kvt = kv.reshape(B, nblk, LANES, DSA_KV_LAT).transpose(0, 1, 3, 2)
        kvt = jnp.concatenate([kvt, jnp.ones((B, nblk, KVT_ONES, LANES), kvt.dtype)], axis=2)
        o_dsa = _dsa(ki, kv, kvt, qi, q, zh3, bias_tiles, p["w_dsa_uv"], front=front, topk=topk)
        h = _mix(o_gla.reshape(R, D), o_dsa.reshape(R, D), o_ml.reshape(R, D), zh,
                 p["w_br_gla"], p["w_br_dsa"], p["w_br_ml"], p["w_out"], p["g_post_mix"], h, tm=tm_small)
        u = _norm_matmul(h, p["g_pre_mlp"], p["w_up"], zero_ff, tm=tm_mid, tn=1024,
                         relu2=True, out_dtype=ACT_DT, name="mlp_up")
        h = _matmul_norm_res(u, p["w_down"], p["g_post_mlp"], h, tm=tm_mid)
        return h, None

    h, _ = lax.scan(layer, h0, layers)
    return h.reshape(B, Tp, D)[:, front + N_META:]
```

```python
import functools
import math

import numpy as np
import jax
import jax.numpy as jnp
from jax import lax
from jax.experimental import pallas as pl
from jax.experimental.pallas import tpu as pltpu

D_MODEL = 1024
N_META = 16
GLA_H, GLA_DK, GLA_DV, GLA_LR, GLA_TAU = 4, 128, 256, 16, 16.0
DSA_H, DSA_DH, DSA_Q_LORA, DSA_KV_LAT = 16, 64, 256, 128
IDX_H, IDX_D, TOPK_MAX = 8, 64, 256
ML_H, ML_DK, ML_DV, ML_CONV = 4, 128, 256, 4
T5_BUCKETS, T5_MAX_DIST = 32, 128
D_FF = 4 * D_MODEL
EPS = 1e-6
NEG = -1e30
LOG2E = math.log2(math.e)

LANES = 128
SUBLANES = 8
VMEM_LIMIT = 56 * 1024 * 1024

MXU_DT = jnp.bfloat16
ACT_DT = jnp.bfloat16

F32 = jnp.float32
I32 = jnp.int32
INT_MIN = -2147483648
HI = lax.Precision.HIGHEST
NT_DIMS = (((1,), (1,)), ((), ()))

Z_GLA_Q, Z_GLA_K, Z_GLA_V, Z_GLA_R = 0, 512, 1024, 2048
Z_ML_Q, Z_ML_K, Z_ML_V, Z_ML_O = 3072, 3584, 4096, 5120
NZL = 6144
Z_GATE_A, Z_GATE_B, Z_GATE_C = 0, 1024, 2048
Z_CQ, Z_CKV = 3072, 3328
Z_SM0, Z_SM1, Z_SM2 = 3456, 3584, 3712
NZH = 3840
KVT_ONES = 16
PREP_ROWS_MAX = 1088
DSA_STREAMS = 4
ATTN_STREAMS = 2
KB_GROUP = 4
GLA_ROWS = 2
MIX_CHUNKS = 6


def _params(sem):
    return pltpu.CompilerParams(dimension_semantics=sem, vmem_limit_bytes=VMEM_LIMIT)


def _mm(a, b):
    return jnp.dot(a.astype(MXU_DT), b.astype(MXU_DT), preferred_element_type=F32)


def _mm_nt(a, b):
    return lax.dot_general(a.astype(MXU_DT), b.astype(MXU_DT), NT_DIMS, preferred_element_type=F32)


def _rms(x, g):
    return x * lax.rsqrt(jnp.mean(x * x, axis=-1, keepdims=True) + EPS) * g


def _log_sigmoid(x):
    return jnp.minimum(x, 0.0) - jnp.log1p(jnp.exp(-jnp.abs(x)))


def _sigmoid(x):
    return 0.5 * jnp.tanh(0.5 * x) + 0.5


def _norm_matmul_kernel(x_ref, g_ref, w_ref, b_ref, o_ref, *, relu2, tn):
    xn = _rms(x_ref[...], g_ref[...]).astype(MXU_DT)
    for n0 in range(0, o_ref.shape[1], tn):
        acc = jnp.dot(xn, w_ref[:, n0:n0 + tn], preferred_element_type=F32) + b_ref[:, n0:n0 + tn]
        if relu2:
            acc = jnp.square(jnp.maximum(acc, 0.0))
        o_ref[:, n0:n0 + tn] = acc.astype(o_ref.dtype)


def _norm_matmul(x, g, w, b, *, tm, tn, relu2, out_dtype, name):
    R, K = x.shape
    N = w.shape[1]
    return pl.pallas_call(
        functools.partial(_norm_matmul_kernel, relu2=relu2, tn=tn),
        grid=(R // tm,),
        in_specs=[pl.BlockSpec((tm, K), lambda i: (i, 0)),
                  pl.BlockSpec((1, K), lambda i: (0, 0)),
                  pl.BlockSpec((K, N), lambda i: (0, 0)),
                  pl.BlockSpec((1, N), lambda i: (0, 0))],
        out_specs=pl.BlockSpec((tm, N), lambda i: (i, 0)),
        out_shape=jax.ShapeDtypeStruct((R, N), out_dtype),
        compiler_params=_params(("parallel",)),
        name=name,
    )(x, g, w, b)


def _matmul_norm_res_kernel(u_ref, w_ref, g_ref, h_ref, o_ref):
    y = jnp.dot(u_ref[...], w_ref[...], preferred_element_type=F32)
    o_ref[...] = h_ref[...] + _rms(y, g_ref[...])


def _matmul_norm_res(u, w, g, h, *, tm):
    R, K = u.shape
    N = w.shape[1]
    return pl.pallas_call(
        _matmul_norm_res_kernel,
        grid=(R // tm,),
        in_specs=[pl.BlockSpec((tm, K), lambda i: (i, 0)),
                  pl.BlockSpec((K, N), lambda i: (0, 0)),
                  pl.BlockSpec((1, N), lambda i: (0, 0)),
                  pl.BlockSpec((tm, N), lambda i: (i, 0))],
        out_specs=pl.BlockSpec((tm, N), lambda i: (i, 0)),
        out_shape=jax.ShapeDtypeStruct((R, N), F32),
        compiler_params=_params(("parallel",)),
        name="mlp_down",
    )(u, w, g, h)


def _mix_kernel(oa_ref, ob_ref, oc_ref, ga_ref, gb_ref, gc_ref, wa_ref, wb_ref, wc_ref,
                wo_ref, g_ref, h_ref, o_ref):
    ya = jnp.dot(oa_ref[...], wa_ref[...], preferred_element_type=F32)
    yb = jnp.dot(ob_ref[...], wb_ref[...], preferred_element_type=F32)
    yc = jnp.dot(oc_ref[...], wc_ref[...], preferred_element_type=F32)
    mix = _sigmoid(ga_ref[...]) * ya + _sigmoid(gb_ref[...]) * yb + _sigmoid(gc_ref[...]) * yc
    y = _mm(mix, wo_ref[...])
    o_ref[...] = h_ref[...] + _rms(y, g_ref[...])


def _mix(oa, ob, oc, z, wa, wb, wc, wo, g, h, *, tm):
    R, D = h.shape
    row = lambda i: (i, 0)
    const = lambda i: (0, 0)
    gate = lambda off: pl.BlockSpec((tm, D), lambda i: (i, off // D))
    return pl.pallas_call(
        _mix_kernel,
        grid=(R // tm,),
        in_specs=[pl.BlockSpec((tm, D), row), pl.BlockSpec((tm, D), row), pl.BlockSpec((tm, D), row),
                  gate(Z_GATE_A), gate(Z_GATE_B), gate(Z_GATE_C),
                  pl.BlockSpec((D, D), const), pl.BlockSpec((D, D), const), pl.BlockSpec((D, D), const),
                  pl.BlockSpec((D, D), const), pl.BlockSpec((1, D), const), pl.BlockSpec((tm, D), row)],
        out_specs=pl.BlockSpec((tm, D), row),
        out_shape=jax.ShapeDtypeStruct((R, D), F32),
        compiler_params=_params(("parallel",)),
        name="mix_out",
    )(oa, ob, oc, z, z, z, wa, wb, wc, wo, g, h)


def _gla_kernel(q_ref, k_ref, v_ref, r_ref, a_ref, wa_ref, ba_ref, gh_ref, o_ref, s_ref, *, front, nchunks, nb):
    C = LANES
    tb = pl.program_id(1)

    @pl.when(tb == 0)
    def _():
        s_ref[...] = jnp.zeros_like(s_ref)

    r0 = lax.broadcasted_iota(I32, (C, C), 0)
    r1 = lax.broadcasted_iota(I32, (C, C), 1)
    causal = r0 >= r1
    tril = causal.astype(F32)
    rowi = lax.broadcasted_iota(I32, (C, 1), 0)
    X = [(b, h) for b in range(nb) for h in range(GLA_H)]
    N = range(len(X))
    kq = [slice(h * GLA_DK, (h + 1) * GLA_DK) for _, h in X]
    kv = [slice(h * GLA_DV, (h + 1) * GLA_DV) for _, h in X]

    def body(c, carry):
        sl = pl.ds(pl.multiple_of(c * C, C), C)
        live = ((tb * MIX_CHUNKS + c) * C + rowi) >= front
        a = [a_ref[b, sl, :].astype(MXU_DT) for b in range(nb)]
        g = [_log_sigmoid(jnp.dot(a[b], wa_ref[h], preferred_element_type=F32) + ba_ref[h]) / GLA_TAU for b, h in X]
        G = [jnp.dot(tril, g[i], precision=HI, preferred_element_type=F32) for i in N]
        q = [q_ref[b, sl, kq[i]].astype(F32) * (GLA_DK ** -0.5) for i, (b, h) in enumerate(X)]
        k = [jnp.where(live, k_ref[b, sl, kq[i]].astype(F32), 0.0) for i, (b, h) in enumerate(X)]
        v = [jnp.where(live, v_ref[b, sl, kv[i]].astype(F32), 0.0).astype(MXU_DT) for i, (b, h) in enumerate(X)]
        S = [s_ref[b, h] for b, h in X]
        g_last = [G[i][C - 1:C, :] for i in N]
        g_mid = [G[i][C // 2 - 1:C // 2, :] for i in N]
        o_inter = [_mm(q[i] * jnp.exp(G[i]), S[i]) for i in N]
        A = [jnp.where(causal, _mm_nt(q[i] * jnp.exp(G[i] - g_mid[i]), k[i] * jnp.exp(g_mid[i] - G[i])), 0.0) for i in N]
        o_intra = [_mm(A[i], v[i]) for i in N]
        kd_t = [jnp.transpose(k[i] * jnp.exp(g_last[i] - G[i])) for i in N]
        dec_col = [jnp.transpose(jnp.broadcast_to(jnp.exp(g_last[i]), (C, GLA_DK))) for i in N]
        for i, (b, h) in enumerate(X):
            s_ref[b, h] = jnp.concatenate([dec_col[i], dec_col[i]], axis=1) * S[i] + _mm(kd_t[i], v[i])
        for i, (b, h) in enumerate(X):
            r = r_ref[b, sl, kv[i]].astype(F32)
            o = _rms(o_inter[i] + o_intra[i], gh_ref[...]) * (r * _sigmoid(r))
            o_ref[b, sl, kv[i]] = o.astype(o_ref.dtype)
        return carry

    lax.fori_loop(0, jnp.minimum(MIX_CHUNKS, nchunks - tb * MIX_CHUNKS), body, 0)


def _gla(zl, zh, wa, ba, gh, *, front):
    B, Tp, _ = zl.shape
    nchunks = Tp // LANES
    TB = min(MIX_CHUNKS * LANES, Tp)
    WK, WV = GLA_H * GLA_DK, GLA_H * GLA_DV
    nb = GLA_ROWS if B % GLA_ROWS == 0 else 1
    return pl.pallas_call(
        functools.partial(_gla_kernel, front=front, nchunks=nchunks, nb=nb),
        grid=(B // nb, pl.cdiv(Tp, TB)),
        in_specs=[pl.BlockSpec((nb, TB, WK), lambda b, t: (b, t, Z_GLA_Q // WK)),
                  pl.BlockSpec((nb, TB, WK), lambda b, t: (b, t, Z_GLA_K // WK)),
                  pl.BlockSpec((nb, TB, WV), lambda b, t: (b, t, Z_GLA_V // WV)),
                  pl.BlockSpec((nb, TB, WV), lambda b, t: (b, t, Z_GLA_R // WV)),
                  pl.BlockSpec((nb, TB, LANES), lambda b, t: (b, t, Z_SM0 // LANES)),
                  pl.BlockSpec((GLA_H, LANES, GLA_DK), lambda b, t: (0, 0, 0)),
                  pl.BlockSpec((GLA_H, 1, GLA_DK), lambda b, t: (0, 0, 0)),
                  pl.BlockSpec((1, GLA_DV), lambda b, t: (0, 0))],
        out_specs=pl.BlockSpec((nb, TB, WV), lambda b, t: (b, t, 0)),
        out_shape=jax.ShapeDtypeStruct((B, Tp, WV), ACT_DT),
        scratch_shapes=[pltpu.VMEM((nb, GLA_H, GLA_DK, GLA_DV), F32)],
        compiler_params=_params(("parallel", "arbitrary")),
        name="gla",
    )(zl, zl, zl, zl, zh, wa, ba, gh)


def _mlstm_kernel(q_ref, k_ref, v_ref, og_ref, gcol_ref, cw_ref, fbrow_ref, fb_ref, gh_ref, o_ref,
                  s_ref, m_ref, pq_ref, pk_ref, *, front, nchunks):
    C = LANES
    WK = ML_H * ML_DK
    tb = pl.program_id(1)

    @pl.when(tb == 0)
    def _():
        s_ref[...] = jnp.zeros_like(s_ref)
        m_ref[...] = jnp.zeros_like(m_ref)
        pq_ref[...] = jnp.zeros_like(pq_ref)
        pk_ref[...] = jnp.zeros_like(pk_ref)

    r0 = lax.broadcasted_iota(I32, (C, C), 0)
    r1 = lax.broadcasted_iota(I32, (C, C), 1)
    causal = r0 >= r1
    tril = causal.astype(F32)
    triu = (r0 <= r1).astype(F32)
    rowi = lax.broadcasted_iota(I32, (C, 1), 0)

    def conv_silu(x, tail, w):
        acc = x * w[ML_CONV - 1:ML_CONV, :]
        for d in range(1, ML_CONV):
            sh = pltpu.roll(x, d, 0)
            head = jnp.where(rowi[:SUBLANES] >= d, sh[:SUBLANES], pltpu.roll(tail, d, 0))
            sh = jnp.concatenate([head, sh[SUBLANES:]], axis=0)
            acc = acc + sh * w[ML_CONV - 1 - d:ML_CONV - d, :]
        return acc * _sigmoid(acc)

    def body(c, carry):
        sl = pl.ds(pl.multiple_of(c * C, C), C)
        cg = tb * MIX_CHUNKS + c
        live = (cg * C + rowi) >= front
        xq = jnp.where(live, q_ref[0, sl, :].astype(F32), 0.0)
        xk = jnp.where(live, k_ref[0, sl, :].astype(F32), 0.0)
        q_all = conv_silu(xq, pq_ref[...], cw_ref[:, :WK]) * (ML_DK ** -0.5)
        k_all = conv_silu(xk, pk_ref[...], cw_ref[:, WK:])
        pq_ref[...] = xq[C - SUBLANES:, :]
        pk_ref[...] = xk[C - SUBLANES:, :]
        lf_col = _log_sigmoid(gcol_ref[0, sl, :] + fbrow_ref[...])
        H = range(ML_H)
        vs = [slice(h * ML_DV, (h + 1) * ML_DV) for h in H]
        q = [q_all[:, h * ML_DK:(h + 1) * ML_DK].astype(MXU_DT) for h in H]
        k = [k_all[:, h * ML_DK:(h + 1) * ML_DK] for h in H]
        ones = jnp.ones((C, LANES), MXU_DT)
        v_aug = [jnp.concatenate([jnp.where(live, v_ref[0, sl, vs[h]].astype(F32), 0.0).astype(MXU_DT), ones], axis=1)
                 for h in H]
        m_prev = [m_ref[h, 0:1, :] for h in H]
        S = [s_ref[h] for h in H]
        sel = [jnp.dot(lf_col, (r0 == ML_H + h).astype(F32), precision=HI, preferred_element_type=F32) for h in H]
        b_bc = [jnp.dot(tril, sel[h], precision=HI, preferred_element_type=F32) for h in H]
        g_t = jnp.transpose(gcol_ref[0, sl, :])
        li_row = [g_t[h:h + 1, :] for h in H]
        lf_row = [_log_sigmoid(g_t[ML_H + h:ML_H + h + 1, :] + fb_ref[h]) for h in H]
        b_row = [jnp.dot(lf_row[h], triu, precision=HI, preferred_element_type=F32) for h in H]
        b_last = [b_row[h][:, C - 1:C] for h in H]
        qk = [_mm_nt(q[h], k[h]) for h in H]
        qs = [_mm(q[h], S[h]) for h in H]
        Dm = [jnp.where(causal, b_bc[h] - (b_row[h] - li_row[h]), -jnp.inf) for h in H]
        m_inter = [b_bc[h] + m_prev[h] for h in H]
        m = [jnp.maximum(m_inter[h], jnp.max(Dm[h], axis=1, keepdims=True)) for h in H]
        pv = [_mm(jnp.exp(Dm[h] - m[h]) * qk[h], v_aug[h]) for h in H]
        dec = [b_last[h] - b_row[h] + li_row[h] for h in H]
        m_new = [jnp.maximum(b_last[h] + m_prev[h], jnp.max(dec[h], axis=1, keepdims=True)) for h in H]
        kt_w = [jnp.transpose(k[h]) * jnp.exp(dec[h] - m_new[h]) for h in H]
        for h in H:
            w_old = jnp.exp(b_last[h] + m_prev[h] - m_new[h])
            s_ref[h] = jnp.concatenate([w_old, w_old, w_old], axis=1) * S[h] + _mm(kt_w[h], v_aug[h])
            m_ref[h] = jnp.broadcast_to(m_new[h], (SUBLANES, C))
        for h in H:
            w_inter = jnp.exp(m_inter[h] - m[h])
            num = jnp.concatenate([w_inter, w_inter, w_inter], axis=1) * qs[h] + pv[h]
            den = num[:, ML_DV:]
            dd = jnp.maximum(jnp.abs(den), jnp.exp(-m[h]))
            hval = num[:, :ML_DV] / jnp.concatenate([dd, dd], axis=1)
            og = og_ref[0, sl, vs[h]].astype(F32)
            o_ref[0, sl, vs[h]] = (_rms(hval, gh_ref[...]) * _sigmoid(og)).astype(o_ref.dtype)
        return carry

    lax.fori_loop(0, jnp.minimum(MIX_CHUNKS, nchunks - tb * MIX_CHUNKS), body, 0)


def _mlstm(zl, zh, conv_w, fb_row, fb, gh, *, front):
    B, Tp, _ = zl.shape
    nchunks = Tp // LANES
    TB = min(MIX_CHUNKS * LANES, Tp)
    WK, WV = ML_H * ML_DK, ML_H * ML_DV
    return pl.pallas_call(
        functools.partial(_mlstm_kernel, front=front, nchunks=nchunks),
        grid=(B, pl.cdiv(Tp, TB)),
        in_specs=[pl.BlockSpec((1, TB, WK), lambda b, t: (b, t, Z_ML_Q // WK)),
                  pl.BlockSpec((1, TB, WK), lambda b, t: (b, t, Z_ML_K // WK)),
                  pl.BlockSpec((1, TB, WV), lambda b, t: (b, t, Z_ML_V // WV)),
                  pl.BlockSpec((1, TB, WV), lambda b, t: (b, t, Z_ML_O // WV)),
                  pl.BlockSpec((1, TB, LANES), lambda b, t: (b, t, Z_SM2 // LANES)),
                  pl.BlockSpec((ML_CONV, 2 * WK), lambda b, t: (0, 0)),
                  pl.BlockSpec((1, LANES), lambda b, t: (0, 0)),
                  pl.BlockSpec(memory_space=pltpu.SMEM),
                  pl.BlockSpec((1, ML_DV), lambda b, t: (0, 0))],
        out_specs=pl.BlockSpec((1, TB, WV), lambda b, t: (b, t, 0)),
        out_shape=jax.ShapeDtypeStruct((B, Tp, WV), ACT_DT),
        scratch_shapes=[pltpu.VMEM((ML_H, ML_DK, ML_DV + LANES), F32), pltpu.VMEM((ML_H, SUBLANES, LANES), F32),
                        pltpu.VMEM((SUBLANES, WK), F32), pltpu.VMEM((SUBLANES, WK), F32)],
        compiler_params=_params(("parallel", "arbitrary")),
        name="mlstm",
    )(zl, zl, zl, zl, zh, conv_w, fb_row, fb, gh)


def _dsa_prep_kernel(cq_ref, ckv_ref, sm_ref, gq_ref, gkv_ref, lng_ref, lnb_ref, wuq_ref, wiq_ref,
                     q_ref, qi_ref, kv_ref, ki_ref):
    cq = _rms(cq_ref[0], gq_ref[...]).astype(MXU_DT)
    for hp in range(DSA_H // 2):
        w2 = wuq_ref[:, 2 * hp * DSA_KV_LAT:(2 * hp + 2) * DSA_KV_LAT]
        qf = jnp.dot(cq, w2, preferred_element_type=F32) * (DSA_KV_LAT ** -0.5 * LOG2E)
        q_ref[0, 2 * hp] = qf[:, :DSA_KV_LAT].astype(q_ref.dtype)
        q_ref[0, 2 * hp + 1] = qf[:, DSA_KV_LAT:].astype(q_ref.dtype)
    for hh in range(IDX_H):
        qi = jnp.dot(cq, wiq_ref[hh], preferred_element_type=F32) * (IDX_D ** -0.5)
        qi_ref[0, hh] = qi.astype(qi_ref.dtype)
    kv_ref[0] = _rms(ckv_ref[0], gkv_ref[...]).astype(kv_ref.dtype)
    x = sm_ref[0][:, :IDX_D]
    mu = jnp.mean(x, axis=-1, keepdims=True)
    var = jnp.mean(jnp.square(x - mu), axis=-1, keepdims=True)
    ki_ref[0] = ((x - mu) * lax.rsqrt(var + EPS) * lng_ref[...] + lnb_ref[...]).astype(ki_ref.dtype)


def _dsa_prep(z, gq, gkv, lng, lnb, wuq, wiq):
    B, Tp, _ = z.shape
    tm = max(t for t in range(2 * SUBLANES, PREP_ROWS_MAX + 1, 2 * SUBLANES) if Tp % t == 0)
    c2 = lambda b, i: (0, 0)
    return pl.pallas_call(
        _dsa_prep_kernel,
        grid=(B, Tp // tm),
        in_specs=[pl.BlockSpec((1, tm, DSA_Q_LORA), lambda b, i: (b, i, Z_CQ // DSA_Q_LORA)),
                  pl.BlockSpec((1, tm, DSA_KV_LAT), lambda b, i: (b, i, Z_CKV // DSA_KV_LAT)),
                  pl.BlockSpec((1, tm, LANES), lambda b, i: (b, i, Z_SM1 // LANES)),
                  pl.BlockSpec((1, DSA_Q_LORA), c2), pl.BlockSpec((1, DSA_KV_LAT), c2),
                  pl.BlockSpec((1, IDX_D), c2), pl.BlockSpec((1, IDX_D), c2),
                  pl.BlockSpec((DSA_Q_LORA, DSA_H * DSA_KV_LAT), c2),
                  pl.BlockSpec((IDX_H, DSA_Q_LORA, IDX_D), lambda b, i: (0, 0, 0))],
        out_specs=[pl.BlockSpec((1, DSA_H, tm, DSA_KV_LAT), lambda b, i: (b, 0, i, 0)),
                   pl.BlockSpec((1, IDX_H, tm, IDX_D), lambda b, i: (b, 0, i, 0)),
                   pl.BlockSpec((1, tm, DSA_KV_LAT), lambda b, i: (b, i, 0)),
                   pl.BlockSpec((1, tm, IDX_D), lambda b, i: (b, i, 0))],
        out_shape=[jax.ShapeDtypeStruct((B, DSA_H, Tp, DSA_KV_LAT), ACT_DT),
                   jax.ShapeDtypeStruct((B, IDX_H, Tp, IDX_D), ACT_DT),
                   jax.ShapeDtypeStruct((B, Tp, DSA_KV_LAT), ACT_DT),
                   jax.ShapeDtypeStruct((B, Tp, IDX_D), ACT_DT)],
        compiler_params=_params(("parallel", "parallel")),
        name="dsa_prep",
    )(z, z, z, gq, gkv, lng, lnb, wuq, wiq)


def _sortable(x):
    bits = pltpu.bitcast(x, I32)
    key = jnp.where(bits < 0, bits ^ 0x7FFFFFFF, bits)
    return jnp.where(x == 0.0, 0, key)


_KEY_NEG = int(np.array(NEG, np.float32).view(np.int32)) ^ 0x7FFFFFFF


def _dsa_kernel(ki_ref, kv_ref, kvt_ref, qi_ref, q_ref, wi_ref, bias_ref, wuv_ref, o_ref,
                key_ref, madd_ref, lg_ref, acc_ref, *, front, topk, Tp, ns):
    QB = LANES
    S = range(ns)
    j = pl.program_id(1)
    nkb = j + 1
    ngrp = nkb // KB_GROUP
    s_loc = lax.broadcasted_iota(I32, (QB, QB), 0)
    t_glob = j * QB + lax.broadcasted_iota(I32, (QB, QB), 1)
    wi = [wi_ref[s] * (IDX_H ** -0.5) for s in S]
    n_rest = Tp - nkb * QB

    def blk(kb):
        return pl.ds(pl.multiple_of(kb * QB, QB), QB)

    def score_block(kb, s):
        ki = ki_ref[s, blk(kb), :]
        acc = jnp.zeros((QB, QB), F32)
        for hp in range(IDX_H // 2):
            qi2 = qi_ref[s, 2 * hp:2 * hp + 2].reshape(2 * QB, IDX_D)
            sc = lax.dot_general(ki, qi2, NT_DIMS, preferred_element_type=F32)
            acc = acc + jnp.maximum(sc[:, :QB], 0.0) * wi[s][2 * hp:2 * hp + 1, :]
            acc = acc + jnp.maximum(sc[:, QB:], 0.0) * wi[s][2 * hp + 1:2 * hp + 2, :]
        s_glob = kb * QB + s_loc
        key = _sortable(jnp.where(s_glob <= t_glob, acc, NEG))
        key_ref[s, blk(kb), :] = jnp.where(s_glob >= front, key, INT_MIN)

    def score_group(i, c):
        for u in range(KB_GROUP):
            for s in S:
                score_block(KB_GROUP * i + u, s)
        return c

    def score_single(kb, c):
        for s in S:
            score_block(kb, s)
        return c

    lax.fori_loop(0, ngrp, score_group, 0)
    lax.fori_loop(ngrp * KB_GROUP, nkb, score_single, 0)

    def fold32(hit):
        return jnp.sum(hit.reshape(QB // SUBLANES, SUBLANES, QB), axis=0)

    def count(pred_fn):
        def hit_of(kb, s):
            return pred_fn(key_ref[s, blk(kb), :], kb, s).astype(I32)

        def group(i, accs):
            out = []
            for s in S:
                hit = hit_of(KB_GROUP * i, s)
                for u in range(1, KB_GROUP):
                    hit = hit + hit_of(KB_GROUP * i + u, s)
                out.append(accs[s] + fold32(hit))
            return tuple(out)

        accs = lax.fori_loop(0, ngrp, group, tuple(jnp.zeros((SUBLANES, QB), I32) for _ in S))
        accs = lax.fori_loop(ngrp * KB_GROUP, nkb,
                             lambda kb, a: tuple(a[s] + fold32(hit_of(kb, s)) for s in S), accs)
        return [jnp.sum(a, axis=0, keepdims=True) for a in accs]

    def count_ge(cands):
        cnt = count(lambda kblk, kb, s: kblk >= cands[s])
        return [cnt[s] + jnp.where(_KEY_NEG >= cands[s], n_rest, 0) for s in S]

    c0 = count_ge([jnp.zeros((1, QB), I32) for _ in S])
    base = tuple(jnp.where(c0[s] >= topk, 0, INT_MIN).astype(I32) for s in S)

    def bit_body(i, base):
        cands = [base[s] | jnp.left_shift(jnp.int32(1), 30 - i) for s in S]
        cnt = count_ge(cands)
        return tuple(jnp.where(cnt[s] >= topk, cands[s], base[s]) for s in S)

    tau = lax.fori_loop(0, 31, bit_body, base)

    def valid_of(kb):
        s_glob = kb * QB + s_loc
        return (s_glob >= front) & (s_glob <= t_glob)

    c_gt = count(lambda kblk, kb, s: kblk > tau[s])
    c_eq = count(lambda kblk, kb, s: (kblk == tau[s]) & valid_of(kb))
    need = [topk - (c_gt[s] + jnp.where(_KEY_NEG > tau[s], n_rest, 0)) for s in S]

    for s in S:
        @pl.when(jnp.max((c_eq[s] > need[s]).astype(I32)) > 0)
        def _(s=s):
            lstrict = (s_loc > lax.broadcasted_iota(I32, (QB, QB), 1)).astype(MXU_DT)
            needf = need[s].astype(F32)

            def tie_body(kb, seen):
                kblk = key_ref[s, blk(kb), :]
                tie = (kblk == tau[s]) & valid_of(kb)
                tief = tie.astype(F32)
                rank = seen + jnp.dot(lstrict, tief.astype(MXU_DT), preferred_element_type=F32)
                key_ref[s, blk(kb), :] = jnp.where(tie & (rank >= needf), INT_MIN, kblk)
                return seen + jnp.sum(tief, axis=0, keepdims=True)

            lax.fori_loop(0, nkb, tie_body, jnp.zeros((1, QB), F32))

    def mask_body(kb, c):
        for s in S:
            sel = (key_ref[s, blk(kb), :] >= tau[s]) & valid_of(kb)
            madd_ref[s, blk(kb), :] = jnp.where(sel, 0.0, NEG)
        return c

    lax.fori_loop(0, nkb, mask_body, 0)

    NP = DSA_H // 2
    grp = (QB // SUBLANES, SUBLANES, 2 * QB)

    AS = ATTN_STREAMS if ns % ATTN_STREAMS == 0 else 1

    def attend(g, c):
        ss = [g * AS + u for u in range(AS)]
        U = range(AS)
        acc_ref[...] = jnp.zeros_like(acc_ref)

        def score_inputs(kb):
            out = []
            for u in U:
                ma = madd_ref[ss[u], blk(kb), :]
                out.append((kv_ref[ss[u], blk(kb), :], jnp.concatenate([ma, ma], axis=1)))
            return out, jnp.minimum(j - kb, 2)

        def stage1(p, slot, inputs):
            per, rel = inputs
            out = []
            for u in U:
                kvb, ma2 = per[u]
                q2 = q_ref[ss[u], 2 * p:2 * p + 2].reshape(2 * QB, DSA_KV_LAT)
                lg = lax.dot_general(kvb, q2, NT_DIMS, preferred_element_type=F32) + bias_ref[p, rel] + ma2
                lg_ref[u, slot, p] = lg
                out.append(jnp.max(jnp.max(lg.reshape(grp), axis=0), axis=0, keepdims=True))
            return out

        def stage2(p, slot, kvts, m_old, bm):
            out = []
            for u in U:
                m_new = jnp.maximum(m_old[u], bm[u])
                pm = jnp.exp2(lg_ref[u, slot, p] - m_new)
                acc_ref[u, p] = (jnp.exp2(m_old[u] - m_new) * acc_ref[u, p]
                                 + jnp.dot(kvts[u], pm.astype(MXU_DT), preferred_element_type=F32))
                out.append(m_new)
            return out

        def kvt_of(kb):
            return [kvt_ref[ss[u], kb] for u in U]

        inp0 = score_inputs(0)
        bm0 = tuple(tuple(stage1(p, 0, inp0)) for p in range(NP))

        def attn_body(i, carry):
            ms, bm_a = carry
            a = 2 * i
            inp_b = score_inputs(a + 1)
            inp_c = score_inputs(jnp.minimum(a + 2, nkb - 1))
            kvt_a = kvt_of(a)
            kvt_b = kvt_of(a + 1)
            ms_a, bm_b = [], []
            for p in range(NP):
                bm_b.append(tuple(stage1(p, 1, inp_b)))
                ms_a.append(tuple(stage2(p, 0, kvt_a, ms[p], bm_a[p])))
            ms_b, bm_c = [], []
            for p in range(NP):
                bm_c.append(tuple(stage1(p, 0, inp_c)))
                ms_b.append(tuple(stage2(p, 1, kvt_b, ms_a[p], bm_b[p])))
            return tuple(ms_b), tuple(bm_c)

        ninf = tuple(tuple(jnp.full((1, 2 * QB), -jnp.inf, F32) for _ in U) for _ in range(NP))
        ms, bm_last = lax.fori_loop(0, nkb // 2, attn_body, (ninf, bm0))

        @pl.when(nkb % 2 == 1)
        def _():
            kvts = kvt_of(nkb - 1)
            for p in range(NP):
                stage2(p, 0, kvts, ms[p], bm_last[p])

        for u in U:
            for p in range(NP):
                acc = acc_ref[u, p]
                o_t = acc[:DSA_KV_LAT] / acc[DSA_KV_LAT:DSA_KV_LAT + 1]
                outs = []
                for hh in range(2):
                    o_h = jnp.transpose(o_t[:, hh * QB:(hh + 1) * QB])
                    outs.append(_mm(o_h, wuv_ref[2 * p + hh]))
                o_ref[ss[u], :, 2 * p * DSA_DH:(2 * p + 2) * DSA_DH] = jnp.concatenate(outs, axis=1).astype(o_ref.dtype)
        return c

    lax.fori_loop(0, ns // AS, attend, 0)


def _dsa(ki, kv, kvt, qi, q, wi_t, bias_tiles, wuv, *, front, topk):
    B, Tp, _ = kv.shape
    nblk = Tp // LANES
    ns = DSA_STREAMS if B % DSA_STREAMS == 0 else 1
    return pl.pallas_call(
        functools.partial(_dsa_kernel, front=front, topk=topk, Tp=Tp, ns=ns),
        grid=(B // ns, nblk),
        in_specs=[pl.BlockSpec((ns, Tp, IDX_D), lambda b, j: (b, 0, 0)),
                  pl.BlockSpec((ns, Tp, DSA_KV_LAT), lambda b, j: (b, 0, 0)),
                  pl.BlockSpec((ns, nblk, DSA_KV_LAT + KVT_ONES, LANES), lambda b, j: (b, 0, 0, 0)),
                  pl.BlockSpec((ns, IDX_H, LANES, IDX_D), lambda b, j: (b, 0, j, 0)),
                  pl.BlockSpec((ns, DSA_H, LANES, DSA_KV_LAT), lambda b, j: (b, 0, j, 0)),
                  pl.BlockSpec((ns, IDX_H, LANES), lambda b, j: (b, 0, j)),
                  pl.BlockSpec((DSA_H // 2, 3, LANES, 2 * LANES), lambda b, j: (0, 0, 0, 0)),
                  pl.BlockSpec((DSA_H, DSA_KV_LAT, DSA_DH), lambda b, j: (0, 0, 0))],
        out_specs=pl.BlockSpec((ns, LANES, DSA_H * DSA_DH), lambda b, j: (b, j, 0)),
        out_shape=jax.ShapeDtypeStruct((B, Tp, DSA_H * DSA_DH), ACT_DT),
        scratch_shapes=[pltpu.VMEM((ns, Tp, LANES), I32), pltpu.VMEM((ns, Tp, LANES), F32),
                        pltpu.VMEM((ATTN_STREAMS, 2, DSA_H // 2, LANES, 2 * LANES), F32),
                        pltpu.VMEM((ATTN_STREAMS, DSA_H // 2, DSA_KV_LAT + KVT_ONES, 2 * LANES), F32)],
        compiler_params=_params(("parallel", "arbitrary")),
        name="dsa_attn",
    )(ki, kv, kvt, qi, q, wi_t, bias_tiles, wuv)


def _t5_bucket(rel):
    n = jnp.maximum(rel, 0)
    max_exact = T5_BUCKETS // 2
    nf = jnp.maximum(n, 1).astype(F32)
    large = max_exact + (jnp.log(nf / max_exact) / math.log(T5_MAX_DIST / max_exact)
                         * (T5_BUCKETS - max_exact)).astype(I32)
    large = jnp.minimum(large, T5_BUCKETS - 1)
    return jnp.where(n < max_exact, n, large)


def _t5_tiles(t5_bias):
    s = jnp.arange(LANES, dtype=I32)[:, None]
    t = jnp.arange(LANES, dtype=I32)[None, :]
    buckets = jnp.stack([_t5_bucket(shift + t - s) for shift in (0, LANES, 2 * LANES)], axis=0)
    onehot = (buckets[..., None] == jnp.arange(T5_BUCKETS, dtype=I32)).astype(F32)
    tiles = jnp.einsum("kstb,bh->ksth", onehot, t5_bias.astype(F32), precision=HI)
    tiles = tiles.transpose(3, 0, 1, 2).reshape(DSA_H // 2, 2, 3, LANES, LANES)
    return (tiles.transpose(0, 2, 3, 1, 4).reshape(DSA_H // 2, 3, LANES, 2 * LANES) * LOG2E).astype(F32)


def _pack_in_proj(w_in, b_in):
    splits = np.cumsum([GLA_H * GLA_DK, GLA_H * GLA_DK, GLA_H * GLA_DV, GLA_H * GLA_DV, GLA_LR,
                        DSA_Q_LORA, DSA_KV_LAT, IDX_D, IDX_H,
                        ML_H * ML_DK, ML_H * ML_DK, ML_H * ML_DV, ML_H * ML_DV, ML_H, ML_H,
                        D_MODEL, D_MODEL])[:].tolist()

    def pack(a):
        (gq, gk, gv, gr, ga, cq, ckv, ik, iw, mq, mk, mv, mo, mi, mf, a_, b_, c_) = jnp.split(a, splits, axis=-1)
        pad = lambda n: jnp.zeros(a.shape[:-1] + (n,), a.dtype)
        lo = jnp.concatenate([gq, gk, gv, gr, mq, mk, mv, mo], axis=-1)
        hi = jnp.concatenate([a_, b_, c_, cq, ckv,
                              ga, pad(LANES - GLA_LR),
                              ik, iw, pad(LANES - IDX_D - IDX_H),
                              mi, mf, pad(LANES - 2 * ML_H)], axis=-1)
        return lo, hi

    return pack(w_in), pack(b_in)


def kernel(x, meta_tokens, t5_bias, g_pre_mix, w_in, b_in, w_gla_a2, b_gla_a, g_gla_head, w_br_gla, g_dsa_q, w_dsa_uq, w_idx_q, g_dsa_kv, w_dsa_uv, idx_ln_g, idx_ln_b, w_br_dsa, ml_conv, ml_f_bias, g_ml_head, w_br_ml, w_out, g_post_mix, g_pre_mlp, w_up, w_down, g_post_mlp):
    B, S, D = x.shape
    L = w_in.shape[0]
    T = S + N_META
    front = (-T) % LANES
    Tp = T + front
    R = B * Tp
    nblk = Tp // LANES
    topk = min(TOPK_MAX, (T - N_META) // 4)
    tm_mid = 512 if R % 512 == 0 else LANES
    tm_small = 512 if R % 512 == 0 else LANES

    (w_lo, w_hi), (b_lo, b_hi) = _pack_in_proj(w_in, b_in)
    wa2 = jnp.pad(w_gla_a2, ((0, 0), (0, LANES - GLA_LR), (0, 0)))
    wa2 = wa2.reshape(L, LANES, GLA_H, GLA_DK).transpose(0, 2, 1, 3).astype(MXU_DT)
    fb_row = jnp.pad(ml_f_bias, ((0, 0), (ML_H, LANES - 2 * ML_H)))[:, None, :]
    layers = dict(
        g_pre_mix=g_pre_mix[:, None, :], w_lo=w_lo.astype(MXU_DT), b_lo=b_lo[:, None, :],
        w_hi=w_hi.astype(MXU_DT), b_hi=b_hi[:, None, :],
        wa2=wa2, ba=b_gla_a.reshape(L, GLA_H, 1, GLA_DK), g_gla_head=g_gla_head[:, None, :],
        w_br_gla=w_br_gla.astype(MXU_DT),
        g_dsa_q=g_dsa_q[:, None, :], w_dsa_uq=w_dsa_uq.astype(MXU_DT),
        w_idx_q=w_idx_q.reshape(L, DSA_Q_LORA, IDX_H, IDX_D).transpose(0, 2, 1, 3).astype(MXU_DT),
        g_dsa_kv=g_dsa_kv[:, None, :], w_dsa_uv=w_dsa_uv.astype(MXU_DT),
        idx_ln_g=idx_ln_g[:, None, :], idx_ln_b=idx_ln_b[:, None, :], w_br_dsa=w_br_dsa.astype(MXU_DT),
        ml_conv=ml_conv, fb_row=fb_row, ml_f_bias=ml_f_bias, g_ml_head=g_ml_head[:, None, :],
        w_br_ml=w_br_ml.astype(MXU_DT), w_out=w_out.astype(MXU_DT), g_post_mix=g_post_mix[:, None, :],
        g_pre_mlp=g_pre_mlp[:, None, :], w_up=w_up.astype(MXU_DT), w_down=w_down.astype(MXU_DT),
        g_post_mlp=g_post_mlp[:, None, :],
    )
    bias_tiles = _t5_tiles(t5_bias)
    zero_ff = jnp.zeros((1, D_FF), F32)

    meta = jnp.broadcast_to(meta_tokens.astype(x.dtype)[None], (B, N_META, D))
    h0 = jnp.concatenate([jnp.zeros((B, front, D), x.dtype), meta, x], axis=1).reshape(R, D)

    def layer(h, p):
        zl = _norm_matmul(h, p["g_pre_mix"], p["w_lo"], p["b_lo"], tm=tm_mid, tn=1024, relu2=False,
                          out_dtype=ACT_DT, name="in_proj_lo").reshape(B, Tp, NZL)
        zh = _norm_matmul(h, p["g_pre_mix"], p["w_hi"], p["b_hi"], tm=tm_mid, tn=768, relu2=False,
                          out_dtype=F32, name="in_proj_hi")
        zh3 = zh.reshape(B, Tp, NZH)
        o_gla = _gla(zl, zh3, p["wa2"], p["ba"], p["g_gla_head"], front=front)
        o_ml = _mlstm(zl, zh3, p["ml_conv"], p["fb_row"], p["ml_f_bias"], p["g_ml_head"], front=front)
        q, qi, kv, ki = _dsa_prep(zh3, p["g_dsa_q"], p["g_dsa_kv"], p["idx_ln_g"], p["idx_ln_b"],
                                  p["w_dsa_uq"], p["w_idx_q"])
        kvt = kv.reshape(B, nblk, LANES, DSA_KV_LAT).transpose(0, 1, 3, 2)
        kvt = jnp.concatenate([kvt, jnp.ones((B, nblk, KVT_ONES, LANES), kvt.dtype)], axis=2)
        wi_t = zh3[:, :, Z_SM1 + IDX_D:Z_SM1 + IDX_D + IDX_H].transpose(0, 2, 1)
        o_dsa = _dsa(ki, kv, kvt, qi, q, wi_t, bias_tiles, p["w_dsa_uv"], front=front, topk=topk)
        h = _mix(o_gla.reshape(R, D), o_dsa.reshape(R, D), o_ml.reshape(R, D), zh,
                 p["w_br_gla"], p["w_br_dsa"], p["w_br_ml"], p["w_out"], p["g_post_mix"], h, tm=tm_small)
        u = _norm_matmul(h, p["g_pre_mlp"], p["w_up"], zero_ff, tm=tm_mid, tn=1024,
                         relu2=True, out_dtype=ACT_DT, name="mlp_up")
        h = _matmul_norm_res(u, p["w_down"], p["g_post_mlp"], h, tm=tm_mid)
        return h, None

    h, _ = lax.scan(layer, h0, layers)
    return h.reshape(B, Tp, D)[:, front + N_META:]
```

```python
import functools
import math

import numpy as np
import jax
import jax.numpy as jnp
from jax import lax
from jax.experimental import pallas as pl
from jax.experimental.pallas import tpu as pltpu

D_MODEL = 1024
N_META = 16
GLA_H, GLA_DK, GLA_DV, GLA_LR, GLA_TAU = 4, 128, 256, 16, 16.0
DSA_H, DSA_DH, DSA_Q_LORA, DSA_KV_LAT = 16, 64, 256, 128
IDX_H, IDX_D, TOPK_MAX = 8, 64, 256
ML_H, ML_DK, ML_DV, ML_CONV = 4, 128, 256, 4
T5_BUCKETS, T5_MAX_DIST = 32, 128
D_FF = 4 * D_MODEL
EPS = 1e-6
NEG = -1e30
LOG2E = math.log2(math.e)

LANES = 128
SUBLANES = 8
VMEM_LIMIT = 56 * 1024 * 1024

MXU_DT = jnp.bfloat16
ACT_DT = jnp.bfloat16

F32 = jnp.float32
I32 = jnp.int32
INT_MIN = -2147483648
HI = lax.Precision.HIGHEST
NT_DIMS = (((1,), (1,)), ((), ()))

Z_GLA_Q, Z_GLA_K, Z_GLA_V, Z_GLA_R = 0, 512, 1024, 2048
Z_ML_Q, Z_ML_K, Z_ML_V, Z_ML_O = 3072, 3584, 4096, 5120
NZL = 6144
Z_GATE_A, Z_GATE_B, Z_GATE_C = 0, 1024, 2048
Z_CQ, Z_CKV = 3072, 3328
Z_SM0, Z_SM1, Z_SM2 = 3456, 3584, 3712
NZH = 3840
KVT_ONES = 16
PREP_ROWS_MAX = 1088
DSA_STREAMS = 4
ATTN_STREAMS = 2
KB_GROUP = 4
GLA_ROWS = 2
MIX_CHUNKS = 6


def _params(sem):
    return pltpu.CompilerParams(dimension_semantics=sem, vmem_limit_bytes=VMEM_LIMIT)


def _mm(a, b):
    return jnp.dot(a.astype(MXU_DT), b.astype(MXU_DT), preferred_element_type=F32)


def _mm_nt(a, b):
    return lax.dot_general(a.astype(MXU_DT), b.astype(MXU_DT), NT_DIMS, preferred_element_type=F32)


def _rms(x, g):
    return x * lax.rsqrt(jnp.mean(x * x, axis=-1, keepdims=True) + EPS) * g


def _log_sigmoid(x):
    return jnp.minimum(x, 0.0) - jnp.log1p(jnp.exp(-jnp.abs(x)))


def _sigmoid(x):
    return 0.5 * jnp.tanh(0.5 * x) + 0.5


def _norm_matmul_kernel(x_ref, g_ref, w_ref, b_ref, o_ref, *, relu2, tn):
    xn = _rms(x_ref[...], g_ref[...]).astype(MXU_DT)
    for n0 in range(0, o_ref.shape[1], tn):
        acc = jnp.dot(xn, w_ref[:, n0:n0 + tn], preferred_element_type=F32) + b_ref[:, n0:n0 + tn]
        if relu2:
            acc = jnp.square(jnp.maximum(acc, 0.0))
        o_ref[:, n0:n0 + tn] = acc.astype(o_ref.dtype)


def _norm_matmul(x, g, w, b, *, tm, tn, relu2, out_dtype, name):
    R, K = x.shape
    N = w.shape[1]
    return pl.pallas_call(
        functools.partial(_norm_matmul_kernel, relu2=relu2, tn=tn),
        grid=(R // tm,),
        in_specs=[pl.BlockSpec((tm, K), lambda i: (i, 0)),
                  pl.BlockSpec((1, K), lambda i: (0, 0)),
                  pl.BlockSpec((K, N), lambda i: (0, 0)),
                  pl.BlockSpec((1, N), lambda i: (0, 0))],
        out_specs=pl.BlockSpec((tm, N), lambda i: (i, 0)),
        out_shape=jax.ShapeDtypeStruct((R, N), out_dtype),
        compiler_params=_params(("parallel",)),
        name=name,
    )(x, g, w, b)


def _matmul_norm_res_kernel(u_ref, w_ref, g_ref, h_ref, o_ref):
    y = jnp.dot(u_ref[...], w_ref[...], preferred_element_type=F32)
    o_ref[...] = h_ref[...] + _rms(y, g_ref[...])


def _matmul_norm_res(u, w, g, h, *, tm):
    R, K = u.shape
    N = w.shape[1]
    return pl.pallas_call(
        _matmul_norm_res_kernel,
        grid=(R // tm,),
        in_specs=[pl.BlockSpec((tm, K), lambda i: (i, 0)),
                  pl.BlockSpec((K, N), lambda i: (0, 0)),
                  pl.BlockSpec((1, N), lambda i: (0, 0)),
                  pl.BlockSpec((tm, N), lambda i: (i, 0))],
        out_specs=pl.BlockSpec((tm, N), lambda i: (i, 0)),
        out_shape=jax.ShapeDtypeStruct((R, N), F32),
        compiler_params=_params(("parallel",)),
        name="mlp_down",
    )(u, w, g, h)


def _mix_kernel(oa_ref, ob_ref, oc_ref, ga_ref, gb_ref, gc_ref, wa_ref, wb_ref, wc_ref,
                wo_ref, g_ref, h_ref, o_ref):
    ya = jnp.dot(oa_ref[...], wa_ref[...], preferred_element_type=F32)
    yb = jnp.dot(ob_ref[...], wb_ref[...], preferred_element_type=F32)
    yc = jnp.dot(oc_ref[...], wc_ref[...], preferred_element_type=F32)
    mix = _sigmoid(ga_ref[...]) * ya + _sigmoid(gb_ref[...]) * yb + _sigmoid(gc_ref[...]) * yc
    y = _mm(mix, wo_ref[...])
    o_ref[...] = h_ref[...] + _rms(y, g_ref[...])


def _mix(oa, ob, oc, z, wa, wb, wc, wo, g, h, *, tm):
    R, D = h.shape
    row = lambda i: (i, 0)
    const = lambda i: (0, 0)
    gate = lambda off: pl.BlockSpec((tm, D), lambda i: (i, off // D))
    return pl.pallas_call(
        _mix_kernel,
        grid=(R // tm,),
        in_specs=[pl.BlockSpec((tm, D), row), pl.BlockSpec((tm, D), row), pl.BlockSpec((tm, D), row),
                  gate(Z_GATE_A), gate(Z_GATE_B), gate(Z_GATE_C),
                  pl.BlockSpec((D, D), const), pl.BlockSpec((D, D), const), pl.BlockSpec((D, D), const),
                  pl.BlockSpec((D, D), const), pl.BlockSpec((1, D), const), pl.BlockSpec((tm, D), row)],
        out_specs=pl.BlockSpec((tm, D), row),
        out_shape=jax.ShapeDtypeStruct((R, D), F32),
        compiler_params=_params(("parallel",)),
        name="mix_out",
    )(oa, ob, oc, z, z, z, wa, wb, wc, wo, g, h)


def _gla_kernel(q_ref, k_ref, v_ref, r_ref, a_ref, wa_ref, ba_ref, gh_ref, o_ref, s_ref, *, front, nchunks, nb):
    C = LANES
    tb = pl.program_id(1)

    @pl.when(tb == 0)
    def _():
        s_ref[...] = jnp.zeros_like(s_ref)

    r0 = lax.broadcasted_iota(I32, (C, C), 0)
    r1 = lax.broadcasted_iota(I32, (C, C), 1)
    causal = r0 >= r1
    tril = causal.astype(F32)
    rowi = lax.broadcasted_iota(I32, (C, 1), 0)
    X = [(b, h) for b in range(nb) for h in range(GLA_H)]
    N = range(len(X))
    kq = [slice(h * GLA_DK, (h + 1) * GLA_DK) for _, h in X]
    kv = [slice(h * GLA_DV, (h + 1) * GLA_DV) for _, h in X]

    def body(c, carry):
        sl = pl.ds(pl.multiple_of(c * C, C), C)
        live = ((tb * MIX_CHUNKS + c) * C + rowi) >= front
        a = [a_ref[b, sl, :].astype(MXU_DT) for b in range(nb)]
        g = [_log_sigmoid(jnp.dot(a[b], wa_ref[h], preferred_element_type=F32) + ba_ref[h]) / GLA_TAU for b, h in X]
        G = [jnp.dot(tril, g[i], precision=HI, preferred_element_type=F32) for i in N]
        q = [q_ref[b, sl, kq[i]].astype(F32) * (GLA_DK ** -0.5) for i, (b, h) in enumerate(X)]
        k = [jnp.where(live, k_ref[b, sl, kq[i]].astype(F32), 0.0) for i, (b, h) in enumerate(X)]
        v = [jnp.where(live, v_ref[b, sl, kv[i]].astype(F32), 0.0).astype(MXU_DT) for i, (b, h) in enumerate(X)]
        S = [s_ref[b, h] for b, h in X]
        g_last = [G[i][C - 1:C, :] for i in N]
        g_mid = [G[i][C // 2 - 1:C // 2, :] for i in N]
        o_inter = [_mm(q[i] * jnp.exp(G[i]), S[i]) for i in N]
        A = [jnp.where(causal, _mm_nt(q[i] * jnp.exp(G[i] - g_mid[i]), k[i] * jnp.exp(g_mid[i] - G[i])), 0.0) for i in N]
        o_intra = [_mm(A[i], v[i]) for i in N]
        kd_t = [jnp.transpose(k[i] * jnp.exp(g_last[i] - G[i])) for i in N]
        dec_col = [jnp.transpose(jnp.broadcast_to(jnp.exp(g_last[i]), (C, GLA_DK))) for i in N]
        for i, (b, h) in enumerate(X):
            s_ref[b, h] = jnp.concatenate([dec_col[i], dec_col[i]], axis=1) * S[i] + _mm(kd_t[i], v[i])
        for i, (b, h) in enumerate(X):
            r = r_ref[b, sl, kv[i]].astype(F32)
            o = _rms(o_inter[i] + o_intra[i], gh_ref[...]) * (r * _sigmoid(r))
            o_ref[b, sl, kv[i]] = o.astype(o_ref.dtype)
        return carry

    lax.fori_loop(0, jnp.minimum(MIX_CHUNKS, nchunks - tb * MIX_CHUNKS), body, 0)


def _gla(zl, zh, wa, ba, gh, *, front):
    B, Tp, _ = zl.shape
    nchunks = Tp // LANES
    TB = min(MIX_CHUNKS * LANES, Tp)
    WK, WV = GLA_H * GLA_DK, GLA_H * GLA_DV
    nb = GLA_ROWS if B % GLA_ROWS == 0 else 1
    return pl.pallas_call(
        functools.partial(_gla_kernel, front=front, nchunks=nchunks, nb=nb),
        grid=(B // nb, pl.cdiv(Tp, TB)),
        in_specs=[pl.BlockSpec((nb, TB, WK), lambda b, t: (b, t, Z_GLA_Q // WK)),
                  pl.BlockSpec((nb, TB, WK), lambda b, t: (b, t, Z_GLA_K // WK)),
                  pl.BlockSpec((nb, TB, WV), lambda b, t: (b, t, Z_GLA_V // WV)),
                  pl.BlockSpec((nb, TB, WV), lambda b, t: (b, t, Z_GLA_R // WV)),
                  pl.BlockSpec((nb, TB, LANES), lambda b, t: (b, t, Z_SM0 // LANES)),
                  pl.BlockSpec((GLA_H, LANES, GLA_DK), lambda b, t: (0, 0, 0)),
                  pl.BlockSpec((GLA_H, 1, GLA_DK), lambda b, t: (0, 0, 0)),
                  pl.BlockSpec((1, GLA_DV), lambda b, t: (0, 0))],
        out_specs=pl.BlockSpec((nb, TB, WV), lambda b, t: (b, t, 0)),
        out_shape=jax.ShapeDtypeStruct((B, Tp, WV), ACT_DT),
        scratch_shapes=[pltpu.VMEM((nb, GLA_H, GLA_DK, GLA_DV), F32)],
        compiler_params=_params(("parallel", "arbitrary")),
        name="gla",
    )(zl, zl, zl, zl, zh, wa, ba, gh)


def _mlstm_kernel(q_ref, k_ref, v_ref, og_ref, gcol_ref, cw_ref, fbrow_ref, fb_ref, gh_ref, o_ref,
                  s_ref, m_ref, pq_ref, pk_ref, *, front, nchunks):
    C = LANES
    WK = ML_H * ML_DK
    tb = pl.program_id(1)

    @pl.when(tb == 0)
    def _():
        s_ref[...] = jnp.zeros_like(s_ref)
        m_ref[...] = jnp.zeros_like(m_ref)
        pq_ref[...] = jnp.zeros_like(pq_ref)
        pk_ref[...] = jnp.zeros_like(pk_ref)

    r0 = lax.broadcasted_iota(I32, (C, C), 0)
    r1 = lax.broadcasted_iota(I32, (C, C), 1)
    causal = r0 >= r1
    tril = causal.astype(F32)
    triu = (r0 <= r1).astype(F32)
    rowi = lax.broadcasted_iota(I32, (C, 1), 0)

    def conv_silu(x, tail, w):
        acc = x * w[ML_CONV - 1:ML_CONV, :]
        for d in range(1, ML_CONV):
            sh = pltpu.roll(x, d, 0)
            head = jnp.where(rowi[:SUBLANES] >= d, sh[:SUBLANES], pltpu.roll(tail, d, 0))
            sh = jnp.concatenate([head, sh[SUBLANES:]], axis=0)
            acc = acc + sh * w[ML_CONV - 1 - d:ML_CONV - d, :]
        return acc * _sigmoid(acc)

    def body(c, carry):
        sl = pl.ds(pl.multiple_of(c * C, C), C)
        cg = tb * MIX_CHUNKS + c
        live = (cg * C + rowi) >= front
        xq = jnp.where(live, q_ref[0, sl, :].astype(F32), 0.0)
        xk = jnp.where(live, k_ref[0, sl, :].astype(F32), 0.0)
        q_all = conv_silu(xq, pq_ref[...], cw_ref[:, :WK]) * (ML_DK ** -0.5)
        k_all = conv_silu(xk, pk_ref[...], cw_ref[:, WK:])
        pq_ref[...] = xq[C - SUBLANES:, :]
        pk_ref[...] = xk[C - SUBLANES:, :]
        lf_col = _log_sigmoid(gcol_ref[0, sl, :] + fbrow_ref[...])
        H = range(ML_H)
        vs = [slice(h * ML_DV, (h + 1) * ML_DV) for h in H]
        q = [q_all[:, h * ML_DK:(h + 1) * ML_DK].astype(MXU_DT) for h in H]
        k = [k_all[:, h * ML_DK:(h + 1) * ML_DK] for h in H]
        ones = jnp.ones((C, LANES), MXU_DT)
        v_aug = [jnp.concatenate([jnp.where(live, v_ref[0, sl, vs[h]].astype(F32), 0.0).astype(MXU_DT), ones], axis=1)
                 for h in H]
        m_prev = [m_ref[h, 0:1, :] for h in H]
        S = [s_ref[h] for h in H]
        sel = [jnp.dot(lf_col, (r0 == ML_H + h).astype(F32), precision=HI, preferred_element_type=F32) for h in H]
        b_bc = [jnp.dot(tril, sel[h], precision=HI, preferred_element_type=F32) for h in H]
        g_t = jnp.transpose(gcol_ref[0, sl, :])
        li_row = [g_t[h:h + 1, :] for h in H]
        lf_row = [_log_sigmoid(g_t[ML_H + h:ML_H + h + 1, :] + fb_ref[h]) for h in H]
        b_row = [jnp.dot(lf_row[h], triu, precision=HI, preferred_element_type=F32) for h in H]
        b_last = [b_row[h][:, C - 1:C] for h in H]
        qk = [_mm_nt(q[h], k[h]) for h in H]
        qs = [_mm(q[h], S[h]) for h in H]
        Dm = [jnp.where(causal, b_bc[h] - (b_row[h] - li_row[h]), -jnp.inf) for h in H]
        m_inter = [b_bc[h] + m_prev[h] for h in H]
        m = [jnp.maximum(m_inter[h], jnp.max(Dm[h], axis=1, keepdims=True)) for h in H]
        pv = [_mm(jnp.exp(Dm[h] - m[h]) * qk[h], v_aug[h]) for h in H]
        dec = [b_last[h] - b_row[h] + li_row[h] for h in H]
        m_new = [jnp.maximum(b_last[h] + m_prev[h], jnp.max(dec[h], axis=1, keepdims=True)) for h in H]
        kt_w = [jnp.transpose(k[h]) * jnp.exp(dec[h] - m_new[h]) for h in H]
        for h in H:
            w_old = jnp.exp(b_last[h] + m_prev[h] - m_new[h])
            s_ref[h] = jnp.concatenate([w_old, w_old, w_old], axis=1) * S[h] + _mm(kt_w[h], v_aug[h])
            m_ref[h] = jnp.broadcast_to(m_new[h], (SUBLANES, C))
        for h in H:
            w_inter = jnp.exp(m_inter[h] - m[h])
            num = jnp.concatenate([w_inter, w_inter, w_inter], axis=1) * qs[h] + pv[h]
            den = num[:, ML_DV:]
            dd = jnp.maximum(jnp.abs(den), jnp.exp(-m[h]))
            hval = num[:, :ML_DV] / jnp.concatenate([dd, dd], axis=1)
            og = og_ref[0, sl, vs[h]].astype(F32)
            o_ref[0, sl, vs[h]] = (_rms(hval, gh_ref[...]) * _sigmoid(og)).astype(o_ref.dtype)
        return carry

    lax.fori_loop(0, jnp.minimum(MIX_CHUNKS, nchunks - tb * MIX_CHUNKS), body, 0)


def _mlstm(zl, zh, conv_w, fb_row, fb, gh, *, front):
    B, Tp, _ = zl.shape
    nchunks = Tp // LANES
    TB = min(MIX_CHUNKS * LANES, Tp)
    WK, WV = ML_H * ML_DK, ML_H * ML_DV
    return pl.pallas_call(
        functools.partial(_mlstm_kernel, front=front, nchunks=nchunks),
        grid=(B, pl.cdiv(Tp, TB)),
        in_specs=[pl.BlockSpec((1, TB, WK), lambda b, t: (b, t, Z_ML_Q // WK)),
                  pl.BlockSpec((1, TB, WK), lambda b, t: (b, t, Z_ML_K // WK)),
                  pl.BlockSpec((1, TB, WV), lambda b, t: (b, t, Z_ML_V // WV)),
                  pl.BlockSpec((1, TB, WV), lambda b, t: (b, t, Z_ML_O // WV)),
                  pl.BlockSpec((1, TB, LANES), lambda b, t: (b, t, Z_SM2 // LANES)),
                  pl.BlockSpec((ML_CONV, 2 * WK), lambda b, t: (0, 0)),
                  pl.BlockSpec((1, LANES), lambda b, t: (0, 0)),
                  pl.BlockSpec(memory_space=pltpu.SMEM),
                  pl.BlockSpec((1, ML_DV), lambda b, t: (0, 0))],
        out_specs=pl.BlockSpec((1, TB, WV), lambda b, t: (b, t, 0)),
        out_shape=jax.ShapeDtypeStruct((B, Tp, WV), ACT_DT),
        scratch_shapes=[pltpu.VMEM((ML_H, ML_DK, ML_DV + LANES), F32), pltpu.VMEM((ML_H, SUBLANES, LANES), F32),
                        pltpu.VMEM((SUBLANES, WK), F32), pltpu.VMEM((SUBLANES, WK), F32)],
        compiler_params=_params(("parallel", "arbitrary")),
        name="mlstm",
    )(zl, zl, zl, zl, zh, conv_w, fb_row, fb, gh)


def _dsa_prep_kernel(cq_ref, ckv_ref, sm_ref, gq_ref, gkv_ref, lng_ref, lnb_ref, wuq_ref, wiq_ref,
                     q_ref, qi_ref, kv_ref, ki_ref):
    cq = _rms(cq_ref[0], gq_ref[...]).astype(MXU_DT)
    for hp in range(DSA_H // 2):
        w2 = wuq_ref[:, 2 * hp * DSA_KV_LAT:(2 * hp + 2) * DSA_KV_LAT]
        qf = jnp.dot(cq, w2, preferred_element_type=F32) * (DSA_KV_LAT ** -0.5 * LOG2E)
        q_ref[0, 2 * hp] = qf[:, :DSA_KV_LAT].astype(q_ref.dtype)
        q_ref[0, 2 * hp + 1] = qf[:, DSA_KV_LAT:].astype(q_ref.dtype)
    for hh in range(IDX_H):
        qi = jnp.dot(cq, wiq_ref[hh], preferred_element_type=F32) * (IDX_D ** -0.5)
        qi_ref[0, hh] = qi.astype(qi_ref.dtype)
    kv_ref[0] = _rms(ckv_ref[0], gkv_ref[...]).astype(kv_ref.dtype)
    x = sm_ref[0][:, :IDX_D]
    mu = jnp.mean(x, axis=-1, keepdims=True)
    var = jnp.mean(jnp.square(x - mu), axis=-1, keepdims=True)
    ki_ref[0] = ((x - mu) * lax.rsqrt(var + EPS) * lng_ref[...] + lnb_ref[...]).astype(ki_ref.dtype)


def _dsa_prep(z, gq, gkv, lng, lnb, wuq, wiq):
    B, Tp, _ = z.shape
    tm = max(t for t in range(2 * SUBLANES, PREP_ROWS_MAX + 1, 2 * SUBLANES) if Tp % t == 0)
    c2 = lambda b, i: (0, 0)
    return pl.pallas_call(
        _dsa_prep_kernel,
        grid=(B, Tp // tm),
        in_specs=[pl.BlockSpec((1, tm, DSA_Q_LORA), lambda b, i: (b, i, Z_CQ // DSA_Q_LORA)),
                  pl.BlockSpec((1, tm, DSA_KV_LAT), lambda b, i: (b, i, Z_CKV // DSA_KV_LAT)),
                  pl.BlockSpec((1, tm, LANES), lambda b, i: (b, i, Z_SM1 // LANES)),
                  pl.BlockSpec((1, DSA_Q_LORA), c2), pl.BlockSpec((1, DSA_KV_LAT), c2),
                  pl.BlockSpec((1, IDX_D), c2), pl.BlockSpec((1, IDX_D), c2),
                  pl.BlockSpec((DSA_Q_LORA, DSA_H * DSA_KV_LAT), c2),
                  pl.BlockSpec((IDX_H, DSA_Q_LORA, IDX_D), lambda b, i: (0, 0, 0))],
        out_specs=[pl.BlockSpec((1, DSA_H, tm, DSA_KV_LAT), lambda b, i: (b, 0, i, 0)),
                   pl.BlockSpec((1, IDX_H, tm, IDX_D), lambda b, i: (b, 0, i, 0)),
                   pl.BlockSpec((1, tm, DSA_KV_LAT), lambda b, i: (b, i, 0)),
                   pl.BlockSpec((1, tm, IDX_D), lambda b, i: (b, i, 0))],
        out_shape=[jax.ShapeDtypeStruct((B, DSA_H, Tp, DSA_KV_LAT), ACT_DT),
                   jax.ShapeDtypeStruct((B, IDX_H, Tp, IDX_D), ACT_DT),
                   jax.ShapeDtypeStruct((B, Tp, DSA_KV_LAT), ACT_DT),
                   jax.ShapeDtypeStruct((B, Tp, IDX_D), ACT_DT)],
        compiler_params=_params(("parallel", "parallel")),
        name="dsa_prep",
    )(z, z, z, gq, gkv, lng, lnb, wuq, wiq)


def _sortable(x):
    bits = pltpu.bitcast(x, I32)
    key = jnp.where(bits < 0, bits ^ 0x7FFFFFFF, bits)
    return jnp.where(x == 0.0, 0, key)


_KEY_NEG = int(np.array(NEG, np.float32).view(np.int32)) ^ 0x7FFFFFFF


def _dsa_kernel(ki_ref, kv_ref, kvt_ref, qi_ref, q_ref, wi_ref, bias_ref, wuv_ref, o_ref,
                key_ref, madd_ref, lg_ref, acc_ref, *, front, topk, Tp, ns):
    QB = LANES
    S = range(ns)
    j = pl.program_id(1)
    nkb = j + 1
    ngrp = nkb // KB_GROUP
    s_loc = lax.broadcasted_iota(I32, (QB, QB), 0)
    t_glob = j * QB + lax.broadcasted_iota(I32, (QB, QB), 1)
    wi = [wi_ref[s] * (IDX_H ** -0.5) for s in S]
    n_rest = Tp - nkb * QB

    def blk(kb):
        return pl.ds(pl.multiple_of(kb * QB, QB), QB)

    def score_block(kb, s):
        ki = ki_ref[s, blk(kb), :]
        acc = jnp.zeros((QB, QB), F32)
        for hp in range(IDX_H // 2):
            qi2 = qi_ref[s, 2 * hp:2 * hp + 2].reshape(2 * QB, IDX_D)
            sc = lax.dot_general(ki, qi2, NT_DIMS, preferred_element_type=F32)
            acc = acc + jnp.maximum(sc[:, :QB], 0.0) * wi[s][2 * hp:2 * hp + 1, :]
            acc = acc + jnp.maximum(sc[:, QB:], 0.0) * wi[s][2 * hp + 1:2 * hp + 2, :]
        s_glob = kb * QB + s_loc
        key = _sortable(jnp.where(s_glob <= t_glob, acc, NEG))
        key_ref[s, blk(kb), :] = jnp.where(s_glob >= front, key, INT_MIN)

    def score_group(i, c):
        for u in range(KB_GROUP):
            for s in S:
                score_block(KB_GROUP * i + u, s)
        return c

    def score_single(kb, c):
        for s in S:
            score_block(kb, s)
        return c

    lax.fori_loop(0, ngrp, score_group, 0)
    lax.fori_loop(ngrp * KB_GROUP, nkb, score_single, 0)

    def fold32(hit):
        return jnp.sum(hit.reshape(QB // SUBLANES, SUBLANES, QB), axis=0)

    def count(pred_fn):
        def hit_of(kb, s):
            return pred_fn(key_ref[s, blk(kb), :], kb, s).astype(I32)

        def group(i, accs):
            out = []
            for s in S:
                hit = hit_of(KB_GROUP * i, s)
                for u in range(1, KB_GROUP):
                    hit = hit + hit_of(KB_GROUP * i + u, s)
                out.append(accs[s] + fold32(hit))
            return tuple(out)

        accs = lax.fori_loop(0, ngrp, group, tuple(jnp.zeros((SUBLANES, QB), I32) for _ in S))
        accs = lax.fori_loop(ngrp * KB_GROUP, nkb,
                             lambda kb, a: tuple(a[s] + fold32(hit_of(kb, s)) for s in S), accs)
        return [jnp.sum(a, axis=0, keepdims=True) for a in accs]

    def count_ge(cands):
        cnt = count(lambda kblk, kb, s: kblk >= cands[s])
        return [cnt[s] + jnp.where(_KEY_NEG >= cands[s], n_rest, 0) for s in S]

    c0 = count_ge([jnp.zeros((1, QB), I32) for _ in S])
    base = tuple(jnp.where(c0[s] >= topk, 0, INT_MIN).astype(I32) for s in S)

    def bit_body(i, base):
        cands = [base[s] | jnp.left_shift(jnp.int32(1), 30 - i) for s in S]
        cnt = count_ge(cands)
        return tuple(jnp.where(cnt[s] >= topk, cands[s], base[s]) for s in S)

    tau = lax.fori_loop(0, 31, bit_body, base)

    def valid_of(kb):
        s_glob = kb * QB + s_loc
        return (s_glob >= front) & (s_glob <= t_glob)

    c_gt = count(lambda kblk, kb, s: kblk > tau[s])
    c_eq = count(lambda kblk, kb, s: (kblk == tau[s]) & valid_of(kb))
    need = [topk - (c_gt[s] + jnp.where(_KEY_NEG > tau[s], n_rest, 0)) for s in S]

    for s in S:
        @pl.when(jnp.max((c_eq[s] > need[s]).astype(I32)) > 0)
        def _(s=s):
            lstrict = (s_loc > lax.broadcasted_iota(I32, (QB, QB), 1)).astype(MXU_DT)
            needf = need[s].astype(F32)

            def tie_body(kb, seen):
                kblk = key_ref[s, blk(kb), :]
                tie = (kblk == tau[s]) & valid_of(kb)
                tief = tie.astype(F32)
                rank = seen + jnp.dot(lstrict, tief.astype(MXU_DT), preferred_element_type=F32)
                key_ref[s, blk(kb), :] = jnp.where(tie & (rank >= needf), INT_MIN, kblk)
                return seen + jnp.sum(tief, axis=0, keepdims=True)

            lax.fori_loop(0, nkb, tie_body, jnp.zeros((1, QB), F32))

    def mask_body(kb, c):
        for s in S:
            sel = (key_ref[s, blk(kb), :] >= tau[s]) & valid_of(kb)
            madd_ref[s, blk(kb), :] = jnp.where(sel, 0.0, NEG)
        return c

    lax.fori_loop(0, nkb, mask_body, 0)

    NP = DSA_H // 2
    grp = (QB // SUBLANES, SUBLANES, 2 * QB)

    AS = ATTN_STREAMS if ns % ATTN_STREAMS == 0 else 1

    def attend(g, c):
        ss = [g * AS + u for u in range(AS)]
        U = range(AS)
        acc_ref[...] = jnp.zeros_like(acc_ref)

        def score_inputs(kb):
            out = []
            for u in U:
                ma = madd_ref[ss[u], blk(kb), :]
                out.append((kv_ref[ss[u], blk(kb), :], jnp.concatenate([ma, ma], axis=1)))
            return out, jnp.minimum(j - kb, 2)

        def stage1(p, slot, inputs):
            per, rel = inputs
            out = []
            for u in U:
                kvb, ma2 = per[u]
                q2 = q_ref[ss[u], 2 * p:2 * p + 2].reshape(2 * QB, DSA_KV_LAT)
                lg = lax.dot_general(kvb, q2, NT_DIMS, preferred_element_type=F32) + bias_ref[p, rel] + ma2
                lg_ref[u, slot, p] = lg
                out.append(jnp.max(jnp.max(lg.reshape(grp), axis=0), axis=0, keepdims=True))
            return out

        def stage2(p, slot, kvts, m_old, bm):
            out = []
            for u in U:
                m_new = jnp.maximum(m_old[u], bm[u])
                pm = jnp.exp2(lg_ref[u, slot, p] - m_new)
                acc_ref[u, p] = (jnp.exp2(m_old[u] - m_new) * acc_ref[u, p]
                                 + jnp.dot(kvts[u], pm.astype(MXU_DT), preferred_element_type=F32))
                out.append(m_new)
            return out

        def kvt_of(kb):
            return [kvt_ref[ss[u], kb] for u in U]

        inp0 = score_inputs(0)
        bm0 = tuple(tuple(stage1(p, 0, inp0)) for p in range(NP))

        def attn_body(i, carry):
            ms, bm_a = carry
            a = 2 * i
            inp_b = score_inputs(a + 1)
            inp_c = score_inputs(jnp.minimum(a + 2, nkb - 1))
            kvt_a = kvt_of(a)
            kvt_b = kvt_of(a + 1)
            ms_a, bm_b = [], []
            for p in range(NP):
                bm_b.append(tuple(stage1(p, 1, inp_b)))
                ms_a.append(tuple(stage2(p, 0, kvt_a, ms[p], bm_a[p])))
            ms_b, bm_c = [], []
            for p in range(NP):
                bm_c.append(tuple(stage1(p, 0, inp_c)))
                ms_b.append(tuple(stage2(p, 1, kvt_b, ms_a[p], bm_b[p])))
            return tuple(ms_b), tuple(bm_c)

        ninf = tuple(tuple(jnp.full((1, 2 * QB), -jnp.inf, F32) for _ in U) for _ in range(NP))
        ms, bm_last = lax.fori_loop(0, nkb // 2, attn_body, (ninf, bm0))

        @pl.when(nkb % 2 == 1)
        def _():
            kvts = kvt_of(nkb - 1)
            for p in range(NP):
                stage2(p, 0, kvts, ms[p], bm_last[p])

        for u in U:
            for p in range(NP):
                acc = acc_ref[u, p]
                o_t = acc[:DSA_KV_LAT] / acc[DSA_KV_LAT:DSA_KV_LAT + 1]
                outs = []
                for hh in range(2):
                    o_h = jnp.transpose(o_t[:, hh * QB:(hh + 1) * QB])
                    outs.append(_mm(o_h, wuv_ref[2 * p + hh]))
                o_ref[ss[u], :, 2 * p * DSA_DH:(2 * p + 2) * DSA_DH] = jnp.concatenate(outs, axis=1).astype(o_ref.dtype)
        return c

    lax.fori_loop(0, ns // AS, attend, 0)


def _dsa(ki, kv, kvt, qi, q, wi_t, bias_tiles, wuv, *, front, topk):
    B, Tp, _ = kv.shape
    nblk = Tp // LANES
    ns = DSA_STREAMS if B % DSA_STREAMS == 0 else 1
    return pl.pallas_call(
        functools.partial(_dsa_kernel, front=front, topk=topk, Tp=Tp, ns=ns),
        grid=(B // ns, nblk),
        in_specs=[pl.BlockSpec((ns, Tp, IDX_D), lambda b, j: (b, 0, 0)),
                  pl.BlockSpec((ns, Tp, DSA_KV_LAT), lambda b, j: (b, 0, 0)),
                  pl.BlockSpec((ns, nblk, DSA_KV_LAT + KVT_ONES, LANES), lambda b, j: (b, 0, 0, 0)),
                  pl.BlockSpec((ns, IDX_H, LANES, IDX_D), lambda b, j: (b, 0, j, 0)),
                  pl.BlockSpec((ns, DSA_H, LANES, DSA_KV_LAT), lambda b, j: (b, 0, j, 0)),
                  pl.BlockSpec((ns, IDX_H, LANES), lambda b, j: (b, 0, j)),
                  pl.BlockSpec((DSA_H // 2, 3, LANES, 2 * LANES), lambda b, j: (0, 0, 0, 0)),
                  pl.BlockSpec((DSA_H, DSA_KV_LAT, DSA_DH), lambda b, j: (0, 0, 0))],
        out_specs=pl.BlockSpec((ns, LANES, DSA_H * DSA_DH), lambda b, j: (b, j, 0)),
        out_shape=jax.ShapeDtypeStruct((B, Tp, DSA_H * DSA_DH), ACT_DT),
        scratch_shapes=[pltpu.VMEM((ns, Tp, LANES), I32), pltpu.VMEM((ns, Tp, LANES), F32),
                        pltpu.VMEM((ATTN_STREAMS, 2, DSA_H // 2, LANES, 2 * LANES), F32),
                        pltpu.VMEM((ATTN_STREAMS, DSA_H // 2, DSA_KV_LAT + KVT_ONES, 2 * LANES), F32)],
        compiler_params=_params(("parallel", "arbitrary")),
        name="dsa_attn",
    )(ki, kv, kvt, qi, q, wi_t, bias_tiles, wuv)


def _t5_bucket(rel):
    n = jnp.maximum(rel, 0)
    max_exact = T5_BUCKETS // 2
    nf = jnp.maximum(n, 1).astype(F32)
    large = max_exact + (jnp.log(nf / max_exact) / math.log(T5_MAX_DIST / max_exact)
                         * (T5_BUCKETS - max_exact)).astype(I32)
    large = jnp.minimum(large, T5_BUCKETS - 1)
    return jnp.where(n < max_exact, n, large)


def _t5_tiles(t5_bias):
    s = jnp.arange(LANES, dtype=I32)[:, None]
    t = jnp.arange(LANES, dtype=I32)[None, :]
    tiles = []
    for shift in (0, LANES, 2 * LANES):
        tiles.append(t5_bias[_t5_bucket(shift + t - s)])
    tiles = jnp.stack(tiles, axis=0)
    tiles = tiles.transpose(3, 0, 1, 2).reshape(DSA_H // 2, 2, 3, LANES, LANES)
    return (tiles.transpose(0, 2, 3, 1, 4).reshape(DSA_H // 2, 3, LANES, 2 * LANES) * LOG2E).astype(F32)


def _pack_in_proj(w_in, b_in):
    splits = np.cumsum([GLA_H * GLA_DK, GLA_H * GLA_DK, GLA_H * GLA_DV, GLA_H * GLA_DV, GLA_LR,
                        DSA_Q_LORA, DSA_KV_LAT, IDX_D, IDX_H,
                        ML_H * ML_DK, ML_H * ML_DK, ML_H * ML_DV, ML_H * ML_DV, ML_H, ML_H,
                        D_MODEL, D_MODEL])[:].tolist()

    def pack(a):
        (gq, gk, gv, gr, ga, cq, ckv, ik, iw, mq, mk, mv, mo, mi, mf, a_, b_, c_) = jnp.split(a, splits, axis=-1)
        pad = lambda n: jnp.zeros(a.shape[:-1] + (n,), a.dtype)
        lo = jnp.concatenate([gq, gk, gv, gr, mq, mk, mv, mo], axis=-1)
        hi = jnp.concatenate([a_, b_, c_, cq, ckv,
                              ga, pad(LANES - GLA_LR),
                              ik, iw, pad(LANES - IDX_D - IDX_H),
                              mi, mf, pad(LANES - 2 * ML_H)], axis=-1)
        return lo, hi

    return pack(w_in), pack(b_in)


def kernel(x, meta_tokens, t5_bias, g_pre_mix, w_in, b_in, w_gla_a2, b_gla_a, g_gla_head, w_br_gla, g_dsa_q, w_dsa_uq, w_idx_q, g_dsa_kv, w_dsa_uv, idx_ln_g, idx_ln_b, w_br_dsa, ml_conv, ml_f_bias, g_ml_head, w_br_ml, w_out, g_post_mix, g_pre_mlp, w_up, w_down, g_post_mlp):
    B, S, D = x.shape
    L = w_in.shape[0]
    T = S + N_META
    front = (-T) % LANES
    Tp = T + front
    R = B * Tp
    nblk = Tp // LANES
    topk = min(TOPK_MAX, (T - N_META) // 4)
    tm_mid = 512 if R % 512 == 0 else LANES
    tm_small = 512 if R % 512 == 0 else LANES

    (w_lo, w_hi), (b_lo, b_hi) = _pack_in_proj(w_in, b_in)
    wa2 = jnp.pad(w_gla_a2, ((0, 0), (0, LANES - GLA_LR), (0, 0)))
    wa2 = wa2.reshape(L, LANES, GLA_H, GLA_DK).transpose(0, 2, 1, 3).astype(MXU_DT)
    fb_row = jnp.pad(ml_f_bias, ((0, 0), (ML_H, LANES - 2 * ML_H)))[:, None, :]
    layers = dict(
        g_pre_mix=g_pre_mix[:, None, :], w_lo=w_lo.astype(MXU_DT), b_lo=b_lo[:, None, :],
        w_hi=w_hi.astype(MXU_DT), b_hi=b_hi[:, None, :],
        wa2=wa2, ba=b_gla_a.reshape(L, GLA_H, 1, GLA_DK), g_gla_head=g_gla_head[:, None, :],
        w_br_gla=w_br_gla.astype(MXU_DT),
        g_dsa_q=g_dsa_q[:, None, :], w_dsa_uq=w_dsa_uq.astype(MXU_DT),
        w_idx_q=w_idx_q.reshape(L, DSA_Q_LORA, IDX_H, IDX_D).transpose(0, 2, 1, 3).astype(MXU_DT),
        g_dsa_kv=g_dsa_kv[:, None, :], w_dsa_uv=w_dsa_uv.astype(MXU_DT),
        idx_ln_g=idx_ln_g[:, None, :], idx_ln_b=idx_ln_b[:, None, :], w_br_dsa=w_br_dsa.astype(MXU_DT),
        ml_conv=ml_conv, fb_row=fb_row, ml_f_bias=ml_f_bias, g_ml_head=g_ml_head[:, None, :],
        w_br_ml=w_br_ml.astype(MXU_DT), w_out=w_out.astype(MXU_DT), g_post_mix=g_post_mix[:, None, :],
        g_pre_mlp=g_pre_mlp[:, None, :], w_up=w_up.astype(MXU_DT), w_down=w_down.astype(MXU_DT),
        g_post_mlp=g_post_mlp[:, None, :],
    )
    bias_tiles = _t5_tiles(t5_bias)
    zero_ff = jnp.zeros((1, D_FF), F32)

    meta = jnp.broadcast_to(meta_tokens.astype(x.dtype)[None], (B, N_META, D))
    h0 = jnp.concatenate([jnp.zeros((B, front, D), x.dtype), meta, x], axis=1).reshape(R, D)

    def layer(h, p):
        zl = _norm_matmul(h, p["g_pre_mix"], p["w_lo"], p["b_lo"], tm=tm_mid, tn=1024, relu2=False,
                          out_dtype=ACT_DT, name="in_proj_lo").reshape(B, Tp, NZL)
        zh = _norm_matmul(h, p["g_pre_mix"], p["w_hi"], p["b_hi"], tm=tm_mid, tn=768, relu2=False,
                          out_dtype=F32, name="in_proj_hi")
        zh3 = zh.reshape(B, Tp, NZH)
        o_gla = _gla(zl, zh3, p["wa2"], p["ba"], p["g_gla_head"], front=front)
        o_ml = _mlstm(zl, zh3, p["ml_conv"], p["fb_row"], p["ml_f_bias"], p["g_ml_head"], front=front)
        q, qi, kv, ki = _dsa_prep(zh3, p["g_dsa_q"], p["g_dsa_kv"], p["idx_ln_g"], p["idx_ln_b"],
                                  p["w_dsa_uq"], p["w_idx_q"])
        kvt = kv.reshape(B, nblk, LANES, DSA_KV_LAT).transpose(0, 1, 3, 2)
        kvt = jnp.concatenate([kvt, jnp.ones((B, nblk, KVT_ONES, LANES), kvt.dtype)], axis=2)
        wi_t = zh3[:, :, Z_SM1 + IDX_D:Z_SM1 + IDX_D + IDX_H].transpose(0, 2, 1)
        o_dsa = _dsa(ki, kv, kvt, qi, q, wi_t, bias_tiles, p["w_dsa_uv"], front=front, topk=topk)
        h = _mix(o_gla.reshape(R, D), o_dsa.reshape(R, D), o_ml.reshape(R, D), zh,
                 p["w_br_gla"], p["w_br_dsa"], p["w_br_ml"], p["w_out"], p["g_post_mix"], h, tm=tm_small)
        u = _norm_matmul(h, p["g_pre_mlp"], p["w_up"], zero_ff, tm=tm_mid, tn=1024,
                         relu2=True, out_dtype=ACT_DT, name="mlp_up")
        h = _matmul_norm_res(u, p["w_down"], p["g_post_mlp"], h, tm=tm_mid)
        return h, None

    h, _ = lax.scan(layer, h0, layers)
    return h.reshape(B, Tp, D)[:, front + N_META:]
```

```python
import functools
import math

import numpy as np
import jax
import jax.numpy as jnp
from jax import lax
from jax.experimental import pallas as pl
from jax.experimental.pallas import tpu as pltpu

D_MODEL = 1024
N_META = 16
GLA_H, GLA_DK, GLA_DV, GLA_LR, GLA_TAU = 4, 128, 256, 16, 16.0
DSA_H, DSA_DH, DSA_Q_LORA, DSA_KV_LAT = 16, 64, 256, 128
IDX_H, IDX_D, TOPK_MAX = 8, 64, 256
ML_H, ML_DK, ML_DV, ML_CONV = 4, 128, 256, 4
T5_BUCKETS, T5_MAX_DIST = 32, 128
D_FF = 4 * D_MODEL
EPS = 1e-6
NEG = -1e30
LOG2E = math.log2(math.e)

LANES = 128
SUBLANES = 8
VMEM_LIMIT = 56 * 1024 * 1024

MXU_DT = jnp.bfloat16
ACT_DT = jnp.bfloat16

F32 = jnp.float32
I32 = jnp.int32
INT_MIN = -2147483648
HI = lax.Precision.HIGHEST
NT_DIMS = (((1,), (1,)), ((), ()))

Z_GLA_Q, Z_GLA_K, Z_GLA_V, Z_GLA_R = 0, 512, 1024, 2048
Z_ML_Q, Z_ML_K, Z_ML_V, Z_ML_O = 3072, 3584, 4096, 5120
NZL = 6144
Z_GATE_A, Z_GATE_B, Z_GATE_C = 0, 1024, 2048
Z_CQ, Z_CKV = 3072, 3328
Z_SM0, Z_SM1, Z_SM2 = 3456, 3584, 3712
NZH = 3840
KVT_ONES = 16
ROW_TILE = 512
PREP_ROWS_MAX = 1088
DSA_STREAMS = 4
ATTN_STREAMS = 2
KB_GROUP = 4
GLA_ROWS = 2
MIX_CHUNKS = 6


def _params(sem):
    return pltpu.CompilerParams(dimension_semantics=sem, vmem_limit_bytes=VMEM_LIMIT)


def _mm(a, b):
    return jnp.dot(a.astype(MXU_DT), b.astype(MXU_DT), preferred_element_type=F32)


def _mm_nt(a, b):
    return lax.dot_general(a.astype(MXU_DT), b.astype(MXU_DT), NT_DIMS, preferred_element_type=F32)


def _rms(x, g):
    return x * lax.rsqrt(jnp.mean(x * x, axis=-1, keepdims=True) + EPS) * g


def _log_sigmoid(x):
    return jnp.minimum(x, 0.0) - jnp.log1p(jnp.exp(-jnp.abs(x)))


def _sigmoid(x):
    return 0.5 * jnp.tanh(0.5 * x) + 0.5


def _norm_matmul_kernel(x_ref, g_ref, w_ref, b_ref, o_ref, *, relu2, tn):
    xn = _rms(x_ref[...], g_ref[...]).astype(MXU_DT)
    for n0 in range(0, o_ref.shape[1], tn):
        acc = jnp.dot(xn, w_ref[:, n0:n0 + tn], preferred_element_type=F32) + b_ref[:, n0:n0 + tn]
        if relu2:
            acc = jnp.square(jnp.maximum(acc, 0.0))
        o_ref[:, n0:n0 + tn] = acc.astype(o_ref.dtype)


def _norm_matmul(x, g, w, b, *, tm, tn, relu2, out_dtype, name):
    R, K = x.shape
    N = w.shape[1]
    return pl.pallas_call(
        functools.partial(_norm_matmul_kernel, relu2=relu2, tn=tn),
        grid=(R // tm,),
        in_specs=[pl.BlockSpec((tm, K), lambda i: (i, 0)),
                  pl.BlockSpec((1, K), lambda i: (0, 0)),
                  pl.BlockSpec((K, N), lambda i: (0, 0)),
                  pl.BlockSpec((1, N), lambda i: (0, 0))],
        out_specs=pl.BlockSpec((tm, N), lambda i: (i, 0)),
        out_shape=jax.ShapeDtypeStruct((R, N), out_dtype),
        compiler_params=_params(("parallel",)),
        name=name,
    )(x, g, w, b)


def _matmul_norm_res_kernel(u_ref, w_ref, g_ref, h_ref, o_ref):
    y = jnp.dot(u_ref[...], w_ref[...], preferred_element_type=F32)
    o_ref[...] = h_ref[...] + _rms(y, g_ref[...])


def _matmul_norm_res(u, w, g, h, *, tm):
    R, K = u.shape
    N = w.shape[1]
    return pl.pallas_call(
        _matmul_norm_res_kernel,
        grid=(R // tm,),
        in_specs=[pl.BlockSpec((tm, K), lambda i: (i, 0)),
                  pl.BlockSpec((K, N), lambda i: (0, 0)),
                  pl.BlockSpec((1, N), lambda i: (0, 0)),
                  pl.BlockSpec((tm, N), lambda i: (i, 0))],
        out_specs=pl.BlockSpec((tm, N), lambda i: (i, 0)),
        out_shape=jax.ShapeDtypeStruct((R, N), F32),
        compiler_params=_params(("parallel",)),
        name="mlp_down",
    )(u, w, g, h)


def _mix_kernel(oa_ref, ob_ref, oc_ref, ga_ref, gb_ref, gc_ref, wa_ref, wb_ref, wc_ref,
                wo_ref, g_ref, h_ref, o_ref):
    ya = jnp.dot(oa_ref[...], wa_ref[...], preferred_element_type=F32)
    yb = jnp.dot(ob_ref[...], wb_ref[...], preferred_element_type=F32)
    yc = jnp.dot(oc_ref[...], wc_ref[...], preferred_element_type=F32)
    mix = _sigmoid(ga_ref[...]) * ya + _sigmoid(gb_ref[...]) * yb + _sigmoid(gc_ref[...]) * yc
    y = _mm(mix, wo_ref[...])
    o_ref[...] = h_ref[...] + _rms(y, g_ref[...])


def _mix(oa, ob, oc, z, wa, wb, wc, wo, g, h, *, tm):
    R, D = h.shape
    row = lambda i: (i, 0)
    const = lambda i: (0, 0)
    gate = lambda off: pl.BlockSpec((tm, D), lambda i: (i, off // D))
    return pl.pallas_call(
        _mix_kernel,
        grid=(R // tm,),
        in_specs=[pl.BlockSpec((tm, D), row), pl.BlockSpec((tm, D), row), pl.BlockSpec((tm, D), row),
                  gate(Z_GATE_A), gate(Z_GATE_B), gate(Z_GATE_C),
                  pl.BlockSpec((D, D), const), pl.BlockSpec((D, D), const), pl.BlockSpec((D, D), const),
                  pl.BlockSpec((D, D), const), pl.BlockSpec((1, D), const), pl.BlockSpec((tm, D), row)],
        out_specs=pl.BlockSpec((tm, D), row),
        out_shape=jax.ShapeDtypeStruct((R, D), F32),
        compiler_params=_params(("parallel",)),
        name="mix_out",
    )(oa, ob, oc, z, z, z, wa, wb, wc, wo, g, h)


def _gla_kernel(q_ref, k_ref, v_ref, r_ref, a_ref, wa_ref, ba_ref, gh_ref, o_ref, s_ref, *, front, nchunks, nb):
    C = LANES
    tb = pl.program_id(1)

    @pl.when(tb == 0)
    def _():
        s_ref[...] = jnp.zeros_like(s_ref)

    r0 = lax.broadcasted_iota(I32, (C, C), 0)
    r1 = lax.broadcasted_iota(I32, (C, C), 1)
    causal = r0 >= r1
    tril = causal.astype(F32)
    rowi = lax.broadcasted_iota(I32, (C, 1), 0)
    X = [(b, h) for b in range(nb) for h in range(GLA_H)]
    N = range(len(X))
    kq = [slice(h * GLA_DK, (h + 1) * GLA_DK) for _, h in X]
    kv = [slice(h * GLA_DV, (h + 1) * GLA_DV) for _, h in X]

    def body(c, carry):
        sl = pl.ds(pl.multiple_of(c * C, C), C)
        live = ((tb * MIX_CHUNKS + c) * C + rowi) >= front
        a = [a_ref[b, sl, :].astype(MXU_DT) for b in range(nb)]
        g = [_log_sigmoid(jnp.dot(a[b], wa_ref[h], preferred_element_type=F32) + ba_ref[h]) / GLA_TAU for b, h in X]
        G = [jnp.dot(tril, g[i], precision=HI, preferred_element_type=F32) for i in N]
        q = [q_ref[b, sl, kq[i]].astype(F32) * (GLA_DK ** -0.5) for i, (b, h) in enumerate(X)]
        k = [jnp.where(live, k_ref[b, sl, kq[i]].astype(F32), 0.0) for i, (b, h) in enumerate(X)]
        v = [jnp.where(live, v_ref[b, sl, kv[i]].astype(F32), 0.0).astype(MXU_DT) for i, (b, h) in enumerate(X)]
        S = [s_ref[b, h] for b, h in X]
        g_last = [G[i][C - 1:C, :] for i in N]
        g_mid = [G[i][C // 2 - 1:C // 2, :] for i in N]
        o_inter = [_mm(q[i] * jnp.exp(G[i]), S[i]) for i in N]
        A = [jnp.where(causal, _mm_nt(q[i] * jnp.exp(G[i] - g_mid[i]), k[i] * jnp.exp(g_mid[i] - G[i])), 0.0) for i in N]
        o_intra = [_mm(A[i], v[i]) for i in N]
        kd_t = [jnp.transpose(k[i] * jnp.exp(g_last[i] - G[i])) for i in N]
        dec_col = [jnp.transpose(jnp.broadcast_to(jnp.exp(g_last[i]), (C, GLA_DK))) for i in N]
        for i, (b, h) in enumerate(X):
            s_ref[b, h] = jnp.concatenate([dec_col[i], dec_col[i]], axis=1) * S[i] + _mm(kd_t[i], v[i])
        for i, (b, h) in enumerate(X):
            r = r_ref[b, sl, kv[i]].astype(F32)
            o = _rms(o_inter[i] + o_intra[i], gh_ref[...]) * (r * _sigmoid(r))
            o_ref[b, sl, kv[i]] = o.astype(o_ref.dtype)
        return carry

    lax.fori_loop(0, jnp.minimum(MIX_CHUNKS, nchunks - tb * MIX_CHUNKS), body, 0)


def _gla(zl, zh, wa, ba, gh, *, front):
    B, Tp, _ = zl.shape
    nchunks = Tp // LANES
    TB = min(MIX_CHUNKS * LANES, Tp)
    WK, WV = GLA_H * GLA_DK, GLA_H * GLA_DV
    nb = GLA_ROWS if B % GLA_ROWS == 0 else 1
    return pl.pallas_call(
        functools.partial(_gla_kernel, front=front, nchunks=nchunks, nb=nb),
        grid=(B // nb, pl.cdiv(Tp, TB)),
        in_specs=[pl.BlockSpec((nb, TB, WK), lambda b, t: (b, t, Z_GLA_Q // WK)),
                  pl.BlockSpec((nb, TB, WK), lambda b, t: (b, t, Z_GLA_K // WK)),
                  pl.BlockSpec((nb, TB, WV), lambda b, t: (b, t, Z_GLA_V // WV)),
                  pl.BlockSpec((nb, TB, WV), lambda b, t: (b, t, Z_GLA_R // WV)),
                  pl.BlockSpec((nb, TB, LANES), lambda b, t: (b, t, Z_SM0 // LANES)),
                  pl.BlockSpec((GLA_H, LANES, GLA_DK), lambda b, t: (0, 0, 0)),
                  pl.BlockSpec((GLA_H, 1, GLA_DK), lambda b, t: (0, 0, 0)),
                  pl.BlockSpec((1, GLA_DV), lambda b, t: (0, 0))],
        out_specs=pl.BlockSpec((nb, TB, WV), lambda b, t: (b, t, 0)),
        out_shape=jax.ShapeDtypeStruct((B, Tp, WV), ACT_DT),
        scratch_shapes=[pltpu.VMEM((nb, GLA_H, GLA_DK, GLA_DV), F32)],
        compiler_params=_params(("parallel", "arbitrary")),
        name="gla",
    )(zl, zl, zl, zl, zh, wa, ba, gh)


def _mlstm_kernel(q_ref, k_ref, v_ref, og_ref, gcol_ref, cw_ref, fbrow_ref, fb_ref, gh_ref, o_ref,
                  s_ref, m_ref, pq_ref, pk_ref, *, front, nchunks):
    C = LANES
    WK = ML_H * ML_DK
    tb = pl.program_id(1)

    @pl.when(tb == 0)
    def _():
        s_ref[...] = jnp.zeros_like(s_ref)
        m_ref[...] = jnp.zeros_like(m_ref)
        pq_ref[...] = jnp.zeros_like(pq_ref)
        pk_ref[...] = jnp.zeros_like(pk_ref)

    r0 = lax.broadcasted_iota(I32, (C, C), 0)
    r1 = lax.broadcasted_iota(I32, (C, C), 1)
    causal = r0 >= r1
    tril = causal.astype(F32)
    triu = (r0 <= r1).astype(F32)
    rowi = lax.broadcasted_iota(I32, (C, 1), 0)

    def conv_silu(x, tail, w):
        acc = x * w[ML_CONV - 1:ML_CONV, :]
        for d in range(1, ML_CONV):
            sh = pltpu.roll(x, d, 0)
            head = jnp.where(rowi[:SUBLANES] >= d, sh[:SUBLANES], pltpu.roll(tail, d, 0))
            sh = jnp.concatenate([head, sh[SUBLANES:]], axis=0)
            acc = acc + sh * w[ML_CONV - 1 - d:ML_CONV - d, :]
        return acc * _sigmoid(acc)

    def body(c, carry):
        sl = pl.ds(pl.multiple_of(c * C, C), C)
        cg = tb * MIX_CHUNKS + c
        live = (cg * C + rowi) >= front
        xq = jnp.where(live, q_ref[0, sl, :].astype(F32), 0.0)
        xk = jnp.where(live, k_ref[0, sl, :].astype(F32), 0.0)
        q_all = conv_silu(xq, pq_ref[...], cw_ref[:, :WK]) * (ML_DK ** -0.5)
        k_all = conv_silu(xk, pk_ref[...], cw_ref[:, WK:])
        pq_ref[...] = xq[C - SUBLANES:, :]
        pk_ref[...] = xk[C - SUBLANES:, :]
        lf_col = _log_sigmoid(gcol_ref[0, sl, :] + fbrow_ref[...])
        H = range(ML_H)
        vs = [slice(h * ML_DV, (h + 1) * ML_DV) for h in H]
        q = [q_all[:, h * ML_DK:(h + 1) * ML_DK].astype(MXU_DT) for h in H]
        k = [k_all[:, h * ML_DK:(h + 1) * ML_DK] for h in H]
        ones = jnp.ones((C, LANES), MXU_DT)
        v_aug = [jnp.concatenate([jnp.where(live, v_ref[0, sl, vs[h]].astype(F32), 0.0).astype(MXU_DT), ones], axis=1)
                 for h in H]
        m_prev = [m_ref[h, 0:1, :] for h in H]
        S = [s_ref[h] for h in H]
        sel = [jnp.dot(lf_col, (r0 == ML_H + h).astype(F32), precision=HI, preferred_element_type=F32) for h in H]
        b_bc = [jnp.dot(tril, sel[h], precision=HI, preferred_element_type=F32) for h in H]
        g_t = jnp.transpose(gcol_ref[0, sl, :])
        li_row = [g_t[h:h + 1, :] for h in H]
        lf_row = [_log_sigmoid(g_t[ML_H + h:ML_H + h + 1, :] + fb_ref[h]) for h in H]
        b_row = [jnp.dot(lf_row[h], triu, precision=HI, preferred_element_type=F32) for h in H]
        b_last = [b_row[h][:, C - 1:C] for h in H]
        qk = [_mm_nt(q[h], k[h]) for h in H]
        qs = [_mm(q[h], S[h]) for h in H]
        Dm = [jnp.where(causal, b_bc[h] - (b_row[h] - li_row[h]), -jnp.inf) for h in H]
        m_inter = [b_bc[h] + m_prev[h] for h in H]
        m = [jnp.maximum(m_inter[h], jnp.max(Dm[h], axis=1, keepdims=True)) for h in H]
        pv = [_mm(jnp.exp(Dm[h] - m[h]) * qk[h], v_aug[h]) for h in H]
        dec = [b_last[h] - b_row[h] + li_row[h] for h in H]
        m_new = [jnp.maximum(b_last[h] + m_prev[h], jnp.max(dec[h], axis=1, keepdims=True)) for h in H]
        kt_w = [jnp.transpose(k[h]) * jnp.exp(dec[h] - m_new[h]) for h in H]
        for h in H:
            w_old = jnp.exp(b_last[h] + m_prev[h] - m_new[h])
            s_ref[h] = jnp.concatenate([w_old, w_old, w_old], axis=1) * S[h] + _mm(kt_w[h], v_aug[h])
            m_ref[h] = jnp.broadcast_to(m_new[h], (SUBLANES, C))
        for h in H:
            w_inter = jnp.exp(m_inter[h] - m[h])
            num = jnp.concatenate([w_inter, w_inter, w_inter], axis=1) * qs[h] + pv[h]
            den = num[:, ML_DV:]
            dd = jnp.maximum(jnp.abs(den), jnp.exp(-m[h]))
            hval = num[:, :ML_DV] / jnp.concatenate([dd, dd], axis=1)
            og = og_ref[0, sl, vs[h]].astype(F32)
            o_ref[0, sl, vs[h]] = (_rms(hval, gh_ref[...]) * _sigmoid(og)).astype(o_ref.dtype)
        return carry

    lax.fori_loop(0, jnp.minimum(MIX_CHUNKS, nchunks - tb * MIX_CHUNKS), body, 0)


def _mlstm(zl, zh, conv_w, fb_row, fb, gh, *, front):
    B, Tp, _ = zl.shape
    nchunks = Tp // LANES
    TB = min(MIX_CHUNKS * LANES, Tp)
    WK, WV = ML_H * ML_DK, ML_H * ML_DV
    return pl.pallas_call(
        functools.partial(_mlstm_kernel, front=front, nchunks=nchunks),
        grid=(B, pl.cdiv(Tp, TB)),
        in_specs=[pl.BlockSpec((1, TB, WK), lambda b, t: (b, t, Z_ML_Q // WK)),
                  pl.BlockSpec((1, TB, WK), lambda b, t: (b, t, Z_ML_K // WK)),
                  pl.BlockSpec((1, TB, WV), lambda b, t: (b, t, Z_ML_V // WV)),
                  pl.BlockSpec((1, TB, WV), lambda b, t: (b, t, Z_ML_O // WV)),
                  pl.BlockSpec((1, TB, LANES), lambda b, t: (b, t, Z_SM2 // LANES)),
                  pl.BlockSpec((ML_CONV, 2 * WK), lambda b, t: (0, 0)),
                  pl.BlockSpec((1, LANES), lambda b, t: (0, 0)),
                  pl.BlockSpec(memory_space=pltpu.SMEM),
                  pl.BlockSpec((1, ML_DV), lambda b, t: (0, 0))],
        out_specs=pl.BlockSpec((1, TB, WV), lambda b, t: (b, t, 0)),
        out_shape=jax.ShapeDtypeStruct((B, Tp, WV), ACT_DT),
        scratch_shapes=[pltpu.VMEM((ML_H, ML_DK, ML_DV + LANES), F32), pltpu.VMEM((ML_H, SUBLANES, LANES), F32),
                        pltpu.VMEM((SUBLANES, WK), F32), pltpu.VMEM((SUBLANES, WK), F32)],
        compiler_params=_params(("parallel", "arbitrary")),
        name="mlstm",
    )(zl, zl, zl, zl, zh, conv_w, fb_row, fb, gh)


def _dsa_prep_kernel(cq_ref, ckv_ref, sm_ref, gq_ref, gkv_ref, lng_ref, lnb_ref, wuq_ref, wiq_ref,
                     q_ref, qi_ref, kv_ref, ki_ref):
    cq = _rms(cq_ref[0], gq_ref[...]).astype(MXU_DT)
    for hp in range(DSA_H // 2):
        w2 = wuq_ref[:, 2 * hp * DSA_KV_LAT:(2 * hp + 2) * DSA_KV_LAT]
        qf = jnp.dot(cq, w2, preferred_element_type=F32) * (DSA_KV_LAT ** -0.5 * LOG2E)
        q_ref[0, 2 * hp] = qf[:, :DSA_KV_LAT].astype(q_ref.dtype)
        q_ref[0, 2 * hp + 1] = qf[:, DSA_KV_LAT:].astype(q_ref.dtype)
    for hh in range(IDX_H):
        qi = jnp.dot(cq, wiq_ref[hh], preferred_element_type=F32) * (IDX_D ** -0.5)
        qi_ref[0, hh] = qi.astype(qi_ref.dtype)
    kv_ref[0] = _rms(ckv_ref[0], gkv_ref[...]).astype(kv_ref.dtype)
    x = sm_ref[0][:, :IDX_D]
    mu = jnp.mean(x, axis=-1, keepdims=True)
    var = jnp.mean(jnp.square(x - mu), axis=-1, keepdims=True)
    ki_ref[0] = ((x - mu) * lax.rsqrt(var + EPS) * lng_ref[...] + lnb_ref[...]).astype(ki_ref.dtype)


def _dsa_prep(z, gq, gkv, lng, lnb, wuq, wiq):
    B, Tp, _ = z.shape
    tm = max(t for t in range(2 * SUBLANES, PREP_ROWS_MAX + 1, 2 * SUBLANES) if Tp % t == 0)
    c2 = lambda b, i: (0, 0)
    return pl.pallas_call(
        _dsa_prep_kernel,
        grid=(B, Tp // tm),
        in_specs=[pl.BlockSpec((1, tm, DSA_Q_LORA), lambda b, i: (b, i, Z_CQ // DSA_Q_LORA)),
                  pl.BlockSpec((1, tm, DSA_KV_LAT), lambda b, i: (b, i, Z_CKV // DSA_KV_LAT)),
                  pl.BlockSpec((1, tm, LANES), lambda b, i: (b, i, Z_SM1 // LANES)),
                  pl.BlockSpec((1, DSA_Q_LORA), c2), pl.BlockSpec((1, DSA_KV_LAT), c2),
                  pl.BlockSpec((1, IDX_D), c2), pl.BlockSpec((1, IDX_D), c2),
                  pl.BlockSpec((DSA_Q_LORA, DSA_H * DSA_KV_LAT), c2),
                  pl.BlockSpec((IDX_H, DSA_Q_LORA, IDX_D), lambda b, i: (0, 0, 0))],
        out_specs=[pl.BlockSpec((1, DSA_H, tm, DSA_KV_LAT), lambda b, i: (b, 0, i, 0)),
                   pl.BlockSpec((1, IDX_H, tm, IDX_D), lambda b, i: (b, 0, i, 0)),
                   pl.BlockSpec((1, tm, DSA_KV_LAT), lambda b, i: (b, i, 0)),
                   pl.BlockSpec((1, tm, IDX_D), lambda b, i: (b, i, 0))],
        out_shape=[jax.ShapeDtypeStruct((B, DSA_H, Tp, DSA_KV_LAT), ACT_DT),
                   jax.ShapeDtypeStruct((B, IDX_H, Tp, IDX_D), ACT_DT),
                   jax.ShapeDtypeStruct((B, Tp, DSA_KV_LAT), ACT_DT),
                   jax.ShapeDtypeStruct((B, Tp, IDX_D), ACT_DT)],
        compiler_params=_params(("parallel", "parallel")),
        name="dsa_prep",
    )(z, z, z, gq, gkv, lng, lnb, wuq, wiq)


def _sortable(x):
    bits = pltpu.bitcast(x, I32)
    key = jnp.where(bits < 0, bits ^ 0x7FFFFFFF, bits)
    return jnp.where(x == 0.0, 0, key)


_KEY_NEG = int(np.array(NEG, np.float32).view(np.int32)) ^ 0x7FFFFFFF


def _dsa_kernel(ki_ref, kv_ref, kvt_ref, qi_ref, q_ref, wi_ref, bias_ref, wuv_ref, o_ref,
                key_ref, madd_ref, lg_ref, acc_ref, *, front, topk, Tp, ns):
    QB = LANES
    S = range(ns)
    j = pl.program_id(1)
    nkb = j + 1
    ngrp = nkb // KB_GROUP
    s_loc = lax.broadcasted_iota(I32, (QB, QB), 0)
    t_glob = j * QB + lax.broadcasted_iota(I32, (QB, QB), 1)
    wi = [wi_ref[s] * (IDX_H ** -0.5) for s in S]
    n_rest = Tp - nkb * QB

    def blk(kb):
        return pl.ds(pl.multiple_of(kb * QB, QB), QB)

    def score_block(kb, s):
        ki = ki_ref[s, blk(kb), :]
        acc = jnp.zeros((QB, QB), F32)
        for hp in range(IDX_H // 2):
            qi2 = qi_ref[s, 2 * hp:2 * hp + 2].reshape(2 * QB, IDX_D)
            sc = lax.dot_general(ki, qi2, NT_DIMS, preferred_element_type=F32)
            acc = acc + jnp.maximum(sc[:, :QB], 0.0) * wi[s][2 * hp:2 * hp + 1, :]
            acc = acc + jnp.maximum(sc[:, QB:], 0.0) * wi[s][2 * hp + 1:2 * hp + 2, :]
        s_glob = kb * QB + s_loc
        key = _sortable(jnp.where(s_glob <= t_glob, acc, NEG))
        key_ref[s, blk(kb), :] = jnp.where(s_glob >= front, key, INT_MIN)

    def score_group(i, c):
        for u in range(KB_GROUP):
            for s in S:
                score_block(KB_GROUP * i + u, s)
        return c

    def score_single(kb, c):
        for s in S:
            score_block(kb, s)
        return c

    lax.fori_loop(0, ngrp, score_group, 0)
    lax.fori_loop(ngrp * KB_GROUP, nkb, score_single, 0)

    def fold32(hit):
        return jnp.sum(hit.reshape(QB // SUBLANES, SUBLANES, QB), axis=0)

    def count(pred_fn):
        def hit_of(kb, s):
            return pred_fn(key_ref[s, blk(kb), :], kb, s).astype(I32)

        def group(i, accs):
            out = []
            for s in S:
                hit = hit_of(KB_GROUP * i, s)
                for u in range(1, KB_GROUP):
                    hit = hit + hit_of(KB_GROUP * i + u, s)
                out.append(accs[s] + fold32(hit))
            return tuple(out)

        accs = lax.fori_loop(0, ngrp, group, tuple(jnp.zeros((SUBLANES, QB), I32) for _ in S))
        accs = lax.fori_loop(ngrp * KB_GROUP, nkb,
                             lambda kb, a: tuple(a[s] + fold32(hit_of(kb, s)) for s in S), accs)
        return [jnp.sum(a, axis=0, keepdims=True) for a in accs]

    def count_ge(cands):
        cnt = count(lambda kblk, kb, s: kblk >= cands[s])
        return [cnt[s] + jnp.where(_KEY_NEG >= cands[s], n_rest, 0) for s in S]

    c0 = count_ge([jnp.zeros((1, QB), I32) for _ in S])
    few = nkb * QB - front <= topk
    base = tuple(jnp.where(few, INT_MIN + 1, jnp.where(c0[s] >= topk, 0, INT_MIN)).astype(I32) for s in S)

    def bit_body(i, base):
        cands = [base[s] | jnp.left_shift(jnp.int32(1), 30 - i) for s in S]
        cnt = count_ge(cands)
        return tuple(jnp.where(cnt[s] >= topk, cands[s], base[s]) for s in S)

    tau = lax.fori_loop(0, jnp.where(few, 0, 31), bit_body, base)

    def valid_of(kb):
        s_glob = kb * QB + s_loc
        return (s_glob >= front) & (s_glob <= t_glob)

    c_gt = count(lambda kblk, kb, s: kblk > tau[s])
    c_eq = count(lambda kblk, kb, s: (kblk == tau[s]) & valid_of(kb))
    need = [topk - (c_gt[s] + jnp.where(_KEY_NEG > tau[s], n_rest, 0)) for s in S]

    for s in S:
        @pl.when(jnp.max((c_eq[s] > jnp.maximum(need[s], 0)).astype(I32)) > 0)
        def _(s=s):
            lstrict = (s_loc > lax.broadcasted_iota(I32, (QB, QB), 1)).astype(MXU_DT)
            needf = need[s].astype(F32)

            def tie_body(kb, seen):
                kblk = key_ref[s, blk(kb), :]
                tie = (kblk == tau[s]) & valid_of(kb)
                tief = tie.astype(F32)
                rank = seen + jnp.dot(lstrict, tief.astype(MXU_DT), preferred_element_type=F32)
                key_ref[s, blk(kb), :] = jnp.where(tie & (rank >= needf), INT_MIN, kblk)
                return seen + jnp.sum(tief, axis=0, keepdims=True)

            lax.fori_loop(0, nkb, tie_body, jnp.zeros((1, QB), F32))

    def mask_body(kb, c):
        for s in S:
            sel = (key_ref[s, blk(kb), :] >= tau[s]) & valid_of(kb)
            madd_ref[s, blk(kb), :] = jnp.where(sel, 0.0, NEG)
        return c

    lax.fori_loop(0, nkb, mask_body, 0)

    NP = DSA_H // 2
    grp = (QB // SUBLANES, SUBLANES, 2 * QB)

    AS = ATTN_STREAMS if ns % ATTN_STREAMS == 0 else 1

    def attend(g, c):
        ss = [g * AS + u for u in range(AS)]
        U = range(AS)
        acc_ref[...] = jnp.zeros_like(acc_ref)

        def score_inputs(kb):
            out = []
            for u in U:
                ma = madd_ref[ss[u], blk(kb), :]
                out.append((kv_ref[ss[u], blk(kb), :], jnp.concatenate([ma, ma], axis=1)))
            return out, jnp.minimum(j - kb, 2)

        def stage1(p, slot, inputs):
            per, rel = inputs
            out = []
            for u in U:
                kvb, ma2 = per[u]
                q2 = q_ref[ss[u], 2 * p:2 * p + 2].reshape(2 * QB, DSA_KV_LAT)
                lg = lax.dot_general(kvb, q2, NT_DIMS, preferred_element_type=F32) + bias_ref[p, rel] + ma2
                lg_ref[u, slot, p] = lg
                out.append(jnp.max(jnp.max(lg.reshape(grp), axis=0), axis=0, keepdims=True))
            return out

        def stage2(p, slot, kvts, m_old, bm):
            out = []
            for u in U:
                m_new = jnp.maximum(m_old[u], bm[u])
                pm = jnp.exp2(lg_ref[u, slot, p] - m_new)
                acc_ref[u, p] = (jnp.exp2(m_old[u] - m_new) * acc_ref[u, p]
                                 + jnp.dot(kvts[u], pm.astype(MXU_DT), preferred_element_type=F32))
                out.append(m_new)
            return out

        def kvt_of(kb):
            return [kvt_ref[ss[u], kb] for u in U]

        inp0 = score_inputs(0)
        bm0 = tuple(tuple(stage1(p, 0, inp0)) for p in range(NP))

        def attn_body(i, carry):
            ms, bm_a = carry
            a = 2 * i
            inp_b = score_inputs(a + 1)
            inp_c = score_inputs(jnp.minimum(a + 2, nkb - 1))
            kvt_a = kvt_of(a)
            kvt_b = kvt_of(a + 1)
            ms_a, bm_b = [], []
            for p in range(NP):
                bm_b.append(tuple(stage1(p, 1, inp_b)))
                ms_a.append(tuple(stage2(p, 0, kvt_a, ms[p], bm_a[p])))
            ms_b, bm_c = [], []
            for p in range(NP):
                bm_c.append(tuple(stage1(p, 0, inp_c)))
                ms_b.append(tuple(stage2(p, 1, kvt_b, ms_a[p], bm_b[p])))
            return tuple(ms_b), tuple(bm_c)

        ninf = tuple(tuple(jnp.full((1, 2 * QB), -jnp.inf, F32) for _ in U) for _ in range(NP))
        ms, bm_last = lax.fori_loop(0, nkb // 2, attn_body, (ninf, bm0))

        @pl.when(nkb % 2 == 1)
        def _():
            kvts = kvt_of(nkb - 1)
            for p in range(NP):
                stage2(p, 0, kvts, ms[p], bm_last[p])

        for u in U:
            for p in range(NP):
                acc = acc_ref[u, p]
                o_t = acc[:DSA_KV_LAT] / acc[DSA_KV_LAT:DSA_KV_LAT + 1]
                outs = []
                for hh in range(2):
                    o_h = jnp.transpose(o_t[:, hh * QB:(hh + 1) * QB])
                    outs.append(_mm(o_h, wuv_ref[2 * p + hh]))
                o_ref[ss[u], :, 2 * p * DSA_DH:(2 * p + 2) * DSA_DH] = jnp.concatenate(outs, axis=1).astype(o_ref.dtype)
        return c

    lax.fori_loop(0, ns // AS, attend, 0)


def _dsa(ki, kv, kvt, qi, q, wi_t, bias_tiles, wuv, *, front, topk):
    B, Tp, _ = kv.shape
    nblk = Tp // LANES
    ns = DSA_STREAMS if B % DSA_STREAMS == 0 else 1
    return pl.pallas_call(
        functools.partial(_dsa_kernel, front=front, topk=topk, Tp=Tp, ns=ns),
        grid=(B // ns, nblk),
        in_specs=[pl.BlockSpec((ns, Tp, IDX_D), lambda b, j: (b, 0, 0)),
                  pl.BlockSpec((ns, Tp, DSA_KV_LAT), lambda b, j: (b, 0, 0)),
                  pl.BlockSpec((ns, nblk, DSA_KV_LAT + KVT_ONES, LANES), lambda b, j: (b, 0, 0, 0)),
                  pl.BlockSpec((ns, IDX_H, LANES, IDX_D), lambda b, j: (b, 0, j, 0)),
                  pl.BlockSpec((ns, DSA_H, LANES, DSA_KV_LAT), lambda b, j: (b, 0, j, 0)),
                  pl.BlockSpec((ns, IDX_H, LANES), lambda b, j: (b, 0, j)),
                  pl.BlockSpec((DSA_H // 2, 3, LANES, 2 * LANES), lambda b, j: (0, 0, 0, 0)),
                  pl.BlockSpec((DSA_H, DSA_KV_LAT, DSA_DH), lambda b, j: (0, 0, 0))],
        out_specs=pl.BlockSpec((ns, LANES, DSA_H * DSA_DH), lambda b, j: (b, j, 0)),
        out_shape=jax.ShapeDtypeStruct((B, Tp, DSA_H * DSA_DH), ACT_DT),
        scratch_shapes=[pltpu.VMEM((ns, Tp, LANES), I32), pltpu.VMEM((ns, Tp, LANES), F32),
                        pltpu.VMEM((ATTN_STREAMS, 2, DSA_H // 2, LANES, 2 * LANES), F32),
                        pltpu.VMEM((ATTN_STREAMS, DSA_H // 2, DSA_KV_LAT + KVT_ONES, 2 * LANES), F32)],
        compiler_params=_params(("parallel", "arbitrary")),
        name="dsa_attn",
    )(ki, kv, kvt, qi, q, wi_t, bias_tiles, wuv)


def _t5_bucket(rel):
    n = jnp.maximum(rel, 0)
    max_exact = T5_BUCKETS // 2
    nf = jnp.maximum(n, 1).astype(F32)
    large = max_exact + (jnp.log(nf / max_exact) / math.log(T5_MAX_DIST / max_exact)
                         * (T5_BUCKETS - max_exact)).astype(I32)
    large = jnp.minimum(large, T5_BUCKETS - 1)
    return jnp.where(n < max_exact, n, large)


def _t5_tiles(t5_bias):
    s = jnp.arange(LANES, dtype=I32)[:, None]
    t = jnp.arange(LANES, dtype=I32)[None, :]
    tiles = []
    for shift in (0, LANES, 2 * LANES):
        tiles.append(t5_bias[_t5_bucket(shift + t - s)])
    tiles = jnp.stack(tiles, axis=0)
    tiles = tiles.transpose(3, 0, 1, 2).reshape(DSA_H // 2, 2, 3, LANES, LANES)
    return (tiles.transpose(0, 2, 3, 1, 4).reshape(DSA_H // 2, 3, LANES, 2 * LANES) * LOG2E).astype(F32)


def _pack_in_proj(w_in, b_in):
    splits = np.cumsum([GLA_H * GLA_DK, GLA_H * GLA_DK, GLA_H * GLA_DV, GLA_H * GLA_DV, GLA_LR,
                        DSA_Q_LORA, DSA_KV_LAT, IDX_D, IDX_H,
                        ML_H * ML_DK, ML_H * ML_DK, ML_H * ML_DV, ML_H * ML_DV, ML_H, ML_H,
                        D_MODEL, D_MODEL])[:].tolist()

    def pack(a):
        (gq, gk, gv, gr, ga, cq, ckv, ik, iw, mq, mk, mv, mo, mi, mf, a_, b_, c_) = jnp.split(a, splits, axis=-1)
        pad = lambda n: jnp.zeros(a.shape[:-1] + (n,), a.dtype)
        lo = jnp.concatenate([gq, gk, gv, gr, mq, mk, mv, mo], axis=-1)
        hi = jnp.concatenate([a_, b_, c_, cq, ckv,
                              ga, pad(LANES - GLA_LR),
                              ik, iw, pad(LANES - IDX_D - IDX_H),
                              mi, mf, pad(LANES - 2 * ML_H)], axis=-1)
        return lo, hi

    return pack(w_in), pack(b_in)


def kernel(x, meta_tokens, t5_bias, g_pre_mix, w_in, b_in, w_gla_a2, b_gla_a, g_gla_head, w_br_gla, g_dsa_q, w_dsa_uq, w_idx_q, g_dsa_kv, w_dsa_uv, idx_ln_g, idx_ln_b, w_br_dsa, ml_conv, ml_f_bias, g_ml_head, w_br_ml, w_out, g_post_mix, g_pre_mlp, w_up, w_down, g_post_mlp):
    B, S, D = x.shape
    L = w_in.shape[0]
    T = S + N_META
    front = (-T) % LANES
    Tp = T + front
    R = B * Tp
    nblk = Tp // LANES
    topk = min(TOPK_MAX, (T - N_META) // 4)
    tm = ROW_TILE if R % ROW_TILE == 0 else LANES

    (w_lo, w_hi), (b_lo, b_hi) = _pack_in_proj(w_in, b_in)
    wa2 = jnp.pad(w_gla_a2, ((0, 0), (0, LANES - GLA_LR), (0, 0)))
    wa2 = wa2.reshape(L, LANES, GLA_H, GLA_DK).transpose(0, 2, 1, 3).astype(MXU_DT)
    fb_row = jnp.pad(ml_f_bias, ((0, 0), (ML_H, LANES - 2 * ML_H)))[:, None, :]
    layers = dict(
        g_pre_mix=g_pre_mix[:, None, :], w_lo=w_lo.astype(MXU_DT), b_lo=b_lo[:, None, :],
        w_hi=w_hi.astype(MXU_DT), b_hi=b_hi[:, None, :],
        wa2=wa2, ba=b_gla_a.reshape(L, GLA_H, 1, GLA_DK), g_gla_head=g_gla_head[:, None, :],
        w_br_gla=w_br_gla.astype(MXU_DT),
        g_dsa_q=g_dsa_q[:, None, :], w_dsa_uq=w_dsa_uq.astype(MXU_DT),
        w_idx_q=w_idx_q.reshape(L, DSA_Q_LORA, IDX_H, IDX_D).transpose(0, 2, 1, 3).astype(MXU_DT),
        g_dsa_kv=g_dsa_kv[:, None, :], w_dsa_uv=w_dsa_uv.astype(MXU_DT),
        idx_ln_g=idx_ln_g[:, None, :], idx_ln_b=idx_ln_b[:, None, :], w_br_dsa=w_br_dsa.astype(MXU_DT),
        ml_conv=ml_conv, fb_row=fb_row, ml_f_bias=ml_f_bias, g_ml_head=g_ml_head[:, None, :],
        w_br_ml=w_br_ml.astype(MXU_DT), w_out=w_out.astype(MXU_DT), g_post_mix=g_post_mix[:, None, :],
        g_pre_mlp=g_pre_mlp[:, None, :], w_up=w_up.astype(MXU_DT), w_down=w_down.astype(MXU_DT),
        g_post_mlp=g_post_mlp[:, None, :],
    )
    bias_tiles = _t5_tiles(t5_bias)
    zero_ff = jnp.zeros((1, D_FF), F32)

    meta = jnp.broadcast_to(meta_tokens.astype(x.dtype)[None], (B, N_META, D))
    h0 = jnp.concatenate([jnp.zeros((B, front, D), x.dtype), meta, x], axis=1).reshape(R, D)

    def layer(h, p):
        zl = _norm_matmul(h, p["g_pre_mix"], p["w_lo"], p["b_lo"], tm=tm, tn=1024, relu2=False,
                          out_dtype=ACT_DT, name="in_proj_lo").reshape(B, Tp, NZL)
        zh = _norm_matmul(h, p["g_pre_mix"], p["w_hi"], p["b_hi"], tm=tm, tn=768, relu2=False,
                          out_dtype=F32, name="in_proj_hi")
        zh3 = zh.reshape(B, Tp, NZH)
        o_gla = _gla(zl, zh3, p["wa2"], p["ba"], p["g_gla_head"], front=front)
        o_ml = _mlstm(zl, zh3, p["ml_conv"], p["fb_row"], p["ml_f_bias"], p["g_ml_head"], front=front)
        q, qi, kv, ki = _dsa_prep(zh3, p["g_dsa_q"], p["g_dsa_kv"], p["idx_ln_g"], p["idx_ln_b"],
                                  p["w_dsa_uq"], p["w_idx_q"])
        kvt = kv.reshape(B, nblk, LANES, DSA_KV_LAT).transpose(0, 1, 3, 2)
        kvt = jnp.concatenate([kvt, jnp.ones((B, nblk, KVT_ONES, LANES), kvt.dtype)], axis=2)
        wi_t = zh3[:, :, Z_SM1 + IDX_D:Z_SM1 + IDX_D + IDX_H].transpose(0, 2, 1)
        o_dsa = _dsa(ki, kv, kvt, qi, q, wi_t, bias_tiles, p["w_dsa_uv"], front=front, topk=topk)
        h = _mix(o_gla.reshape(R, D), o_dsa.reshape(R, D), o_ml.reshape(R, D), zh,
                 p["w_br_gla"], p["w_br_dsa"], p["w_br_ml"], p["w_out"], p["g_post_mix"], h, tm=tm)
        u = _norm_matmul(h, p["g_pre_mlp"], p["w_up"], zero_ff, tm=tm, tn=1024,
                         relu2=True, out_dtype=ACT_DT, name="mlp_up")
        h = _matmul_norm_res(u, p["w_down"], p["g_post_mlp"], h, tm=tm)
        return h, None

    h, _ = lax.scan(layer, h0, layers)
    return h.reshape(B, Tp, D)[:, front + N_META:]
```

```python
import functools
import math

import numpy as np
import jax
import jax.numpy as jnp
from jax import lax
from jax.experimental import pallas as pl
from jax.experimental.pallas import tpu as pltpu

D_MODEL = 1024
N_META = 16
GLA_H, GLA_DK, GLA_DV, GLA_LR, GLA_TAU = 4, 128, 256, 16, 16.0
DSA_H, DSA_DH, DSA_Q_LORA, DSA_KV_LAT = 16, 64, 256, 128
IDX_H, IDX_D, TOPK_MAX = 8, 64, 256
ML_H, ML_DK, ML_DV, ML_CONV = 4, 128, 256, 4
T5_BUCKETS, T5_MAX_DIST = 32, 128
D_FF = 4 * D_MODEL
EPS = 1e-6
NEG = -1e30
LOG2E = math.log2(math.e)

LANES = 128
SUBLANES = 8
VMEM_LIMIT = 56 * 1024 * 1024

MXU_DT = jnp.bfloat16
ACT_DT = jnp.bfloat16

F32 = jnp.float32
I32 = jnp.int32
INT_MIN = -2147483648
HI = lax.Precision.HIGHEST
NT_DIMS = (((1,), (1,)), ((), ()))

Z_GLA_Q, Z_GLA_K, Z_GLA_V, Z_GLA_R = 0, 512, 1024, 2048
Z_ML_Q, Z_ML_K, Z_ML_V, Z_ML_O = 3072, 3584, 4096, 5120
NZL = 6144
Z_GATE_A, Z_GATE_B, Z_GATE_C = 0, 1024, 2048
Z_CQ, Z_CKV = 3072, 3328
Z_SM0, Z_SM1, Z_SM2 = 3456, 3584, 3712
NZH = 3840
KVT_ONES = 16
ROW_TILE = 512
PREP_ROWS_MAX = 1088
DSA_STREAMS = 4
ATTN_STREAMS = 2
KB_GROUP = 4
GLA_ROWS = 2
MIX_CHUNKS = 6


def _params(sem):
    return pltpu.CompilerParams(dimension_semantics=sem, vmem_limit_bytes=VMEM_LIMIT)


def _mm(a, b):
    return jnp.dot(a.astype(MXU_DT), b.astype(MXU_DT), preferred_element_type=F32)


def _mm_nt(a, b):
    return lax.dot_general(a.astype(MXU_DT), b.astype(MXU_DT), NT_DIMS, preferred_element_type=F32)


def _rms(x, g):
    return x * lax.rsqrt(jnp.mean(x * x, axis=-1, keepdims=True) + EPS) * g


def _log_sigmoid(x):
    return jnp.minimum(x, 0.0) - jnp.log1p(jnp.exp(-jnp.abs(x)))


def _sigmoid(x):
    return 0.5 * jnp.tanh(0.5 * x) + 0.5


def _norm_matmul_kernel(x_ref, g_ref, w_ref, b_ref, o_ref, *, relu2, tn):
    xn = _rms(x_ref[...], g_ref[...]).astype(MXU_DT)
    for n0 in range(0, o_ref.shape[1], tn):
        acc = jnp.dot(xn, w_ref[:, n0:n0 + tn], preferred_element_type=F32) + b_ref[:, n0:n0 + tn]
        if relu2:
            acc = jnp.square(jnp.maximum(acc, 0.0))
        o_ref[:, n0:n0 + tn] = acc.astype(o_ref.dtype)


def _norm_matmul(x, g, w, b, *, tm, tn, relu2, out_dtype, name):
    R, K = x.shape
    N = w.shape[1]
    return pl.pallas_call(
        functools.partial(_norm_matmul_kernel, relu2=relu2, tn=tn),
        grid=(R // tm,),
        in_specs=[pl.BlockSpec((tm, K), lambda i: (i, 0)),
                  pl.BlockSpec((1, K), lambda i: (0, 0)),
                  pl.BlockSpec((K, N), lambda i: (0, 0)),
                  pl.BlockSpec((1, N), lambda i: (0, 0))],
        out_specs=pl.BlockSpec((tm, N), lambda i: (i, 0)),
        out_shape=jax.ShapeDtypeStruct((R, N), out_dtype),
        compiler_params=_params(("parallel",)),
        name=name,
    )(x, g, w, b)


def _matmul_norm_res_kernel(u_ref, w_ref, g_ref, h_ref, o_ref):
    y = jnp.dot(u_ref[...], w_ref[...], preferred_element_type=F32)
    o_ref[...] = h_ref[...] + _rms(y, g_ref[...])


def _matmul_norm_res(u, w, g, h, *, tm):
    R, K = u.shape
    N = w.shape[1]
    return pl.pallas_call(
        _matmul_norm_res_kernel,
        grid=(R // tm,),
        in_specs=[pl.BlockSpec((tm, K), lambda i: (i, 0)),
                  pl.BlockSpec((K, N), lambda i: (0, 0)),
                  pl.BlockSpec((1, N), lambda i: (0, 0)),
                  pl.BlockSpec((tm, N), lambda i: (i, 0))],
        out_specs=pl.BlockSpec((tm, N), lambda i: (i, 0)),
        out_shape=jax.ShapeDtypeStruct((R, N), F32),
        compiler_params=_params(("parallel",)),
        name="mlp_down",
    )(u, w, g, h)


def _mix_kernel(oa_ref, ob_ref, oc_ref, ga_ref, gb_ref, gc_ref, wa_ref, wb_ref, wc_ref,
                wo_ref, g_ref, h_ref, o_ref):
    ya = jnp.dot(oa_ref[...], wa_ref[...], preferred_element_type=F32)
    yb = jnp.dot(ob_ref[...], wb_ref[...], preferred_element_type=F32)
    yc = jnp.dot(oc_ref[...], wc_ref[...], preferred_element_type=F32)
    mix = _sigmoid(ga_ref[...]) * ya + _sigmoid(gb_ref[...]) * yb + _sigmoid(gc_ref[...]) * yc
    y = _mm(mix, wo_ref[...])
    o_ref[...] = h_ref[...] + _rms(y, g_ref[...])


def _mix(oa, ob, oc, z, wa, wb, wc, wo, g, h, *, tm):
    R, D = h.shape
    row = lambda i: (i, 0)
    const = lambda i: (0, 0)
    gate = lambda off: pl.BlockSpec((tm, D), lambda i: (i, off // D))
    return pl.pallas_call(
        _mix_kernel,
        grid=(R // tm,),
        in_specs=[pl.BlockSpec((tm, D), row), pl.BlockSpec((tm, D), row), pl.BlockSpec((tm, D), row),
                  gate(Z_GATE_A), gate(Z_GATE_B), gate(Z_GATE_C),
                  pl.BlockSpec((D, D), const), pl.BlockSpec((D, D), const), pl.BlockSpec((D, D), const),
                  pl.BlockSpec((D, D), const), pl.BlockSpec((1, D), const), pl.BlockSpec((tm, D), row)],
        out_specs=pl.BlockSpec((tm, D), row),
        out_shape=jax.ShapeDtypeStruct((R, D), F32),
        compiler_params=_params(("parallel",)),
        name="mix_out",
    )(oa, ob, oc, z, z, z, wa, wb, wc, wo, g, h)


def _gla_kernel(q_ref, k_ref, v_ref, r_ref, a_ref, wa_ref, ba_ref, gh_ref, o_ref, s_ref, *, front, nchunks, nb):
    C = LANES
    tb = pl.program_id(1)

    @pl.when(tb == 0)
    def _():
        s_ref[...] = jnp.zeros_like(s_ref)

    r0 = lax.broadcasted_iota(I32, (C, C), 0)
    r1 = lax.broadcasted_iota(I32, (C, C), 1)
    causal = r0 >= r1
    tril = causal.astype(F32)
    rowi = lax.broadcasted_iota(I32, (C, 1), 0)
    X = [(b, h) for b in range(nb) for h in range(GLA_H)]
    N = range(len(X))
    kq = [slice(h * GLA_DK, (h + 1) * GLA_DK) for _, h in X]
    kv = [slice(h * GLA_DV, (h + 1) * GLA_DV) for _, h in X]

    def body(c, carry):
        sl = pl.ds(pl.multiple_of(c * C, C), C)
        live = ((tb * MIX_CHUNKS + c) * C + rowi) >= front
        a = [a_ref[b, sl, :].astype(MXU_DT) for b in range(nb)]
        g = [_log_sigmoid(jnp.dot(a[b], wa_ref[h], preferred_element_type=F32) + ba_ref[h]) / GLA_TAU for b, h in X]
        G = [jnp.dot(tril, g[i], precision=HI, preferred_element_type=F32) for i in N]
        q = [q_ref[b, sl, kq[i]].astype(F32) * (GLA_DK ** -0.5) for i, (b, h) in enumerate(X)]
        k = [jnp.where(live, k_ref[b, sl, kq[i]].astype(F32), 0.0) for i, (b, h) in enumerate(X)]
        v = [jnp.where(live, v_ref[b, sl, kv[i]].astype(F32), 0.0).astype(MXU_DT) for i, (b, h) in enumerate(X)]
        S = [s_ref[b, h] for b, h in X]
        g_last = [G[i][C - 1:C, :] for i in N]
        g_mid = [G[i][C // 2 - 1:C // 2, :] for i in N]
        o_inter = [_mm(q[i] * jnp.exp(G[i]), S[i]) for i in N]
        A = [jnp.where(causal, _mm_nt(q[i] * jnp.exp(G[i] - g_mid[i]), k[i] * jnp.exp(g_mid[i] - G[i])), 0.0) for i in N]
        o_intra = [_mm(A[i], v[i]) for i in N]
        kd_t = [jnp.transpose(k[i] * jnp.exp(g_last[i] - G[i])) for i in N]
        dec_col = [jnp.transpose(jnp.broadcast_to(jnp.exp(g_last[i]), (C, GLA_DK))) for i in N]
        for i, (b, h) in enumerate(X):
            s_ref[b, h] = jnp.concatenate([dec_col[i], dec_col[i]], axis=1) * S[i] + _mm(kd_t[i], v[i])
        for i, (b, h) in enumerate(X):
            r = r_ref[b, sl, kv[i]].astype(F32)
            o = _rms(o_inter[i] + o_intra[i], gh_ref[...]) * (r * _sigmoid(r))
            o_ref[b, sl, kv[i]] = o.astype(o_ref.dtype)
        return carry

    lax.fori_loop(0, jnp.minimum(MIX_CHUNKS, nchunks - tb * MIX_CHUNKS), body, 0)


def _gla(zl, zh, wa, ba, gh, *, front):
    B, Tp, _ = zl.shape
    nchunks = Tp // LANES
    TB = min(MIX_CHUNKS * LANES, Tp)
    WK, WV = GLA_H * GLA_DK, GLA_H * GLA_DV
    nb = GLA_ROWS if B % GLA_ROWS == 0 else 1
    return pl.pallas_call(
        functools.partial(_gla_kernel, front=front, nchunks=nchunks, nb=nb),
        grid=(B // nb, pl.cdiv(Tp, TB)),
        in_specs=[pl.BlockSpec((nb, TB, WK), lambda b, t: (b, t, Z_GLA_Q // WK)),
                  pl.BlockSpec((nb, TB, WK), lambda b, t: (b, t, Z_GLA_K // WK)),
                  pl.BlockSpec((nb, TB, WV), lambda b, t: (b, t, Z_GLA_V // WV)),
                  pl.BlockSpec((nb, TB, WV), lambda b, t: (b, t, Z_GLA_R // WV)),
                  pl.BlockSpec((nb, TB, LANES), lambda b, t: (b, t, Z_SM0 // LANES)),
                  pl.BlockSpec((GLA_H, LANES, GLA_DK), lambda b, t: (0, 0, 0)),
                  pl.BlockSpec((GLA_H, 1, GLA_DK), lambda b, t: (0, 0, 0)),
                  pl.BlockSpec((1, GLA_DV), lambda b, t: (0, 0))],
        out_specs=pl.BlockSpec((nb, TB, WV), lambda b, t: (b, t, 0)),
        out_shape=jax.ShapeDtypeStruct((B, Tp, WV), ACT_DT),
        scratch_shapes=[pltpu.VMEM((nb, GLA_H, GLA_DK, GLA_DV), F32)],
        compiler_params=_params(("parallel", "arbitrary")),
        name="gla",
    )(zl, zl, zl, zl, zh, wa, ba, gh)


def _mlstm_kernel(q_ref, k_ref, v_ref, og_ref, gcol_ref, cw_ref, fbrow_ref, fb_ref, gh_ref, o_ref,
                  s_ref, m_ref, pq_ref, pk_ref, *, front, nchunks):
    C = LANES
    WK = ML_H * ML_DK
    tb = pl.program_id(1)

    @pl.when(tb == 0)
    def _():
        s_ref[...] = jnp.zeros_like(s_ref)
        m_ref[...] = jnp.zeros_like(m_ref)
        pq_ref[...] = jnp.zeros_like(pq_ref)
        pk_ref[...] = jnp.zeros_like(pk_ref)

    r0 = lax.broadcasted_iota(I32, (C, C), 0)
    r1 = lax.broadcasted_iota(I32, (C, C), 1)
    causal = r0 >= r1
    tril = causal.astype(F32)
    triu = (r0 <= r1).astype(F32)
    rowi = lax.broadcasted_iota(I32, (C, 1), 0)

    def conv_silu(x, tail, w):
        acc = x * w[ML_CONV - 1:ML_CONV, :]
        for d in range(1, ML_CONV):
            sh = pltpu.roll(x, d, 0)
            head = jnp.where(rowi[:SUBLANES] >= d, sh[:SUBLANES], pltpu.roll(tail, d, 0))
            sh = jnp.concatenate([head, sh[SUBLANES:]], axis=0)
            acc = acc + sh * w[ML_CONV - 1 - d:ML_CONV - d, :]
        return acc * _sigmoid(acc)

    def body(c, carry):
        sl = pl.ds(pl.multiple_of(c * C, C), C)
        cg = tb * MIX_CHUNKS + c
        live = (cg * C + rowi) >= front
        xq = jnp.where(live, q_ref[0, sl, :].astype(F32), 0.0)
        xk = jnp.where(live, k_ref[0, sl, :].astype(F32), 0.0)
        q_all = conv_silu(xq, pq_ref[...], cw_ref[:, :WK]) * (ML_DK ** -0.5)
        k_all = conv_silu(xk, pk_ref[...], cw_ref[:, WK:])
        pq_ref[...] = xq[C - SUBLANES:, :]
        pk_ref[...] = xk[C - SUBLANES:, :]
        lf_col = _log_sigmoid(gcol_ref[0, sl, :] + fbrow_ref[...])
        H = range(ML_H)
        vs = [slice(h * ML_DV, (h + 1) * ML_DV) for h in H]
        q = [q_all[:, h * ML_DK:(h + 1) * ML_DK].astype(MXU_DT) for h in H]
        k = [k_all[:, h * ML_DK:(h + 1) * ML_DK] for h in H]
        ones = jnp.ones((C, LANES), MXU_DT)
        v_aug = [jnp.concatenate([jnp.where(live, v_ref[0, sl, vs[h]].astype(F32), 0.0).astype(MXU_DT), ones], axis=1)
                 for h in H]
        m_prev = [m_ref[h, 0:1, :] for h in H]
        S = [s_ref[h] for h in H]
        sel = [jnp.dot(lf_col, (r0 == ML_H + h).astype(F32), precision=HI, preferred_element_type=F32) for h in H]
        b_bc = [jnp.dot(tril, sel[h], precision=HI, preferred_element_type=F32) for h in H]
        g_t = jnp.transpose(gcol_ref[0, sl, :])
        li_row = [g_t[h:h + 1, :] for h in H]
        lf_row = [_log_sigmoid(g_t[ML_H + h:ML_H + h + 1, :] + fb_ref[h]) for h in H]
        b_row = [jnp.dot(lf_row[h], triu, precision=HI, preferred_element_type=F32) for h in H]
        b_last = [b_row[h][:, C - 1:C] for h in H]
        qk = [_mm_nt(q[h], k[h]) for h in H]
        qs = [_mm(q[h], S[h]) for h in H]
        Dm = [jnp.where(causal, b_bc[h] - (b_row[h] - li_row[h]), -jnp.inf) for h in H]
        m_inter = [b_bc[h] + m_prev[h] for h in H]
        m = [jnp.maximum(m_inter[h], jnp.max(Dm[h], axis=1, keepdims=True)) for h in H]
        pv = [_mm(jnp.exp(Dm[h] - m[h]) * qk[h], v_aug[h]) for h in H]
        dec = [b_last[h] - b_row[h] + li_row[h] for h in H]
        m_new = [jnp.maximum(b_last[h] + m_prev[h], jnp.max(dec[h], axis=1, keepdims=True)) for h in H]
        kt_w = [jnp.transpose(k[h]) * jnp.exp(dec[h] - m_new[h]) for h in H]
        for h in H:
            w_old = jnp.exp(b_last[h] + m_prev[h] - m_new[h])
            s_ref[h] = jnp.concatenate([w_old, w_old, w_old], axis=1) * S[h] + _mm(kt_w[h], v_aug[h])
            m_ref[h] = jnp.broadcast_to(m_new[h], (SUBLANES, C))
        for h in H:
            w_inter = jnp.exp(m_inter[h] - m[h])
            num = jnp.concatenate([w_inter, w_inter, w_inter], axis=1) * qs[h] + pv[h]
            den = num[:, ML_DV:]
            dd = jnp.maximum(jnp.abs(den), jnp.exp(-m[h]))
            hval = num[:, :ML_DV] / jnp.concatenate([dd, dd], axis=1)
            og = og_ref[0, sl, vs[h]].astype(F32)
            o_ref[0, sl, vs[h]] = (_rms(hval, gh_ref[...]) * _sigmoid(og)).astype(o_ref.dtype)
        return carry

    lax.fori_loop(0, jnp.minimum(MIX_CHUNKS, nchunks - tb * MIX_CHUNKS), body, 0)


def _mlstm(zl, zh, conv_w, fb_row, fb, gh, *, front):
    B, Tp, _ = zl.shape
    nchunks = Tp // LANES
    TB = min(MIX_CHUNKS * LANES, Tp)
    WK, WV = ML_H * ML_DK, ML_H * ML_DV
    return pl.pallas_call(
        functools.partial(_mlstm_kernel, front=front, nchunks=nchunks),
        grid=(B, pl.cdiv(Tp, TB)),
        in_specs=[pl.BlockSpec((1, TB, WK), lambda b, t: (b, t, Z_ML_Q // WK)),
                  pl.BlockSpec((1, TB, WK), lambda b, t: (b, t, Z_ML_K // WK)),
                  pl.BlockSpec((1, TB, WV), lambda b, t: (b, t, Z_ML_V // WV)),
                  pl.BlockSpec((1, TB, WV), lambda b, t: (b, t, Z_ML_O // WV)),
                  pl.BlockSpec((1, TB, LANES), lambda b, t: (b, t, Z_SM2 // LANES)),
                  pl.BlockSpec((ML_CONV, 2 * WK), lambda b, t: (0, 0)),
                  pl.BlockSpec((1, LANES), lambda b, t: (0, 0)),
                  pl.BlockSpec(memory_space=pltpu.SMEM),
                  pl.BlockSpec((1, ML_DV), lambda b, t: (0, 0))],
        out_specs=pl.BlockSpec((1, TB, WV), lambda b, t: (b, t, 0)),
        out_shape=jax.ShapeDtypeStruct((B, Tp, WV), ACT_DT),
        scratch_shapes=[pltpu.VMEM((ML_H, ML_DK, ML_DV + LANES), F32), pltpu.VMEM((ML_H, SUBLANES, LANES), F32),
                        pltpu.VMEM((SUBLANES, WK), F32), pltpu.VMEM((SUBLANES, WK), F32)],
        compiler_params=_params(("parallel", "arbitrary")),
        name="mlstm",
    )(zl, zl, zl, zl, zh, conv_w, fb_row, fb, gh)


def _dsa_prep_kernel(cq_ref, ckv_ref, sm_ref, gq_ref, gkv_ref, lng_ref, lnb_ref, wuq_ref, wiq_ref,
                     q_ref, qi_ref, kv_ref, ki_ref):
    cq = _rms(cq_ref[0], gq_ref[...]).astype(MXU_DT)
    for hp in range(DSA_H // 2):
        w2 = wuq_ref[:, 2 * hp * DSA_KV_LAT:(2 * hp + 2) * DSA_KV_LAT]
        qf = jnp.dot(cq, w2, preferred_element_type=F32) * (DSA_KV_LAT ** -0.5 * LOG2E)
        q_ref[0, 2 * hp] = qf[:, :DSA_KV_LAT].astype(q_ref.dtype)
        q_ref[0, 2 * hp + 1] = qf[:, DSA_KV_LAT:].astype(q_ref.dtype)
    for hh in range(IDX_H):
        qi = jnp.dot(cq, wiq_ref[hh], preferred_element_type=F32) * (IDX_D ** -0.5)
        qi_ref[0, hh] = qi.astype(qi_ref.dtype)
    kv_ref[0] = _rms(ckv_ref[0], gkv_ref[...]).astype(kv_ref.dtype)
    x = sm_ref[0][:, :IDX_D]
    mu = jnp.mean(x, axis=-1, keepdims=True)
    var = jnp.mean(jnp.square(x - mu), axis=-1, keepdims=True)
    ki_ref[0] = ((x - mu) * lax.rsqrt(var + EPS) * lng_ref[...] + lnb_ref[...]).astype(ki_ref.dtype)


def _dsa_prep(z, gq, gkv, lng, lnb, wuq, wiq):
    B, Tp, _ = z.shape
    tm = max(t for t in range(2 * SUBLANES, PREP_ROWS_MAX + 1, 2 * SUBLANES) if Tp % t == 0)
    c2 = lambda b, i: (0, 0)
    return pl.pallas_call(
        _dsa_prep_kernel,
        grid=(B, Tp // tm),
        in_specs=[pl.BlockSpec((1, tm, DSA_Q_LORA), lambda b, i: (b, i, Z_CQ // DSA_Q_LORA)),
                  pl.BlockSpec((1, tm, DSA_KV_LAT), lambda b, i: (b, i, Z_CKV // DSA_KV_LAT)),
                  pl.BlockSpec((1, tm, LANES), lambda b, i: (b, i, Z_SM1 // LANES)),
                  pl.BlockSpec((1, DSA_Q_LORA), c2), pl.BlockSpec((1, DSA_KV_LAT), c2),
                  pl.BlockSpec((1, IDX_D), c2), pl.BlockSpec((1, IDX_D), c2),
                  pl.BlockSpec((DSA_Q_LORA, DSA_H * DSA_KV_LAT), c2),
                  pl.BlockSpec((IDX_H, DSA_Q_LORA, IDX_D), lambda b, i: (0, 0, 0))],
        out_specs=[pl.BlockSpec((1, DSA_H, tm, DSA_KV_LAT), lambda b, i: (b, 0, i, 0)),
                   pl.BlockSpec((1, IDX_H, tm, IDX_D), lambda b, i: (b, 0, i, 0)),
                   pl.BlockSpec((1, tm, DSA_KV_LAT), lambda b, i: (b, i, 0)),
                   pl.BlockSpec((1, tm, IDX_D), lambda b, i: (b, i, 0))],
        out_shape=[jax.ShapeDtypeStruct((B, DSA_H, Tp, DSA_KV_LAT), ACT_DT),
                   jax.ShapeDtypeStruct((B, IDX_H, Tp, IDX_D), ACT_DT),
                   jax.ShapeDtypeStruct((B, Tp, DSA_KV_LAT), ACT_DT),
                   jax.ShapeDtypeStruct((B, Tp, IDX_D), ACT_DT)],
        compiler_params=_params(("parallel", "parallel")),
        name="dsa_prep",
    )(z, z, z, gq, gkv, lng, lnb, wuq, wiq)


def _sortable(x):
    bits = pltpu.bitcast(x, I32)
    key = jnp.where(bits < 0, bits ^ 0x7FFFFFFF, bits)
    return jnp.where(x == 0.0, 0, key)


_KEY_NEG = int(np.array(NEG, np.float32).view(np.int32)) ^ 0x7FFFFFFF


def _dsa_kernel(ki_ref, kv_ref, kvt_ref, qi_ref, q_ref, wi_ref, bias_ref, wuv_ref, o_ref,
                key_ref, madd_ref, lg_ref, acc_ref, *, front, topk, Tp, ns):
    QB = LANES
    S = range(ns)
    j = pl.program_id(1)
    nkb = j + 1
    ngrp = nkb // KB_GROUP
    s_loc = lax.broadcasted_iota(I32, (QB, QB), 0)
    t_glob = j * QB + lax.broadcasted_iota(I32, (QB, QB), 1)
    wi = [wi_ref[s] * (IDX_H ** -0.5) for s in S]
    n_rest = Tp - nkb * QB

    def blk(kb):
        return pl.ds(pl.multiple_of(kb * QB, QB), QB)

    def score_block(kb, s):
        ki = ki_ref[s, blk(kb), :]
        acc = jnp.zeros((QB, QB), F32)
        for hp in range(IDX_H // 2):
            qi2 = qi_ref[s, 2 * hp:2 * hp + 2].reshape(2 * QB, IDX_D)
            sc = lax.dot_general(ki, qi2, NT_DIMS, preferred_element_type=F32)
            acc = acc + jnp.maximum(sc[:, :QB], 0.0) * wi[s][2 * hp:2 * hp + 1, :]
            acc = acc + jnp.maximum(sc[:, QB:], 0.0) * wi[s][2 * hp + 1:2 * hp + 2, :]
        s_glob = kb * QB + s_loc
        key = _sortable(jnp.where(s_glob <= t_glob, acc, NEG))
        key_ref[s, blk(kb), :] = jnp.where(s_glob >= front, key, INT_MIN)

    def score_group(i, c):
        for u in range(KB_GROUP):
            for s in S:
                score_block(KB_GROUP * i + u, s)
        return c

    def score_single(kb, c):
        for s in S:
            score_block(kb, s)
        return c

    lax.fori_loop(0, ngrp, score_group, 0)
    lax.fori_loop(ngrp * KB_GROUP, nkb, score_single, 0)

    def fold32(hit):
        return jnp.sum(hit.reshape(QB // SUBLANES, SUBLANES, QB), axis=0)

    def count(pred_fn):
        def hit_of(kb, s):
            return pred_fn(key_ref[s, blk(kb), :], kb, s).astype(I32)

        def group(i, accs):
            out = []
            for s in S:
                hit = hit_of(KB_GROUP * i, s)
                for u in range(1, KB_GROUP):
                    hit = hit + hit_of(KB_GROUP * i + u, s)
                out.append(accs[s] + fold32(hit))
            return tuple(out)

        accs = lax.fori_loop(0, ngrp, group, tuple(jnp.zeros((SUBLANES, QB), I32) for _ in S))
        accs = lax.fori_loop(ngrp * KB_GROUP, nkb,
                             lambda kb, a: tuple(a[s] + fold32(hit_of(kb, s)) for s in S), accs)
        return [jnp.sum(a, axis=0, keepdims=True) for a in accs]

    def count_ge(cands):
        cnt = count(lambda kblk, kb, s: kblk >= cands[s])
        return [cnt[s] + jnp.where(_KEY_NEG >= cands[s], n_rest, 0) for s in S]

    c0 = count_ge([jnp.zeros((1, QB), I32) for _ in S])
    few = nkb * QB - front <= topk
    base = tuple(jnp.where(few, INT_MIN + 1, jnp.where(c0[s] >= topk, 0, INT_MIN)).astype(I32) for s in S)

    def bit_body(i, base):
        cands = [base[s] | jnp.left_shift(jnp.int32(1), 30 - i) for s in S]
        cnt = count_ge(cands)
        return tuple(jnp.where(cnt[s] >= topk, cands[s], base[s]) for s in S)

    tau = lax.fori_loop(0, jnp.where(few, 0, 31), bit_body, base)

    def valid_of(kb):
        s_glob = kb * QB + s_loc
        return (s_glob >= front) & (s_glob <= t_glob)

    c_gt = count(lambda kblk, kb, s: kblk > tau[s])
    c_eq = count(lambda kblk, kb, s: (kblk == tau[s]) & valid_of(kb))
    need = [topk - (c_gt[s] + jnp.where(_KEY_NEG > tau[s], n_rest, 0)) for s in S]

    for s in S:
        @pl.when(jnp.max((c_eq[s] > jnp.maximum(need[s], 0)).astype(I32)) > 0)
        def _(s=s):
            lstrict = (s_loc > lax.broadcasted_iota(I32, (QB, QB), 1)).astype(MXU_DT)
            needf = need[s].astype(F32)

            def tie_body(kb, seen):
                kblk = key_ref[s, blk(kb), :]
                tie = (kblk == tau[s]) & valid_of(kb)
                tief = tie.astype(F32)
                rank = seen + jnp.dot(lstrict, tief.astype(MXU_DT), preferred_element_type=F32)
                key_ref[s, blk(kb), :] = jnp.where(tie & (rank >= needf), INT_MIN, kblk)
                return seen + jnp.sum(tief, axis=0, keepdims=True)

            lax.fori_loop(0, nkb, tie_body, jnp.zeros((1, QB), F32))

    def mask_body(kb, c):
        for s in S:
            sel = (key_ref[s, blk(kb), :] >= tau[s]) & valid_of(kb)
            madd_ref[s, blk(kb), :] = jnp.where(sel, 0.0, NEG)
        return c

    lax.fori_loop(0, nkb, mask_body, 0)

    NP = DSA_H // 2
    grp = (QB // SUBLANES, SUBLANES, 2 * QB)

    AS = ATTN_STREAMS if ns % ATTN_STREAMS == 0 else 1

    def attend(g, c):
        ss = [g * AS + u for u in range(AS)]
        U = range(AS)
        acc_ref[...] = jnp.zeros_like(acc_ref)

        nu = (nkb + 1) // 2

        def unit_inputs(un):
            out = []
            for u in U:
                per = []
                for hb in range(2):
                    kb = 2 * un + hb
                    kbc = jnp.minimum(kb, nkb - 1)
                    ma = jnp.where(kb < nkb, madd_ref[ss[u], blk(kbc), :], NEG)
                    per.append((kv_ref[ss[u], blk(kbc), :], jnp.concatenate([ma, ma], axis=1), jnp.minimum(j - kbc, 2)))
                out.append(per)
            return out

        def stage1(p, slot, inputs):
            out = []
            for u in U:
                q2 = q_ref[ss[u], 2 * p:2 * p + 2].reshape(2 * QB, DSA_KV_LAT)
                bm = None
                for hb, (kvb, ma2, rel) in enumerate(inputs[u]):
                    lg = lax.dot_general(kvb, q2, NT_DIMS, preferred_element_type=F32) + bias_ref[p, rel] + ma2
                    lg_ref[u, slot, p, hb * QB:(hb + 1) * QB, :] = lg
                    mx = jnp.max(jnp.max(lg.reshape(grp), axis=0), axis=0, keepdims=True)
                    bm = mx if bm is None else jnp.maximum(bm, mx)
                out.append(bm)
            return out

        def stage2(p, slot, kvts, m_old, bm):
            out = []
            for u in U:
                m_new = jnp.maximum(m_old[u], bm[u])
                pm = jnp.exp2(lg_ref[u, slot, p] - m_new)
                acc_ref[u, p] = (jnp.exp2(m_old[u] - m_new) * acc_ref[u, p]
                                 + jnp.dot(kvts[u], pm.astype(MXU_DT), preferred_element_type=F32))
                out.append(m_new)
            return out

        def kvt_of(un):
            return [jnp.concatenate([kvt_ref[ss[u], 2 * un], kvt_ref[ss[u], jnp.minimum(2 * un + 1, nkb - 1)]], axis=1)
                    for u in U]

        inp0 = unit_inputs(0)
        bm0 = tuple(tuple(stage1(p, 0, inp0)) for p in range(NP))

        def attn_body(i, carry):
            ms, bm_a = carry
            a = 2 * i
            inp_b = unit_inputs(a + 1)
            inp_c = unit_inputs(jnp.minimum(a + 2, nu - 1))
            kvt_a = kvt_of(a)
            kvt_b = kvt_of(a + 1)
            ms_a, bm_b = [], []
            for p in range(NP):
                bm_b.append(tuple(stage1(p, 1, inp_b)))
                ms_a.append(tuple(stage2(p, 0, kvt_a, ms[p], bm_a[p])))
            ms_b, bm_c = [], []
            for p in range(NP):
                bm_c.append(tuple(stage1(p, 0, inp_c)))
                ms_b.append(tuple(stage2(p, 1, kvt_b, ms_a[p], bm_b[p])))
            return tuple(ms_b), tuple(bm_c)

        ninf = tuple(tuple(jnp.full((1, 2 * QB), -jnp.inf, F32) for _ in U) for _ in range(NP))
        ms, bm_last = lax.fori_loop(0, nu // 2, attn_body, (ninf, bm0))

        @pl.when(nu % 2 == 1)
        def _():
            kvts = kvt_of(nu - 1)
            for p in range(NP):
                stage2(p, 0, kvts, ms[p], bm_last[p])

        for u in U:
            for p in range(NP):
                acc = acc_ref[u, p]
                o_t = acc[:DSA_KV_LAT] / acc[DSA_KV_LAT:DSA_KV_LAT + 1]
                outs = []
                for hh in range(2):
                    o_h = jnp.transpose(o_t[:, hh * QB:(hh + 1) * QB])
                    outs.append(_mm(o_h, wuv_ref[2 * p + hh]))
                o_ref[ss[u], :, 2 * p * DSA_DH:(2 * p + 2) * DSA_DH] = jnp.concatenate(outs, axis=1).astype(o_ref.dtype)
        return c

    lax.fori_loop(0, ns // AS, attend, 0)


def _dsa(ki, kv, kvt, qi, q, wi_t, bias_tiles, wuv, *, front, topk):
    B, Tp, _ = kv.shape
    nblk = Tp // LANES
    ns = DSA_STREAMS if B % DSA_STREAMS == 0 else 1
    return pl.pallas_call(
        functools.partial(_dsa_kernel, front=front, topk=topk, Tp=Tp, ns=ns),
        grid=(B // ns, nblk),
        in_specs=[pl.BlockSpec((ns, Tp, IDX_D), lambda b, j: (b, 0, 0)),
                  pl.BlockSpec((ns, Tp, DSA_KV_LAT), lambda b, j: (b, 0, 0)),
                  pl.BlockSpec((ns, nblk, DSA_KV_LAT + KVT_ONES, LANES), lambda b, j: (b, 0, 0, 0)),
                  pl.BlockSpec((ns, IDX_H, LANES, IDX_D), lambda b, j: (b, 0, j, 0)),
                  pl.BlockSpec((ns, DSA_H, LANES, DSA_KV_LAT), lambda b, j: (b, 0, j, 0)),
                  pl.BlockSpec((ns, IDX_H, LANES), lambda b, j: (b, 0, j)),
                  pl.BlockSpec((DSA_H // 2, 3, LANES, 2 * LANES), lambda b, j: (0, 0, 0, 0)),
                  pl.BlockSpec((DSA_H, DSA_KV_LAT, DSA_DH), lambda b, j: (0, 0, 0))],
        out_specs=pl.BlockSpec((ns, LANES, DSA_H * DSA_DH), lambda b, j: (b, j, 0)),
        out_shape=jax.ShapeDtypeStruct((B, Tp, DSA_H * DSA_DH), ACT_DT),
        scratch_shapes=[pltpu.VMEM((ns, Tp, LANES), I32), pltpu.VMEM((ns, Tp, LANES), F32),
                        pltpu.VMEM((ATTN_STREAMS, 2, DSA_H // 2, 2 * LANES, 2 * LANES), F32),
                        pltpu.VMEM((ATTN_STREAMS, DSA_H // 2, DSA_KV_LAT + KVT_ONES, 2 * LANES), F32)],
        compiler_params=_params(("parallel", "arbitrary")),
        name="dsa_attn",
    )(ki, kv, kvt, qi, q, wi_t, bias_tiles, wuv)


def _t5_bucket(rel):
    n = jnp.maximum(rel, 0)
    max_exact = T5_BUCKETS // 2
    nf = jnp.maximum(n, 1).astype(F32)
    large = max_exact + (jnp.log(nf / max_exact) / math.log(T5_MAX_DIST / max_exact)
                         * (T5_BUCKETS - max_exact)).astype(I32)
    large = jnp.minimum(large, T5_BUCKETS - 1)
    return jnp.where(n < max_exact, n, large)


def _t5_tiles(t5_bias):
    s = jnp.arange(LANES, dtype=I32)[:, None]
    t = jnp.arange(LANES, dtype=I32)[None, :]
    tiles = []
    for shift in (0, LANES, 2 * LANES):
        tiles.append(t5_bias[_t5_bucket(shift + t - s)])
    tiles = jnp.stack(tiles, axis=0)
    tiles = tiles.transpose(3, 0, 1, 2).reshape(DSA_H // 2, 2, 3, LANES, LANES)
    return (tiles.transpose(0, 2, 3, 1, 4).reshape(DSA_H // 2, 3, LANES, 2 * LANES) * LOG2E).astype(F32)


def _pack_in_proj(w_in, b_in):
    splits = np.cumsum([GLA_H * GLA_DK, GLA_H * GLA_DK, GLA_H * GLA_DV, GLA_H * GLA_DV, GLA_LR,
                        DSA_Q_LORA, DSA_KV_LAT, IDX_D, IDX_H,
                        ML_H * ML_DK, ML_H * ML_DK, ML_H * ML_DV, ML_H * ML_DV, ML_H, ML_H,
                        D_MODEL, D_MODEL])[:].tolist()

    def pack(a):
        (gq, gk, gv, gr, ga, cq, ckv, ik, iw, mq, mk, mv, mo, mi, mf, a_, b_, c_) = jnp.split(a, splits, axis=-1)
        pad = lambda n: jnp.zeros(a.shape[:-1] + (n,), a.dtype)
        lo = jnp.concatenate([gq, gk, gv, gr, mq, mk, mv, mo], axis=-1)
        hi = jnp.concatenate([a_, b_, c_, cq, ckv,
                              ga, pad(LANES - GLA_LR),
                              ik, iw, pad(LANES - IDX_D - IDX_H),
                              mi, mf, pad(LANES - 2 * ML_H)], axis=-1)
        return lo, hi

    return pack(w_in), pack(b_in)


def kernel(x, meta_tokens, t5_bias, g_pre_mix, w_in, b_in, w_gla_a2, b_gla_a, g_gla_head, w_br_gla, g_dsa_q, w_dsa_uq, w_idx_q, g_dsa_kv, w_dsa_uv, idx_ln_g, idx_ln_b, w_br_dsa, ml_conv, ml_f_bias, g_ml_head, w_br_ml, w_out, g_post_mix, g_pre_mlp, w_up, w_down, g_post_mlp):
    B, S, D = x.shape
    L = w_in.shape[0]
    T = S + N_META
    front = (-T) % LANES
    Tp = T + front
    R = B * Tp
    nblk = Tp // LANES
    topk = min(TOPK_MAX, (T - N_META) // 4)
    tm = ROW_TILE if R % ROW_TILE == 0 else LANES

    (w_lo, w_hi), (b_lo, b_hi) = _pack_in_proj(w_in, b_in)
    wa2 = jnp.pad(w_gla_a2, ((0, 0), (0, LANES - GLA_LR), (0, 0)))
    wa2 = wa2.reshape(L, LANES, GLA_H, GLA_DK).transpose(0, 2, 1, 3).astype(MXU_DT)
    fb_row = jnp.pad(ml_f_bias, ((0, 0), (ML_H, LANES - 2 * ML_H)))[:, None, :]
    layers = dict(
        g_pre_mix=g_pre_mix[:, None, :], w_lo=w_lo.astype(MXU_DT), b_lo=b_lo[:, None, :],
        w_hi=w_hi.astype(MXU_DT), b_hi=b_hi[:, None, :],
        wa2=wa2, ba=b_gla_a.reshape(L, GLA_H, 1, GLA_DK), g_gla_head=g_gla_head[:, None, :],
        w_br_gla=w_br_gla.astype(MXU_DT),
        g_dsa_q=g_dsa_q[:, None, :], w_dsa_uq=w_dsa_uq.astype(MXU_DT),
        w_idx_q=w_idx_q.reshape(L, DSA_Q_LORA, IDX_H, IDX_D).transpose(0, 2, 1, 3).astype(MXU_DT),
        g_dsa_kv=g_dsa_kv[:, None, :], w_dsa_uv=w_dsa_uv.astype(MXU_DT),
        idx_ln_g=idx_ln_g[:, None, :], idx_ln_b=idx_ln_b[:, None, :], w_br_dsa=w_br_dsa.astype(MXU_DT),
        ml_conv=ml_conv, fb_row=fb_row, ml_f_bias=ml_f_bias, g_ml_head=g_ml_head[:, None, :],
        w_br_ml=w_br_ml.astype(MXU_DT), w_out=w_out.astype(MXU_DT), g_post_mix=g_post_mix[:, None, :],
        g_pre_mlp=g_pre_mlp[:, None, :], w_up=w_up.astype(MXU_DT), w_down=w_down.astype(MXU_DT),
        g_post_mlp=g_post_mlp[:, None, :],
    )
    bias_tiles = _t5_tiles(t5_bias)
    zero_ff = jnp.zeros((1, D_FF), F32)

    meta = jnp.broadcast_to(meta_tokens.astype(x.dtype)[None], (B, N_META, D))
    h0 = jnp.concatenate([jnp.zeros((B, front, D), x.dtype), meta, x], axis=1).reshape(R, D)

    def layer(h, p):
        zl = _norm_matmul(h, p["g_pre_mix"], p["w_lo"], p["b_lo"], tm=tm, tn=1024, relu2=False,
                          out_dtype=ACT_DT, name="in_proj_lo").reshape(B, Tp, NZL)
        zh = _norm_matmul(h, p["g_pre_mix"], p["w_hi"], p["b_hi"], tm=tm, tn=768, relu2=False,
                          out_dtype=F32, name="in_proj_hi")
        zh3 = zh.reshape(B, Tp, NZH)
        o_gla = _gla(zl, zh3, p["wa2"], p["ba"], p["g_gla_head"], front=front)
        o_ml = _mlstm(zl, zh3, p["ml_conv"], p["fb_row"], p["ml_f_bias"], p["g_ml_head"], front=front)
        q, qi, kv, ki = _dsa_prep(zh3, p["g_dsa_q"], p["g_dsa_kv"], p["idx_ln_g"], p["idx_ln_b"],
                                  p["w_dsa_uq"], p["w_idx_q"])
        kvt = kv.reshape(B, nblk, LANES, DSA_KV_LAT).transpose(0, 1, 3, 2)
        kvt = jnp.concatenate([kvt, jnp.ones((B, nblk, KVT_ONES, LANES), kvt.dtype)], axis=2)
        wi_t = zh3[:, :, Z_SM1 + IDX_D:Z_SM1 + IDX_D + IDX_H].transpose(0, 2, 1)
        o_dsa = _dsa(ki, kv, kvt, qi, q, wi_t, bias_tiles, p["w_dsa_uv"], front=front, topk=topk)
        h = _mix(o_gla.reshape(R, D), o_dsa.reshape(R, D), o_ml.reshape(R, D), zh,
                 p["w_br_gla"], p["w_br_dsa"], p["w_br_ml"], p["w_out"], p["g_post_mix"], h, tm=tm)
        u = _norm_matmul(h, p["g_pre_mlp"], p["w_up"], zero_ff, tm=tm, tn=1024,
                         relu2=True, out_dtype=ACT_DT, name="mlp_up")
        h = _matmul_norm_res(u, p["w_down"], p["g_post_mlp"], h, tm=tm)
        return h, None

    h, _ = lax.scan(layer, h0, layers)
    return h.reshape(B, Tp, D)[:, front + N_META:]
```

```python
import functools
import math

import numpy as np
import jax
import jax.numpy as jnp
from jax import lax
from jax.experimental import pallas as pl
from jax.experimental.pallas import tpu as pltpu

D_MODEL = 1024
N_META = 16
GLA_H, GLA_DK, GLA_DV, GLA_LR, GLA_TAU = 4, 128, 256, 16, 16.0
DSA_H, DSA_DH, DSA_Q_LORA, DSA_KV_LAT = 16, 64, 256, 128
IDX_H, IDX_D, TOPK_MAX = 8, 64, 256
ML_H, ML_DK, ML_DV, ML_CONV = 4, 128, 256, 4
T5_BUCKETS, T5_MAX_DIST = 32, 128
D_FF = 4 * D_MODEL
EPS = 1e-6
NEG = -1e30
LOG2E = math.log2(math.e)

LANES = 128
SUBLANES = 8
VMEM_LIMIT = 56 * 1024 * 1024

MXU_DT = jnp.bfloat16
ACT_DT = jnp.bfloat16

F32 = jnp.float32
I32 = jnp.int32
INT_MIN = -2147483648
HI = lax.Precision.HIGHEST
NT_DIMS = (((1,), (1,)), ((), ()))

Z_GLA_Q, Z_GLA_K, Z_GLA_V, Z_GLA_R = 0, 512, 1024, 2048
Z_ML_Q, Z_ML_K, Z_ML_V, Z_ML_O = 3072, 3584, 4096, 5120
NZL = 6144
Z_GATE_A, Z_GATE_B, Z_GATE_C = 0, 1024, 2048
Z_CQ, Z_CKV = 3072, 3328
Z_SM0, Z_SM1, Z_SM2 = 3456, 3584, 3712
NZH = 3840
KVT_ONES = 16
ROW_TILE = 512
PREP_ROWS_MAX = 1088
DSA_STREAMS = 4
ATTN_STREAMS = 2
KB_GROUP = 4
GLA_ROWS = 2
MIX_CHUNKS = 6


def _params(sem):
    return pltpu.CompilerParams(dimension_semantics=sem, vmem_limit_bytes=VMEM_LIMIT)


def _mm(a, b):
    return jnp.dot(a.astype(MXU_DT), b.astype(MXU_DT), preferred_element_type=F32)


def _mm_nt(a, b):
    return lax.dot_general(a.astype(MXU_DT), b.astype(MXU_DT), NT_DIMS, preferred_element_type=F32)


def _rms(x, g):
    return x * lax.rsqrt(jnp.mean(x * x, axis=-1, keepdims=True) + EPS) * g


def _log_sigmoid(x):
    return jnp.minimum(x, 0.0) - jnp.log1p(jnp.exp(-jnp.abs(x)))


def _sigmoid(x):
    return 0.5 * jnp.tanh(0.5 * x) + 0.5


def _norm_matmul_kernel(x_ref, g_ref, w_ref, b_ref, o_ref, *, relu2, tn):
    xn = _rms(x_ref[...], g_ref[...]).astype(MXU_DT)
    for n0 in range(0, o_ref.shape[1], tn):
        acc = jnp.dot(xn, w_ref[:, n0:n0 + tn], preferred_element_type=F32) + b_ref[:, n0:n0 + tn]
        if relu2:
            acc = jnp.square(jnp.maximum(acc, 0.0))
        o_ref[:, n0:n0 + tn] = acc.astype(o_ref.dtype)


def _norm_matmul(x, g, w, b, *, tm, tn, relu2, out_dtype, name):
    R, K = x.shape
    N = w.shape[1]
    return pl.pallas_call(
        functools.partial(_norm_matmul_kernel, relu2=relu2, tn=tn),
        grid=(R // tm,),
        in_specs=[pl.BlockSpec((tm, K), lambda i: (i, 0)),
                  pl.BlockSpec((1, K), lambda i: (0, 0)),
                  pl.BlockSpec((K, N), lambda i: (0, 0)),
                  pl.BlockSpec((1, N), lambda i: (0, 0))],
        out_specs=pl.BlockSpec((tm, N), lambda i: (i, 0)),
        out_shape=jax.ShapeDtypeStruct((R, N), out_dtype),
        compiler_params=_params(("parallel",)),
        name=name,
    )(x, g, w, b)


def _matmul_norm_res_kernel(u_ref, w_ref, g_ref, h_ref, o_ref):
    y = jnp.dot(u_ref[...], w_ref[...], preferred_element_type=F32)
    o_ref[...] = h_ref[...] + _rms(y, g_ref[...])


def _matmul_norm_res(u, w, g, h, *, tm):
    R, K = u.shape
    N = w.shape[1]
    return pl.pallas_call(
        _matmul_norm_res_kernel,
        grid=(R // tm,),
        in_specs=[pl.BlockSpec((tm, K), lambda i: (i, 0)),
                  pl.BlockSpec((K, N), lambda i: (0, 0)),
                  pl.BlockSpec((1, N), lambda i: (0, 0)),
                  pl.BlockSpec((tm, N), lambda i: (i, 0))],
        out_specs=pl.BlockSpec((tm, N), lambda i: (i, 0)),
        out_shape=jax.ShapeDtypeStruct((R, N), F32),
        compiler_params=_params(("parallel",)),
        name="mlp_down",
    )(u, w, g, h)


def _mix_kernel(oa_ref, ob_ref, oc_ref, ga_ref, gb_ref, gc_ref, wa_ref, wb_ref, wc_ref,
                wo_ref, g_ref, h_ref, o_ref):
    ya = jnp.dot(oa_ref[...], wa_ref[...], preferred_element_type=F32)
    yb = jnp.dot(ob_ref[...], wb_ref[...], preferred_element_type=F32)
    yc = jnp.dot(oc_ref[...], wc_ref[...], preferred_element_type=F32)
    mix = _sigmoid(ga_ref[...]) * ya + _sigmoid(gb_ref[...]) * yb + _sigmoid(gc_ref[...]) * yc
    y = _mm(mix, wo_ref[...])
    o_ref[...] = h_ref[...] + _rms(y, g_ref[...])


def _mix(oa, ob, oc, z, wa, wb, wc, wo, g, h, *, tm):
    R, D = h.shape
    row = lambda i: (i, 0)
    const = lambda i: (0, 0)
    gate = lambda off: pl.BlockSpec((tm, D), lambda i: (i, off // D))
    return pl.pallas_call(
        _mix_kernel,
        grid=(R // tm,),
        in_specs=[pl.BlockSpec((tm, D), row), pl.BlockSpec((tm, D), row), pl.BlockSpec((tm, D), row),
                  gate(Z_GATE_A), gate(Z_GATE_B), gate(Z_GATE_C),
                  pl.BlockSpec((D, D), const), pl.BlockSpec((D, D), const), pl.BlockSpec((D, D), const),
                  pl.BlockSpec((D, D), const), pl.BlockSpec((1, D), const), pl.BlockSpec((tm, D), row)],
        out_specs=pl.BlockSpec((tm, D), row),
        out_shape=jax.ShapeDtypeStruct((R, D), F32),
        compiler_params=_params(("parallel",)),
        name="mix_out",
    )(oa, ob, oc, z, z, z, wa, wb, wc, wo, g, h)


def _gla_kernel(q_ref, k_ref, v_ref, r_ref, a_ref, wa_ref, ba_ref, gh_ref, o_ref, s_ref, *, front, nchunks, nb):
    C = LANES
    tb = pl.program_id(1)

    @pl.when(tb == 0)
    def _():
        s_ref[...] = jnp.zeros_like(s_ref)

    r0 = lax.broadcasted_iota(I32, (C, C), 0)
    r1 = lax.broadcasted_iota(I32, (C, C), 1)
    causal = r0 >= r1
    tril = causal.astype(F32)
    rowi = lax.broadcasted_iota(I32, (C, 1), 0)
    X = [(b, h) for b in range(nb) for h in range(GLA_H)]
    N = range(len(X))
    kq = [slice(h * GLA_DK, (h + 1) * GLA_DK) for _, h in X]
    kv = [slice(h * GLA_DV, (h + 1) * GLA_DV) for _, h in X]

    def body(c, carry):
        sl = pl.ds(pl.multiple_of(c * C, C), C)
        live = ((tb * MIX_CHUNKS + c) * C + rowi) >= front
        a = [a_ref[b, sl, :].astype(MXU_DT) for b in range(nb)]
        g = [_log_sigmoid(jnp.dot(a[b], wa_ref[h], preferred_element_type=F32) + ba_ref[h]) / GLA_TAU for b, h in X]
        G = [jnp.dot(tril, g[i], precision=HI, preferred_element_type=F32) for i in N]
        q = [q_ref[b, sl, kq[i]].astype(F32) * (GLA_DK ** -0.5) for i, (b, h) in enumerate(X)]
        k = [jnp.where(live, k_ref[b, sl, kq[i]].astype(F32), 0.0) for i, (b, h) in enumerate(X)]
        v = [jnp.where(live, v_ref[b, sl, kv[i]].astype(F32), 0.0).astype(MXU_DT) for i, (b, h) in enumerate(X)]
        S = [s_ref[b, h] for b, h in X]
        g_last = [G[i][C - 1:C, :] for i in N]
        g_mid = [G[i][C // 2 - 1:C // 2, :] for i in N]
        o_inter = [_mm(q[i] * jnp.exp(G[i]), S[i]) for i in N]
        A = [jnp.where(causal, _mm_nt(q[i] * jnp.exp(G[i] - g_mid[i]), k[i] * jnp.exp(g_mid[i] - G[i])), 0.0) for i in N]
        o_intra = [_mm(A[i], v[i]) for i in N]
        kd_t = [jnp.transpose(k[i] * jnp.exp(g_last[i] - G[i])) for i in N]
        dec_col = [jnp.transpose(jnp.broadcast_to(jnp.exp(g_last[i]), (C, GLA_DK))) for i in N]
        for i, (b, h) in enumerate(X):
            s_ref[b, h] = jnp.concatenate([dec_col[i], dec_col[i]], axis=1) * S[i] + _mm(kd_t[i], v[i])
        for i, (b, h) in enumerate(X):
            r = r_ref[b, sl, kv[i]].astype(F32)
            o = _rms(o_inter[i] + o_intra[i], gh_ref[...]) * (r * _sigmoid(r))
            o_ref[b, sl, kv[i]] = o.astype(o_ref.dtype)
        return carry

    lax.fori_loop(0, jnp.minimum(MIX_CHUNKS, nchunks - tb * MIX_CHUNKS), body, 0)


def _gla(zl, zh, wa, ba, gh, *, front):
    B, Tp, _ = zl.shape
    nchunks = Tp // LANES
    TB = min(MIX_CHUNKS * LANES, Tp)
    WK, WV = GLA_H * GLA_DK, GLA_H * GLA_DV
    nb = GLA_ROWS if B % GLA_ROWS == 0 else 1
    return pl.pallas_call(
        functools.partial(_gla_kernel, front=front, nchunks=nchunks, nb=nb),
        grid=(B // nb, pl.cdiv(Tp, TB)),
        in_specs=[pl.BlockSpec((nb, TB, WK), lambda b, t: (b, t, Z_GLA_Q // WK)),
                  pl.BlockSpec((nb, TB, WK), lambda b, t: (b, t, Z_GLA_K // WK)),
                  pl.BlockSpec((nb, TB, WV), lambda b, t: (b, t, Z_GLA_V // WV)),
                  pl.BlockSpec((nb, TB, WV), lambda b, t: (b, t, Z_GLA_R // WV)),
                  pl.BlockSpec((nb, TB, LANES), lambda b, t: (b, t, Z_SM0 // LANES)),
                  pl.BlockSpec((GLA_H, LANES, GLA_DK), lambda b, t: (0, 0, 0)),
                  pl.BlockSpec((GLA_H, 1, GLA_DK), lambda b, t: (0, 0, 0)),
                  pl.BlockSpec((1, GLA_DV), lambda b, t: (0, 0))],
        out_specs=pl.BlockSpec((nb, TB, WV), lambda b, t: (b, t, 0)),
        out_shape=jax.ShapeDtypeStruct((B, Tp, WV), ACT_DT),
        scratch_shapes=[pltpu.VMEM((nb, GLA_H, GLA_DK, GLA_DV), F32)],
        compiler_params=_params(("parallel", "arbitrary")),
        name="gla",
    )(zl, zl, zl, zl, zh, wa, ba, gh)


def _mlstm_kernel(q_ref, k_ref, v_ref, og_ref, gcol_ref, cw_ref, fbrow_ref, fb_ref, gh_ref, o_ref,
                  s_ref, m_ref, pq_ref, pk_ref, *, front, nchunks):
    C = LANES
    WK = ML_H * ML_DK
    tb = pl.program_id(1)

    @pl.when(tb == 0)
    def _():
        s_ref[...] = jnp.zeros_like(s_ref)
        m_ref[...] = jnp.zeros_like(m_ref)
        pq_ref[...] = jnp.zeros_like(pq_ref)
        pk_ref[...] = jnp.zeros_like(pk_ref)

    r0 = lax.broadcasted_iota(I32, (C, C), 0)
    r1 = lax.broadcasted_iota(I32, (C, C), 1)
    causal = r0 >= r1
    tril = causal.astype(F32)
    triu = (r0 <= r1).astype(F32)
    rowi = lax.broadcasted_iota(I32, (C, 1), 0)

    def conv_silu(x, tail, w):
        acc = x * w[ML_CONV - 1:ML_CONV, :]
        for d in range(1, ML_CONV):
            sh = pltpu.roll(x, d, 0)
            head = jnp.where(rowi[:SUBLANES] >= d, sh[:SUBLANES], pltpu.roll(tail, d, 0))
            sh = jnp.concatenate([head, sh[SUBLANES:]], axis=0)
            acc = acc + sh * w[ML_CONV - 1 - d:ML_CONV - d, :]
        return acc * _sigmoid(acc)

    def body(c, carry):
        sl = pl.ds(pl.multiple_of(c * C, C), C)
        cg = tb * MIX_CHUNKS + c
        live = (cg * C + rowi) >= front
        xq = jnp.where(live, q_ref[0, sl, :].astype(F32), 0.0)
        xk = jnp.where(live, k_ref[0, sl, :].astype(F32), 0.0)
        q_all = conv_silu(xq, pq_ref[...], cw_ref[:, :WK]) * (ML_DK ** -0.5)
        k_all = conv_silu(xk, pk_ref[...], cw_ref[:, WK:])
        pq_ref[...] = xq[C - SUBLANES:, :]
        pk_ref[...] = xk[C - SUBLANES:, :]
        lf_col = _log_sigmoid(gcol_ref[0, sl, :] + fbrow_ref[...])
        H = range(ML_H)
        vs = [slice(h * ML_DV, (h + 1) * ML_DV) for h in H]
        q = [q_all[:, h * ML_DK:(h + 1) * ML_DK].astype(MXU_DT) for h in H]
        k = [k_all[:, h * ML_DK:(h + 1) * ML_DK] for h in H]
        ones = jnp.ones((C, LANES), MXU_DT)
        v_aug = [jnp.concatenate([jnp.where(live, v_ref[0, sl, vs[h]].astype(F32), 0.0).astype(MXU_DT), ones], axis=1)
                 for h in H]
        m_prev = [m_ref[h, 0:1, :] for h in H]
        S = [s_ref[h] for h in H]
        b_all = jnp.dot(tril, lf_col, precision=HI, preferred_element_type=F32)
        b_bc = [jnp.dot(b_all, (r0 == ML_H + h).astype(F32), precision=HI, preferred_element_type=F32) for h in H]
        g_t = jnp.transpose(gcol_ref[0, sl, :])
        li_row = [g_t[h:h + 1, :] for h in H]
        lf_row = [_log_sigmoid(g_t[ML_H + h:ML_H + h + 1, :] + fb_ref[h]) for h in H]
        b_row = [jnp.dot(lf_row[h], triu, precision=HI, preferred_element_type=F32) for h in H]
        b_last = [b_row[h][:, C - 1:C] for h in H]
        qk = [_mm_nt(q[h], k[h]) for h in H]
        qs = [_mm(q[h], S[h]) for h in H]
        Dm = [jnp.where(causal, b_bc[h] - (b_row[h] - li_row[h]), -jnp.inf) for h in H]
        m_inter = [b_bc[h] + m_prev[h] for h in H]
        m = [jnp.maximum(m_inter[h], jnp.max(Dm[h], axis=1, keepdims=True)) for h in H]
        pv = [_mm(jnp.exp(Dm[h] - m[h]) * qk[h], v_aug[h]) for h in H]
        dec = [b_last[h] - b_row[h] + li_row[h] for h in H]
        m_new = [jnp.maximum(b_last[h] + m_prev[h], jnp.max(dec[h], axis=1, keepdims=True)) for h in H]
        kt_w = [jnp.transpose(k[h]) * jnp.exp(dec[h] - m_new[h]) for h in H]
        for h in H:
            w_old = jnp.exp(b_last[h] + m_prev[h] - m_new[h])
            s_ref[h] = jnp.concatenate([w_old, w_old, w_old], axis=1) * S[h] + _mm(kt_w[h], v_aug[h])
            m_ref[h] = jnp.broadcast_to(m_new[h], (SUBLANES, C))
        for h in H:
            w_inter = jnp.exp(m_inter[h] - m[h])
            num = jnp.concatenate([w_inter, w_inter, w_inter], axis=1) * qs[h] + pv[h]
            den = num[:, ML_DV:]
            dd = jnp.maximum(jnp.abs(den), jnp.exp(-m[h]))
            hval = num[:, :ML_DV] / jnp.concatenate([dd, dd], axis=1)
            og = og_ref[0, sl, vs[h]].astype(F32)
            o_ref[0, sl, vs[h]] = (_rms(hval, gh_ref[...]) * _sigmoid(og)).astype(o_ref.dtype)
        return carry

    lax.fori_loop(0, jnp.minimum(MIX_CHUNKS, nchunks - tb * MIX_CHUNKS), body, 0)


def _mlstm(zl, zh, conv_w, fb_row, fb, gh, *, front):
    B, Tp, _ = zl.shape
    nchunks = Tp // LANES
    TB = min(MIX_CHUNKS * LANES, Tp)
    WK, WV = ML_H * ML_DK, ML_H * ML_DV
    return pl.pallas_call(
        functools.partial(_mlstm_kernel, front=front, nchunks=nchunks),
        grid=(B, pl.cdiv(Tp, TB)),
        in_specs=[pl.BlockSpec((1, TB, WK), lambda b, t: (b, t, Z_ML_Q // WK)),
                  pl.BlockSpec((1, TB, WK), lambda b, t: (b, t, Z_ML_K // WK)),
                  pl.BlockSpec((1, TB, WV), lambda b, t: (b, t, Z_ML_V // WV)),
                  pl.BlockSpec((1, TB, WV), lambda b, t: (b, t, Z_ML_O // WV)),
                  pl.BlockSpec((1, TB, LANES), lambda b, t: (b, t, Z_SM2 // LANES)),
                  pl.BlockSpec((ML_CONV, 2 * WK), lambda b, t: (0, 0)),
                  pl.BlockSpec((1, LANES), lambda b, t: (0, 0)),
                  pl.BlockSpec(memory_space=pltpu.SMEM),
                  pl.BlockSpec((1, ML_DV), lambda b, t: (0, 0))],
        out_specs=pl.BlockSpec((1, TB, WV), lambda b, t: (b, t, 0)),
        out_shape=jax.ShapeDtypeStruct((B, Tp, WV), ACT_DT),
        scratch_shapes=[pltpu.VMEM((ML_H, ML_DK, ML_DV + LANES), F32), pltpu.VMEM((ML_H, SUBLANES, LANES), F32),
                        pltpu.VMEM((SUBLANES, WK), F32), pltpu.VMEM((SUBLANES, WK), F32)],
        compiler_params=_params(("parallel", "arbitrary")),
        name="mlstm",
    )(zl, zl, zl, zl, zh, conv_w, fb_row, fb, gh)


def _dsa_prep_kernel(cq_ref, ckv_ref, sm_ref, gq_ref, gkv_ref, lng_ref, lnb_ref, wuq_ref, wiq_ref,
                     q_ref, qi_ref, kv_ref, ki_ref):
    cq = _rms(cq_ref[0], gq_ref[...]).astype(MXU_DT)
    for hp in range(DSA_H // 2):
        w2 = wuq_ref[:, 2 * hp * DSA_KV_LAT:(2 * hp + 2) * DSA_KV_LAT]
        qf = jnp.dot(cq, w2, preferred_element_type=F32) * (DSA_KV_LAT ** -0.5 * LOG2E)
        q_ref[0, 2 * hp] = qf[:, :DSA_KV_LAT].astype(q_ref.dtype)
        q_ref[0, 2 * hp + 1] = qf[:, DSA_KV_LAT:].astype(q_ref.dtype)
    for hh in range(IDX_H):
        qi = jnp.dot(cq, wiq_ref[hh], preferred_element_type=F32) * (IDX_D ** -0.5)
        qi_ref[0, hh] = qi.astype(qi_ref.dtype)
    kv_ref[0] = _rms(ckv_ref[0], gkv_ref[...]).astype(kv_ref.dtype)
    x = sm_ref[0][:, :IDX_D]
    mu = jnp.mean(x, axis=-1, keepdims=True)
    var = jnp.mean(jnp.square(x - mu), axis=-1, keepdims=True)
    ki_ref[0] = ((x - mu) * lax.rsqrt(var + EPS) * lng_ref[...] + lnb_ref[...]).astype(ki_ref.dtype)


def _dsa_prep(z, gq, gkv, lng, lnb, wuq, wiq):
    B, Tp, _ = z.shape
    tm = max(t for t in range(2 * SUBLANES, PREP_ROWS_MAX + 1, 2 * SUBLANES) if Tp % t == 0)
    c2 = lambda b, i: (0, 0)
    return pl.pallas_call(
        _dsa_prep_kernel,
        grid=(B, Tp // tm),
        in_specs=[pl.BlockSpec((1, tm, DSA_Q_LORA), lambda b, i: (b, i, Z_CQ // DSA_Q_LORA)),
                  pl.BlockSpec((1, tm, DSA_KV_LAT), lambda b, i: (b, i, Z_CKV // DSA_KV_LAT)),
                  pl.BlockSpec((1, tm, LANES), lambda b, i: (b, i, Z_SM1 // LANES)),
                  pl.BlockSpec((1, DSA_Q_LORA), c2), pl.BlockSpec((1, DSA_KV_LAT), c2),
                  pl.BlockSpec((1, IDX_D), c2), pl.BlockSpec((1, IDX_D), c2),
                  pl.BlockSpec((DSA_Q_LORA, DSA_H * DSA_KV_LAT), c2),
                  pl.BlockSpec((IDX_H, DSA_Q_LORA, IDX_D), lambda b, i: (0, 0, 0))],
        out_specs=[pl.BlockSpec((1, DSA_H, tm, DSA_KV_LAT), lambda b, i: (b, 0, i, 0)),
                   pl.BlockSpec((1, IDX_H, tm, IDX_D), lambda b, i: (b, 0, i, 0)),
                   pl.BlockSpec((1, tm, DSA_KV_LAT), lambda b, i: (b, i, 0)),
                   pl.BlockSpec((1, tm, IDX_D), lambda b, i: (b, i, 0))],
        out_shape=[jax.ShapeDtypeStruct((B, DSA_H, Tp, DSA_KV_LAT), ACT_DT),
                   jax.ShapeDtypeStruct((B, IDX_H, Tp, IDX_D), ACT_DT),
                   jax.ShapeDtypeStruct((B, Tp, DSA_KV_LAT), ACT_DT),
                   jax.ShapeDtypeStruct((B, Tp, IDX_D), ACT_DT)],
        compiler_params=_params(("parallel", "parallel")),
        name="dsa_prep",
    )(z, z, z, gq, gkv, lng, lnb, wuq, wiq)


def _sortable(x):
    bits = pltpu.bitcast(x, I32)
    key = jnp.where(bits < 0, bits ^ 0x7FFFFFFF, bits)
    return jnp.where(x == 0.0, 0, key)


_KEY_NEG = int(np.array(NEG, np.float32).view(np.int32)) ^ 0x7FFFFFFF


def _dsa_kernel(ki_ref, kv_ref, kvt_ref, qi_ref, q_ref, wi_ref, bias_ref, wuv_ref, o_ref,
                key_ref, madd_ref, lg_ref, acc_ref, *, front, topk, Tp, ns):
    QB = LANES
    S = range(ns)
    j = pl.program_id(1)
    nkb = j + 1
    ngrp = nkb // KB_GROUP
    s_loc = lax.broadcasted_iota(I32, (QB, QB), 0)
    t_glob = j * QB + lax.broadcasted_iota(I32, (QB, QB), 1)
    wi = [wi_ref[s] * (IDX_H ** -0.5) for s in S]
    n_rest = Tp - nkb * QB

    def blk(kb):
        return pl.ds(pl.multiple_of(kb * QB, QB), QB)

    def score_block(kb, s):
        ki = ki_ref[s, blk(kb), :]
        acc = jnp.zeros((QB, QB), F32)
        for hp in range(IDX_H // 2):
            qi2 = qi_ref[s, 2 * hp:2 * hp + 2].reshape(2 * QB, IDX_D)
            sc = lax.dot_general(ki, qi2, NT_DIMS, preferred_element_type=F32)
            acc = acc + jnp.maximum(sc[:, :QB], 0.0) * wi[s][2 * hp:2 * hp + 1, :]
            acc = acc + jnp.maximum(sc[:, QB:], 0.0) * wi[s][2 * hp + 1:2 * hp + 2, :]
        s_glob = kb * QB + s_loc
        key = _sortable(jnp.where(s_glob <= t_glob, acc, NEG))
        key_ref[s, blk(kb), :] = jnp.where(s_glob >= front, key, INT_MIN)

    def score_group(i, c):
        for u in range(KB_GROUP):
            for s in S:
                score_block(KB_GROUP * i + u, s)
        return c

    def score_single(kb, c):
        for s in S:
            score_block(kb, s)
        return c

    lax.fori_loop(0, ngrp, score_group, 0)
    lax.fori_loop(ngrp * KB_GROUP, nkb, score_single, 0)

    def fold32(hit):
        return jnp.sum(hit.reshape(QB // SUBLANES, SUBLANES, QB), axis=0)

    def count(pred_fn):
        def hit_of(kb, s):
            return pred_fn(key_ref[s, blk(kb), :], kb, s).astype(I32)

        def group(i, accs):
            out = []
            for s in S:
                hit = hit_of(KB_GROUP * i, s)
                for u in range(1, KB_GROUP):
                    hit = hit + hit_of(KB_GROUP * i + u, s)
                out.append(accs[s] + fold32(hit))
            return tuple(out)

        accs = lax.fori_loop(0, ngrp, group, tuple(jnp.zeros((SUBLANES, QB), I32) for _ in S))
        accs = lax.fori_loop(ngrp * KB_GROUP, nkb,
                             lambda kb, a: tuple(a[s] + fold32(hit_of(kb, s)) for s in S), accs)
        return [jnp.sum(a, axis=0, keepdims=True) for a in accs]

    def count_ge(cands):
        cnt = count(lambda kblk, kb, s: kblk >= cands[s])
        return [cnt[s] + jnp.where(_KEY_NEG >= cands[s], n_rest, 0) for s in S]

    c0 = count_ge([jnp.zeros((1, QB), I32) for _ in S])
    few = nkb * QB - front <= topk
    base = tuple(jnp.where(few, INT_MIN + 1, jnp.where(c0[s] >= topk, 0, INT_MIN)).astype(I32) for s in S)

    def bit_body(i, base):
        cands = [base[s] | jnp.left_shift(jnp.int32(1), 30 - i) for s in S]
        cnt = count_ge(cands)
        return tuple(jnp.where(cnt[s] >= topk, cands[s], base[s]) for s in S)

    tau = lax.fori_loop(0, jnp.where(few, 0, 31), bit_body, base)

    def valid_of(kb):
        s_glob = kb * QB + s_loc
        return (s_glob >= front) & (s_glob <= t_glob)

    c_gt = count(lambda kblk, kb, s: kblk > tau[s])
    c_eq = count(lambda kblk, kb, s: (kblk == tau[s]) & valid_of(kb))
    need = [topk - (c_gt[s] + jnp.where(_KEY_NEG > tau[s], n_rest, 0)) for s in S]

    for s in S:
        @pl.when(jnp.max((c_eq[s] > jnp.maximum(need[s], 0)).astype(I32)) > 0)
        def _(s=s):
            lstrict = (s_loc > lax.broadcasted_iota(I32, (QB, QB), 1)).astype(MXU_DT)
            needf = need[s].astype(F32)

            def tie_body(kb, seen):
                kblk = key_ref[s, blk(kb), :]
                tie = (kblk == tau[s]) & valid_of(kb)
                tief = tie.astype(F32)
                rank = seen + jnp.dot(lstrict, tief.astype(MXU_DT), preferred_element_type=F32)
                key_ref[s, blk(kb), :] = jnp.where(tie & (rank >= needf), INT_MIN, kblk)
                return seen + jnp.sum(tief, axis=0, keepdims=True)

            lax.fori_loop(0, nkb, tie_body, jnp.zeros((1, QB), F32))

    def mask_body(kb, c):
        for s in S:
            sel = (key_ref[s, blk(kb), :] >= tau[s]) & valid_of(kb)
            madd_ref[s, blk(kb), :] = jnp.where(sel, 0.0, NEG)
        return c

    lax.fori_loop(0, nkb, mask_body, 0)

    NP = DSA_H // 2
    grp = (QB // SUBLANES, SUBLANES, 2 * QB)

    AS = ATTN_STREAMS if ns % ATTN_STREAMS == 0 else 1

    def attend(g, c):
        ss = [g * AS + u for u in range(AS)]
        U = range(AS)
        acc_ref[...] = jnp.zeros_like(acc_ref)

        nu = (nkb + 1) // 2

        def unit_inputs(un):
            out = []
            for u in U:
                per = []
                for hb in range(2):
                    kb = 2 * un + hb
                    kbc = jnp.minimum(kb, nkb - 1)
                    ma = jnp.where(kb < nkb, madd_ref[ss[u], blk(kbc), :], NEG)
                    per.append((kv_ref[ss[u], blk(kbc), :], jnp.concatenate([ma, ma], axis=1), jnp.minimum(j - kbc, 2)))
                out.append(per)
            return out

        def stage1(p, slot, inputs):
            out = []
            for u in U:
                q2 = q_ref[ss[u], 2 * p:2 * p + 2].reshape(2 * QB, DSA_KV_LAT)
                bm = None
                for hb, (kvb, ma2, rel) in enumerate(inputs[u]):
                    lg = lax.dot_general(kvb, q2, NT_DIMS, preferred_element_type=F32) + bias_ref[p, rel] + ma2
                    lg_ref[u, slot, p, hb * QB:(hb + 1) * QB, :] = lg
                    mx = jnp.max(jnp.max(lg.reshape(grp), axis=0), axis=0, keepdims=True)
                    bm = mx if bm is None else jnp.maximum(bm, mx)
                out.append(bm)
            return out

        def stage2(p, slot, kvts, m_old, bm):
            out = []
            for u in U:
                m_new = jnp.maximum(m_old[u], bm[u])
                pm = jnp.exp2(lg_ref[u, slot, p] - m_new)
                acc_ref[u, p] = (jnp.exp2(m_old[u] - m_new) * acc_ref[u, p]
                                 + jnp.dot(kvts[u], pm.astype(MXU_DT), preferred_element_type=F32))
                out.append(m_new)
            return out

        def kvt_of(un):
            return [jnp.concatenate([kvt_ref[ss[u], 2 * un], kvt_ref[ss[u], jnp.minimum(2 * un + 1, nkb - 1)]], axis=1)
                    for u in U]

        inp0 = unit_inputs(0)
        bm0 = tuple(tuple(stage1(p, 0, inp0)) for p in range(NP))

        def attn_body(i, carry):
            ms, bm_a = carry
            a = 2 * i
            inp_b = unit_inputs(a + 1)
            inp_c = unit_inputs(jnp.minimum(a + 2, nu - 1))
            kvt_a = kvt_of(a)
            kvt_b = kvt_of(a + 1)
            ms_a, bm_b = [], []
            for p in range(NP):
                bm_b.append(tuple(stage1(p, 1, inp_b)))
                ms_a.append(tuple(stage2(p, 0, kvt_a, ms[p], bm_a[p])))
            ms_b, bm_c = [], []
            for p in range(NP):
                bm_c.append(tuple(stage1(p, 0, inp_c)))
                ms_b.append(tuple(stage2(p, 1, kvt_b, ms_a[p], bm_b[p])))
            return tuple(ms_b), tuple(bm_c)

        ninf = tuple(tuple(jnp.full((1, 2 * QB), -jnp.inf, F32) for _ in U) for _ in range(NP))
        ms, bm_last = lax.fori_loop(0, nu // 2, attn_body, (ninf, bm0))

        @pl.when(nu % 2 == 1)
        def _():
            kvts = kvt_of(nu - 1)
            for p in range(NP):
                stage2(p, 0, kvts, ms[p], bm_last[p])

        for u in U:
            for p in range(NP):
                acc = acc_ref[u, p]
                o_t = acc[:DSA_KV_LAT] / acc[DSA_KV_LAT:DSA_KV_LAT + 1]
                outs = []
                for hh in range(2):
                    o_h = jnp.transpose(o_t[:, hh * QB:(hh + 1) * QB])
                    outs.append(_mm(o_h, wuv_ref[2 * p + hh]))
                o_ref[ss[u], :, 2 * p * DSA_DH:(2 * p + 2) * DSA_DH] = jnp.concatenate(outs, axis=1).astype(o_ref.dtype)
        return c

    lax.fori_loop(0, ns // AS, attend, 0)


def _dsa(ki, kv, kvt, qi, q, wi_t, bias_tiles, wuv, *, front, topk):
    B, Tp, _ = kv.shape
    nblk = Tp // LANES
    ns = DSA_STREAMS if B % DSA_STREAMS == 0 else 1
    return pl.pallas_call(
        functools.partial(_dsa_kernel, front=front, topk=topk, Tp=Tp, ns=ns),
        grid=(B // ns, nblk),
        in_specs=[pl.BlockSpec((ns, Tp, IDX_D), lambda b, j: (b, 0, 0)),
                  pl.BlockSpec((ns, Tp, DSA_KV_LAT), lambda b, j: (b, 0, 0)),
                  pl.BlockSpec((ns, nblk, DSA_KV_LAT + KVT_ONES, LANES), lambda b, j: (b, 0, 0, 0)),
                  pl.BlockSpec((ns, IDX_H, LANES, IDX_D), lambda b, j: (b, 0, j, 0)),
                  pl.BlockSpec((ns, DSA_H, LANES, DSA_KV_LAT), lambda b, j: (b, 0, j, 0)),
                  pl.BlockSpec((ns, IDX_H, LANES), lambda b, j: (b, 0, j)),
                  pl.BlockSpec((DSA_H // 2, 3, LANES, 2 * LANES), lambda b, j: (0, 0, 0, 0)),
                  pl.BlockSpec((DSA_H, DSA_KV_LAT, DSA_DH), lambda b, j: (0, 0, 0))],
        out_specs=pl.BlockSpec((ns, LANES, DSA_H * DSA_DH), lambda b, j: (b, j, 0)),
        out_shape=jax.ShapeDtypeStruct((B, Tp, DSA_H * DSA_DH), ACT_DT),
        scratch_shapes=[pltpu.VMEM((ns, Tp, LANES), I32), pltpu.VMEM((ns, Tp, LANES), F32),
                        pltpu.VMEM((ATTN_STREAMS, 2, DSA_H // 2, 2 * LANES, 2 * LANES), F32),
                        pltpu.VMEM((ATTN_STREAMS, DSA_H // 2, DSA_KV_LAT + KVT_ONES, 2 * LANES), F32)],
        compiler_params=_params(("parallel", "arbitrary")),
        name="dsa_attn",
    )(ki, kv, kvt, qi, q, wi_t, bias_tiles, wuv)


def _t5_bucket(rel):
    n = jnp.maximum(rel, 0)
    max_exact = T5_BUCKETS // 2
    nf = jnp.maximum(n, 1).astype(F32)
    large = max_exact + (jnp.log(nf / max_exact) / math.log(T5_MAX_DIST / max_exact)
                         * (T5_BUCKETS - max_exact)).astype(I32)
    large = jnp.minimum(large, T5_BUCKETS - 1)
    return jnp.where(n < max_exact, n, large)


def _t5_tiles(t5_bias):
    s = jnp.arange(LANES, dtype=I32)[:, None]
    t = jnp.arange(LANES, dtype=I32)[None, :]
    tiles = []
    for shift in (0, LANES, 2 * LANES):
        tiles.append(t5_bias[_t5_bucket(shift + t - s)])
    tiles = jnp.stack(tiles, axis=0)
    tiles = tiles.transpose(3, 0, 1, 2).reshape(DSA_H // 2, 2, 3, LANES, LANES)
    return (tiles.transpose(0, 2, 3, 1, 4).reshape(DSA_H // 2, 3, LANES, 2 * LANES) * LOG2E).astype(F32)


def _pack_in_proj(w_in, b_in):
    splits = np.cumsum([GLA_H * GLA_DK, GLA_H * GLA_DK, GLA_H * GLA_DV, GLA_H * GLA_DV, GLA_LR,
                        DSA_Q_LORA, DSA_KV_LAT, IDX_D, IDX_H,
                        ML_H * ML_DK, ML_H * ML_DK, ML_H * ML_DV, ML_H * ML_DV, ML_H, ML_H,
                        D_MODEL, D_MODEL])[:].tolist()

    def pack(a):
        (gq, gk, gv, gr, ga, cq, ckv, ik, iw, mq, mk, mv, mo, mi, mf, a_, b_, c_) = jnp.split(a, splits, axis=-1)
        pad = lambda n: jnp.zeros(a.shape[:-1] + (n,), a.dtype)
        lo = jnp.concatenate([gq, gk, gv, gr, mq, mk, mv, mo], axis=-1)
        hi = jnp.concatenate([a_, b_, c_, cq, ckv,
                              ga, pad(LANES - GLA_LR),
                              ik, iw, pad(LANES - IDX_D - IDX_H),
                              mi, mf, pad(LANES - 2 * ML_H)], axis=-1)
        return lo, hi

    return pack(w_in), pack(b_in)


def kernel(x, meta_tokens, t5_bias, g_pre_mix, w_in, b_in, w_gla_a2, b_gla_a, g_gla_head, w_br_gla, g_dsa_q, w_dsa_uq, w_idx_q, g_dsa_kv, w_dsa_uv, idx_ln_g, idx_ln_b, w_br_dsa, ml_conv, ml_f_bias, g_ml_head, w_br_ml, w_out, g_post_mix, g_pre_mlp, w_up, w_down, g_post_mlp):
    B, S, D = x.shape
    L = w_in.shape[0]
    T = S + N_META
    front = (-T) % LANES
    Tp = T + front
    R = B * Tp
    nblk = Tp // LANES
    topk = min(TOPK_MAX, (T - N_META) // 4)
    tm = ROW_TILE if R % ROW_TILE == 0 else LANES

    (w_lo, w_hi), (b_lo, b_hi) = _pack_in_proj(w_in, b_in)
    wa2 = jnp.pad(w_gla_a2, ((0, 0), (0, LANES - GLA_LR), (0, 0)))
    wa2 = wa2.reshape(L, LANES, GLA_H, GLA_DK).transpose(0, 2, 1, 3).astype(MXU_DT)
    fb_row = jnp.pad(ml_f_bias, ((0, 0), (ML_H, LANES - 2 * ML_H)))[:, None, :]
    layers = dict(
        g_pre_mix=g_pre_mix[:, None, :], w_lo=w_lo.astype(MXU_DT), b_lo=b_lo[:, None, :],
        w_hi=w_hi.astype(MXU_DT), b_hi=b_hi[:, None, :],
        wa2=wa2, ba=b_gla_a.reshape(L, GLA_H, 1, GLA_DK), g_gla_head=g_gla_head[:, None, :],
        w_br_gla=w_br_gla.astype(MXU_DT),
        g_dsa_q=g_dsa_q[:, None, :], w_dsa_uq=w_dsa_uq.astype(MXU_DT),
        w_idx_q=w_idx_q.reshape(L, DSA_Q_LORA, IDX_H, IDX_D).transpose(0, 2, 1, 3).astype(MXU_DT),
        g_dsa_kv=g_dsa_kv[:, None, :], w_dsa_uv=w_dsa_uv.astype(MXU_DT),
        idx_ln_g=idx_ln_g[:, None, :], idx_ln_b=idx_ln_b[:, None, :], w_br_dsa=w_br_dsa.astype(MXU_DT),
        ml_conv=ml_conv, fb_row=fb_row, ml_f_bias=ml_f_bias, g_ml_head=g_ml_head[:, None, :],
        w_br_ml=w_br_ml.astype(MXU_DT), w_out=w_out.astype(MXU_DT), g_post_mix=g_post_mix[:, None, :],
        g_pre_mlp=g_pre_mlp[:, None, :], w_up=w_up.astype(MXU_DT), w_down=w_down.astype(MXU_DT),
        g_post_mlp=g_post_mlp[:, None, :],
    )
    bias_tiles = _t5_tiles(t5_bias)
    zero_ff = jnp.zeros((1, D_FF), F32)

    meta = jnp.broadcast_to(meta_tokens.astype(x.dtype)[None], (B, N_META, D))
    h0 = jnp.concatenate([jnp.zeros((B, front, D), x.dtype), meta, x], axis=1).reshape(R, D)

    def layer(h, p):
        zl = _norm_matmul(h, p["g_pre_mix"], p["w_lo"], p["b_lo"], tm=tm, tn=1024, relu2=False,
                          out_dtype=ACT_DT, name="in_proj_lo").reshape(B, Tp, NZL)
        zh = _norm_matmul(h, p["g_pre_mix"], p["w_hi"], p["b_hi"], tm=tm, tn=768, relu2=False,
                          out_dtype=F32, name="in_proj_hi")
        zh3 = zh.reshape(B, Tp, NZH)
        o_gla = _gla(zl, zh3, p["wa2"], p["ba"], p["g_gla_head"], front=front)
        o_ml = _mlstm(zl, zh3, p["ml_conv"], p["fb_row"], p["ml_f_bias"], p["g_ml_head"], front=front)
        q, qi, kv, ki = _dsa_prep(zh3, p["g_dsa_q"], p["g_dsa_kv"], p["idx_ln_g"], p["idx_ln_b"],
                                  p["w_dsa_uq"], p["w_idx_q"])
        kvt = kv.reshape(B, nblk, LANES, DSA_KV_LAT).transpose(0, 1, 3, 2)
        kvt = jnp.concatenate([kvt, jnp.ones((B, nblk, KVT_ONES, LANES), kvt.dtype)], axis=2)
        wi_t = zh3[:, :, Z_SM1 + IDX_D:Z_SM1 + IDX_D + IDX_H].transpose(0, 2, 1)
        o_dsa = _dsa(ki, kv, kvt, qi, q, wi_t, bias_tiles, p["w_dsa_uv"], front=front, topk=topk)
        h = _mix(o_gla.reshape(R, D), o_dsa.reshape(R, D), o_ml.reshape(R, D), zh,
                 p["w_br_gla"], p["w_br_dsa"], p["w_br_ml"], p["w_out"], p["g_post_mix"], h, tm=tm)
        u = _norm_matmul(h, p["g_pre_mlp"], p["w_up"], zero_ff, tm=tm, tn=1024,
                         relu2=True, out_dtype=ACT_DT, name="mlp_up")
        h = _matmul_norm_res(u, p["w_down"], p["g_post_mlp"], h, tm=tm)
        return h, None

    h, _ = lax.scan(layer, h0, layers)
    return h.reshape(B, Tp, D)[:, front + N_META:]
```

```python
import functools
import math

import numpy as np
import jax
import jax.numpy as jnp
from jax import lax
from jax.experimental import pallas as pl
from jax.experimental.pallas import tpu as pltpu

D_MODEL = 1024
N_META = 16
GLA_H, GLA_DK, GLA_DV, GLA_LR, GLA_TAU = 4, 128, 256, 16, 16.0
DSA_H, DSA_DH, DSA_Q_LORA, DSA_KV_LAT = 16, 64, 256, 128
IDX_H, IDX_D, TOPK_MAX = 8, 64, 256
ML_H, ML_DK, ML_DV, ML_CONV = 4, 128, 256, 4
T5_BUCKETS, T5_MAX_DIST = 32, 128
D_FF = 4 * D_MODEL
EPS = 1e-6
NEG = -1e30
LOG2E = math.log2(math.e)

LANES = 128
SUBLANES = 8
VMEM_LIMIT = 56 * 1024 * 1024

MXU_DT = jnp.bfloat16
ACT_DT = jnp.bfloat16

F32 = jnp.float32
I32 = jnp.int32
INT_MIN = -2147483648
HI = lax.Precision.HIGHEST
NT_DIMS = (((1,), (1,)), ((), ()))

Z_GLA_Q, Z_GLA_K, Z_GLA_V, Z_GLA_R = 0, 512, 1024, 2048
Z_ML_Q, Z_ML_K, Z_ML_V, Z_ML_O = 3072, 3584, 4096, 5120
NZL = 6144
Z_GATE_A, Z_GATE_B, Z_GATE_C = 0, 1024, 2048
Z_CQ, Z_CKV = 3072, 3328
Z_SM0, Z_SM1, Z_SM2 = 3456, 3584, 3712
NZH = 3840
KVT_ONES = 16
ROW_TILE = 512
PREP_ROWS_MAX = 1088
DSA_STREAMS = 4
ATTN_STREAMS = 2
KB_GROUP = 4
GLA_ROWS = 2
MIX_CHUNKS = 6


def _params(sem):
    return pltpu.CompilerParams(dimension_semantics=sem, vmem_limit_bytes=VMEM_LIMIT)


def _mm(a, b):
    return jnp.dot(a.astype(MXU_DT), b.astype(MXU_DT), preferred_element_type=F32)


def _mm_nt(a, b):
    return lax.dot_general(a.astype(MXU_DT), b.astype(MXU_DT), NT_DIMS, preferred_element_type=F32)


def _rms(x, g):
    return x * lax.rsqrt(jnp.mean(x * x, axis=-1, keepdims=True) + EPS) * g


def _log_sigmoid(x):
    return jnp.minimum(x, 0.0) - jnp.log1p(jnp.exp(-jnp.abs(x)))


def _sigmoid(x):
    return 0.5 * jnp.tanh(0.5 * x) + 0.5


def _norm_matmul_kernel(x_ref, g_ref, w_ref, b_ref, o_ref, *, relu2, tn):
    xn = _rms(x_ref[...], g_ref[...]).astype(MXU_DT)
    for n0 in range(0, o_ref.shape[1], tn):
        acc = jnp.dot(xn, w_ref[:, n0:n0 + tn], preferred_element_type=F32) + b_ref[:, n0:n0 + tn]
        if relu2:
            acc = jnp.square(jnp.maximum(acc, 0.0))
        o_ref[:, n0:n0 + tn] = acc.astype(o_ref.dtype)


def _norm_matmul(x, g, w, b, *, tm, tn, relu2, out_dtype, name):
    R, K = x.shape
    N = w.shape[1]
    return pl.pallas_call(
        functools.partial(_norm_matmul_kernel, relu2=relu2, tn=tn),
        grid=(R // tm,),
        in_specs=[pl.BlockSpec((tm, K), lambda i: (i, 0)),
                  pl.BlockSpec((1, K), lambda i: (0, 0)),
                  pl.BlockSpec((K, N), lambda i: (0, 0)),
                  pl.BlockSpec((1, N), lambda i: (0, 0))],
        out_specs=pl.BlockSpec((tm, N), lambda i: (i, 0)),
        out_shape=jax.ShapeDtypeStruct((R, N), out_dtype),
        compiler_params=_params(("parallel",)),
        name=name,
    )(x, g, w, b)


def _matmul_norm_res_kernel(u_ref, w_ref, g_ref, h_ref, o_ref):
    y = jnp.dot(u_ref[...], w_ref[...], preferred_element_type=F32)
    o_ref[...] = h_ref[...] + _rms(y, g_ref[...])


def _matmul_norm_res(u, w, g, h, *, tm):
    R, K = u.shape
    N = w.shape[1]
    return pl.pallas_call(
        _matmul_norm_res_kernel,
        grid=(R // tm,),
        in_specs=[pl.BlockSpec((tm, K), lambda i: (i, 0)),
                  pl.BlockSpec((K, N), lambda i: (0, 0)),
                  pl.BlockSpec((1, N), lambda i: (0, 0)),
                  pl.BlockSpec((tm, N), lambda i: (i, 0))],
        out_specs=pl.BlockSpec((tm, N), lambda i: (i, 0)),
        out_shape=jax.ShapeDtypeStruct((R, N), F32),
        compiler_params=_params(("parallel",)),
        name="mlp_down",
    )(u, w, g, h)


def _mix_kernel(oa_ref, ob_ref, oc_ref, ga_ref, gb_ref, gc_ref, wa_ref, wb_ref, wc_ref,
                wo_ref, g_ref, h_ref, o_ref):
    ya = jnp.dot(oa_ref[...], wa_ref[...], preferred_element_type=F32)
    yb = jnp.dot(ob_ref[...], wb_ref[...], preferred_element_type=F32)
    yc = jnp.dot(oc_ref[...], wc_ref[...], preferred_element_type=F32)
    mix = _sigmoid(ga_ref[...]) * ya + _sigmoid(gb_ref[...]) * yb + _sigmoid(gc_ref[...]) * yc
    y = _mm(mix, wo_ref[...])
    o_ref[...] = h_ref[...] + _rms(y, g_ref[...])


def _mix(oa, ob, oc, z, wa, wb, wc, wo, g, h, *, tm):
    R, D = h.shape
    row = lambda i: (i, 0)
    const = lambda i: (0, 0)
    gate = lambda off: pl.BlockSpec((tm, D), lambda i: (i, off // D))
    return pl.pallas_call(
        _mix_kernel,
        grid=(R // tm,),
        in_specs=[pl.BlockSpec((tm, D), row), pl.BlockSpec((tm, D), row), pl.BlockSpec((tm, D), row),
                  gate(Z_GATE_A), gate(Z_GATE_B), gate(Z_GATE_C),
                  pl.BlockSpec((D, D), const), pl.BlockSpec((D, D), const), pl.BlockSpec((D, D), const),
                  pl.BlockSpec((D, D), const), pl.BlockSpec((1, D), const), pl.BlockSpec((tm, D), row)],
        out_specs=pl.BlockSpec((tm, D), row),
        out_shape=jax.ShapeDtypeStruct((R, D), F32),
        compiler_params=_params(("parallel",)),
        name="mix_out",
    )(oa, ob, oc, z, z, z, wa, wb, wc, wo, g, h)


def _gla_kernel(q_ref, k_ref, v_ref, r_ref, a_ref, wa_ref, ba_ref, gh_ref, o_ref, s_ref, *, front, nchunks, nb):
    C = LANES
    tb = pl.program_id(1)

    @pl.when(tb == 0)
    def _():
        s_ref[...] = jnp.zeros_like(s_ref)

    r0 = lax.broadcasted_iota(I32, (C, C), 0)
    r1 = lax.broadcasted_iota(I32, (C, C), 1)
    causal = r0 >= r1
    tril = causal.astype(F32)
    rowi = lax.broadcasted_iota(I32, (C, 1), 0)
    X = [(b, h) for b in range(nb) for h in range(GLA_H)]
    N = range(len(X))
    kq = [slice(h * GLA_DK, (h + 1) * GLA_DK) for _, h in X]
    kv = [slice(h * GLA_DV, (h + 1) * GLA_DV) for _, h in X]

    def body(c, carry):
        sl = pl.ds(pl.multiple_of(c * C, C), C)
        live = ((tb * MIX_CHUNKS + c) * C + rowi) >= front
        a = [a_ref[b, sl, :].astype(MXU_DT) for b in range(nb)]
        g = [_log_sigmoid(jnp.dot(a[b], wa_ref[h], preferred_element_type=F32) + ba_ref[h]) / GLA_TAU for b, h in X]
        G = [jnp.dot(tril, g[i], precision=HI, preferred_element_type=F32) for i in N]
        q = [q_ref[b, sl, kq[i]].astype(F32) * (GLA_DK ** -0.5) for i, (b, h) in enumerate(X)]
        k = [jnp.where(live, k_ref[b, sl, kq[i]].astype(F32), 0.0) for i, (b, h) in enumerate(X)]
        v = [jnp.where(live, v_ref[b, sl, kv[i]].astype(F32), 0.0).astype(MXU_DT) for i, (b, h) in enumerate(X)]
        S = [s_ref[b, h] for b, h in X]
        g_last = [G[i][C - 1:C, :] for i in N]
        g_mid = [G[i][C // 2 - 1:C // 2, :] for i in N]
        o_inter = [_mm(q[i] * jnp.exp(G[i]), S[i]) for i in N]
        A = [jnp.where(causal, _mm_nt(q[i] * jnp.exp(G[i] - g_mid[i]), k[i] * jnp.exp(g_mid[i] - G[i])), 0.0) for i in N]
        o_intra = [_mm(A[i], v[i]) for i in N]
        kd_t = [jnp.transpose(k[i] * jnp.exp(g_last[i] - G[i])) for i in N]
        dec_col = [jnp.transpose(jnp.broadcast_to(jnp.exp(g_last[i]), (C, GLA_DK))) for i in N]
        for i, (b, h) in enumerate(X):
            s_ref[b, h] = jnp.concatenate([dec_col[i], dec_col[i]], axis=1) * S[i] + _mm(kd_t[i], v[i])
        for i, (b, h) in enumerate(X):
            r = r_ref[b, sl, kv[i]].astype(F32)
            o = _rms(o_inter[i] + o_intra[i], gh_ref[...]) * (r * _sigmoid(r))
            o_ref[b, sl, kv[i]] = o.astype(o_ref.dtype)
        return carry

    lax.fori_loop(0, jnp.minimum(MIX_CHUNKS, nchunks - tb * MIX_CHUNKS), body, 0)


def _gla(zl, zh, wa, ba, gh, *, front):
    B, Tp, _ = zl.shape
    nchunks = Tp // LANES
    TB = min(MIX_CHUNKS * LANES, Tp)
    WK, WV = GLA_H * GLA_DK, GLA_H * GLA_DV
    nb = GLA_ROWS if B % GLA_ROWS == 0 else 1
    return pl.pallas_call(
        functools.partial(_gla_kernel, front=front, nchunks=nchunks, nb=nb),
        grid=(B // nb, pl.cdiv(Tp, TB)),
        in_specs=[pl.BlockSpec((nb, TB, WK), lambda b, t: (b, t, Z_GLA_Q // WK)),
                  pl.BlockSpec((nb, TB, WK), lambda b, t: (b, t, Z_GLA_K // WK)),
                  pl.BlockSpec((nb, TB, WV), lambda b, t: (b, t, Z_GLA_V // WV)),
                  pl.BlockSpec((nb, TB, WV), lambda b, t: (b, t, Z_GLA_R // WV)),
                  pl.BlockSpec((nb, TB, LANES), lambda b, t: (b, t, Z_SM0 // LANES)),
                  pl.BlockSpec((GLA_H, LANES, GLA_DK), lambda b, t: (0, 0, 0)),
                  pl.BlockSpec((GLA_H, 1, GLA_DK), lambda b, t: (0, 0, 0)),
                  pl.BlockSpec((1, GLA_DV), lambda b, t: (0, 0))],
        out_specs=pl.BlockSpec((nb, TB, WV), lambda b, t: (b, t, 0)),
        out_shape=jax.ShapeDtypeStruct((B, Tp, WV), ACT_DT),
        scratch_shapes=[pltpu.VMEM((nb, GLA_H, GLA_DK, GLA_DV), F32)],
        compiler_params=_params(("parallel", "arbitrary")),
        name="gla",
    )(zl, zl, zl, zl, zh, wa, ba, gh)


def _mlstm_kernel(q_ref, k_ref, v_ref, og_ref, gcol_ref, cw_ref, fbrow_ref, fb_ref, gh_ref, o_ref,
                  s_ref, m_ref, pq_ref, pk_ref, *, front, nchunks):
    C = LANES
    WK = ML_H * ML_DK
    tb = pl.program_id(1)

    @pl.when(tb == 0)
    def _():
        s_ref[...] = jnp.zeros_like(s_ref)
        m_ref[...] = jnp.zeros_like(m_ref)
        pq_ref[...] = jnp.zeros_like(pq_ref)
        pk_ref[...] = jnp.zeros_like(pk_ref)

    r0 = lax.broadcasted_iota(I32, (C, C), 0)
    r1 = lax.broadcasted_iota(I32, (C, C), 1)
    causal = r0 >= r1
    tril = causal.astype(F32)
    triu = (r0 <= r1).astype(F32)
    rowi = lax.broadcasted_iota(I32, (C, 1), 0)

    def conv_silu(x, tail, w):
        acc = x * w[ML_CONV - 1:ML_CONV, :]
        for d in range(1, ML_CONV):
            sh = pltpu.roll(x, d, 0)
            head = jnp.where(rowi[:SUBLANES] >= d, sh[:SUBLANES], pltpu.roll(tail, d, 0))
            sh = jnp.concatenate([head, sh[SUBLANES:]], axis=0)
            acc = acc + sh * w[ML_CONV - 1 - d:ML_CONV - d, :]
        return acc * _sigmoid(acc)

    def body(c, carry):
        sl = pl.ds(pl.multiple_of(c * C, C), C)
        cg = tb * MIX_CHUNKS + c
        live = (cg * C + rowi) >= front
        xq = jnp.where(live, q_ref[0, sl, :].astype(F32), 0.0)
        xk = jnp.where(live, k_ref[0, sl, :].astype(F32), 0.0)
        q_all = conv_silu(xq, pq_ref[...], cw_ref[:, :WK]) * (ML_DK ** -0.5)
        k_all = conv_silu(xk, pk_ref[...], cw_ref[:, WK:])
        pq_ref[...] = xq[C - SUBLANES:, :]
        pk_ref[...] = xk[C - SUBLANES:, :]
        lf_col = _log_sigmoid(gcol_ref[0, sl, :] + fbrow_ref[...])
        H = range(ML_H)
        vs = [slice(h * ML_DV, (h + 1) * ML_DV) for h in H]
        q = [q_all[:, h * ML_DK:(h + 1) * ML_DK].astype(MXU_DT) for h in H]
        k = [k_all[:, h * ML_DK:(h + 1) * ML_DK] for h in H]
        ones = jnp.ones((C, LANES), MXU_DT)
        v_aug = [jnp.concatenate([jnp.where(live, v_ref[0, sl, vs[h]].astype(F32), 0.0).astype(MXU_DT), ones], axis=1)
                 for h in H]
        m_prev = [m_ref[h, 0:1, :] for h in H]
        S = [s_ref[h] for h in H]
        b_all = jnp.dot(tril, lf_col, precision=HI, preferred_element_type=F32)
        b_bc = [jnp.dot(b_all, (r0 == ML_H + h).astype(F32), precision=HI, preferred_element_type=F32) for h in H]
        g_t = jnp.transpose(gcol_ref[0, sl, :])
        li_row = [g_t[h:h + 1, :] for h in H]
        b_rows = jnp.dot(jnp.transpose(lf_col)[:2 * ML_H, :], triu, precision=HI, preferred_element_type=F32)
        b_row = [b_rows[ML_H + h:ML_H + h + 1, :] for h in H]
        b_last = [b_row[h][:, C - 1:C] for h in H]
        qk = [_mm_nt(q[h], k[h]) for h in H]
        qs = [_mm(q[h], S[h]) for h in H]
        Dm = [jnp.where(causal, b_bc[h] - (b_row[h] - li_row[h]), -jnp.inf) for h in H]
        m_inter = [b_bc[h] + m_prev[h] for h in H]
        m = [jnp.maximum(m_inter[h], jnp.max(Dm[h], axis=1, keepdims=True)) for h in H]
        pv = [_mm(jnp.exp(Dm[h] - m[h]) * qk[h], v_aug[h]) for h in H]
        dec = [b_last[h] - b_row[h] + li_row[h] for h in H]
        m_new = [jnp.maximum(b_last[h] + m_prev[h], jnp.max(dec[h], axis=1, keepdims=True)) for h in H]
        kt_w = [jnp.transpose(k[h]) * jnp.exp(dec[h] - m_new[h]) for h in H]
        for h in H:
            w_old = jnp.exp(b_last[h] + m_prev[h] - m_new[h])
            s_ref[h] = jnp.concatenate([w_old, w_old, w_old], axis=1) * S[h] + _mm(kt_w[h], v_aug[h])
            m_ref[h] = jnp.broadcast_to(m_new[h], (SUBLANES, C))
        for h in H:
            w_inter = jnp.exp(m_inter[h] - m[h])
            num = jnp.concatenate([w_inter, w_inter, w_inter], axis=1) * qs[h] + pv[h]
            den = num[:, ML_DV:]
            dd = jnp.maximum(jnp.abs(den), jnp.exp(-m[h]))
            hval = num[:, :ML_DV] / jnp.concatenate([dd, dd], axis=1)
            og = og_ref[0, sl, vs[h]].astype(F32)
            o_ref[0, sl, vs[h]] = (_rms(hval, gh_ref[...]) * _sigmoid(og)).astype(o_ref.dtype)
        return carry

    lax.fori_loop(0, jnp.minimum(MIX_CHUNKS, nchunks - tb * MIX_CHUNKS), body, 0)


def _mlstm(zl, zh, conv_w, fb_row, fb, gh, *, front):
    B, Tp, _ = zl.shape
    nchunks = Tp // LANES
    TB = min(MIX_CHUNKS * LANES, Tp)
    WK, WV = ML_H * ML_DK, ML_H * ML_DV
    return pl.pallas_call(
        functools.partial(_mlstm_kernel, front=front, nchunks=nchunks),
        grid=(B, pl.cdiv(Tp, TB)),
        in_specs=[pl.BlockSpec((1, TB, WK), lambda b, t: (b, t, Z_ML_Q // WK)),
                  pl.BlockSpec((1, TB, WK), lambda b, t: (b, t, Z_ML_K // WK)),
                  pl.BlockSpec((1, TB, WV), lambda b, t: (b, t, Z_ML_V // WV)),
                  pl.BlockSpec((1, TB, WV), lambda b, t: (b, t, Z_ML_O // WV)),
                  pl.BlockSpec((1, TB, LANES), lambda b, t: (b, t, Z_SM2 // LANES)),
                  pl.BlockSpec((ML_CONV, 2 * WK), lambda b, t: (0, 0)),
                  pl.BlockSpec((1, LANES), lambda b, t: (0, 0)),
                  pl.BlockSpec(memory_space=pltpu.SMEM),
                  pl.BlockSpec((1, ML_DV), lambda b, t: (0, 0))],
        out_specs=pl.BlockSpec((1, TB, WV), lambda b, t: (b, t, 0)),
        out_shape=jax.ShapeDtypeStruct((B, Tp, WV), ACT_DT),
        scratch_shapes=[pltpu.VMEM((ML_H, ML_DK, ML_DV + LANES), F32), pltpu.VMEM((ML_H, SUBLANES, LANES), F32),
                        pltpu.VMEM((SUBLANES, WK), F32), pltpu.VMEM((SUBLANES, WK), F32)],
        compiler_params=_params(("parallel", "arbitrary")),
        name="mlstm",
    )(zl, zl, zl, zl, zh, conv_w, fb_row, fb, gh)


def _dsa_prep_kernel(cq_ref, ckv_ref, sm_ref, gq_ref, gkv_ref, lng_ref, lnb_ref, wuq_ref, wiq_ref,
                     q_ref, qi_ref, kv_ref, ki_ref):
    cq = _rms(cq_ref[0], gq_ref[...]).astype(MXU_DT)
    for hp in range(DSA_H // 2):
        w2 = wuq_ref[:, 2 * hp * DSA_KV_LAT:(2 * hp + 2) * DSA_KV_LAT]
        qf = jnp.dot(cq, w2, preferred_element_type=F32) * (DSA_KV_LAT ** -0.5 * LOG2E)
        q_ref[0, 2 * hp] = qf[:, :DSA_KV_LAT].astype(q_ref.dtype)
        q_ref[0, 2 * hp + 1] = qf[:, DSA_KV_LAT:].astype(q_ref.dtype)
    for hh in range(IDX_H):
        qi = jnp.dot(cq, wiq_ref[hh], preferred_element_type=F32) * (IDX_D ** -0.5)
        qi_ref[0, hh] = qi.astype(qi_ref.dtype)
    kv_ref[0] = _rms(ckv_ref[0], gkv_ref[...]).astype(kv_ref.dtype)
    x = sm_ref[0][:, :IDX_D]
    mu = jnp.mean(x, axis=-1, keepdims=True)
    var = jnp.mean(jnp.square(x - mu), axis=-1, keepdims=True)
    ki_ref[0] = ((x - mu) * lax.rsqrt(var + EPS) * lng_ref[...] + lnb_ref[...]).astype(ki_ref.dtype)


def _dsa_prep(z, gq, gkv, lng, lnb, wuq, wiq):
    B, Tp, _ = z.shape
    tm = max(t for t in range(2 * SUBLANES, PREP_ROWS_MAX + 1, 2 * SUBLANES) if Tp % t == 0)
    c2 = lambda b, i: (0, 0)
    return pl.pallas_call(
        _dsa_prep_kernel,
        grid=(B, Tp // tm),
        in_specs=[pl.BlockSpec((1, tm, DSA_Q_LORA), lambda b, i: (b, i, Z_CQ // DSA_Q_LORA)),
                  pl.BlockSpec((1, tm, DSA_KV_LAT), lambda b, i: (b, i, Z_CKV // DSA_KV_LAT)),
                  pl.BlockSpec((1, tm, LANES), lambda b, i: (b, i, Z_SM1 // LANES)),
                  pl.BlockSpec((1, DSA_Q_LORA), c2), pl.BlockSpec((1, DSA_KV_LAT), c2),
                  pl.BlockSpec((1, IDX_D), c2), pl.BlockSpec((1, IDX_D), c2),
                  pl.BlockSpec((DSA_Q_LORA, DSA_H * DSA_KV_LAT), c2),
                  pl.BlockSpec((IDX_H, DSA_Q_LORA, IDX_D), lambda b, i: (0, 0, 0))],
        out_specs=[pl.BlockSpec((1, DSA_H, tm, DSA_KV_LAT), lambda b, i: (b, 0, i, 0)),
                   pl.BlockSpec((1, IDX_H, tm, IDX_D), lambda b, i: (b, 0, i, 0)),
                   pl.BlockSpec((1, tm, DSA_KV_LAT), lambda b, i: (b, i, 0)),
                   pl.BlockSpec((1, tm, IDX_D), lambda b, i: (b, i, 0))],
        out_shape=[jax.ShapeDtypeStruct((B, DSA_H, Tp, DSA_KV_LAT), ACT_DT),
                   jax.ShapeDtypeStruct((B, IDX_H, Tp, IDX_D), ACT_DT),
                   jax.ShapeDtypeStruct((B, Tp, DSA_KV_LAT), ACT_DT),
                   jax.ShapeDtypeStruct((B, Tp, IDX_D), ACT_DT)],
        compiler_params=_params(("parallel", "parallel")),
        name="dsa_prep",
    )(z, z, z, gq, gkv, lng, lnb, wuq, wiq)


def _sortable(x):
    bits = pltpu.bitcast(x, I32)
    key = jnp.where(bits < 0, bits ^ 0x7FFFFFFF, bits)
    return jnp.where(x == 0.0, 0, key)


_KEY_NEG = int(np.array(NEG, np.float32).view(np.int32)) ^ 0x7FFFFFFF


def _dsa_kernel(ki_ref, kv_ref, kvt_ref, qi_ref, q_ref, wi_ref, bias_ref, wuv_ref, o_ref,
                key_ref, madd_ref, lg_ref, acc_ref, *, front, topk, Tp, ns):
    QB = LANES
    S = range(ns)
    j = pl.program_id(1)
    nkb = j + 1
    ngrp = nkb // KB_GROUP
    s_loc = lax.broadcasted_iota(I32, (QB, QB), 0)
    t_glob = j * QB + lax.broadcasted_iota(I32, (QB, QB), 1)
    wi = [wi_ref[s] * (IDX_H ** -0.5) for s in S]
    n_rest = Tp - nkb * QB

    def blk(kb):
        return pl.ds(pl.multiple_of(kb * QB, QB), QB)

    def score_block(kb, s):
        ki = ki_ref[s, blk(kb), :]
        acc = jnp.zeros((QB, QB), F32)
        for hp in range(IDX_H // 2):
            qi2 = qi_ref[s, 2 * hp:2 * hp + 2].reshape(2 * QB, IDX_D)
            sc = lax.dot_general(ki, qi2, NT_DIMS, preferred_element_type=F32)
            acc = acc + jnp.maximum(sc[:, :QB], 0.0) * wi[s][2 * hp:2 * hp + 1, :]
            acc = acc + jnp.maximum(sc[:, QB:], 0.0) * wi[s][2 * hp + 1:2 * hp + 2, :]
        s_glob = kb * QB + s_loc
        key = _sortable(jnp.where(s_glob <= t_glob, acc, NEG))
        key_ref[s, blk(kb), :] = jnp.where(s_glob >= front, key, INT_MIN)

    def score_group(i, c):
        for u in range(KB_GROUP):
            for s in S:
                score_block(KB_GROUP * i + u, s)
        return c

    def score_single(kb, c):
        for s in S:
            score_block(kb, s)
        return c

    lax.fori_loop(0, ngrp, score_group, 0)
    lax.fori_loop(ngrp * KB_GROUP, nkb, score_single, 0)

    def fold32(hit):
        return jnp.sum(hit.reshape(QB // SUBLANES, SUBLANES, QB), axis=0)

    def count(pred_fn):
        def hit_of(kb, s):
            return pred_fn(key_ref[s, blk(kb), :], kb, s).astype(I32)

        def group(i, accs):
            out = []
            for s in S:
                hit = hit_of(KB_GROUP * i, s)
                for u in range(1, KB_GROUP):
                    hit = hit + hit_of(KB_GROUP * i + u, s)
                out.append(accs[s] + fold32(hit))
            return tuple(out)

        accs = lax.fori_loop(0, ngrp, group, tuple(jnp.zeros((SUBLANES, QB), I32) for _ in S))
        accs = lax.fori_loop(ngrp * KB_GROUP, nkb,
                             lambda kb, a: tuple(a[s] + fold32(hit_of(kb, s)) for s in S), accs)
        return [jnp.sum(a, axis=0, keepdims=True) for a in accs]

    def count_ge(cands):
        cnt = count(lambda kblk, kb, s: kblk >= cands[s])
        return [cnt[s] + jnp.where(_KEY_NEG >= cands[s], n_rest, 0) for s in S]

    c0 = count_ge([jnp.zeros((1, QB), I32) for _ in S])
    few = nkb * QB - front <= topk
    base = tuple(jnp.where(few, INT_MIN + 1, jnp.where(c0[s] >= topk, 0, INT_MIN)).astype(I32) for s in S)

    def bit_body(i, base):
        cands = [base[s] | jnp.left_shift(jnp.int32(1), 30 - i) for s in S]
        cnt = count_ge(cands)
        return tuple(jnp.where(cnt[s] >= topk, cands[s], base[s]) for s in S)

    tau = lax.fori_loop(0, jnp.where(few, 0, 31), bit_body, base)

    def valid_of(kb):
        s_glob = kb * QB + s_loc
        return (s_glob >= front) & (s_glob <= t_glob)

    c_gt = count(lambda kblk, kb, s: kblk > tau[s])
    c_eq = count(lambda kblk, kb, s: (kblk == tau[s]) & valid_of(kb))
    need = [topk - (c_gt[s] + jnp.where(_KEY_NEG > tau[s], n_rest, 0)) for s in S]

    for s in S:
        @pl.when(jnp.max((c_eq[s] > jnp.maximum(need[s], 0)).astype(I32)) > 0)
        def _(s=s):
            lstrict = (s_loc > lax.broadcasted_iota(I32, (QB, QB), 1)).astype(MXU_DT)
            needf = need[s].astype(F32)

            def tie_body(kb, seen):
                kblk = key_ref[s, blk(kb), :]
                tie = (kblk == tau[s]) & valid_of(kb)
                tief = tie.astype(F32)
                rank = seen + jnp.dot(lstrict, tief.astype(MXU_DT), preferred_element_type=F32)
                key_ref[s, blk(kb), :] = jnp.where(tie & (rank >= needf), INT_MIN, kblk)
                return seen + jnp.sum(tief, axis=0, keepdims=True)

            lax.fori_loop(0, nkb, tie_body, jnp.zeros((1, QB), F32))

    def mask_body(kb, c):
        for s in S:
            sel = (key_ref[s, blk(kb), :] >= tau[s]) & valid_of(kb)
            madd_ref[s, blk(kb), :] = jnp.where(sel, 0.0, NEG)
        return c

    lax.fori_loop(0, nkb, mask_body, 0)

    NP = DSA_H // 2
    grp = (QB // SUBLANES, SUBLANES, 2 * QB)

    AS = ATTN_STREAMS if ns % ATTN_STREAMS == 0 else 1

    def attend(g, c):
        ss = [g * AS + u for u in range(AS)]
        U = range(AS)
        acc_ref[...] = jnp.zeros_like(acc_ref)

        nu = (nkb + 1) // 2

        def unit_inputs(un):
            out = []
            for u in U:
                per = []
                for hb in range(2):
                    kb = 2 * un + hb
                    kbc = jnp.minimum(kb, nkb - 1)
                    ma = jnp.where(kb < nkb, madd_ref[ss[u], blk(kbc), :], NEG)
                    per.append((kv_ref[ss[u], blk(kbc), :], jnp.concatenate([ma, ma], axis=1), jnp.minimum(j - kbc, 2)))
                out.append(per)
            return out

        def stage1(p, slot, inputs):
            out = []
            for u in U:
                q2 = q_ref[ss[u], 2 * p:2 * p + 2].reshape(2 * QB, DSA_KV_LAT)
                bm = None
                for hb, (kvb, ma2, rel) in enumerate(inputs[u]):
                    lg = lax.dot_general(kvb, q2, NT_DIMS, preferred_element_type=F32) + bias_ref[p, rel] + ma2
                    lg_ref[u, slot, p, hb * QB:(hb + 1) * QB, :] = lg
                    mx = jnp.max(jnp.max(lg.reshape(grp), axis=0), axis=0, keepdims=True)
                    bm = mx if bm is None else jnp.maximum(bm, mx)
                out.append(bm)
            return out

        def stage2(p, slot, kvts, m_old, bm):
            out = []
            for u in U:
                m_new = jnp.maximum(m_old[u], bm[u])
                pm = jnp.exp2(lg_ref[u, slot, p] - m_new)
                acc_ref[u, p] = (jnp.exp2(m_old[u] - m_new) * acc_ref[u, p]
                                 + jnp.dot(kvts[u], pm.astype(MXU_DT), preferred_element_type=F32))
                out.append(m_new)
            return out

        def kvt_of(un):
            return [jnp.concatenate([kvt_ref[ss[u], 2 * un], kvt_ref[ss[u], jnp.minimum(2 * un + 1, nkb - 1)]], axis=1)
                    for u in U]

        inp0 = unit_inputs(0)
        bm0 = tuple(tuple(stage1(p, 0, inp0)) for p in range(NP))

        def attn_body(i, carry):
            ms, bm_a = carry
            a = 2 * i
            inp_b = unit_inputs(a + 1)
            inp_c = unit_inputs(jnp.minimum(a + 2, nu - 1))
            kvt_a = kvt_of(a)
            kvt_b = kvt_of(a + 1)
            ms_a, bm_b = [], []
            for p in range(NP):
                bm_b.append(tuple(stage1(p, 1, inp_b)))
                ms_a.append(tuple(stage2(p, 0, kvt_a, ms[p], bm_a[p])))
            ms_b, bm_c = [], []
            for p in range(NP):
                bm_c.append(tuple(stage1(p, 0, inp_c)))
                ms_b.append(tuple(stage2(p, 1, kvt_b, ms_a[p], bm_b[p])))
            return tuple(ms_b), tuple(bm_c)

        ninf = tuple(tuple(jnp.full((1, 2 * QB), -jnp.inf, F32) for _ in U) for _ in range(NP))
        ms, bm_last = lax.fori_loop(0, nu // 2, attn_body, (ninf, bm0))

        @pl.when(nu % 2 == 1)
        def _():
            kvts = kvt_of(nu - 1)
            for p in range(NP):
                stage2(p, 0, kvts, ms[p], bm_last[p])

        for u in U:
            for p in range(NP):
                acc = acc_ref[u, p]
                o_t = acc[:DSA_KV_LAT] / acc[DSA_KV_LAT:DSA_KV_LAT + 1]
                outs = []
                for hh in range(2):
                    o_h = jnp.transpose(o_t[:, hh * QB:(hh + 1) * QB])
                    outs.append(_mm(o_h, wuv_ref[2 * p + hh]))
                o_ref[ss[u], :, 2 * p * DSA_DH:(2 * p + 2) * DSA_DH] = jnp.concatenate(outs, axis=1).astype(o_ref.dtype)
        return c

    lax.fori_loop(0, ns // AS, attend, 0)


def _dsa(ki, kv, kvt, qi, q, wi_t, bias_tiles, wuv, *, front, topk):
    B, Tp, _ = kv.shape
    nblk = Tp // LANES
    ns = DSA_STREAMS if B % DSA_STREAMS == 0 else 1
    return pl.pallas_call(
        functools.partial(_dsa_kernel, front=front, topk=topk, Tp=Tp, ns=ns),
        grid=(B // ns, nblk),
        in_specs=[pl.BlockSpec((ns, Tp, IDX_D), lambda b, j: (b, 0, 0)),
                  pl.BlockSpec((ns, Tp, DSA_KV_LAT), lambda b, j: (b, 0, 0)),
                  pl.BlockSpec((ns, nblk, DSA_KV_LAT + KVT_ONES, LANES), lambda b, j: (b, 0, 0, 0)),
                  pl.BlockSpec((ns, IDX_H, LANES, IDX_D), lambda b, j: (b, 0, j, 0)),
                  pl.BlockSpec((ns, DSA_H, LANES, DSA_KV_LAT), lambda b, j: (b, 0, j, 0)),
                  pl.BlockSpec((ns, IDX_H, LANES), lambda b, j: (b, 0, j)),
                  pl.BlockSpec((DSA_H // 2, 3, LANES, 2 * LANES), lambda b, j: (0, 0, 0, 0)),
                  pl.BlockSpec((DSA_H, DSA_KV_LAT, DSA_DH), lambda b, j: (0, 0, 0))],
        out_specs=pl.BlockSpec((ns, LANES, DSA_H * DSA_DH), lambda b, j: (b, j, 0)),
        out_shape=jax.ShapeDtypeStruct((B, Tp, DSA_H * DSA_DH), ACT_DT),
        scratch_shapes=[pltpu.VMEM((ns, Tp, LANES), I32), pltpu.VMEM((ns, Tp, LANES), F32),
                        pltpu.VMEM((ATTN_STREAMS, 2, DSA_H // 2, 2 * LANES, 2 * LANES), F32),
                        pltpu.VMEM((ATTN_STREAMS, DSA_H // 2, DSA_KV_LAT + KVT_ONES, 2 * LANES), F32)],
        compiler_params=_params(("parallel", "arbitrary")),
        name="dsa_attn",
    )(ki, kv, kvt, qi, q, wi_t, bias_tiles, wuv)


def _t5_bucket(rel):
    n = jnp.maximum(rel, 0)
    max_exact = T5_BUCKETS // 2
    nf = jnp.maximum(n, 1).astype(F32)
    large = max_exact + (jnp.log(nf / max_exact) / math.log(T5_MAX_DIST / max_exact)
                         * (T5_BUCKETS - max_exact)).astype(I32)
    large = jnp.minimum(large, T5_BUCKETS - 1)
    return jnp.where(n < max_exact, n, large)


def _t5_tiles(t5_bias):
    s = jnp.arange(LANES, dtype=I32)[:, None]
    t = jnp.arange(LANES, dtype=I32)[None, :]
    tiles = []
    for shift in (0, LANES, 2 * LANES):
        tiles.append(t5_bias[_t5_bucket(shift + t - s)])
    tiles = jnp.stack(tiles, axis=0)
    tiles = tiles.transpose(3, 0, 1, 2).reshape(DSA_H // 2, 2, 3, LANES, LANES)
    return (tiles.transpose(0, 2, 3, 1, 4).reshape(DSA_H // 2, 3, LANES, 2 * LANES) * LOG2E).astype(F32)


def _pack_in_proj(w_in, b_in):
    splits = np.cumsum([GLA_H * GLA_DK, GLA_H * GLA_DK, GLA_H * GLA_DV, GLA_H * GLA_DV, GLA_LR,
                        DSA_Q_LORA, DSA_KV_LAT, IDX_D, IDX_H,
                        ML_H * ML_DK, ML_H * ML_DK, ML_H * ML_DV, ML_H * ML_DV, ML_H, ML_H,
                        D_MODEL, D_MODEL])[:].tolist()

    def pack(a):
        (gq, gk, gv, gr, ga, cq, ckv, ik, iw, mq, mk, mv, mo, mi, mf, a_, b_, c_) = jnp.split(a, splits, axis=-1)
        pad = lambda n: jnp.zeros(a.shape[:-1] + (n,), a.dtype)
        lo = jnp.concatenate([gq, gk, gv, gr, mq, mk, mv, mo], axis=-1)
        hi = jnp.concatenate([a_, b_, c_, cq, ckv,
                              ga, pad(LANES - GLA_LR),
                              ik, iw, pad(LANES - IDX_D - IDX_H),
                              mi, mf, pad(LANES - 2 * ML_H)], axis=-1)
        return lo, hi

    return pack(w_in), pack(b_in)


def kernel(x, meta_tokens, t5_bias, g_pre_mix, w_in, b_in, w_gla_a2, b_gla_a, g_gla_head, w_br_gla, g_dsa_q, w_dsa_uq, w_idx_q, g_dsa_kv, w_dsa_uv, idx_ln_g, idx_ln_b, w_br_dsa, ml_conv, ml_f_bias, g_ml_head, w_br_ml, w_out, g_post_mix, g_pre_mlp, w_up, w_down, g_post_mlp):
    B, S, D = x.shape
    L = w_in.shape[0]
    T = S + N_META
    front = (-T) % LANES
    Tp = T + front
    R = B * Tp
    nblk = Tp // LANES
    topk = min(TOPK_MAX, (T - N_META) // 4)
    tm = ROW_TILE if R % ROW_TILE == 0 else LANES

    (w_lo, w_hi), (b_lo, b_hi) = _pack_in_proj(w_in, b_in)
    wa2 = jnp.pad(w_gla_a2, ((0, 0), (0, LANES - GLA_LR), (0, 0)))
    wa2 = wa2.reshape(L, LANES, GLA_H, GLA_DK).transpose(0, 2, 1, 3).astype(MXU_DT)
    fb_row = jnp.pad(ml_f_bias, ((0, 0), (ML_H, LANES - 2 * ML_H)))[:, None, :]
    layers = dict(
        g_pre_mix=g_pre_mix[:, None, :], w_lo=w_lo.astype(MXU_DT), b_lo=b_lo[:, None, :],
        w_hi=w_hi.astype(MXU_DT), b_hi=b_hi[:, None, :],
        wa2=wa2, ba=b_gla_a.reshape(L, GLA_H, 1, GLA_DK), g_gla_head=g_gla_head[:, None, :],
        w_br_gla=w_br_gla.astype(MXU_DT),
        g_dsa_q=g_dsa_q[:, None, :], w_dsa_uq=w_dsa_uq.astype(MXU_DT),
        w_idx_q=w_idx_q.reshape(L, DSA_Q_LORA, IDX_H, IDX_D).transpose(0, 2, 1, 3).astype(MXU_DT),
        g_dsa_kv=g_dsa_kv[:, None, :], w_dsa_uv=w_dsa_uv.astype(MXU_DT),
        idx_ln_g=idx_ln_g[:, None, :], idx_ln_b=idx_ln_b[:, None, :], w_br_dsa=w_br_dsa.astype(MXU_DT),
        ml_conv=ml_conv, fb_row=fb_row, ml_f_bias=ml_f_bias, g_ml_head=g_ml_head[:, None, :],
        w_br_ml=w_br_ml.astype(MXU_DT), w_out=w_out.astype(MXU_DT), g_post_mix=g_post_mix[:, None, :],
        g_pre_mlp=g_pre_mlp[:, None, :], w_up=w_up.astype(MXU_DT), w_down=w_down.astype(MXU_DT),
        g_post_mlp=g_post_mlp[:, None, :],
    )
    bias_tiles = _t5_tiles(t5_bias)
    zero_ff = jnp.zeros((1, D_FF), F32)

    meta = jnp.broadcast_to(meta_tokens.astype(x.dtype)[None], (B, N_META, D))
    h0 = jnp.concatenate([jnp.zeros((B, front, D), x.dtype), meta, x], axis=1).reshape(R, D)

    def layer(h, p):
        zl = _norm_matmul(h, p["g_pre_mix"], p["w_lo"], p["b_lo"], tm=tm, tn=1024, relu2=False,
                          out_dtype=ACT_DT, name="in_proj_lo").reshape(B, Tp, NZL)
        zh = _norm_matmul(h, p["g_pre_mix"], p["w_hi"], p["b_hi"], tm=tm, tn=768, relu2=False,
                          out_dtype=F32, name="in_proj_hi")
        zh3 = zh.reshape(B, Tp, NZH)
        o_gla = _gla(zl, zh3, p["wa2"], p["ba"], p["g_gla_head"], front=front)
        o_ml = _mlstm(zl, zh3, p["ml_conv"], p["fb_row"], p["ml_f_bias"], p["g_ml_head"], front=front)
        q, qi, kv, ki = _dsa_prep(zh3, p["g_dsa_q"], p["g_dsa_kv"], p["idx_ln_g"], p["idx_ln_b"],
                                  p["w_dsa_uq"], p["w_idx_q"])
        kvt = kv.reshape(B, nblk, LANES, DSA_KV_LAT).transpose(0, 1, 3, 2)
        kvt = jnp.concatenate([kvt, jnp.ones((B, nblk, KVT_ONES, LANES), kvt.dtype)], axis=2)
        wi_t = zh3[:, :, Z_SM1 + IDX_D:Z_SM1 + IDX_D + IDX_H].transpose(0, 2, 1)
        o_dsa = _dsa(ki, kv, kvt, qi, q, wi_t, bias_tiles, p["w_dsa_uv"], front=front, topk=topk)
        h = _mix(o_gla.reshape(R, D), o_dsa.reshape(R, D), o_ml.reshape(R, D), zh,
                 p["w_br_gla"], p["w_br_dsa"], p["w_br_ml"], p["w_out"], p["g_post_mix"], h, tm=tm)
        u = _norm_matmul(h, p["g_pre_mlp"], p["w_up"], zero_ff, tm=tm, tn=1024,
                         relu2=True, out_dtype=ACT_DT, name="mlp_up")
        h = _matmul_norm_res(u, p["w_down"], p["g_post_mlp"], h, tm=tm)
        return h, None

    h, _ = lax.scan(layer, h0, layers)
    return h.reshape(B, Tp, D)[:, front + N_META:]
```
